```python
import math
import jax
import jax.numpy as jnp
from jax import lax
import numpy as np

D_MODEL = 1024
BATCH = 8
SEQ = 4096
DEPTH = 2

F32 = jnp.float32
GRID_W = 64
GROUP_WIDTH = D_MODEL // 2
CHUNK = 64
Q_BLOCK = 128
EPS = 1e-6

HG_HEADS = 4
HG_DIM = GROUP_WIDTH // HG_HEADS
N_HGRN_LAYERS = (DEPTH + 1) // 2

DA_HEADS = 4
DA_DIM = GROUP_WIDTH // (2 * DA_HEADS)
ROPE_DIM = DA_DIM // 4
ROPE_THETA = 500000.0

ML_HEADS = 4
ML_DIM = GROUP_WIDTH // ML_HEADS
ML_CONV = 5

NA_HEADS = 8
NA_DIM = GROUP_WIDTH // NA_HEADS
NA_ROWS = 8
NA_COLS = 16

N_EXPERTS = 128
TOP_K = 8
N_GROUPS = 8
TOPK_GROUPS = 4
EXPERT_FF = D_MODEL // 4
SHARED_FF = D_MODEL // 4
ROUTED_SCALE = 2.5
MOE_BLOCK = 128

EVEN_IN = 8 * GROUP_WIDTH
ODD_IN = 7 * GROUP_WIDTH + 4 * ML_HEADS

kernel_name = 'hybrid_hgrn2_diffattn_mlstm_natten_moe'


def rms_norm(x, gain):
    xf = x.astype(F32)
    y = xf * lax.rsqrt(jnp.mean(xf * xf, axis=-1, keepdims=True) + EPS)
    return (y * gain.astype(F32)).astype(x.dtype)


def ada_modulation(c, w_mod, b_mod):
    m = (jnp.dot(jax.nn.silu(c), w_mod) + b_mod)[:, None, :]
    return jnp.split(m, 6, axis=-1)


def modulate(h, shift, scale):
    return h * (1 + scale) + shift


def to_chunks(t):
    b, h, T = t.shape[:3]
    return jnp.moveaxis(t.reshape(b, h, T // CHUNK, CHUNK, *t.shape[3:]), 2, 0)


def from_chunks(t):
    t = jnp.moveaxis(t, 0, 2)
    return t.reshape(t.shape[0], t.shape[1], -1, *t.shape[4:])


def flip_t(t):
    return jnp.flip(t, axis=2)


def hgrn2_scan(q, k, v, logf):
    b_, h_, _, dk = q.shape
    dv = v.shape[-1]
    mask = jnp.tril(jnp.ones((CHUNK, CHUNK), bool))

    def step(S, inp):
        qc, kc, vc, lfc = inp
        cum = jnp.cumsum(lfc, axis=2)
        diff = cum[:, :, :, None, :] - cum[:, :, None, :, :]
        w = jnp.exp(jnp.where(mask[:, :, None], diff, -jnp.inf))
        attn = jnp.einsum('bhtsd,bhsd->bhts', qc[:, :, :, None, :] * w, kc)
        o = jnp.einsum('bhts,bhse->bhte', attn, vc) + jnp.einsum('bhtd,bhde->bhte', qc * jnp.exp(cum), S)
        last = cum[:, :, -1:, :]
        S = jnp.exp(last[:, :, 0, :])[..., None] * S + jnp.einsum('bhsd,bhse->bhde', kc * jnp.exp(last - cum), vc)
        return S, o

    S0 = jnp.zeros((b_, h_, dk, dv), F32)
    _, o = lax.scan(step, S0, (to_chunks(q), to_chunks(k), to_chunks(v), to_chunks(logf)))
    return from_chunks(o)


def hgrn2_mixer(q, i, f_fwd, f_bwd, g, lb, norm_gain):
    B, T, _ = q.shape

    def heads(t):
        return t.reshape(B, T, HG_HEADS, HG_DIM).transpose(0, 2, 1, 3).astype(F32)

    qh = heads(jax.nn.silu(q))
    vh = heads(i)
    lb_h = lb.astype(F32).reshape(2, 1, HG_HEADS, 1, HG_DIM)

    def forget(fpre, lbd):
        fg = lbd + (1 - lbd) * jax.nn.sigmoid(heads(fpre))
        return 1 - fg, jnp.log(fg)

    kf, lff = forget(f_fwd, lb_h[0])
    kb, lfb = forget(f_bwd, lb_h[1])
    o_f = hgrn2_scan(qh, kf, vh, lff)
    o_b = flip_t(hgrn2_scan(flip_t(qh), flip_t(kb), flip_t(vh), flip_t(lfb)))
    o = rms_norm(o_f + o_b, norm_gain) * jax.nn.silu(heads(g))
    return o.transpose(0, 2, 1, 3).reshape(B, T, GROUP_WIDTH).astype(q.dtype)


def rope_partial(x, cos, sin):
    xr, xp = x[..., :ROPE_DIM], x[..., ROPE_DIM:]
    x1, x2 = xr[..., :ROPE_DIM // 2], xr[..., ROPE_DIM // 2:]
    rot = jnp.concatenate([x1 * cos - x2 * sin, x2 * cos + x1 * sin], axis=-1)
    return jnp.concatenate([rot.astype(x.dtype), xp], axis=-1)


def diff_attention(q, k, v, lq1, lk1, lq2, lk2, subln, layer_idx, cos, sin):
    B, T, _ = q.shape
    lambda_init = 0.8 - 0.6 * math.exp(-0.3 * layer_idx)
    lam = (jnp.exp(jnp.sum(lq1.astype(F32) * lk1.astype(F32)))
           - jnp.exp(jnp.sum(lq2.astype(F32) * lk2.astype(F32))) + lambda_init)
    qh = rope_partial(q.reshape(B, T, DA_HEADS, 2, DA_DIM).transpose(0, 2, 3, 1, 4), cos, sin)
    kh = rope_partial(k.reshape(B, T, DA_HEADS, 2, DA_DIM).transpose(0, 2, 3, 1, 4), cos, sin)
    vh = v.reshape(B, T, DA_HEADS, 2 * DA_DIM).transpose(0, 2, 1, 3)
    scale = DA_DIM ** -0.5
    nb = T // Q_BLOCK
    qb = jnp.moveaxis(qh.reshape(B, DA_HEADS, 2, nb, Q_BLOCK, DA_DIM), 3, 0)

    def block(qblk):
        s = jnp.einsum('bhmqd,bhmkd->bhmqk', qblk, kh).astype(F32) * scale
        p = jax.nn.softmax(s, axis=-1)
        a = p[:, :, 0] - lam * p[:, :, 1]
        return jnp.einsum('bhqk,bhkd->bhqd', a.astype(vh.dtype), vh)

    o = lax.map(block, qb)
    o = jnp.moveaxis(o, 0, 2).reshape(B, DA_HEADS, T, 2 * DA_DIM)
    o = rms_norm(o, subln) * (1 - lambda_init)
    return o.transpose(0, 2, 1, 3).reshape(B, T, GROUP_WIDTH).astype(q.dtype)


def mlstm_scan(q, k, v, ig, lf):
    b_, h_, _, dk = q.shape
    dv = v.shape[-1]
    mask = jnp.tril(jnp.ones((CHUNK, CHUNK), bool))

    def step(carry, inp):
        S, nvec, m = carry
        qc, kc, vc, ic, fc = inp
        cum = jnp.cumsum(fc, axis=-1)
        dmat = jnp.where(mask, cum[..., :, None] - cum[..., None, :] + ic[..., None, :], -jnp.inf)
        g = cum + m[..., None]
        mt = jnp.maximum(g, jnp.max(dmat, axis=-1))
        a = jnp.einsum('bhtd,bhsd->bhts', qc, kc) * jnp.exp(dmat - mt[..., None])
        inter = jnp.exp(g - mt)
        num = inter[..., None] * jnp.einsum('bhtd,bhde->bhte', qc, S) + jnp.einsum('bhts,bhse->bhte', a, vc)
        den = inter * jnp.einsum('bhtd,bhd->bht', qc, nvec) + jnp.sum(a, axis=-1)
        h = num / jnp.maximum(jnp.abs(den), jnp.exp(-mt))[..., None]
        last = cum[..., -1]
        ds = last[..., None] - cum + ic
        m_new = jnp.maximum(last + m, jnp.max(ds, axis=-1))
        decay = jnp.exp(last + m - m_new)
        ws = jnp.exp(ds - m_new[..., None])
        S = decay[..., None, None] * S + jnp.einsum('bhs,bhsd,bhse->bhde', ws, kc, vc)
        nvec = decay[..., None] * nvec + jnp.einsum('bhs,bhsd->bhd', ws, kc)
        return (S, nvec, m_new), h

    init = (jnp.zeros((b_, h_, dk, dv), F32), jnp.zeros((b_, h_, dk), F32), jnp.full((b_, h_), -1e30, F32))
    _, h = lax.scan(step, init, (to_chunks(q), to_chunks(k), to_chunks(v), to_chunks(ig), to_chunks(lf)))
    return from_chunks(h)


def short_conv_centred(x, w, b):
    C = x.shape[-1]
    y = lax.conv_general_dilated(x, w[:, None, :].astype(x.dtype), (1,), 'SAME',
                                 dimension_numbers=('NWC', 'WIO', 'NWC'), feature_group_count=C)
    return y + b.astype(x.dtype)


def mlstm_mixer(q, k, v, o, gates, i_bias, f_bias, conv_w, conv_b, norm_gain):
    B, T, _ = q.shape
    qk = jax.nn.silu(short_conv_centred(jnp.concatenate([q, k], axis=-1), conv_w, conv_b))
    q, k = jnp.split(qk, 2, axis=-1)

    def heads(t):
        return t.reshape(B, T, ML_HEADS, ML_DIM).transpose(0, 2, 1, 3).astype(F32)

    qh, kh, vh = heads(q), heads(k) * ML_DIM ** -0.5, heads(v)
    gt = gates.astype(F32).reshape(B, T, 2, 2, ML_HEADS).transpose(2, 3, 0, 4, 1)
    ig = gt[:, 0] + i_bias.astype(F32)[:, None, :, None]
    lf = jax.nn.log_sigmoid(gt[:, 1] + f_bias.astype(F32)[:, None, :, None])
    h_f = mlstm_scan(qh, kh, vh, ig[0], lf[0])
    h_b = flip_t(mlstm_scan(flip_t(qh), flip_t(kh), flip_t(vh), jnp.flip(ig[1], -1), jnp.flip(lf[1], -1)))
    h = rms_norm(h_f + h_b, norm_gain) * jax.nn.sigmoid(heads(o))
    return h.transpose(0, 2, 1, 3).reshape(B, T, GROUP_WIDTH).astype(v.dtype)


def neighbourhood_attention(q, k, v, rpb):
    B, T, _ = q.shape
    rows = T // GRID_W
    kr = min(NA_ROWS, rows)

    def grid(t):
        return t.reshape(B, rows, GRID_W, NA_HEADS, NA_DIM).transpose(0, 3, 1, 2, 4)

    qg, kg, vg = grid(q), grid(k), grid(v)
    col = np.arange(GRID_W)
    cstart = np.clip(col - NA_COLS // 2, 0, GRID_W - NA_COLS)
    cidx = cstart[:, None] + np.arange(NA_COLS)
    coff = cidx - col[:, None] + NA_COLS - 1
    rpb_c = rpb.astype(F32)[:, :, coff]
    scale = NA_DIM ** -0.5

    def row_block(r):
        rs = jnp.clip(r - kr // 2, 0, rows - kr)
        kw = lax.dynamic_slice_in_dim(kg, rs, kr, axis=2)[:, :, :, cidx]
        vw = lax.dynamic_slice_in_dim(vg, rs, kr, axis=2)[:, :, :, cidx]
        qr = lax.dynamic_index_in_dim(qg, r, axis=2, keepdims=False)
        s = jnp.einsum('bhcd,bhrcjd->bhcrj', qr, kw).astype(F32) * scale
        roff = rs + jnp.arange(kr) - r + NA_ROWS - 1
        s = s + jnp.take(rpb_c, roff, axis=1).transpose(0, 2, 1, 3)[None]
        p = jax.nn.softmax(s.reshape(B, NA_HEADS, GRID_W, kr * NA_COLS), axis=-1).reshape(s.shape)
        return jnp.einsum('bhcrj,bhrcjd->bhcd', p.astype(vw.dtype), vw)

    o = lax.map(row_block, jnp.arange(rows))
    return o.transpose(1, 0, 3, 2, 4).reshape(B, T, GROUP_WIDTH)


def moe_ffn(x, router, router_bias, exp_gate, exp_up, exp_down, sh_gate, sh_up, sh_down):
    B, T, D = x.shape
    n_tok = B * T
    xf = x.reshape(n_tok, D)
    scores = jax.nn.sigmoid(jnp.dot(xf, router).astype(F32))
    sel = scores + router_bias.astype(F32)
    grp_score = jnp.sum(lax.top_k(sel.reshape(n_tok, N_GROUPS, -1), 2)[0], axis=-1)
    grp_idx = lax.top_k(grp_score, TOPK_GROUPS)[1]
    grp_mask = jnp.any(grp_idx[:, :, None] == jnp.arange(N_GROUPS), axis=1)
    sel = jnp.where(jnp.repeat(grp_mask, N_EXPERTS // N_GROUPS, axis=1), sel, -jnp.inf)
    top_idx = lax.top_k(sel, TOP_K)[1]
    top_w = jnp.take_along_axis(scores, top_idx, axis=1)
    top_w = top_w / jnp.sum(top_w, axis=-1, keepdims=True) * ROUTED_SCALE
    n_asg = n_tok * TOP_K
    e_flat = top_idx.reshape(-1)
    order = jnp.argsort(e_flat)
    e_s = e_flat[order]
    tok_s = (order // TOP_K).astype(jnp.int32)
    w_s = top_w.reshape(-1)[order]
    counts = jnp.bincount(e_flat, length=N_EXPERTS)
    padded = (counts + MOE_BLOCK - 1) // MOE_BLOCK * MOE_BLOCK
    start = jnp.cumsum(counts) - counts
    pend = jnp.cumsum(padded)
    pstart = pend - padded
    pos = pstart[e_s] + jnp.arange(n_asg) - start[e_s]
    n_rows = n_asg + N_EXPERTS * MOE_BLOCK
    n_blk = n_rows // MOE_BLOCK
    row_tok = jnp.zeros((n_rows,), jnp.int32).at[pos].set(tok_s).reshape(n_blk, MOE_BLOCK)
    row_w = jnp.zeros((n_rows,), F32).at[pos].set(w_s).reshape(n_blk, MOE_BLOCK)
    blk_exp = jnp.minimum(jnp.searchsorted(pend, jnp.arange(n_blk) * MOE_BLOCK, side='right'), N_EXPERTS - 1)

    def expert_block(acc, inp):
        e, toks, wts = inp
        xb = xf[toks]
        hb = jax.nn.silu(xb @ exp_gate[e]) * (xb @ exp_up[e])
        yb = ((hb @ exp_down[e]).astype(F32) * wts[:, None]).astype(xf.dtype)
        return acc.at[toks].add(yb), None

    routed, _ = lax.scan(expert_block, jnp.zeros_like(xf), (blk_exp, row_tok, row_w))
    shared = (jax.nn.silu(xf @ sh_gate) * (xf @ sh_up)) @ sh_down
    return (routed + shared).reshape(B, T, D)


def even_layer(x, c, layer_idx, lb, cos, sin, norm1, norm2, w_mod, b_mod, w_in, w_out, hgrn_norm,
               lq1, lk1, lq2, lk2, subln, router, router_bias, exp_gate, exp_up, exp_down,
               sh_gate, sh_up, sh_down):
    sh1, sc1, g1, sh2, sc2, g2 = ada_modulation(c, w_mod, b_mod)
    h = modulate(rms_norm(x, norm1), sh1, sc1)
    aq, ai, af, ab, ag, bq, bk, bv = jnp.split(h @ w_in, 8, axis=-1)
    a_out = hgrn2_mixer(aq, ai, af, ab, ag, lb, hgrn_norm)
    b_out = diff_attention(bq, bk, bv, lq1, lk1, lq2, lk2, subln, layer_idx, cos, sin)
    x = x + g1 * (jnp.concatenate([a_out, b_out], axis=-1) @ w_out)
    h = modulate(rms_norm(x, norm2), sh2, sc2)
    return x + g2 * moe_ffn(h, router, router_bias, exp_gate, exp_up, exp_down, sh_gate, sh_up, sh_down)


def odd_layer(x, c, norm1, norm2, w_mod, b_mod, w_in, w_out, conv_w, conv_b, ml_i_bias, ml_f_bias,
              ml_norm, na_rpb, router, router_bias, exp_gate, exp_up, exp_down, sh_gate, sh_up, sh_down):
    sh1, sc1, g1, sh2, sc2, g2 = ada_modulation(c, w_mod, b_mod)
    h = modulate(rms_norm(x, norm1), sh1, sc1)
    G = GROUP_WIDTH
    splits = [G, 2 * G, 3 * G, 4 * G, 4 * G + 4 * ML_HEADS, 5 * G + 4 * ML_HEADS, 6 * G + 4 * ML_HEADS]
    cq, ck, cv, co, cg, dq, dk, dv = jnp.split(h @ w_in, splits, axis=-1)
    c_out = mlstm_mixer(cq, ck, cv, co, cg, ml_i_bias, ml_f_bias, conv_w, conv_b, ml_norm)
    d_out = neighbourhood_attention(dq, dk, dv, na_rpb)
    x = x + g1 * (jnp.concatenate([c_out, d_out], axis=-1) @ w_out)
    h = modulate(rms_norm(x, norm2), sh2, sc2)
    return x + g2 * moe_ffn(h, router, router_bias, exp_gate, exp_up, exp_down, sh_gate, sh_up, sh_down)


def setup_inputs(seed: int = 0) -> dict:
    key = jax.random.key(seed)
    keys = iter(jax.random.split(key, 64))

    def nrm(shape, scale):
        return jax.random.normal(next(keys), shape, jnp.float32) * scale

    def gain(n):
        return 1.0 + nrm((n,), 0.02)

    D = D_MODEL
    sd = D ** -0.5

    def moe(prefix):
        return {
            prefix + 'router': nrm((D, N_EXPERTS), sd),
            prefix + 'router_bias': nrm((N_EXPERTS,), 0.01),
            prefix + 'exp_gate': nrm((N_EXPERTS, D, EXPERT_FF), sd),
            prefix + 'exp_up': nrm((N_EXPERTS, D, EXPERT_FF), sd),
            prefix + 'exp_down': nrm((N_EXPERTS, EXPERT_FF, D), EXPERT_FF ** -0.5),
            prefix + 'sh_gate': nrm((D, SHARED_FF), sd),
            prefix + 'sh_up': nrm((D, SHARED_FF), sd),
            prefix + 'sh_down': nrm((SHARED_FF, D), SHARED_FF ** -0.5),
        }

    inp = {}
    inp['x'] = nrm((BATCH, SEQ, D), 1.0)
    inp['c'] = nrm((BATCH, D), 1.0)
    inp['hgrn_lb_logits'] = nrm((N_HGRN_LAYERS + 1, 2, GROUP_WIDTH), 0.5)
    inp['l0_norm1'] = gain(D)
    inp['l0_norm2'] = gain(D)
    inp['l0_w_mod'] = nrm((D, 6 * D), 0.5 * sd)
    inp['l0_b_mod'] = nrm((6 * D,), 0.02)
    inp['l0_w_in'] = nrm((D, EVEN_IN), sd)
    inp['l0_w_out'] = nrm((2 * GROUP_WIDTH, D), (2 * GROUP_WIDTH) ** -0.5)
    inp['l0_hgrn_norm'] = gain(HG_DIM)
    inp['l0_diff_lq1'] = nrm((DA_DIM,), 0.1)
    inp['l0_diff_lk1'] = nrm((DA_DIM,), 0.1)
    inp['l0_diff_lq2'] = nrm((DA_DIM,), 0.1)
    inp['l0_diff_lk2'] = nrm((DA_DIM,), 0.1)
    inp['l0_diff_subln'] = gain(2 * DA_DIM)
    inp.update(moe('l0_'))
    inp['l1_norm1'] = gain(D)
    inp['l1_norm2'] = gain(D)
    inp['l1_w_mod'] = nrm((D, 6 * D), 0.5 * sd)
    inp['l1_b_mod'] = nrm((6 * D,), 0.02)
    inp['l1_w_in'] = nrm((D, ODD_IN), sd)
    inp['l1_w_out'] = nrm((2 * GROUP_WIDTH, D), (2 * GROUP_WIDTH) ** -0.5)
    inp['l1_conv_w'] = nrm((ML_CONV, 2 * GROUP_WIDTH), ML_CONV ** -0.5)
    inp['l1_conv_b'] = nrm((2 * GROUP_WIDTH,), 0.02)
    inp['l1_ml_i_bias'] = nrm((2, ML_HEADS), 0.1)
    inp['l1_ml_f_bias'] = jnp.linspace(3.0, 6.0, ML_HEADS, dtype=jnp.float32)[None, :] + nrm((2, ML_HEADS), 0.1)
    inp['l1_ml_norm'] = gain(ML_DIM)
    inp['l1_na_rpb'] = nrm((NA_HEADS, 2 * NA_ROWS - 1, 2 * NA_COLS - 1), 0.02)
    inp.update(moe('l1_'))
    inp['final_norm'] = gain(D)
    return inp


def reference(x, c, hgrn_lb_logits,
              l0_norm1, l0_norm2, l0_w_mod, l0_b_mod, l0_w_in, l0_w_out, l0_hgrn_norm,
              l0_diff_lq1, l0_diff_lk1, l0_diff_lq2, l0_diff_lk2, l0_diff_subln,
              l0_router, l0_router_bias, l0_exp_gate, l0_exp_up, l0_exp_down,
              l0_sh_gate, l0_sh_up, l0_sh_down,
              l1_norm1, l1_norm2, l1_w_mod, l1_b_mod, l1_w_in, l1_w_out,
              l1_conv_w, l1_conv_b, l1_ml_i_bias, l1_ml_f_bias, l1_ml_norm, l1_na_rpb,
              l1_router, l1_router_bias, l1_exp_gate, l1_exp_up, l1_exp_down,
              l1_sh_gate, l1_sh_up, l1_sh_down,
              final_norm):
    T = x.shape[1]
    lb_all = jnp.cumsum(jax.nn.softmax(hgrn_lb_logits.astype(F32), axis=0), axis=0)
    pos = jnp.arange(T, dtype=F32)
    inv_freq = ROPE_THETA ** (-jnp.arange(0, ROPE_DIM, 2, dtype=F32) / ROPE_DIM)
    ang = pos[:, None] * inv_freq[None, :]
    cos, sin = jnp.cos(ang), jnp.sin(ang)
    layer_params = [
        (l0_norm1, l0_norm2, l0_w_mod, l0_b_mod, l0_w_in, l0_w_out, l0_hgrn_norm,
         l0_diff_lq1, l0_diff_lk1, l0_diff_lq2, l0_diff_lk2, l0_diff_subln,
         l0_router, l0_router_bias, l0_exp_gate, l0_exp_up, l0_exp_down, l0_sh_gate, l0_sh_up, l0_sh_down),
        (l1_norm1, l1_norm2, l1_w_mod, l1_b_mod, l1_w_in, l1_w_out,
         l1_conv_w, l1_conv_b, l1_ml_i_bias, l1_ml_f_bias, l1_ml_norm, l1_na_rpb,
         l1_router, l1_router_bias, l1_exp_gate, l1_exp_up, l1_exp_down, l1_sh_gate, l1_sh_up, l1_sh_down),
    ]
    for layer in range(DEPTH):
        if layer % 2 == 0:
            x = even_layer(x, c, layer, lb_all[layer // 2], cos, sin, *layer_params[layer])
        else:
            x = odd_layer(x, c, *layer_params[layer])
    return rms_norm(x, final_norm)
```

```python
import functools
import math

import numpy as np
import jax
import jax.numpy as jnp
from jax import lax
from jax.experimental import pallas as pl
from jax.experimental.pallas import tpu as pltpu

F32 = jnp.float32
BF16 = jnp.bfloat16
HIGHEST = lax.Precision.HIGHEST
EPS = 1e-6

GRID_W = 64
GROUP_WIDTH = 512
HEAD_LANES = 128
HG_CHUNK = 64
ML_CHUNK = 128
ML_CONV = 5
DA_DIM = 64
ROPE_DIM = 16
ROPE_THETA = 500000.0
NA_ROWS = 8
NA_COLS = 16
NA_DIM = 64
N_EXPERTS = 128
TOP_K = 8
N_GROUPS = 8
TOPK_GROUPS = 4
ROUTED_SCALE = 2.5
MOE_ROWS = 128
MOE_TOKEN_GROUP = 2048
NEG_BIG = -1e30
VMEM_LIMIT = 48 * 1024 * 1024


def _dot(a, b, **kw):
    return jnp.dot(a, b, preferred_element_type=F32, **kw)


def _dot_nt(a, b):
    return lax.dot_general(a, b, (((1,), (1,)), ((), ())), preferred_element_type=F32)


def _dot_tn(a, b):
    return lax.dot_general(a, b, (((0,), (0,)), ((), ())), preferred_element_type=F32)


def _sigmoid(x):
    return jax.nn.sigmoid(x)


def _silu(x):
    return x * jax.nn.sigmoid(x)


def _log_sigmoid(x):
    return jnp.minimum(x, 0.0) - jnp.log(1.0 + jnp.exp(-jnp.abs(x)))


def _rms(x, gain):
    return x * lax.rsqrt(jnp.mean(x * x, axis=-1, keepdims=True) + EPS) * gain


def _mod_kernel(c_ref, w_ref, b_ref, o_ref):
    o_ref[...] = _dot(_silu(c_ref[...]), w_ref[...], precision=HIGHEST) + b_ref[...]


def ada_mod(c, w_mod, b_mod):
    B, D = c.shape
    N = w_mod.shape[1]
    tn = 1024
    out = pl.pallas_call(
        _mod_kernel,
        out_shape=jax.ShapeDtypeStruct((B, N), F32),
        grid=(N // tn,),
        in_specs=[pl.BlockSpec((B, D), lambda j: (0, 0)),
                  pl.BlockSpec((D, tn), lambda j: (0, j)),
                  pl.BlockSpec((1, tn), lambda j: (0, j))],
        out_specs=pl.BlockSpec((B, tn), lambda j: (0, j)),
        name="ada_mod",
    )(c, w_mod, b_mod.reshape(1, N))
    return out.reshape(B, 6, D)


def _in_kernel(x_ref, mod_ref, gain_ref, w_ref, *rest, has_gate):
    if has_gate:
        wg_ref, o_ref, og_ref, h_ref = rest
    else:
        o_ref, h_ref = rest

    @pl.when(pl.program_id(2) == 0)
    def _():
        h = _rms(x_ref[0], gain_ref[...]) * (1.0 + mod_ref[0, 1:2, :]) + mod_ref[0, 0:1, :]
        h_ref[...] = h.astype(BF16)
        if has_gate:
            og_ref[0] = _dot(h, wg_ref[...], precision=HIGHEST)

    o_ref[0] = _dot(h_ref[...], w_ref[...])


def in_proj(x, mod, gain, w_bf16, w_gate=None):
    B, T, D = x.shape
    N = w_bf16.shape[1]
    tm = min(512, T)
    tn = 512
    has_gate = w_gate is not None
    in_specs = [pl.BlockSpec((1, tm, D), lambda b, i, j: (b, i, 0)),
                pl.BlockSpec((1, 6, D), lambda b, i, j: (b, 0, 0)),
                pl.BlockSpec((1, D), lambda b, i, j: (0, 0)),
                pl.BlockSpec((D, tn), lambda b, i, j: (0, j))]
    out_shape = [jax.ShapeDtypeStruct((B, T, N), F32)]
    out_specs = [pl.BlockSpec((1, tm, tn), lambda b, i, j: (b, i, j))]
    args = [x, mod, gain.reshape(1, D), w_bf16]
    if has_gate:
        in_specs.append(pl.BlockSpec((D, HEAD_LANES), lambda b, i, j: (0, 0)))
        out_shape.append(jax.ShapeDtypeStruct((B, T, HEAD_LANES), F32))
        out_specs.append(pl.BlockSpec((1, tm, HEAD_LANES), lambda b, i, j: (b, i, 0)))
        args.append(w_gate)
    res = pl.pallas_call(
        functools.partial(_in_kernel, has_gate=has_gate),
        out_shape=out_shape,
        grid=(B, T // tm, N // tn),
        in_specs=in_specs,
        out_specs=out_specs,
        scratch_shapes=[pltpu.VMEM((tm, D), BF16)],
        compiler_params=pltpu.CompilerParams(
            dimension_semantics=("arbitrary", "arbitrary", "arbitrary")),
        name="in_proj",
    )(*args)
    return res if has_gate else res[0]


def _hgrn_consts(C):
    t = np.arange(C)
    tri = (t[None, :] <= t[:, None]).astype(np.float32)
    triT = np.ascontiguousarray(tri.T)
    wf, wb, mf = [tri], [triT], []
    levels = int(round(math.log2(C)))
    for l in range(levels):
        size = C >> l
        blk = t // size
        r = blk * size + size // 2
        wf.append(tri - tri[r - 1])
        wb.append(triT - triT[r])
        upper = (t % size) >= size // 2
        mf.append(((blk[:, None] == blk[None, :]) & upper[:, None] & (~upper)[None, :]).astype(np.float32))
    mf.append(np.eye(C, dtype=np.float32))
    ones = np.ones((8, C), np.float32)
    wf.append(ones)
    wb.append(ones)
    mf = np.stack(mf)
    mb = np.ascontiguousarray(np.transpose(mf, (0, 2, 1)))
    return np.concatenate(wf), np.concatenate(wb), mf, mb


def _hgrn_kernel(q_ref, i_ref, ff_ref, fb_ref, g_ref, lb_ref, gain_ref, wf_ref, wb_ref, mf_ref, mb_ref,
                 o_ref, *, C, T):
    n = T // C
    levels = int(round(math.log2(C)))
    dv = q_ref.shape[-1]

    def chunk(c, st, f_ref, lbd, w_ref, m_ref):
        sl = pl.ds(pl.multiple_of(c * C, C), C)
        q = _silu(q_ref[0, sl, :])
        v = i_ref[0, sl, :]
        fg = lbd + (1.0 - lbd) * _sigmoid(f_ref[0, sl, :])
        k = 1.0 - fg
        lf = jnp.log(fg)
        dall = _dot(w_ref[...], lf, precision=HIGHEST)
        cum = dall[0:C]
        tot = dall[(levels + 1) * C:(levels + 1) * C + 1]
        attn = m_ref[levels] * _dot_nt(q.astype(BF16), k.astype(BF16))
        for l in range(levels):
            e = jnp.exp(-jnp.abs(dall[(l + 1) * C:(l + 2) * C]))
            attn = attn + m_ref[l] * _dot_nt((q * e).astype(BF16), (k * e).astype(BF16))
        o = _dot(attn.astype(BF16), v.astype(BF16))
        o = o + _dot_nt((q * jnp.exp(cum)).astype(BF16), st.astype(BF16))
        kt = k * jnp.exp(tot - cum)
        st = st * jnp.exp(tot) + _dot_tn(v.astype(BF16), kt.astype(BF16))
        return sl, o, st

    st0 = jnp.zeros((dv, dv), F32)

    def fwd(c, st):
        sl, o, st = chunk(c, st, ff_ref, lb_ref[0:1, :], wf_ref, mf_ref)
        o_ref[0, sl, :] = o
        return st

    lax.fori_loop(0, n, fwd, st0)

    def bwd(j, st):
        c = n - 1 - j
        sl, o, st = chunk(c, st, fb_ref, lb_ref[1:2, :], wb_ref, mb_ref)
        tot = o_ref[0, sl, :] + o
        o_ref[0, sl, :] = _rms(tot, gain_ref[...]) * _silu(g_ref[0, sl, :])
        return st

    lax.fori_loop(0, n, bwd, st0)


def hgrn2(y, lb, norm_gain):
    B, T, _ = y.shape
    H = GROUP_WIDTH // HEAD_LANES
    C = HG_CHUNK
    wf, wb, mf, mb = (jnp.asarray(a) for a in _hgrn_consts(C))

    def col(group):
        return pl.BlockSpec((1, T, HEAD_LANES), lambda b, h, group=group: (b, 0, group * H + h))

    def const(a):
        nd = a.ndim
        return pl.BlockSpec(a.shape, lambda b, h, nd=nd: (0,) * nd)

    return pl.pallas_call(
        functools.partial(_hgrn_kernel, C=C, T=T),
        out_shape=jax.ShapeDtypeStruct((B, T, GROUP_WIDTH), F32),
        grid=(B, H),
        in_specs=[col(0), col(1), col(2), col(3), col(4),
                  pl.BlockSpec((2, HEAD_LANES), lambda b, h: (0, h)),
                  pl.BlockSpec((1, HEAD_LANES), lambda b, h: (0, 0)),
                  const(wf), const(wb), const(mf), const(mb)],
        out_specs=pl.BlockSpec((1, T, HEAD_LANES), lambda b, h: (b, 0, h)),
        compiler_params=pltpu.CompilerParams(
            dimension_semantics=("arbitrary", "arbitrary"), vmem_limit_bytes=VMEM_LIMIT),
        name="hgrn2",
    )(y, y, y, y, y, lb, norm_gain.reshape(1, HEAD_LANES), wf, wb, mf, mb)


def _rope_tables(T):
    pos = np.arange(T, dtype=np.float32)
    inv_freq = (ROPE_THETA ** (-np.arange(0, ROPE_DIM, 2, dtype=np.float32) / ROPE_DIM)).astype(np.float32)
    ang = pos[:, None] * inv_freq[None, :]
    cos, sin = np.cos(ang), np.sin(ang)
    half = ROPE_DIM // 2
    c = np.ones((T, HEAD_LANES), np.float32)
    s_prev = np.zeros((T, HEAD_LANES), np.float32)
    s_next = np.zeros((T, HEAD_LANES), np.float32)
    for base in range(0, HEAD_LANES, DA_DIM):
        c[:, base:base + half] = cos
        c[:, base + half:base + ROPE_DIM] = cos
        s_next[:, base:base + half] = -sin
        s_prev[:, base + half:base + ROPE_DIM] = sin
    return jnp.asarray(c), jnp.asarray(s_prev), jnp.asarray(s_next)


def _rope(x, c, s_prev, s_next):
    half = ROPE_DIM // 2
    lanes = x.shape[-1]
    return (x * c + pltpu.roll(x, half, axis=1) * s_prev
            + pltpu.roll(x, lanes - half, axis=1) * s_next)


def _diff_kernel(lam_ref, q_ref, k_ref, v_ref, cq_ref, spq_ref, snq_ref, ck_ref, spk_ref, snk_ref,
                 subln_ref, o_ref, kr_ref, vb_ref, *, T, out_scale):
    rt = min(512, T)

    @pl.when(pl.program_id(2) == 0)
    def _():
        for r0 in range(0, T, rt):
            sl = slice(r0, r0 + rt)
            kr_ref[sl, :] = _rope(k_ref[0, sl, :], ck_ref[sl, :], spk_ref[sl, :], snk_ref[sl, :]).astype(BF16)
            vb_ref[sl, :] = v_ref[0, sl, :].astype(BF16)

    q = (_rope(q_ref[0], cq_ref[...], spq_ref[...], snq_ref[...]) * (DA_DIM ** -0.5)).astype(BF16)
    lam = lam_ref[0, 0]
    v = vb_ref[...]

    def softmax_pv(m):
        sl = slice(m * DA_DIM, (m + 1) * DA_DIM)
        s = _dot_nt(q[:, sl], kr_ref[:, sl])
        e = jnp.exp(s - jnp.max(s, axis=-1, keepdims=True))
        return _dot(e.astype(BF16), v) / jnp.sum(e, axis=-1, keepdims=True)

    o = softmax_pv(0) - lam * softmax_pv(1)
    o_ref[0] = _rms(o, subln_ref[...]) * out_scale


def diff_attention(y, lam, subln, layer_idx):
    B, T, _ = y.shape
    H = GROUP_WIDTH // HEAD_LANES
    tq = min(256, T)
    lambda_init = 0.8 - 0.6 * math.exp(-0.3 * layer_idx)
    c, sp, sn = _rope_tables(T)

    def col(group, rows):
        if rows == T:
            return pl.BlockSpec((1, T, HEAD_LANES), lambda b, h, i, group=group: (b, 0, group * H + h))
        return pl.BlockSpec((1, rows, HEAD_LANES), lambda b, h, i, group=group: (b, i, group * H + h))

    tab_q = pl.BlockSpec((tq, HEAD_LANES), lambda b, h, i: (i, 0))
    tab_k = pl.BlockSpec((T, HEAD_LANES), lambda b, h, i: (0, 0))
    return pl.pallas_call(
        functools.partial(_diff_kernel, T=T, out_scale=1.0 - lambda_init),
        out_shape=jax.ShapeDtypeStruct((B, T, GROUP_WIDTH), F32),
        grid=(B, H, T // tq),
        in_specs=[pl.BlockSpec(memory_space=pltpu.SMEM),
                  col(5, tq), col(6, T), col(7, T),
                  tab_q, tab_q, tab_q, tab_k, tab_k, tab_k,
                  pl.BlockSpec((1, HEAD_LANES), lambda b, h, i: (0, 0))],
        out_specs=pl.BlockSpec((1, tq, HEAD_LANES), lambda b, h, i: (b, i, h)),
        scratch_shapes=[pltpu.VMEM((T, HEAD_LANES), BF16), pltpu.VMEM((T, HEAD_LANES), BF16)],
        compiler_params=pltpu.CompilerParams(
            dimension_semantics=("arbitrary", "arbitrary", "arbitrary"), vmem_limit_bytes=VMEM_LIMIT),
        name="diff_attention",
    )(lam.reshape(1, 1), y, y, y, c, sp, sn, c, sp, sn, subln.reshape(1, HEAD_LANES))


def _mlstm_kernel(q_ref, k_ref, v_ref, og_ref, gt_ref, cwq_ref, cwk_ref, cbq_ref, cbk_ref, gbias_ref,
                  gain_ref, tri_ref, o_ref, qc_ref, kc_ref, xp_ref, gx_ref, *, C, T, dk):
    n = T // C
    head = pl.program_id(1)
    pad = 8
    half = ML_CONV // 2
    rt = min(512, T)

    xp_ref[0:pad, :] = jnp.zeros((pad, dk), F32)
    xp_ref[pad + T:pad + T + pad, :] = jnp.zeros((pad, dk), F32)
    for src, cw_ref, cb_ref, dst, scale in ((q_ref, cwq_ref, cbq_ref, qc_ref, 1.0),
                                            (k_ref, cwk_ref, cbk_ref, kc_ref, dk ** -0.5)):
        xp_ref[pad:pad + T, :] = src[0]
        for r0 in range(0, T, rt):
            acc = jnp.zeros((rt, dk), F32) + cb_ref[...]
            for j in range(ML_CONV):
                acc = acc + xp_ref[pad + r0 + j - half:pad + r0 + j - half + rt, :] * cw_ref[j:j + 1, :]
            dst[r0:r0 + rt, :] = _silu(acc) * scale

    lane = lax.broadcasted_iota(jnp.int32, (rt, HEAD_LANES), 1)
    is_f = (lane % 8) >= 4
    for r0 in range(0, T, rt):
        g = gt_ref[0, r0:r0 + rt, :] + gbias_ref[...]
        p = jnp.where(is_f, _log_sigmoid(g), g)
        x = jnp.zeros((rt, HEAD_LANES), F32)
        for j, src_lane in enumerate((0, 4, 8, 12)):
            colv = jnp.sum(jnp.where(lane == src_lane + head, p, 0.0), axis=1, keepdims=True)
            x = jnp.where(lane == j, colv, x)
        gx_ref[r0:r0 + rt, :] = x

    row = lax.broadcasted_iota(jnp.int32, (C, C), 0)
    colm = lax.broadcasted_iota(jnp.int32, (C, C), 1)

    def chunk(c, carry, d):
        s_state, nvec, m = carry
        sl = pl.ds(pl.multiple_of(c * C, C), C)
        q = qc_ref[sl, :]
        k = kc_ref[sl, :]
        v = v_ref[0, sl, :]
        x = gx_ref[sl, :]
        tri = tri_ref[d]
        mask = (colm <= row) if d == 0 else (colm >= row)
        cumx = _dot(tri, x, precision=HIGHEST)
        xt = x.T
        cumxt = cumx.T
        ig_c = x[:, 2 * d:2 * d + 1]
        ig_r = xt[2 * d:2 * d + 1, :]
        cum_c = cumx[:, 2 * d + 1:2 * d + 2]
        cum_r = cumxt[2 * d + 1:2 * d + 2, :]
        tot = jnp.sum(x[:, 2 * d + 1:2 * d + 2], axis=0, keepdims=True)
        dmat = jnp.where(mask, cum_c - cum_r + ig_r, -jnp.inf)
        g = cum_c + m
        mt = jnp.maximum(g, jnp.max(dmat, axis=1, keepdims=True))
        a = _dot_nt(q.astype(BF16), k.astype(BF16)) * jnp.exp(dmat - mt)
        inter = jnp.exp(g - mt)
        num = inter * _dot(q.astype(BF16), s_state.astype(BF16)) + _dot(a.astype(BF16), v.astype(BF16))
        den = inter * jnp.sum(q * nvec, axis=1, keepdims=True) + jnp.sum(a, axis=1, keepdims=True)
        h = num / jnp.maximum(jnp.abs(den), jnp.exp(-mt))
        ds = tot - cum_c + ig_c
        m_new = jnp.maximum(tot + m, jnp.max(ds, axis=0, keepdims=True))
        decay = jnp.exp(tot + m - m_new)
        kw = k * jnp.exp(ds - m_new)
        s_state = decay * s_state + _dot_tn(kw.astype(BF16), v.astype(BF16))
        nvec = decay * nvec + jnp.sum(kw, axis=0, keepdims=True)
        return sl, h, (s_state, nvec, m_new)

    init = (jnp.zeros((dk, dk), F32), jnp.zeros((1, dk), F32), jnp.full((1, 1), NEG_BIG, F32))

    def fwd(c, carry):
        sl, h, carry = chunk(c, carry, 0)
        o_ref[0, sl, :] = h
        return carry

    lax.fori_loop(0, n, fwd, init)

    def bwd(j, carry):
        sl, h, carry = chunk(n - 1 - j, carry, 1)
        o_ref[0, sl, :] = _rms(o_ref[0, sl, :] + h, gain_ref[...]) * _sigmoid(og_ref[0, sl, :])
        return carry

    lax.fori_loop(0, n, bwd, init)


def mlstm(y, gates, conv_w, conv_b, i_bias, f_bias, norm_gain):
    B, T, _ = y.shape
    H = GROUP_WIDTH // HEAD_LANES
    C = min(ML_CHUNK, T)
    t = np.arange(C)
    tri = np.stack([(t[None, :] <= t[:, None]), (t[None, :] >= t[:, None])]).astype(np.float32)
    gbias = jnp.zeros((1, HEAD_LANES), F32)
    gbias = gbias.at[0, 0:4].set(i_bias[0]).at[0, 4:8].set(f_bias[0])
    gbias = gbias.at[0, 8:12].set(i_bias[1]).at[0, 12:16].set(f_bias[1])

    def col(group):
        return pl.BlockSpec((1, T, HEAD_LANES), lambda b, h, group=group: (b, 0, group * H + h))

    conv_q = pl.BlockSpec((ML_CONV, HEAD_LANES), lambda b, h: (0, h))
    conv_k = pl.BlockSpec((ML_CONV, HEAD_LANES), lambda b, h: (0, H + h))
    bias_q = pl.BlockSpec((1, HEAD_LANES), lambda b, h: (0, h))
    bias_k = pl.BlockSpec((1, HEAD_LANES), lambda b, h: (0, H + h))
    cb = conv_b.reshape(1, -1)
    return pl.pallas_call(
        functools.partial(_mlstm_kernel, C=C, T=T, dk=HEAD_LANES),
        out_shape=jax.ShapeDtypeStruct((B, T, GROUP_WIDTH), F32),
        grid=(B, H),
        in_specs=[col(0), col(1), col(2), col(3),
                  pl.BlockSpec((1, T, HEAD_LANES), lambda b, h: (b, 0, 0)),
                  conv_q, conv_k, bias_q, bias_k,
                  pl.BlockSpec((1, HEAD_LANES), lambda b, h: (0, 0)),
                  pl.BlockSpec((1, HEAD_LANES), lambda b, h: (0, 0)),
                  pl.BlockSpec((2, C, C), lambda b, h: (0, 0, 0))],
        out_specs=pl.BlockSpec((1, T, HEAD_LANES), lambda b, h: (b, 0, h)),
        scratch_shapes=[pltpu.VMEM((T, HEAD_LANES), F32), pltpu.VMEM((T, HEAD_LANES), F32),
                        pltpu.VMEM((T + 16, HEAD_LANES), F32), pltpu.VMEM((T, HEAD_LANES), F32)],
        compiler_params=pltpu.CompilerParams(
            dimension_semantics=("arbitrary", "arbitrary"), vmem_limit_bytes=VMEM_LIMIT),
        name="mlstm",
    )(y, y, y, y, gates, conv_w, conv_w, cb, cb, gbias, norm_gain.reshape(1, HEAD_LANES), jnp.asarray(tri))


def _na_bias_table(rpb, rows):
    kr = min(NA_ROWS, rows)
    c = np.arange(GRID_W)
    cstart = np.clip(c - NA_COLS // 2, 0, GRID_W - NA_COLS)
    kc = np.arange(GRID_W)
    valid = (kc[None, :] >= cstart[:, None]) & (kc[None, :] < cstart[:, None] + NA_COLS)
    coff = np.clip(kc[None, :] - c[:, None] + NA_COLS - 1, 0, 2 * NA_COLS - 2)
    di = np.arange(kr)
    i = np.arange(kr)
    roff = i[None, :] - di[:, None] + NA_ROWS - 1
    tab = rpb.astype(F32)[:, roff[:, None, :, None], coff[None, :, None, :]]
    tab = jnp.where(jnp.asarray(valid)[None, None, :, None, :], tab, NEG_BIG)
    return tab.reshape(rpb.shape[0], kr, GRID_W, kr * GRID_W)


def _na_kernel(q_ref, k_ref, v_ref, bm_ref, o_ref, *, rows, kr):
    W = GRID_W
    heads_per_block = HEAD_LANES // NA_DIM

    def body(r, _):
        rs = jnp.clip(r - kr // 2, 0, rows - kr)
        di = r - rs
        qs = pl.ds(pl.multiple_of(r * W, W), W)
        ks = pl.ds(pl.multiple_of(rs * W, W), kr * W)
        q = (q_ref[0, qs, :] * (NA_DIM ** -0.5)).astype(BF16)
        kw = k_ref[0, ks, :].astype(BF16)
        vw = v_ref[0, ks, :].astype(BF16)
        outs = []
        for hh in range(heads_per_block):
            sl = slice(hh * NA_DIM, (hh + 1) * NA_DIM)
            s = _dot_nt(q[:, sl], kw[:, sl]) + bm_ref[hh, di]
            e = jnp.exp(s - jnp.max(s, axis=-1, keepdims=True))
            outs.append(_dot(e.astype(BF16), vw[:, sl]) / jnp.sum(e, axis=-1, keepdims=True))
        o_ref[0, qs, :] = jnp.concatenate(outs, axis=1)
        return 0

    lax.fori_loop(0, rows, body, 0)


def neighbourhood_attention(y, rpb):
    B, T, _ = y.shape
    rows = T // GRID_W
    kr = min(NA_ROWS, rows)
    HB = GROUP_WIDTH // HEAD_LANES
    hpb = HEAD_LANES // NA_DIM
    bm = _na_bias_table(rpb, rows)

    def col(group):
        return pl.BlockSpec((1, T, HEAD_LANES), lambda b, h, group=group: (b, 0, group * HB + h))

    return pl.pallas_call(
        functools.partial(_na_kernel, rows=rows, kr=kr),
        out_shape=jax.ShapeDtypeStruct((B, T, GROUP_WIDTH), F32),
        grid=(B, HB),
        in_specs=[col(4), col(5), col(6),
                  pl.BlockSpec((hpb, kr, GRID_W, kr * GRID_W), lambda b, h: (h, 0, 0, 0))],
        out_specs=pl.BlockSpec((1, T, HEAD_LANES), lambda b, h: (b, 0, h)),
        compiler_params=pltpu.CompilerParams(
            dimension_semantics=("arbitrary", "arbitrary"), vmem_limit_bytes=VMEM_LIMIT),
        name="neighbourhood_attention",
    )(y, y, y, bm)


def _out_kernel(a_ref, b_ref, x_ref, mod_ref, gain_ref, w_ref, r_ref, x1_ref, h2_ref, lg_ref):
    G = a_ref.shape[-1]
    y = _dot(a_ref[0].astype(BF16), w_ref[0:G, :]) + _dot(b_ref[0].astype(BF16), w_ref[G:2 * G, :])
    x1 = x_ref[0] + mod_ref[0, 2:3, :] * y
    x1_ref[0] = x1
    h2 = _rms(x1, gain_ref[...]) * (1.0 + mod_ref[0, 4:5, :]) + mod_ref[0, 3:4, :]
    h2_ref[0] = h2
    lg_ref[0] = _dot(h2, r_ref[...], precision=HIGHEST)


def out_proj(a_out, b_out, x, mod, gain2, w_out_bf16, router):
    B, T, D = x.shape
    G = a_out.shape[-1]
    E = router.shape[1]
    tm = min(512, T)
    return pl.pallas_call(
        _out_kernel,
        out_shape=[jax.ShapeDtypeStruct((B, T, D), F32), jax.ShapeDtypeStruct((B, T, D), F32),
                   jax.ShapeDtypeStruct((B, T, E), F32)],
        grid=(B, T // tm),
        in_specs=[pl.BlockSpec((1, tm, G), lambda b, i: (b, i, 0)),
                  pl.BlockSpec((1, tm, G), lambda b, i: (b, i, 0)),
                  pl.BlockSpec((1, tm, D), lambda b, i: (b, i, 0)),
                  pl.BlockSpec((1, 6, D), lambda b, i: (b, 0, 0)),
                  pl.BlockSpec((1, D), lambda b, i: (0, 0)),
                  pl.BlockSpec((2 * G, D), lambda b, i: (0, 0)),
                  pl.BlockSpec((D, E), lambda b, i: (0, 0))],
        out_specs=[pl.BlockSpec((1, tm, D), lambda b, i: (b, i, 0)),
                   pl.BlockSpec((1, tm, D), lambda b, i: (b, i, 0)),
                   pl.BlockSpec((1, tm, E), lambda b, i: (b, i, 0))],
        compiler_params=pltpu.CompilerParams(
            dimension_semantics=("arbitrary", "arbitrary"), vmem_limit_bytes=VMEM_LIMIT),
        name="out_proj",
    )(a_out, b_out, x, mod, gain2.reshape(1, D), w_out_bf16, router)


def _route_kernel(lg_ref, bias_ref, idx_ref, w_ref):
    scores = _sigmoid(lg_ref[...])
    sel = scores + bias_ref[...]
    tm, E = sel.shape
    lane = lax.broadcasted_iota(jnp.int32, (tm, E), 1)
    per_group = E // N_GROUPS
    grp = lane // per_group
    neg = -jnp.inf

    def first_argmax(x):
        m = jnp.max(x, axis=-1, keepdims=True)
        i = jnp.min(jnp.where(x == m, lane, E), axis=-1, keepdims=True)
        return m, i

    gscore = []
    for g in range(N_GROUPS):
        x = jnp.where(grp == g, sel, neg)
        m1, i1 = first_argmax(x)
        m2 = jnp.max(jnp.where(lane == i1, neg, x), axis=-1, keepdims=True)
        gscore.append(m1 + m2)
    keep = jnp.zeros((tm, E), jnp.bool_)
    for g in range(N_GROUPS):
        beaten = jnp.zeros((tm, 1), jnp.int32)
        for o in range(N_GROUPS):
            if o == g:
                continue
            wins = (gscore[o] > gscore[g]) | ((gscore[o] == gscore[g]) & (o < g))
            beaten = beaten + wins.astype(jnp.int32)
        keep = keep | ((grp == g) & (beaten < TOPK_GROUPS))
    sel = jnp.where(keep, sel, neg)

    idx_out = jnp.zeros((tm, E), jnp.int32)
    w_out = jnp.zeros((tm, E), F32)
    w_sum = jnp.zeros((tm, 1), F32)
    for k in range(TOP_K):
        _, i = first_argmax(sel)
        hit = lane == i
        val = jnp.sum(jnp.where(hit, scores, 0.0), axis=-1, keepdims=True)
        sel = jnp.where(hit, neg, sel)
        idx_out = jnp.where(lane == k, i, idx_out)
        w_out = jnp.where(lane == k, val, w_out)
        w_sum = w_sum + val
    idx_ref[...] = idx_out
    w_ref[...] = w_out / w_sum * ROUTED_SCALE


def route(logits, router_bias):
    N, E = logits.shape
    tm = min(256, N)
    idx, w = pl.pallas_call(
        _route_kernel,
        out_shape=[jax.ShapeDtypeStruct((N, E), jnp.int32), jax.ShapeDtypeStruct((N, E), F32)],
        grid=(N // tm,),
        in_specs=[pl.BlockSpec((tm, E), lambda i: (i, 0)), pl.BlockSpec((1, E), lambda i: (0, 0))],
        out_specs=[pl.BlockSpec((tm, E), lambda i: (i, 0)), pl.BlockSpec((tm, E), lambda i: (i, 0))],
        compiler_params=pltpu.CompilerParams(dimension_semantics=("arbitrary",)),
        name="route",
    )(logits, router_bias.reshape(1, E))
    return idx[:, :TOP_K], w[:, :TOP_K]


def _moe_kernel(off_ref, tok_ref, wl_ref, x_ref, wg_ref, wu_ref, wd_ref, acc_ref, xg_ref, yb_ref,
                *, R, E, per_group):
    g = pl.program_id(0)
    e = pl.program_id(1)

    @pl.when(e == 0)
    def _():
        acc_ref[...] = jnp.zeros(acc_ref.shape, F32)

    @pl.when((g == 0) & (e == 0))
    def _():
        xg_ref[...] = jnp.zeros(xg_ref.shape, F32)

    seg = off_ref[g * E + e]
    cnt = off_ref[g * E + e + 1] - seg
    start = seg - g * per_group

    def sub(sb, _):
        s0 = start + sb * R
        nr = jnp.minimum(R, cnt - sb * R)

        def gather(r, _):
            t = tok_ref[s0 + r]
            xg_ref[pl.ds(r, 1), :] = x_ref[0, pl.ds(t, 1), :]
            return 0

        lax.fori_loop(0, nr, gather, 0)
        xb = xg_ref[...].astype(BF16)
        hmid = _silu(_dot(xb, wg_ref[0])) * _dot(xb, wu_ref[0])
        yb_ref[...] = _dot(hmid.astype(BF16), wd_ref[0])

        def scatter(r, _):
            t = tok_ref[s0 + r]
            w = wl_ref[s0 + r]
            acc_ref[0, pl.ds(t, 1), :] = acc_ref[0, pl.ds(t, 1), :] + w * yb_ref[pl.ds(r, 1), :]
            return 0

        lax.fori_loop(0, nr, scatter, 0)
        return 0

    lax.fori_loop(0, (cnt + R - 1) // R, sub, 0)


def routed_experts(h2, top_idx, top_w, exp_gate, exp_up, exp_down):
    N, D = h2.shape
    E, _, F = exp_gate.shape
    TG = min(MOE_TOKEN_GROUP, N)
    G = N // TG
    per_group = TG * TOP_K
    key = (jnp.arange(N, dtype=jnp.int32)[:, None] // TG) * E + top_idx.astype(jnp.int32)
    key = key.reshape(-1)
    order = jnp.argsort(key)
    key_s = key[order]
    tok_s = ((order // TOP_K) % TG).astype(jnp.int32)
    w_s = top_w.reshape(-1)[order]
    off = jnp.searchsorted(key_s, jnp.arange(G * E + 1, dtype=jnp.int32), side="left").astype(jnp.int32)

    grid_spec = pltpu.PrefetchScalarGridSpec(
        num_scalar_prefetch=1,
        grid=(G, E),
        in_specs=[pl.BlockSpec((per_group,), lambda g, e, off: (g,), memory_space=pltpu.SMEM),
                  pl.BlockSpec((per_group,), lambda g, e, off: (g,), memory_space=pltpu.SMEM),
                  pl.BlockSpec((1, TG, D), lambda g, e, off: (g, 0, 0)),
                  pl.BlockSpec((1, D, F), lambda g, e, off: (e, 0, 0)),
                  pl.BlockSpec((1, D, F), lambda g, e, off: (e, 0, 0)),
                  pl.BlockSpec((1, F, D), lambda g, e, off: (e, 0, 0))],
        out_specs=pl.BlockSpec((1, TG, D), lambda g, e, off: (g, 0, 0)),
        scratch_shapes=[pltpu.VMEM((MOE_ROWS, D), F32), pltpu.VMEM((MOE_ROWS, D), F32)],
    )
    out = pl.pallas_call(
        functools.partial(_moe_kernel, R=MOE_ROWS, E=E, per_group=per_group),
        out_shape=jax.ShapeDtypeStruct((G, TG, D), F32),
        grid_spec=grid_spec,
        compiler_params=pltpu.CompilerParams(
            dimension_semantics=("arbitrary", "arbitrary"), vmem_limit_bytes=VMEM_LIMIT),
        name="routed_experts",
    )(off, tok_s, w_s, h2.reshape(G, TG, D), exp_gate, exp_up, exp_down)
    return out.reshape(N, D)


def _final_kernel(x1_ref, h2_ref, rt_ref, mod_ref, wg_ref, wu_ref, wd_ref, *rest, final):
    if final:
        fg_ref, o_ref = rest
    else:
        (o_ref,) = rest
    hb = h2_ref[0].astype(BF16)
    hmid = _silu(_dot(hb, wg_ref[...])) * _dot(hb, wu_ref[...])
    shared = _dot(hmid.astype(BF16), wd_ref[...])
    x2 = x1_ref[0] + mod_ref[0, 5:6, :] * (rt_ref[0] + shared)
    if final:
        x2 = _rms(x2, fg_ref[...])
    o_ref[0] = x2


def shared_and_residual(x1, h2, routed, mod, sh_gate, sh_up, sh_down, final_gain=None):
    B, T, D = x1.shape
    F = sh_gate.shape[1]
    tm = min(512, T)
    final = final_gain is not None
    tile = pl.BlockSpec((1, tm, D), lambda b, i: (b, i, 0))
    in_specs = [tile, tile, tile,
                pl.BlockSpec((1, 6, D), lambda b, i: (b, 0, 0)),
                pl.BlockSpec((D, F), lambda b, i: (0, 0)),
                pl.BlockSpec((D, F), lambda b, i: (0, 0)),
                pl.BlockSpec((F, D), lambda b, i: (0, 0))]
    args = [x1, h2, routed, mod, sh_gate, sh_up, sh_down]
    if final:
        in_specs.append(pl.BlockSpec((1, D), lambda b, i: (0, 0)))
        args.append(final_gain.reshape(1, D))
    return pl.pallas_call(
        functools.partial(_final_kernel, final=final),
        out_shape=jax.ShapeDtypeStruct((B, T, D), F32),
        grid=(B, T // tm),
        in_specs=in_specs,
        out_specs=tile,
        compiler_params=pltpu.CompilerParams(
            dimension_semantics=("arbitrary", "arbitrary"), vmem_limit_bytes=VMEM_LIMIT),
        name="shared_and_residual",
    )(*args)


def moe_block(x1, h2, logits, mod, router_bias, exp_gate, exp_up, exp_down, sh_gate, sh_up, sh_down,
              final_gain=None):
    B, T, D = x1.shape
    top_idx, top_w = route(logits.reshape(B * T, -1), router_bias)
    routed = routed_experts(h2.reshape(B * T, D), top_idx, top_w,
                            exp_gate.astype(BF16), exp_up.astype(BF16), exp_down.astype(BF16))
    return shared_and_residual(x1, h2, routed.reshape(B, T, D), mod,
                               sh_gate.astype(BF16), sh_up.astype(BF16), sh_down.astype(BF16), final_gain)


def kernel(x, c, hgrn_lb_logits, l0_norm1, l0_norm2, l0_w_mod, l0_b_mod, l0_w_in, l0_w_out, l0_hgrn_norm, l0_diff_lq1, l0_diff_lk1, l0_diff_lq2, l0_diff_lk2, l0_diff_subln, l0_router, l0_router_bias, l0_exp_gate, l0_exp_up, l0_exp_down, l0_sh_gate, l0_sh_up, l0_sh_down, l1_norm1, l1_norm2, l1_w_mod, l1_b_mod, l1_w_in, l1_w_out, l1_conv_w, l1_conv_b, l1_ml_i_bias, l1_ml_f_bias, l1_ml_norm, l1_na_rpb, l1_router, l1_router_bias, l1_exp_gate, l1_exp_up, l1_exp_down, l1_sh_gate, l1_sh_up, l1_sh_down, final_norm):
    G = GROUP_WIDTH
    lb_all = jnp.cumsum(jax.nn.softmax(hgrn_lb_logits.astype(F32), axis=0), axis=0)
    layer_idx = 0
    lambda_init = 0.8 - 0.6 * math.exp(-0.3 * layer_idx)
    lam = (jnp.exp(jnp.sum(l0_diff_lq1.astype(F32) * l0_diff_lk1.astype(F32)))
           - jnp.exp(jnp.sum(l0_diff_lq2.astype(F32) * l0_diff_lk2.astype(F32))) + lambda_init)

    mod0 = ada_mod(c, l0_w_mod, l0_b_mod)
    y0 = in_proj(x, mod0, l0_norm1, l0_w_in.astype(BF16))
    a_out = hgrn2(y0, lb_all[0], l0_hgrn_norm)
    b_out = diff_attention(y0, lam, l0_diff_subln, layer_idx)
    x1, h2, logits = out_proj(a_out, b_out, x, mod0, l0_norm2, l0_w_out.astype(BF16), l0_router)
    xa = moe_block(x1, h2, logits, mod0, l0_router_bias, l0_exp_gate, l0_exp_up, l0_exp_down,
                   l0_sh_gate, l0_sh_up, l0_sh_down)

    mod1 = ada_mod(c, l1_w_mod, l1_b_mod)
    n_gate = l1_w_in.shape[1] - 7 * G
    w_main = jnp.concatenate([l1_w_in[:, :4 * G], l1_w_in[:, 4 * G + n_gate:]], axis=1).astype(BF16)
    w_gate = jnp.pad(l1_w_in[:, 4 * G:4 * G + n_gate], ((0, 0), (0, HEAD_LANES - n_gate)))
    y1, gates = in_proj(xa, mod1, l1_norm1, w_main, w_gate)
    c_out = mlstm(y1, gates, l1_conv_w, l1_conv_b, l1_ml_i_bias, l1_ml_f_bias, l1_ml_norm)
    d_out = neighbourhood_attention(y1, l1_na_rpb)
    x1, h2, logits = out_proj(c_out, d_out, xa, mod1, l1_norm2, l1_w_out.astype(BF16), l1_router)
    return moe_block(x1, h2, logits, mod1, l1_router_bias, l1_exp_gate, l1_exp_up, l1_exp_down,
                     l1_sh_gate, l1_sh_up, l1_sh_down, final_gain=final_norm)
```

```python
import functools
import math

import numpy as np
import jax
import jax.numpy as jnp
from jax import lax
from jax.experimental import pallas as pl
from jax.experimental.pallas import tpu as pltpu

F32 = jnp.float32
BF16 = jnp.bfloat16
HIGHEST = lax.Precision.HIGHEST
EPS = 1e-6

GRID_W = 64
GROUP_WIDTH = 512
HEAD_LANES = 128
HG_CHUNK = 64
ML_CHUNK = 128
ML_CONV = 5
DA_DIM = 64
ROPE_DIM = 16
ROPE_THETA = 500000.0
NA_ROWS = 8
NA_COLS = 16
NA_DIM = 64
N_EXPERTS = 128
TOP_K = 8
N_GROUPS = 8
TOPK_GROUPS = 4
ROUTED_SCALE = 2.5
MOE_ROWS = 128
MOE_TOKEN_GROUP = 2048
NEG_BIG = -1e30
VMEM_LIMIT = 48 * 1024 * 1024


def _dot(a, b, **kw):
    return jnp.dot(a, b, preferred_element_type=F32, **kw)


def _dot_nt(a, b):
    return lax.dot_general(a, b, (((1,), (1,)), ((), ())), preferred_element_type=F32)


def _dot_tn(a, b):
    return lax.dot_general(a, b, (((0,), (0,)), ((), ())), preferred_element_type=F32)


def _sigmoid(x):
    return jax.nn.sigmoid(x)


def _silu(x):
    return x * jax.nn.sigmoid(x)


def _log_sigmoid(x):
    return jnp.minimum(x, 0.0) - jnp.log(1.0 + jnp.exp(-jnp.abs(x)))


def _rms(x, gain):
    return x * lax.rsqrt(jnp.mean(x * x, axis=-1, keepdims=True) + EPS) * gain


def _mod_kernel(c_ref, w_ref, b_ref, o_ref):
    o_ref[...] = _dot(_silu(c_ref[...]), w_ref[...], precision=HIGHEST) + b_ref[...]


def ada_mod(c, w_mod, b_mod):
    B, D = c.shape
    N = w_mod.shape[1]
    tn = 1024
    out = pl.pallas_call(
        _mod_kernel,
        out_shape=jax.ShapeDtypeStruct((B, N), F32),
        grid=(N // tn,),
        in_specs=[pl.BlockSpec((B, D), lambda j: (0, 0)),
                  pl.BlockSpec((D, tn), lambda j: (0, j)),
                  pl.BlockSpec((1, tn), lambda j: (0, j))],
        out_specs=pl.BlockSpec((B, tn), lambda j: (0, j)),
        name="ada_mod",
    )(c, w_mod, b_mod.reshape(1, N))
    return out.reshape(B, 6, D)


def _in_kernel(x_ref, mod_ref, gain_ref, w_ref, *rest, has_gate):
    if has_gate:
        wg_ref, o_ref, og_ref, h_ref = rest
    else:
        o_ref, h_ref = rest

    @pl.when(pl.program_id(2) == 0)
    def _():
        h = _rms(x_ref[0], gain_ref[...]) * (1.0 + mod_ref[0, 1:2, :]) + mod_ref[0, 0:1, :]
        h_ref[...] = h.astype(BF16)
        if has_gate:
            og_ref[0] = _dot(h, wg_ref[...], precision=HIGHEST)

    o_ref[0] = _dot(h_ref[...], w_ref[...])


def in_proj(x, mod, gain, w_bf16, w_gate=None):
    B, T, D = x.shape
    N = w_bf16.shape[1]
    tm = min(512, T)
    tn = 512
    has_gate = w_gate is not None
    in_specs = [pl.BlockSpec((1, tm, D), lambda b, i, j: (b, i, 0)),
                pl.BlockSpec((1, 6, D), lambda b, i, j: (b, 0, 0)),
                pl.BlockSpec((1, D), lambda b, i, j: (0, 0)),
                pl.BlockSpec((D, tn), lambda b, i, j: (0, j))]
    out_shape = [jax.ShapeDtypeStruct((B, T, N), F32)]
    out_specs = [pl.BlockSpec((1, tm, tn), lambda b, i, j: (b, i, j))]
    args = [x, mod, gain.reshape(1, D), w_bf16]
    if has_gate:
        in_specs.append(pl.BlockSpec((D, HEAD_LANES), lambda b, i, j: (0, 0)))
        out_shape.append(jax.ShapeDtypeStruct((B, T, HEAD_LANES), F32))
        out_specs.append(pl.BlockSpec((1, tm, HEAD_LANES), lambda b, i, j: (b, i, 0)))
        args.append(w_gate)
    res = pl.pallas_call(
        functools.partial(_in_kernel, has_gate=has_gate),
        out_shape=out_shape,
        grid=(B, T // tm, N // tn),
        in_specs=in_specs,
        out_specs=out_specs,
        scratch_shapes=[pltpu.VMEM((tm, D), BF16)],
        compiler_params=pltpu.CompilerParams(
            dimension_semantics=("arbitrary", "arbitrary", "arbitrary")),
        name="in_proj",
    )(*args)
    return res if has_gate else res[0]


def _hgrn_consts(C):
    t = np.arange(C)
    tri = (t[None, :] <= t[:, None]).astype(np.float32)
    triT = np.ascontiguousarray(tri.T)
    wf, wb, mf = [tri], [triT], []
    levels = int(round(math.log2(C)))
    for l in range(levels):
        size = C >> l
        blk = t // size
        r = blk * size + size // 2
        wf.append(tri - tri[r - 1])
        wb.append(triT - triT[r])
        upper = (t % size) >= size // 2
        mf.append(((blk[:, None] == blk[None, :]) & upper[:, None] & (~upper)[None, :]).astype(np.float32))
    mf.append(np.eye(C, dtype=np.float32))
    ones = np.ones((8, C), np.float32)
    wf.append(ones)
    wb.append(ones)
    mf = np.stack(mf)
    mb = np.ascontiguousarray(np.transpose(mf, (0, 2, 1)))
    return np.concatenate(wf), np.concatenate(wb), mf, mb


def _hgrn_kernel(q_ref, i_ref, ff_ref, fb_ref, g_ref, lb_ref, gain_ref, wf_ref, wb_ref, mf_ref, mb_ref,
                 o_ref, *, C, T):
    n = T // C
    levels = int(round(math.log2(C)))
    dv = q_ref.shape[-1]

    def chunk(c, st, f_ref, lbd, w_ref, m_ref):
        sl = pl.ds(pl.multiple_of(c * C, C), C)
        q = _silu(q_ref[0, sl, :])
        v = i_ref[0, sl, :]
        fg = lbd + (1.0 - lbd) * _sigmoid(f_ref[0, sl, :])
        k = 1.0 - fg
        lf = jnp.log(fg)
        dall = _dot(w_ref[...], lf, precision=HIGHEST)
        cum = dall[0:C]
        tot = dall[(levels + 1) * C:(levels + 1) * C + 1]
        attn = m_ref[levels] * _dot_nt(q.astype(BF16), k.astype(BF16))
        for l in range(levels):
            e = jnp.exp(-jnp.abs(dall[(l + 1) * C:(l + 2) * C]))
            attn = attn + m_ref[l] * _dot_nt((q * e).astype(BF16), (k * e).astype(BF16))
        o = _dot(attn.astype(BF16), v.astype(BF16))
        o = o + _dot_nt((q * jnp.exp(cum)).astype(BF16), st.astype(BF16))
        kt = k * jnp.exp(tot - cum)
        st = st * jnp.exp(tot) + _dot_tn(v.astype(BF16), kt.astype(BF16))
        return sl, o, st

    st0 = jnp.zeros((dv, dv), F32)

    def fwd(c, st):
        sl, o, st = chunk(c, st, ff_ref, lb_ref[0:1, :], wf_ref, mf_ref)
        o_ref[0, sl, :] = o
        return st

    lax.fori_loop(0, n, fwd, st0)

    def bwd(j, st):
        c = n - 1 - j
        sl, o, st = chunk(c, st, fb_ref, lb_ref[1:2, :], wb_ref, mb_ref)
        tot = o_ref[0, sl, :] + o
        o_ref[0, sl, :] = _rms(tot, gain_ref[...]) * _silu(g_ref[0, sl, :])
        return st

    lax.fori_loop(0, n, bwd, st0)


def hgrn2(y, lb, norm_gain):
    B, T, _ = y.shape
    H = GROUP_WIDTH // HEAD_LANES
    C = HG_CHUNK
    wf, wb, mf, mb = (jnp.asarray(a) for a in _hgrn_consts(C))

    def col(group):
        return pl.BlockSpec((1, T, HEAD_LANES), lambda b, h, group=group: (b, 0, group * H + h))

    def const(a):
        nd = a.ndim
        return pl.BlockSpec(a.shape, lambda b, h, nd=nd: (0,) * nd)

    return pl.pallas_call(
        functools.partial(_hgrn_kernel, C=C, T=T),
        out_shape=jax.ShapeDtypeStruct((B, T, GROUP_WIDTH), F32),
        grid=(B, H),
        in_specs=[col(0), col(1), col(2), col(3), col(4),
                  pl.BlockSpec((2, HEAD_LANES), lambda b, h: (0, h)),
                  pl.BlockSpec((1, HEAD_LANES), lambda b, h: (0, 0)),
                  const(wf), const(wb), const(mf), const(mb)],
        out_specs=pl.BlockSpec((1, T, HEAD_LANES), lambda b, h: (b, 0, h)),
        compiler_params=pltpu.CompilerParams(
            dimension_semantics=("arbitrary", "arbitrary"), vmem_limit_bytes=VMEM_LIMIT),
        name="hgrn2",
    )(y, y, y, y, y, lb, norm_gain.reshape(1, HEAD_LANES), wf, wb, mf, mb)


def _rope_tables(T):
    pos = np.arange(T, dtype=np.float32)
    inv_freq = (ROPE_THETA ** (-np.arange(0, ROPE_DIM, 2, dtype=np.float32) / ROPE_DIM)).astype(np.float32)
    ang = pos[:, None] * inv_freq[None, :]
    cos, sin = np.cos(ang), np.sin(ang)
    half = ROPE_DIM // 2
    c = np.ones((T, HEAD_LANES), np.float32)
    s_prev = np.zeros((T, HEAD_LANES), np.float32)
    s_next = np.zeros((T, HEAD_LANES), np.float32)
    for base in range(0, HEAD_LANES, DA_DIM):
        c[:, base:base + half] = cos
        c[:, base + half:base + ROPE_DIM] = cos
        s_next[:, base:base + half] = -sin
        s_prev[:, base + half:base + ROPE_DIM] = sin
    return jnp.asarray(c), jnp.asarray(s_prev), jnp.asarray(s_next)


def _rope(x, c, s_prev, s_next):
    half = ROPE_DIM // 2
    lanes = x.shape[-1]
    return (x * c + pltpu.roll(x, half, axis=1) * s_prev
            + pltpu.roll(x, lanes - half, axis=1) * s_next)


def _diff_kernel(lam_ref, q_ref, k_ref, v_ref, cq_ref, spq_ref, snq_ref, ck_ref, spk_ref, snk_ref,
                 subln_ref, o_ref, kr_ref, vb_ref, *, T, out_scale):
    rt = min(512, T)

    @pl.when(pl.program_id(2) == 0)
    def _():
        for r0 in range(0, T, rt):
            sl = slice(r0, r0 + rt)
            kr_ref[sl, :] = _rope(k_ref[0, sl, :], ck_ref[sl, :], spk_ref[sl, :], snk_ref[sl, :]).astype(BF16)
            vb_ref[sl, :] = v_ref[0, sl, :].astype(BF16)

    q = (_rope(q_ref[0], cq_ref[...], spq_ref[...], snq_ref[...]) * (DA_DIM ** -0.5)).astype(BF16)
    lam = lam_ref[0, 0]
    v = vb_ref[...]

    def softmax_pv(m):
        sl = slice(m * DA_DIM, (m + 1) * DA_DIM)
        s = _dot_nt(q[:, sl], kr_ref[:, sl])
        e = jnp.exp(s - jnp.max(s, axis=-1, keepdims=True))
        return _dot(e.astype(BF16), v) / jnp.sum(e, axis=-1, keepdims=True)

    o = softmax_pv(0) - lam * softmax_pv(1)
    o_ref[0] = _rms(o, subln_ref[...]) * out_scale


def diff_attention(y, lam, subln, layer_idx):
    B, T, _ = y.shape
    H = GROUP_WIDTH // HEAD_LANES
    tq = min(256, T)
    lambda_init = 0.8 - 0.6 * math.exp(-0.3 * layer_idx)
    c, sp, sn = _rope_tables(T)

    def col(group, rows):
        if rows == T:
            return pl.BlockSpec((1, T, HEAD_LANES), lambda b, h, i, group=group: (b, 0, group * H + h))
        return pl.BlockSpec((1, rows, HEAD_LANES), lambda b, h, i, group=group: (b, i, group * H + h))

    tab_q = pl.BlockSpec((tq, HEAD_LANES), lambda b, h, i: (i, 0))
    tab_k = pl.BlockSpec((T, HEAD_LANES), lambda b, h, i: (0, 0))
    return pl.pallas_call(
        functools.partial(_diff_kernel, T=T, out_scale=1.0 - lambda_init),
        out_shape=jax.ShapeDtypeStruct((B, T, GROUP_WIDTH), F32),
        grid=(B, H, T // tq),
        in_specs=[pl.BlockSpec(memory_space=pltpu.SMEM),
                  col(5, tq), col(6, T), col(7, T),
                  tab_q, tab_q, tab_q, tab_k, tab_k, tab_k,
                  pl.BlockSpec((1, HEAD_LANES), lambda b, h, i: (0, 0))],
        out_specs=pl.BlockSpec((1, tq, HEAD_LANES), lambda b, h, i: (b, i, h)),
        scratch_shapes=[pltpu.VMEM((T, HEAD_LANES), BF16), pltpu.VMEM((T, HEAD_LANES), BF16)],
        compiler_params=pltpu.CompilerParams(
            dimension_semantics=("arbitrary", "arbitrary", "arbitrary"), vmem_limit_bytes=VMEM_LIMIT),
        name="diff_attention",
    )(lam.reshape(1, 1), y, y, y, c, sp, sn, c, sp, sn, subln.reshape(1, HEAD_LANES))


def _mlstm_kernel(q_ref, k_ref, v_ref, og_ref, gt_ref, cwq_ref, cwk_ref, cbq_ref, cbk_ref, gbias_ref,
                  gain_ref, tri_ref, o_ref, qc_ref, kc_ref, xp_ref, gx_ref, *, C, T, dk):
    n = T // C
    head = pl.program_id(1)
    pad = 8
    half = ML_CONV // 2
    rt = min(512, T)

    xp_ref[0:pad, :] = jnp.zeros((pad, dk), F32)
    xp_ref[pad + T:pad + T + pad, :] = jnp.zeros((pad, dk), F32)
    for src, cw_ref, cb_ref, dst, scale in ((q_ref, cwq_ref, cbq_ref, qc_ref, 1.0),
                                            (k_ref, cwk_ref, cbk_ref, kc_ref, dk ** -0.5)):
        xp_ref[pad:pad + T, :] = src[0]
        for r0 in range(0, T, rt):
            acc = jnp.zeros((rt, dk), F32) + cb_ref[...]
            for j in range(ML_CONV):
                acc = acc + xp_ref[pad + r0 + j - half:pad + r0 + j - half + rt, :] * cw_ref[j:j + 1, :]
            dst[r0:r0 + rt, :] = _silu(acc) * scale

    lane = lax.broadcasted_iota(jnp.int32, (rt, HEAD_LANES), 1)
    is_f = (lane % 8) >= 4
    for r0 in range(0, T, rt):
        g = gt_ref[0, r0:r0 + rt, :] + gbias_ref[...]
        p = jnp.where(is_f, _log_sigmoid(g), g)
        x = jnp.zeros((rt, HEAD_LANES), F32)
        for j, src_lane in enumerate((0, 4, 8, 12)):
            colv = jnp.sum(jnp.where(lane == src_lane + head, p, 0.0), axis=1, keepdims=True)
            x = jnp.where(lane == j, colv, x)
        gx_ref[r0:r0 + rt, :] = x

    row = lax.broadcasted_iota(jnp.int32, (C, C), 0)
    colm = lax.broadcasted_iota(jnp.int32, (C, C), 1)

    def chunk(c, carry, d):
        s_state, nvec, m = carry
        sl = pl.ds(pl.multiple_of(c * C, C), C)
        q = qc_ref[sl, :]
        k = kc_ref[sl, :]
        v = v_ref[0, sl, :]
        x = gx_ref[sl, :]
        tri = tri_ref[d]
        mask = (colm <= row) if d == 0 else (colm >= row)
        cumx = _dot(tri, x, precision=HIGHEST)
        xt = x.T
        cumxt = cumx.T
        ig_c = x[:, 2 * d:2 * d + 1]
        ig_r = xt[2 * d:2 * d + 1, :]
        cum_c = cumx[:, 2 * d + 1:2 * d + 2]
        cum_r = cumxt[2 * d + 1:2 * d + 2, :]
        tot = jnp.sum(x[:, 2 * d + 1:2 * d + 2], axis=0, keepdims=True)
        dmat = jnp.where(mask, cum_c - cum_r + ig_r, -jnp.inf)
        g = cum_c + m
        mt = jnp.maximum(g, jnp.max(dmat, axis=1, keepdims=True))
        a = _dot_nt(q.astype(BF16), k.astype(BF16)) * jnp.exp(dmat - mt)
        inter = jnp.exp(g - mt)
        num = inter * _dot(q.astype(BF16), s_state.astype(BF16)) + _dot(a.astype(BF16), v.astype(BF16))
        den = inter * jnp.sum(q * nvec, axis=1, keepdims=True) + jnp.sum(a, axis=1, keepdims=True)
        h = num / jnp.maximum(jnp.abs(den), jnp.exp(-mt))
        ds = tot - cum_c + ig_c
        m_new = jnp.maximum(tot + m, jnp.max(ds, axis=0, keepdims=True))
        decay = jnp.exp(tot + m - m_new)
        kw = k * jnp.exp(ds - m_new)
        s_state = decay * s_state + _dot_tn(kw.astype(BF16), v.astype(BF16))
        nvec = decay * nvec + jnp.sum(kw, axis=0, keepdims=True)
        return sl, h, (s_state, nvec, m_new)

    init = (jnp.zeros((dk, dk), F32), jnp.zeros((1, dk), F32), jnp.full((1, 1), NEG_BIG, F32))

    def fwd(c, carry):
        sl, h, carry = chunk(c, carry, 0)
        o_ref[0, sl, :] = h
        return carry

    lax.fori_loop(0, n, fwd, init)

    def bwd(j, carry):
        sl, h, carry = chunk(n - 1 - j, carry, 1)
        o_ref[0, sl, :] = _rms(o_ref[0, sl, :] + h, gain_ref[...]) * _sigmoid(og_ref[0, sl, :])
        return carry

    lax.fori_loop(0, n, bwd, init)


def mlstm(y, gates, conv_w, conv_b, i_bias, f_bias, norm_gain):
    B, T, _ = y.shape
    H = GROUP_WIDTH // HEAD_LANES
    C = min(ML_CHUNK, T)
    t = np.arange(C)
    tri = np.stack([(t[None, :] <= t[:, None]), (t[None, :] >= t[:, None])]).astype(np.float32)
    gbias = jnp.zeros((1, HEAD_LANES), F32)
    gbias = gbias.at[0, 0:4].set(i_bias[0]).at[0, 4:8].set(f_bias[0])
    gbias = gbias.at[0, 8:12].set(i_bias[1]).at[0, 12:16].set(f_bias[1])

    def col(group):
        return pl.BlockSpec((1, T, HEAD_LANES), lambda b, h, group=group: (b, 0, group * H + h))

    conv_q = pl.BlockSpec((ML_CONV, HEAD_LANES), lambda b, h: (0, h))
    conv_k = pl.BlockSpec((ML_CONV, HEAD_LANES), lambda b, h: (0, H + h))
    bias_q = pl.BlockSpec((1, HEAD_LANES), lambda b, h: (0, h))
    bias_k = pl.BlockSpec((1, HEAD_LANES), lambda b, h: (0, H + h))
    cb = conv_b.reshape(1, -1)
    return pl.pallas_call(
        functools.partial(_mlstm_kernel, C=C, T=T, dk=HEAD_LANES),
        out_shape=jax.ShapeDtypeStruct((B, T, GROUP_WIDTH), F32),
        grid=(B, H),
        in_specs=[col(0), col(1), col(2), col(3),
                  pl.BlockSpec((1, T, HEAD_LANES), lambda b, h: (b, 0, 0)),
                  conv_q, conv_k, bias_q, bias_k,
                  pl.BlockSpec((1, HEAD_LANES), lambda b, h: (0, 0)),
                  pl.BlockSpec((1, HEAD_LANES), lambda b, h: (0, 0)),
                  pl.BlockSpec((2, C, C), lambda b, h: (0, 0, 0))],
        out_specs=pl.BlockSpec((1, T, HEAD_LANES), lambda b, h: (b, 0, h)),
        scratch_shapes=[pltpu.VMEM((T, HEAD_LANES), F32), pltpu.VMEM((T, HEAD_LANES), F32),
                        pltpu.VMEM((T + 16, HEAD_LANES), F32), pltpu.VMEM((T, HEAD_LANES), F32)],
        compiler_params=pltpu.CompilerParams(
            dimension_semantics=("arbitrary", "arbitrary"), vmem_limit_bytes=VMEM_LIMIT),
        name="mlstm",
    )(y, y, y, y, gates, conv_w, conv_w, cb, cb, gbias, norm_gain.reshape(1, HEAD_LANES), jnp.asarray(tri))


def _na_bias_table(rpb, rows):
    kr = min(NA_ROWS, rows)
    c = np.arange(GRID_W)
    cstart = np.clip(c - NA_COLS // 2, 0, GRID_W - NA_COLS)
    kc = np.arange(GRID_W)
    valid = (kc[None, :] >= cstart[:, None]) & (kc[None, :] < cstart[:, None] + NA_COLS)
    coff = np.clip(kc[None, :] - c[:, None] + NA_COLS - 1, 0, 2 * NA_COLS - 2)
    di = np.arange(kr)
    i = np.arange(kr)
    roff = i[None, :] - di[:, None] + NA_ROWS - 1
    heads = rpb.shape[0]
    cols = jnp.where(jnp.asarray(valid)[None, None], rpb.astype(F32)[:, :, coff], NEG_BIG)
    tab = jnp.take(cols, jnp.asarray(roff.reshape(-1)), axis=1)
    tab = tab.reshape(heads, kr, kr, GRID_W, GRID_W).transpose(0, 1, 3, 2, 4)
    return tab.reshape(heads, kr, GRID_W, kr * GRID_W)


def _na_kernel(q_ref, k_ref, v_ref, bm_ref, o_ref, *, rows, kr):
    W = GRID_W
    heads_per_block = HEAD_LANES // NA_DIM

    def body(r, _):
        rs = jnp.clip(r - kr // 2, 0, rows - kr)
        di = r - rs
        qs = pl.ds(pl.multiple_of(r * W, W), W)
        ks = pl.ds(pl.multiple_of(rs * W, W), kr * W)
        q = (q_ref[0, qs, :] * (NA_DIM ** -0.5)).astype(BF16)
        kw = k_ref[0, ks, :].astype(BF16)
        vw = v_ref[0, ks, :].astype(BF16)
        outs = []
        for hh in range(heads_per_block):
            sl = slice(hh * NA_DIM, (hh + 1) * NA_DIM)
            s = _dot_nt(q[:, sl], kw[:, sl]) + bm_ref[hh, di]
            e = jnp.exp(s - jnp.max(s, axis=-1, keepdims=True))
            outs.append(_dot(e.astype(BF16), vw[:, sl]) / jnp.sum(e, axis=-1, keepdims=True))
        o_ref[0, qs, :] = jnp.concatenate(outs, axis=1)
        return 0

    lax.fori_loop(0, rows, body, 0)


def neighbourhood_attention(y, rpb):
    B, T, _ = y.shape
    rows = T // GRID_W
    kr = min(NA_ROWS, rows)
    HB = GROUP_WIDTH // HEAD_LANES
    hpb = HEAD_LANES // NA_DIM
    bm = _na_bias_table(rpb, rows)

    def col(group):
        return pl.BlockSpec((1, T, HEAD_LANES), lambda b, h, group=group: (b, 0, group * HB + h))

    return pl.pallas_call(
        functools.partial(_na_kernel, rows=rows, kr=kr),
        out_shape=jax.ShapeDtypeStruct((B, T, GROUP_WIDTH), F32),
        grid=(B, HB),
        in_specs=[col(4), col(5), col(6),
                  pl.BlockSpec((hpb, kr, GRID_W, kr * GRID_W), lambda b, h: (h, 0, 0, 0))],
        out_specs=pl.BlockSpec((1, T, HEAD_LANES), lambda b, h: (b, 0, h)),
        compiler_params=pltpu.CompilerParams(
            dimension_semantics=("arbitrary", "arbitrary"), vmem_limit_bytes=VMEM_LIMIT),
        name="neighbourhood_attention",
    )(y, y, y, bm)


def _out_kernel(a_ref, b_ref, x_ref, mod_ref, gain_ref, w_ref, r_ref, x1_ref, h2_ref, lg_ref):
    G = a_ref.shape[-1]
    y = _dot(a_ref[0].astype(BF16), w_ref[0:G, :]) + _dot(b_ref[0].astype(BF16), w_ref[G:2 * G, :])
    x1 = x_ref[0] + mod_ref[0, 2:3, :] * y
    x1_ref[0] = x1
    h2 = _rms(x1, gain_ref[...]) * (1.0 + mod_ref[0, 4:5, :]) + mod_ref[0, 3:4, :]
    h2_ref[0] = h2
    lg_ref[0] = _dot(h2, r_ref[...], precision=HIGHEST)


def out_proj(a_out, b_out, x, mod, gain2, w_out_bf16, router):
    B, T, D = x.shape
    G = a_out.shape[-1]
    E = router.shape[1]
    tm = min(512, T)
    return pl.pallas_call(
        _out_kernel,
        out_shape=[jax.ShapeDtypeStruct((B, T, D), F32), jax.ShapeDtypeStruct((B, T, D), F32),
                   jax.ShapeDtypeStruct((B, T, E), F32)],
        grid=(B, T // tm),
        in_specs=[pl.BlockSpec((1, tm, G), lambda b, i: (b, i, 0)),
                  pl.BlockSpec((1, tm, G), lambda b, i: (b, i, 0)),
                  pl.BlockSpec((1, tm, D), lambda b, i: (b, i, 0)),
                  pl.BlockSpec((1, 6, D), lambda b, i: (b, 0, 0)),
                  pl.BlockSpec((1, D), lambda b, i: (0, 0)),
                  pl.BlockSpec((2 * G, D), lambda b, i: (0, 0)),
                  pl.BlockSpec((D, E), lambda b, i: (0, 0))],
        out_specs=[pl.BlockSpec((1, tm, D), lambda b, i: (b, i, 0)),
                   pl.BlockSpec((1, tm, D), lambda b, i: (b, i, 0)),
                   pl.BlockSpec((1, tm, E), lambda b, i: (b, i, 0))],
        compiler_params=pltpu.CompilerParams(
            dimension_semantics=("arbitrary", "arbitrary"), vmem_limit_bytes=VMEM_LIMIT),
        name="out_proj",
    )(a_out, b_out, x, mod, gain2.reshape(1, D), w_out_bf16, router)


def _route_kernel(lg_ref, bias_ref, idx_ref, w_ref, cnt_ref, *, tiles_per_group):
    @pl.when(pl.program_id(0) % tiles_per_group == 0)
    def _():
        cnt_ref[...] = jnp.zeros(cnt_ref.shape, F32)

    scores = _sigmoid(lg_ref[...])
    sel = scores + bias_ref[...]
    tm, E = sel.shape
    lane = lax.broadcasted_iota(jnp.int32, (tm, E), 1)
    per_group = E // N_GROUPS
    grp = lane // per_group
    neg = -jnp.inf

    def first_argmax(x):
        m = jnp.max(x, axis=-1, keepdims=True)
        i = jnp.min(jnp.where(x == m, lane, E), axis=-1, keepdims=True)
        return m, i

    gscore = []
    for g in range(N_GROUPS):
        x = jnp.where(grp == g, sel, neg)
        m1, i1 = first_argmax(x)
        m2 = jnp.max(jnp.where(lane == i1, neg, x), axis=-1, keepdims=True)
        gscore.append(m1 + m2)
    keep = jnp.zeros((tm, E), jnp.bool_)
    for g in range(N_GROUPS):
        beaten = jnp.zeros((tm, 1), jnp.int32)
        for o in range(N_GROUPS):
            if o == g:
                continue
            wins = (gscore[o] > gscore[g]) | ((gscore[o] == gscore[g]) & (o < g))
            beaten = beaten + wins.astype(jnp.int32)
        keep = keep | ((grp == g) & (beaten < TOPK_GROUPS))
    sel = jnp.where(keep, sel, neg)

    idx_out = jnp.zeros((tm, E), jnp.int32)
    w_out = jnp.zeros((tm, E), F32)
    w_sum = jnp.zeros((tm, 1), F32)
    chosen = jnp.zeros((tm, E), F32)
    for k in range(TOP_K):
        _, i = first_argmax(sel)
        hit = lane == i
        val = jnp.sum(jnp.where(hit, scores, 0.0), axis=-1, keepdims=True)
        sel = jnp.where(hit, neg, sel)
        chosen = jnp.where(hit, 1.0, chosen)
        idx_out = jnp.where(lane == k, i, idx_out)
        w_out = jnp.where(lane == k, val, w_out)
        w_sum = w_sum + val
    idx_ref[...] = idx_out
    w_ref[...] = w_out / w_sum * ROUTED_SCALE
    cnt_ref[0] = cnt_ref[0] + jnp.sum(chosen, axis=0, keepdims=True)


def route(logits, router_bias, group_tokens):
    N, E = logits.shape
    tm = min(256, N)
    tpg = group_tokens // tm
    idx, w, cnt = pl.pallas_call(
        functools.partial(_route_kernel, tiles_per_group=tpg),
        out_shape=[jax.ShapeDtypeStruct((N, E), jnp.int32), jax.ShapeDtypeStruct((N, E), F32),
                   jax.ShapeDtypeStruct((N // group_tokens, 1, E), F32)],
        grid=(N // tm,),
        in_specs=[pl.BlockSpec((tm, E), lambda i: (i, 0)), pl.BlockSpec((1, E), lambda i: (0, 0))],
        out_specs=[pl.BlockSpec((tm, E), lambda i: (i, 0)), pl.BlockSpec((tm, E), lambda i: (i, 0)),
                   pl.BlockSpec((1, 1, E), lambda i, tpg=tpg: (i // tpg, 0, 0))],
        compiler_params=pltpu.CompilerParams(dimension_semantics=("arbitrary",)),
        name="route",
    )(logits, router_bias.reshape(1, E))
    return idx[:, :TOP_K], w[:, :TOP_K], cnt.reshape(-1, E).astype(jnp.int32)


def _moe_kernel(off_ref, tok_ref, wl_ref, x_ref, wg_ref, wu_ref, wd_ref, acc_ref, xg_ref, yb_ref,
                *, R, E, per_group):
    g = pl.program_id(0)
    e = pl.program_id(1)

    @pl.when(e == 0)
    def _():
        acc_ref[...] = jnp.zeros(acc_ref.shape, F32)

    @pl.when((g == 0) & (e == 0))
    def _():
        xg_ref[...] = jnp.zeros(xg_ref.shape, F32)

    seg = off_ref[g * E + e]
    cnt = off_ref[g * E + e + 1] - seg
    start = seg - g * per_group

    def sub(sb, _):
        s0 = start + sb * R
        nr = jnp.minimum(R, cnt - sb * R)

        def gather(r, _):
            t = tok_ref[s0 + r]
            xg_ref[pl.ds(r, 1), :] = x_ref[0, pl.ds(t, 1), :]
            return 0

        lax.fori_loop(0, nr, gather, 0)
        xb = xg_ref[...].astype(BF16)
        hmid = _silu(_dot(xb, wg_ref[0])) * _dot(xb, wu_ref[0])
        yb_ref[...] = _dot(hmid.astype(BF16), wd_ref[0])

        def scatter(r, _):
            t = tok_ref[s0 + r]
            w = wl_ref[s0 + r]
            acc_ref[0, pl.ds(t, 1), :] = acc_ref[0, pl.ds(t, 1), :] + w * yb_ref[pl.ds(r, 1), :]
            return 0

        lax.fori_loop(0, nr, scatter, 0)
        return 0

    lax.fori_loop(0, (cnt + R - 1) // R, sub, 0)


def routed_experts(h2, top_idx, top_w, counts, exp_gate, exp_up, exp_down):
    N, D = h2.shape
    E, _, F = exp_gate.shape
    TG = min(MOE_TOKEN_GROUP, N)
    G = N // TG
    per_group = TG * TOP_K
    key = (jnp.arange(N, dtype=jnp.int32)[:, None] // TG) * E + top_idx.astype(jnp.int32)
    order = jnp.argsort(key.reshape(-1))
    tok_s = ((order // TOP_K) % TG).astype(jnp.int32)
    w_s = top_w.reshape(-1)[order]
    off = jnp.concatenate([jnp.zeros((1,), jnp.int32), jnp.cumsum(counts.reshape(-1)).astype(jnp.int32)])

    grid_spec = pltpu.PrefetchScalarGridSpec(
        num_scalar_prefetch=1,
        grid=(G, E),
        in_specs=[pl.BlockSpec((per_group,), lambda g, e, off: (g,), memory_space=pltpu.SMEM),
                  pl.BlockSpec((per_group,), lambda g, e, off: (g,), memory_space=pltpu.SMEM),
                  pl.BlockSpec((1, TG, D), lambda g, e, off: (g, 0, 0)),
                  pl.BlockSpec((1, D, F), lambda g, e, off: (e, 0, 0)),
                  pl.BlockSpec((1, D, F), lambda g, e, off: (e, 0, 0)),
                  pl.BlockSpec((1, F, D), lambda g, e, off: (e, 0, 0))],
        out_specs=pl.BlockSpec((1, TG, D), lambda g, e, off: (g, 0, 0)),
        scratch_shapes=[pltpu.VMEM((MOE_ROWS, D), F32), pltpu.VMEM((MOE_ROWS, D), F32)],
    )
    out = pl.pallas_call(
        functools.partial(_moe_kernel, R=MOE_ROWS, E=E, per_group=per_group),
        out_shape=jax.ShapeDtypeStruct((G, TG, D), F32),
        grid_spec=grid_spec,
        compiler_params=pltpu.CompilerParams(
            dimension_semantics=("arbitrary", "arbitrary"), vmem_limit_bytes=VMEM_LIMIT),
        name="routed_experts",
    )(off, tok_s, w_s, h2.reshape(G, TG, D), exp_gate, exp_up, exp_down)
    return out.reshape(N, D)


def _final_kernel(x1_ref, h2_ref, rt_ref, mod_ref, wg_ref, wu_ref, wd_ref, *rest, final):
    if final:
        fg_ref, o_ref = rest
    else:
        (o_ref,) = rest
    hb = h2_ref[0].astype(BF16)
    hmid = _silu(_dot(hb, wg_ref[...])) * _dot(hb, wu_ref[...])
    shared = _dot(hmid.astype(BF16), wd_ref[...])
    x2 = x1_ref[0] + mod_ref[0, 5:6, :] * (rt_ref[0] + shared)
    if final:
        x2 = _rms(x2, fg_ref[...])
    o_ref[0] = x2


def shared_and_residual(x1, h2, routed, mod, sh_gate, sh_up, sh_down, final_gain=None):
    B, T, D = x1.shape
    F = sh_gate.shape[1]
    tm = min(512, T)
    final = final_gain is not None
    tile = pl.BlockSpec((1, tm, D), lambda b, i: (b, i, 0))
    in_specs = [tile, tile, tile,
                pl.BlockSpec((1, 6, D), lambda b, i: (b, 0, 0)),
                pl.BlockSpec((D, F), lambda b, i: (0, 0)),
                pl.BlockSpec((D, F), lambda b, i: (0, 0)),
                pl.BlockSpec((F, D), lambda b, i: (0, 0))]
    args = [x1, h2, routed, mod, sh_gate, sh_up, sh_down]
    if final:
        in_specs.append(pl.BlockSpec((1, D), lambda b, i: (0, 0)))
        args.append(final_gain.reshape(1, D))
    return pl.pallas_call(
        functools.partial(_final_kernel, final=final),
        out_shape=jax.ShapeDtypeStruct((B, T, D), F32),
        grid=(B, T // tm),
        in_specs=in_specs,
        out_specs=tile,
        compiler_params=pltpu.CompilerParams(
            dimension_semantics=("arbitrary", "arbitrary"), vmem_limit_bytes=VMEM_LIMIT),
        name="shared_and_residual",
    )(*args)


def moe_block(x1, h2, logits, mod, router_bias, exp_gate, exp_up, exp_down, sh_gate, sh_up, sh_down,
              final_gain=None):
    B, T, D = x1.shape
    top_idx, top_w, counts = route(logits.reshape(B * T, -1), router_bias, min(MOE_TOKEN_GROUP, B * T))
    routed = routed_experts(h2.reshape(B * T, D), top_idx, top_w, counts,
                            exp_gate.astype(BF16), exp_up.astype(BF16), exp_down.astype(BF16))
    return shared_and_residual(x1, h2, routed.reshape(B, T, D), mod,
                               sh_gate.astype(BF16), sh_up.astype(BF16), sh_down.astype(BF16), final_gain)


def kernel(x, c, hgrn_lb_logits, l0_norm1, l0_norm2, l0_w_mod, l0_b_mod, l0_w_in, l0_w_out, l0_hgrn_norm, l0_diff_lq1, l0_diff_lk1, l0_diff_lq2, l0_diff_lk2, l0_diff_subln, l0_router, l0_router_bias, l0_exp_gate, l0_exp_up, l0_exp_down, l0_sh_gate, l0_sh_up, l0_sh_down, l1_norm1, l1_norm2, l1_w_mod, l1_b_mod, l1_w_in, l1_w_out, l1_conv_w, l1_conv_b, l1_ml_i_bias, l1_ml_f_bias, l1_ml_norm, l1_na_rpb, l1_router, l1_router_bias, l1_exp_gate, l1_exp_up, l1_exp_down, l1_sh_gate, l1_sh_up, l1_sh_down, final_norm):
    G = GROUP_WIDTH
    lb_all = jnp.cumsum(jax.nn.softmax(hgrn_lb_logits.astype(F32), axis=0), axis=0)
    layer_idx = 0
    lambda_init = 0.8 - 0.6 * math.exp(-0.3 * layer_idx)
    lam = (jnp.exp(jnp.sum(l0_diff_lq1.astype(F32) * l0_diff_lk1.astype(F32)))
           - jnp.exp(jnp.sum(l0_diff_lq2.astype(F32) * l0_diff_lk2.astype(F32))) + lambda_init)

    mod0 = ada_mod(c, l0_w_mod, l0_b_mod)
    y0 = in_proj(x, mod0, l0_norm1, l0_w_in.astype(BF16))
    a_out = hgrn2(y0, lb_all[0], l0_hgrn_norm)
    b_out = diff_attention(y0, lam, l0_diff_subln, layer_idx)
    x1, h2, logits = out_proj(a_out, b_out, x, mod0, l0_norm2, l0_w_out.astype(BF16), l0_router)
    xa = moe_block(x1, h2, logits, mod0, l0_router_bias, l0_exp_gate, l0_exp_up, l0_exp_down,
                   l0_sh_gate, l0_sh_up, l0_sh_down)

    mod1 = ada_mod(c, l1_w_mod, l1_b_mod)
    n_gate = l1_w_in.shape[1] - 7 * G
    w_main = jnp.concatenate([l1_w_in[:, :4 * G], l1_w_in[:, 4 * G + n_gate:]], axis=1).astype(BF16)
    w_gate = jnp.pad(l1_w_in[:, 4 * G:4 * G + n_gate], ((0, 0), (0, HEAD_LANES - n_gate)))
    y1, gates = in_proj(xa, mod1, l1_norm1, w_main, w_gate)
    c_out = mlstm(y1, gates, l1_conv_w, l1_conv_b, l1_ml_i_bias, l1_ml_f_bias, l1_ml_norm)
    d_out = neighbourhood_attention(y1, l1_na_rpb)
    x1, h2, logits = out_proj(c_out, d_out, xa, mod1, l1_norm2, l1_w_out.astype(BF16), l1_router)
    return moe_block(x1, h2, logits, mod1, l1_router_bias, l1_exp_gate, l1_exp_up, l1_exp_down,
                     l1_sh_gate, l1_sh_up, l1_sh_down, final_gain=final_norm)
```

```python
import functools
import math

import numpy as np
import jax
import jax.numpy as jnp
from jax import lax
from jax.experimental import pallas as pl
from jax.experimental.pallas import tpu as pltpu

F32 = jnp.float32
BF16 = jnp.bfloat16
HIGHEST = lax.Precision.HIGHEST
EPS = 1e-6

GRID_W = 64
GROUP_WIDTH = 512
HEAD_LANES = 128
HG_CHUNK = 64
ML_CHUNK = 128
ML_CONV = 5
DA_DIM = 64
ROPE_DIM = 16
ROPE_THETA = 500000.0
NA_ROWS = 8
NA_COLS = 16
NA_DIM = 64
N_EXPERTS = 128
TOP_K = 8
N_GROUPS = 8
TOPK_GROUPS = 4
ROUTED_SCALE = 2.5
MOE_ROWS = 128
MOE_TOKEN_GROUP = 2048
MOE_UNROLL = 8
NEG_BIG = -1e30
VMEM_LIMIT = 48 * 1024 * 1024


def _dot(a, b, **kw):
    return jnp.dot(a, b, preferred_element_type=F32, **kw)


def _dot_nt(a, b):
    return lax.dot_general(a, b, (((1,), (1,)), ((), ())), preferred_element_type=F32)


def _dot_tn(a, b):
    return lax.dot_general(a, b, (((0,), (0,)), ((), ())), preferred_element_type=F32)


def _sigmoid(x):
    return jax.nn.sigmoid(x)


def _silu(x):
    return x * jax.nn.sigmoid(x)


def _log_sigmoid(x):
    return jnp.minimum(x, 0.0) - jnp.log(1.0 + jnp.exp(-jnp.abs(x)))


def _rms(x, gain):
    return x * lax.rsqrt(jnp.mean(x * x, axis=-1, keepdims=True) + EPS) * gain


def _mod_kernel(c_ref, w_ref, b_ref, o_ref):
    o_ref[...] = _dot(_silu(c_ref[...]), w_ref[...], precision=HIGHEST) + b_ref[...]


def ada_mod(c, w_mod, b_mod):
    B, D = c.shape
    N = w_mod.shape[1]
    tn = 1024
    out = pl.pallas_call(
        _mod_kernel,
        out_shape=jax.ShapeDtypeStruct((B, N), F32),
        grid=(N // tn,),
        in_specs=[pl.BlockSpec((B, D), lambda j: (0, 0)),
                  pl.BlockSpec((D, tn), lambda j: (0, j)),
                  pl.BlockSpec((1, tn), lambda j: (0, j))],
        out_specs=pl.BlockSpec((B, tn), lambda j: (0, j)),
        name="ada_mod",
    )(c, w_mod, b_mod.reshape(1, N))
    return out.reshape(B, 6, D)


def _in_kernel(x_ref, mod_ref, gain_ref, w_ref, *rest, has_gate):
    if has_gate:
        wg_ref, o_ref, og_ref, h_ref = rest
    else:
        o_ref, h_ref = rest

    @pl.when(pl.program_id(2) == 0)
    def _():
        h = _rms(x_ref[0], gain_ref[...]) * (1.0 + mod_ref[0, 1:2, :]) + mod_ref[0, 0:1, :]
        h_ref[...] = h.astype(BF16)
        if has_gate:
            og_ref[0] = _dot(h, wg_ref[...], precision=HIGHEST)

    o_ref[0] = _dot(h_ref[...], w_ref[...])


def in_proj(x, mod, gain, w_bf16, w_gate=None):
    B, T, D = x.shape
    N = w_bf16.shape[1]
    tm = min(512, T)
    tn = 512
    has_gate = w_gate is not None
    in_specs = [pl.BlockSpec((1, tm, D), lambda b, i, j: (b, i, 0)),
                pl.BlockSpec((1, 6, D), lambda b, i, j: (b, 0, 0)),
                pl.BlockSpec((1, D), lambda b, i, j: (0, 0)),
                pl.BlockSpec((D, tn), lambda b, i, j: (0, j))]
    out_shape = [jax.ShapeDtypeStruct((B, T, N), F32)]
    out_specs = [pl.BlockSpec((1, tm, tn), lambda b, i, j: (b, i, j))]
    args = [x, mod, gain.reshape(1, D), w_bf16]
    if has_gate:
        in_specs.append(pl.BlockSpec((D, HEAD_LANES), lambda b, i, j: (0, 0)))
        out_shape.append(jax.ShapeDtypeStruct((B, T, HEAD_LANES), F32))
        out_specs.append(pl.BlockSpec((1, tm, HEAD_LANES), lambda b, i, j: (b, i, 0)))
        args.append(w_gate)
    res = pl.pallas_call(
        functools.partial(_in_kernel, has_gate=has_gate),
        out_shape=out_shape,
        grid=(B, T // tm, N // tn),
        in_specs=in_specs,
        out_specs=out_specs,
        scratch_shapes=[pltpu.VMEM((tm, D), BF16)],
        compiler_params=pltpu.CompilerParams(
            dimension_semantics=("arbitrary", "arbitrary", "arbitrary")),
        name="in_proj",
    )(*args)
    return res if has_gate else res[0]


def _hgrn_consts(C):
    t = np.arange(C)
    tri = (t[None, :] <= t[:, None]).astype(np.float32)
    triT = np.ascontiguousarray(tri.T)
    wf, wb, mf = [tri], [triT], []
    levels = int(round(math.log2(C)))
    for l in range(levels):
        size = C >> l
        blk = t // size
        r = blk * size + size // 2
        wf.append(tri - tri[r - 1])
        wb.append(triT - triT[r])
        upper = (t % size) >= size // 2
        mf.append(((blk[:, None] == blk[None, :]) & upper[:, None] & (~upper)[None, :]).astype(np.float32))
    mf.append(np.eye(C, dtype=np.float32))
    ones = np.ones((8, C), np.float32)
    wf.append(ones)
    wb.append(ones)
    mf = np.stack(mf)
    mb = np.ascontiguousarray(np.transpose(mf, (0, 2, 1)))
    return np.concatenate(wf), np.concatenate(wb), mf, mb


def _split3(x):
    hi = x.astype(BF16)
    r1 = x - hi.astype(F32)
    mid = r1.astype(BF16)
    lo = (r1 - mid.astype(F32)).astype(BF16)
    return hi, mid, lo


def _hgrn_kernel(q_ref, i_ref, ff_ref, fb_ref, g_ref, lb_ref, gain_ref, wf_ref, wb_ref, mf_ref, mb_ref,
                 o_ref, of_ref, ob_ref, *, C, T):
    n = T // C
    levels = int(round(math.log2(C)))
    dv = q_ref.shape[-1]

    def chunk(c, st, f_ref, lbd, w_ref, m_ref):
        sl = pl.ds(pl.multiple_of(c * C, C), C)
        q = _silu(q_ref[0, sl, :])
        v = i_ref[0, sl, :]
        fg = lbd + (1.0 - lbd) * _sigmoid(f_ref[0, sl, :])
        k = 1.0 - fg
        lf = jnp.log(fg)
        d3 = _dot(w_ref[...], jnp.concatenate(_split3(lf), axis=1))
        dall = d3[:, 0:dv] + d3[:, dv:2 * dv] + d3[:, 2 * dv:3 * dv]
        cum = dall[0:C]
        tot = dall[(levels + 1) * C:(levels + 1) * C + 1]
        attn = m_ref[levels] * _dot_nt(q.astype(BF16), k.astype(BF16))
        for l in range(levels):
            e = jnp.exp(-jnp.abs(dall[(l + 1) * C:(l + 2) * C]))
            attn = attn + m_ref[l] * _dot_nt((q * e).astype(BF16), (k * e).astype(BF16))
        o = _dot(attn.astype(BF16), v.astype(BF16))
        o = o + _dot_nt((q * jnp.exp(cum)).astype(BF16), st.astype(BF16))
        kt = k * jnp.exp(tot - cum)
        st = st * jnp.exp(tot) + _dot_tn(v.astype(BF16), kt.astype(BF16))
        return sl, o, st

    st0 = jnp.zeros((dv, dv), F32)

    def step(j, carry):
        st_f, st_b = carry
        sl, o, st_f = chunk(j, st_f, ff_ref, lb_ref[0:1, :], wf_ref, mf_ref)
        of_ref[sl, :] = o
        sl, o, st_b = chunk(n - 1 - j, st_b, fb_ref, lb_ref[1:2, :], wb_ref, mb_ref)
        ob_ref[sl, :] = o
        return st_f, st_b

    lax.fori_loop(0, n, step, (st0, st0))

    rt = min(512, T)
    for r0 in range(0, T, rt):
        sl = slice(r0, r0 + rt)
        o_ref[0, sl, :] = _rms(of_ref[sl, :] + ob_ref[sl, :], gain_ref[...]) * _silu(g_ref[0, sl, :])


def hgrn2(y, lb, norm_gain):
    B, T, _ = y.shape
    H = GROUP_WIDTH // HEAD_LANES
    C = HG_CHUNK
    wf, wb, mf, mb = _hgrn_consts(C)
    wf, wb = jnp.asarray(wf, BF16), jnp.asarray(wb, BF16)
    mf, mb = jnp.asarray(mf), jnp.asarray(mb)

    def col(group):
        return pl.BlockSpec((1, T, HEAD_LANES), lambda b, h, group=group: (b, 0, group * H + h))

    def const(a):
        nd = a.ndim
        return pl.BlockSpec(a.shape, lambda b, h, nd=nd: (0,) * nd)

    return pl.pallas_call(
        functools.partial(_hgrn_kernel, C=C, T=T),
        out_shape=jax.ShapeDtypeStruct((B, T, GROUP_WIDTH), F32),
        grid=(B, H),
        in_specs=[col(0), col(1), col(2), col(3), col(4),
                  pl.BlockSpec((2, HEAD_LANES), lambda b, h: (0, h)),
                  pl.BlockSpec((1, HEAD_LANES), lambda b, h: (0, 0)),
                  const(wf), const(wb), const(mf), const(mb)],
        out_specs=pl.BlockSpec((1, T, HEAD_LANES), lambda b, h: (b, 0, h)),
        scratch_shapes=[pltpu.VMEM((T, HEAD_LANES), F32), pltpu.VMEM((T, HEAD_LANES), F32)],
        compiler_params=pltpu.CompilerParams(
            dimension_semantics=("arbitrary", "arbitrary"), vmem_limit_bytes=VMEM_LIMIT),
        name="hgrn2",
    )(y, y, y, y, y, lb, norm_gain.reshape(1, HEAD_LANES), wf, wb, mf, mb)


def _rope_tables(T):
    pos = np.arange(T, dtype=np.float32)
    inv_freq = (ROPE_THETA ** (-np.arange(0, ROPE_DIM, 2, dtype=np.float32) / ROPE_DIM)).astype(np.float32)
    ang = pos[:, None] * inv_freq[None, :]
    cos, sin = np.cos(ang), np.sin(ang)
    half = ROPE_DIM // 2
    c = np.ones((T, HEAD_LANES), np.float32)
    s_prev = np.zeros((T, HEAD_LANES), np.float32)
    s_next = np.zeros((T, HEAD_LANES), np.float32)
    for base in range(0, HEAD_LANES, DA_DIM):
        c[:, base:base + half] = cos
        c[:, base + half:base + ROPE_DIM] = cos
        s_next[:, base:base + half] = -sin
        s_prev[:, base + half:base + ROPE_DIM] = sin
    return jnp.asarray(c), jnp.asarray(s_prev), jnp.asarray(s_next)


def _rope(x, c, s_prev, s_next):
    half = ROPE_DIM // 2
    lanes = x.shape[-1]
    return (x * c + pltpu.roll(x, half, axis=1) * s_prev
            + pltpu.roll(x, lanes - half, axis=1) * s_next)


def _diff_kernel(lam_ref, q_ref, k_ref, v_ref, cq_ref, spq_ref, snq_ref, ck_ref, spk_ref, snk_ref,
                 subln_ref, o_ref, kr_ref, vb_ref, *, T, out_scale):
    rt = min(512, T)

    @pl.when(pl.program_id(2) == 0)
    def _():
        for r0 in range(0, T, rt):
            sl = slice(r0, r0 + rt)
            kr_ref[sl, :] = _rope(k_ref[0, sl, :], ck_ref[sl, :], spk_ref[sl, :], snk_ref[sl, :]).astype(BF16)
            vb_ref[sl, :] = v_ref[0, sl, :].astype(BF16)

    q = (_rope(q_ref[0], cq_ref[...], spq_ref[...], snq_ref[...]) * (DA_DIM ** -0.5)).astype(BF16)
    lam = lam_ref[0, 0]
    v = vb_ref[...]

    def softmax_pv(m):
        sl = slice(m * DA_DIM, (m + 1) * DA_DIM)
        s = _dot_nt(q[:, sl], kr_ref[:, sl])
        e = jnp.exp(s - jnp.max(s, axis=-1, keepdims=True))
        return _dot(e.astype(BF16), v) / jnp.sum(e, axis=-1, keepdims=True)

    o = softmax_pv(0) - lam * softmax_pv(1)
    o_ref[0] = _rms(o, subln_ref[...]) * out_scale


def diff_attention(y, lam, subln, layer_idx):
    B, T, _ = y.shape
    H = GROUP_WIDTH // HEAD_LANES
    tq = min(256, T)
    lambda_init = 0.8 - 0.6 * math.exp(-0.3 * layer_idx)
    c, sp, sn = _rope_tables(T)

    def col(group, rows):
        if rows == T:
            return pl.BlockSpec((1, T, HEAD_LANES), lambda b, h, i, group=group: (b, 0, group * H + h))
        return pl.BlockSpec((1, rows, HEAD_LANES), lambda b, h, i, group=group: (b, i, group * H + h))

    tab_q = pl.BlockSpec((tq, HEAD_LANES), lambda b, h, i: (i, 0))
    tab_k = pl.BlockSpec((T, HEAD_LANES), lambda b, h, i: (0, 0))
    return pl.pallas_call(
        functools.partial(_diff_kernel, T=T, out_scale=1.0 - lambda_init),
        out_shape=jax.ShapeDtypeStruct((B, T, GROUP_WIDTH), F32),
        grid=(B, H, T // tq),
        in_specs=[pl.BlockSpec(memory_space=pltpu.SMEM),
                  col(5, tq), col(6, T), col(7, T),
                  tab_q, tab_q, tab_q, tab_k, tab_k, tab_k,
                  pl.BlockSpec((1, HEAD_LANES), lambda b, h, i: (0, 0))],
        out_specs=pl.BlockSpec((1, tq, HEAD_LANES), lambda b, h, i: (b, i, h)),
        scratch_shapes=[pltpu.VMEM((T, HEAD_LANES), BF16), pltpu.VMEM((T, HEAD_LANES), BF16)],
        compiler_params=pltpu.CompilerParams(
            dimension_semantics=("arbitrary", "arbitrary", "arbitrary"), vmem_limit_bytes=VMEM_LIMIT),
        name="diff_attention",
    )(lam.reshape(1, 1), y, y, y, c, sp, sn, c, sp, sn, subln.reshape(1, HEAD_LANES))


def _mlstm_kernel(q_ref, k_ref, v_ref, og_ref, gt_ref, cwq_ref, cwk_ref, cbq_ref, cbk_ref, gbias_ref,
                  gain_ref, tri_ref, o_ref, qc_ref, kc_ref, xp_ref, gx_ref, hf_ref, hb_ref, *, C, T, dk):
    n = T // C
    head = pl.program_id(1)
    pad = 8
    half = ML_CONV // 2
    rt = min(512, T)

    xp_ref[0:pad, :] = jnp.zeros((pad, dk), F32)
    xp_ref[pad + T:pad + T + pad, :] = jnp.zeros((pad, dk), F32)
    for src, cw_ref, cb_ref, dst, scale in ((q_ref, cwq_ref, cbq_ref, qc_ref, 1.0),
                                            (k_ref, cwk_ref, cbk_ref, kc_ref, dk ** -0.5)):
        xp_ref[pad:pad + T, :] = src[0]
        for r0 in range(0, T, rt):
            acc = jnp.zeros((rt, dk), F32) + cb_ref[...]
            for j in range(ML_CONV):
                acc = acc + xp_ref[pad + r0 + j - half:pad + r0 + j - half + rt, :] * cw_ref[j:j + 1, :]
            dst[r0:r0 + rt, :] = _silu(acc) * scale

    lane = lax.broadcasted_iota(jnp.int32, (rt, HEAD_LANES), 1)
    is_f = (lane % 8) >= 4
    for r0 in range(0, T, rt):
        g = gt_ref[0, r0:r0 + rt, :] + gbias_ref[...]
        p = jnp.where(is_f, _log_sigmoid(g), g)
        x = jnp.zeros((rt, HEAD_LANES), F32)
        for j, src_lane in enumerate((0, 4, 8, 12)):
            colv = jnp.sum(jnp.where(lane == src_lane + head, p, 0.0), axis=1, keepdims=True)
            x = jnp.where(lane == j, colv, x)
        gx_ref[r0:r0 + rt, :] = x

    row = lax.broadcasted_iota(jnp.int32, (C, C), 0)
    colm = lax.broadcasted_iota(jnp.int32, (C, C), 1)

    def chunk(c, carry, d):
        s_state, nvec, m = carry
        sl = pl.ds(pl.multiple_of(c * C, C), C)
        q = qc_ref[sl, :]
        k = kc_ref[sl, :]
        v = v_ref[0, sl, :]
        x = gx_ref[sl, :]
        tri = tri_ref[d]
        mask = (colm <= row) if d == 0 else (colm >= row)
        cumx = _dot(tri, x, precision=HIGHEST)
        xt = x.T
        cumxt = cumx.T
        ig_c = x[:, 2 * d:2 * d + 1]
        ig_r = xt[2 * d:2 * d + 1, :]
        cum_c = cumx[:, 2 * d + 1:2 * d + 2]
        cum_r = cumxt[2 * d + 1:2 * d + 2, :]
        tot = jnp.sum(x[:, 2 * d + 1:2 * d + 2], axis=0, keepdims=True)
        dmat = jnp.where(mask, cum_c - cum_r + ig_r, -jnp.inf)
        g = cum_c + m
        mt = jnp.maximum(g, jnp.max(dmat, axis=1, keepdims=True))
        a = _dot_nt(q.astype(BF16), k.astype(BF16)) * jnp.exp(dmat - mt)
        inter = jnp.exp(g - mt)
        num = inter * _dot(q.astype(BF16), s_state.astype(BF16)) + _dot(a.astype(BF16), v.astype(BF16))
        den = inter * jnp.sum(q * nvec, axis=1, keepdims=True) + jnp.sum(a, axis=1, keepdims=True)
        h = num / jnp.maximum(jnp.abs(den), jnp.exp(-mt))
        ds = tot - cum_c + ig_c
        m_new = jnp.maximum(tot + m, jnp.max(ds, axis=0, keepdims=True))
        decay = jnp.exp(tot + m - m_new)
        kw = k * jnp.exp(ds - m_new)
        s_state = decay * s_state + _dot_tn(kw.astype(BF16), v.astype(BF16))
        nvec = decay * nvec + jnp.sum(kw, axis=0, keepdims=True)
        return sl, h, (s_state, nvec, m_new)

    init = (jnp.zeros((dk, dk), F32), jnp.zeros((1, dk), F32), jnp.full((1, 1), NEG_BIG, F32))

    def step(j, carry):
        carry_f, carry_b = carry
        sl, h, carry_f = chunk(j, carry_f, 0)
        hf_ref[sl, :] = h
        sl, h, carry_b = chunk(n - 1 - j, carry_b, 1)
        hb_ref[sl, :] = h
        return carry_f, carry_b

    lax.fori_loop(0, n, step, (init, init))

    for r0 in range(0, T, rt):
        sl = slice(r0, r0 + rt)
        o_ref[0, sl, :] = _rms(hf_ref[sl, :] + hb_ref[sl, :], gain_ref[...]) * _sigmoid(og_ref[0, sl, :])


def mlstm(y, gates, conv_w, conv_b, i_bias, f_bias, norm_gain):
    B, T, _ = y.shape
    H = GROUP_WIDTH // HEAD_LANES
    C = min(ML_CHUNK, T)
    t = np.arange(C)
    tri = np.stack([(t[None, :] <= t[:, None]), (t[None, :] >= t[:, None])]).astype(np.float32)
    gbias = jnp.zeros((1, HEAD_LANES), F32)
    gbias = gbias.at[0, 0:4].set(i_bias[0]).at[0, 4:8].set(f_bias[0])
    gbias = gbias.at[0, 8:12].set(i_bias[1]).at[0, 12:16].set(f_bias[1])

    def col(group):
        return pl.BlockSpec((1, T, HEAD_LANES), lambda b, h, group=group: (b, 0, group * H + h))

    conv_q = pl.BlockSpec((ML_CONV, HEAD_LANES), lambda b, h: (0, h))
    conv_k = pl.BlockSpec((ML_CONV, HEAD_LANES), lambda b, h: (0, H + h))
    bias_q = pl.BlockSpec((1, HEAD_LANES), lambda b, h: (0, h))
    bias_k = pl.BlockSpec((1, HEAD_LANES), lambda b, h: (0, H + h))
    cb = conv_b.reshape(1, -1)
    return pl.pallas_call(
        functools.partial(_mlstm_kernel, C=C, T=T, dk=HEAD_LANES),
        out_shape=jax.ShapeDtypeStruct((B, T, GROUP_WIDTH), F32),
        grid=(B, H),
        in_specs=[col(0), col(1), col(2), col(3),
                  pl.BlockSpec((1, T, HEAD_LANES), lambda b, h: (b, 0, 0)),
                  conv_q, conv_k, bias_q, bias_k,
                  pl.BlockSpec((1, HEAD_LANES), lambda b, h: (0, 0)),
                  pl.BlockSpec((1, HEAD_LANES), lambda b, h: (0, 0)),
                  pl.BlockSpec((2, C, C), lambda b, h: (0, 0, 0))],
        out_specs=pl.BlockSpec((1, T, HEAD_LANES), lambda b, h: (b, 0, h)),
        scratch_shapes=[pltpu.VMEM((T, HEAD_LANES), F32), pltpu.VMEM((T, HEAD_LANES), F32),
                        pltpu.VMEM((T + 16, HEAD_LANES), F32), pltpu.VMEM((T, HEAD_LANES), F32),
                        pltpu.VMEM((T, HEAD_LANES), F32), pltpu.VMEM((T, HEAD_LANES), F32)],
        compiler_params=pltpu.CompilerParams(
            dimension_semantics=("arbitrary", "arbitrary"), vmem_limit_bytes=VMEM_LIMIT),
        name="mlstm",
    )(y, y, y, y, gates, conv_w, conv_w, cb, cb, gbias, norm_gain.reshape(1, HEAD_LANES), jnp.asarray(tri))


def _na_bias_table(rpb, rows):
    kr = min(NA_ROWS, rows)
    c = np.arange(GRID_W)
    cstart = np.clip(c - NA_COLS // 2, 0, GRID_W - NA_COLS)
    kc = np.arange(GRID_W)
    valid = (kc[None, :] >= cstart[:, None]) & (kc[None, :] < cstart[:, None] + NA_COLS)
    coff = np.clip(kc[None, :] - c[:, None] + NA_COLS - 1, 0, 2 * NA_COLS - 2)
    di = np.arange(kr)
    i = np.arange(kr)
    roff = i[None, :] - di[:, None] + NA_ROWS - 1
    heads = rpb.shape[0]
    cols = jnp.where(jnp.asarray(valid)[None, None], rpb.astype(F32)[:, :, coff], NEG_BIG)
    tab = jnp.take(cols, jnp.asarray(roff.reshape(-1)), axis=1)
    tab = tab.reshape(heads, kr, kr, GRID_W, GRID_W).transpose(0, 1, 3, 2, 4)
    return tab.reshape(heads, kr, GRID_W, kr * GRID_W)


def _na_kernel(q_ref, k_ref, v_ref, bm_ref, o_ref, *, rows, kr):
    W = GRID_W
    heads_per_block = HEAD_LANES // NA_DIM

    def one_row(r):
        rs = jnp.clip(r - kr // 2, 0, rows - kr)
        di = r - rs
        qs = pl.ds(pl.multiple_of(r * W, W), W)
        ks = pl.ds(pl.multiple_of(rs * W, W), kr * W)
        q = (q_ref[0, qs, :] * (NA_DIM ** -0.5)).astype(BF16)
        kw = k_ref[0, ks, :].astype(BF16)
        vw = v_ref[0, ks, :].astype(BF16)
        outs = []
        for hh in range(heads_per_block):
            sl = slice(hh * NA_DIM, (hh + 1) * NA_DIM)
            s = _dot_nt(q[:, sl], kw[:, sl]) + bm_ref[hh, di]
            e = jnp.exp(s - jnp.max(s, axis=-1, keepdims=True))
            outs.append(_dot(e.astype(BF16), vw[:, sl]) / jnp.sum(e, axis=-1, keepdims=True))
        o_ref[0, qs, :] = jnp.concatenate(outs, axis=1)

    def body(j, _):
        one_row(2 * j)
        one_row(2 * j + 1)
        return 0

    lax.fori_loop(0, rows // 2, body, 0)


def neighbourhood_attention(y, rpb):
    B, T, _ = y.shape
    rows = T // GRID_W
    kr = min(NA_ROWS, rows)
    HB = GROUP_WIDTH // HEAD_LANES
    hpb = HEAD_LANES // NA_DIM
    bm = _na_bias_table(rpb, rows)

    def col(group):
        return pl.BlockSpec((1, T, HEAD_LANES), lambda b, h, group=group: (b, 0, group * HB + h))

    return pl.pallas_call(
        functools.partial(_na_kernel, rows=rows, kr=kr),
        out_shape=jax.ShapeDtypeStruct((B, T, GROUP_WIDTH), F32),
        grid=(B, HB),
        in_specs=[col(4), col(5), col(6),
                  pl.BlockSpec((hpb, kr, GRID_W, kr * GRID_W), lambda b, h: (h, 0, 0, 0))],
        out_specs=pl.BlockSpec((1, T, HEAD_LANES), lambda b, h: (b, 0, h)),
        compiler_params=pltpu.CompilerParams(
            dimension_semantics=("arbitrary", "arbitrary"), vmem_limit_bytes=VMEM_LIMIT),
        name="neighbourhood_attention",
    )(y, y, y, bm)


def _out_kernel(a_ref, b_ref, x_ref, mod_ref, gain_ref, w_ref, r_ref, x1_ref, h2_ref, lg_ref):
    G = a_ref.shape[-1]
    y = _dot(a_ref[0].astype(BF16), w_ref[0:G, :]) + _dot(b_ref[0].astype(BF16), w_ref[G:2 * G, :])
    x1 = x_ref[0] + mod_ref[0, 2:3, :] * y
    x1_ref[0] = x1
    h2 = _rms(x1, gain_ref[...]) * (1.0 + mod_ref[0, 4:5, :]) + mod_ref[0, 3:4, :]
    h2_ref[0] = h2
    lg_ref[...] = lax.dot_general(r_ref[...], h2, (((1,), (1,)), ((), ())), precision=HIGHEST,
                                  preferred_element_type=F32)


def out_proj(a_out, b_out, x, mod, gain2, w_out_bf16, router):
    B, T, D = x.shape
    G = a_out.shape[-1]
    E = router.shape[1]
    tm = min(512, T)
    nt = T // tm
    return pl.pallas_call(
        _out_kernel,
        out_shape=[jax.ShapeDtypeStruct((B, T, D), F32), jax.ShapeDtypeStruct((B, T, D), F32),
                   jax.ShapeDtypeStruct((E, B * T), F32)],
        grid=(B, T // tm),
        in_specs=[pl.BlockSpec((1, tm, G), lambda b, i: (b, i, 0)),
                  pl.BlockSpec((1, tm, G), lambda b, i: (b, i, 0)),
                  pl.BlockSpec((1, tm, D), lambda b, i: (b, i, 0)),
                  pl.BlockSpec((1, 6, D), lambda b, i: (b, 0, 0)),
                  pl.BlockSpec((1, D), lambda b, i: (0, 0)),
                  pl.BlockSpec((2 * G, D), lambda b, i: (0, 0)),
                  pl.BlockSpec((E, D), lambda b, i: (0, 0))],
        out_specs=[pl.BlockSpec((1, tm, D), lambda b, i: (b, i, 0)),
                   pl.BlockSpec((1, tm, D), lambda b, i: (b, i, 0)),
                   pl.BlockSpec((E, tm), lambda b, i, nt=nt: (0, b * nt + i))],
        compiler_params=pltpu.CompilerParams(
            dimension_semantics=("arbitrary", "arbitrary"), vmem_limit_bytes=VMEM_LIMIT),
        name="out_proj",
    )(a_out, b_out, x, mod, gain2.reshape(1, D), w_out_bf16, router.T)


def _route_kernel(lg_ref, bias_ref, idx_ref, w_ref, cnt_ref, *, tiles_per_group):
    @pl.when(pl.program_id(0) % tiles_per_group == 0)
    def _():
        cnt_ref[...] = jnp.zeros(cnt_ref.shape, F32)

    scores = _sigmoid(lg_ref[...])
    sel = scores + bias_ref[...]
    E, tm = sel.shape
    per_group = E // N_GROUPS
    neg = -jnp.inf
    eid = lax.broadcasted_iota(jnp.int32, (E, tm), 0).astype(F32)
    eid_g = lax.broadcasted_iota(jnp.int32, (per_group, tm), 0).astype(F32)

    def first_argmax(x, ids, sentinel):
        m = jnp.max(x, axis=0, keepdims=True)
        i = jnp.min(jnp.where(x == m, ids, sentinel), axis=0, keepdims=True)
        return m, i

    parts, gscore = [], []
    for g in range(N_GROUPS):
        x = sel[g * per_group:(g + 1) * per_group]
        parts.append(x)
        m1, i1 = first_argmax(x, eid_g, float(per_group))
        m2 = jnp.max(jnp.where(eid_g == i1, neg, x), axis=0, keepdims=True)
        gscore.append(m1 + m2)
    kept = []
    for g in range(N_GROUPS):
        beaten = jnp.zeros((1, tm), F32)
        for o in range(N_GROUPS):
            if o == g:
                continue
            wins = (gscore[o] >= gscore[g]) if o < g else (gscore[o] > gscore[g])
            beaten = beaten + wins.astype(F32)
        kept.append(jnp.where(beaten < TOPK_GROUPS, parts[g], neg))
    sel = jnp.concatenate(kept, axis=0)

    ids, vals = [], []
    w_sum = jnp.zeros((1, tm), F32)
    chosen = jnp.zeros((E, tm), F32)
    for k in range(TOP_K):
        _, i = first_argmax(sel, eid, float(E))
        hit = eid == i
        val = jnp.sum(jnp.where(hit, scores, 0.0), axis=0, keepdims=True)
        sel = jnp.where(hit, neg, sel)
        chosen = jnp.where(hit, 1.0, chosen)
        ids.append(i)
        vals.append(val)
        w_sum = w_sum + val
    idx_ref[...] = jnp.concatenate(ids, axis=0).astype(jnp.int32)
    w_ref[...] = jnp.concatenate(vals, axis=0) / w_sum * ROUTED_SCALE
    part = chosen[:, 0:HEAD_LANES]
    for l0 in range(HEAD_LANES, tm, HEAD_LANES):
        part = part + chosen[:, l0:l0 + HEAD_LANES]
    cnt_ref[0] = cnt_ref[0] + part


def route(logits_t, router_bias, group_tokens):
    E, N = logits_t.shape
    tm = min(512, N)
    tpg = group_tokens // tm
    idx, w, cnt = pl.pallas_call(
        functools.partial(_route_kernel, tiles_per_group=tpg),
        out_shape=[jax.ShapeDtypeStruct((TOP_K, N), jnp.int32), jax.ShapeDtypeStruct((TOP_K, N), F32),
                   jax.ShapeDtypeStruct((N // group_tokens, E, HEAD_LANES), F32)],
        grid=(N // tm,),
        in_specs=[pl.BlockSpec((E, tm), lambda i: (0, i)), pl.BlockSpec((E, 1), lambda i: (0, 0))],
        out_specs=[pl.BlockSpec((TOP_K, tm), lambda i: (0, i)), pl.BlockSpec((TOP_K, tm), lambda i: (0, i)),
                   pl.BlockSpec((1, E, HEAD_LANES), lambda i, tpg=tpg: (i // tpg, 0, 0))],
        compiler_params=pltpu.CompilerParams(dimension_semantics=("arbitrary",)),
        name="route",
    )(logits_t, router_bias.reshape(E, 1))
    return idx, w, jnp.sum(cnt, axis=-1).astype(jnp.int32)


def _moe_kernel(off_ref, tok_ref, wl_ref, x_ref, wg_ref, wu_ref, wd_ref, acc_ref, xg_ref, yb_ref,
                *, R, E, per_group):
    g = pl.program_id(0)
    e = pl.program_id(1)

    @pl.when(e == 0)
    def _():
        acc_ref[...] = jnp.zeros(acc_ref.shape, F32)

    seg = off_ref[g * E + e]
    cnt = off_ref[g * E + e + 1] - seg
    start = seg - g * per_group
    U = MOE_UNROLL
    last = per_group - 1

    def sub(sb, _):
        s0 = start + sb * R
        nr = jnp.minimum(R, cnt - sb * R)

        def gather(j, _):
            r0 = pl.multiple_of(j * U, U)
            for i in range(U):
                t = tok_ref[jnp.minimum(s0 + r0 + i, last)]
                xg_ref[pl.ds(r0 + i, 1), :] = x_ref[0, pl.ds(t, 1), :]
            return 0

        lax.fori_loop(0, R // U, gather, 0)
        xb = xg_ref[...].astype(BF16)
        hmid = _silu(_dot(xb, wg_ref[0])) * _dot(xb, wu_ref[0])
        yb_ref[...] = _dot(hmid.astype(BF16), wd_ref[0])

        def scatter_group(j, _):
            r0 = pl.multiple_of(j * U, U)
            toks = [tok_ref[s0 + r0 + i] for i in range(U)]
            wts = [wl_ref[s0 + r0 + i] for i in range(U)]
            new = [acc_ref[0, pl.ds(toks[i], 1), :] + wts[i] * yb_ref[pl.ds(r0 + i, 1), :] for i in range(U)]
            for i in range(U):
                acc_ref[0, pl.ds(toks[i], 1), :] = new[i]
            return 0

        groups = nr // U
        lax.fori_loop(0, groups, scatter_group, 0)

        def scatter_row(r, _):
            t = tok_ref[s0 + r]
            acc_ref[0, pl.ds(t, 1), :] = acc_ref[0, pl.ds(t, 1), :] + wl_ref[s0 + r] * yb_ref[pl.ds(r, 1), :]
            return 0

        lax.fori_loop(groups * U, nr, scatter_row, 0)
        return 0

    lax.fori_loop(0, (cnt + R - 1) // R, sub, 0)


def routed_experts(h2, top_idx, top_w, counts, exp_gate, exp_up, exp_down):
    N, D = h2.shape
    E, _, F = exp_gate.shape
    TG = min(MOE_TOKEN_GROUP, N)
    G = N // TG
    per_group = TG * TOP_K
    key = (jnp.arange(N, dtype=jnp.int32)[None, :] // TG) * E + top_idx
    order = jnp.argsort(key.reshape(-1))
    tok_s = ((order % N) % TG).astype(jnp.int32)
    w_s = top_w.reshape(-1)[order]
    off = jnp.concatenate([jnp.zeros((1,), jnp.int32), jnp.cumsum(counts.reshape(-1)).astype(jnp.int32)])

    grid_spec = pltpu.PrefetchScalarGridSpec(
        num_scalar_prefetch=1,
        grid=(G, E),
        in_specs=[pl.BlockSpec((per_group,), lambda g, e, off: (g,), memory_space=pltpu.SMEM),
                  pl.BlockSpec((per_group,), lambda g, e, off: (g,), memory_space=pltpu.SMEM),
                  pl.BlockSpec((1, TG, D), lambda g, e, off: (g, 0, 0)),
                  pl.BlockSpec((1, D, F), lambda g, e, off: (e, 0, 0)),
                  pl.BlockSpec((1, D, F), lambda g, e, off: (e, 0, 0)),
                  pl.BlockSpec((1, F, D), lambda g, e, off: (e, 0, 0))],
        out_specs=pl.BlockSpec((1, TG, D), lambda g, e, off: (g, 0, 0)),
        scratch_shapes=[pltpu.VMEM((MOE_ROWS, D), F32), pltpu.VMEM((MOE_ROWS, D), F32)],
    )
    out = pl.pallas_call(
        functools.partial(_moe_kernel, R=MOE_ROWS, E=E, per_group=per_group),
        out_shape=jax.ShapeDtypeStruct((G, TG, D), F32),
        grid_spec=grid_spec,
        compiler_params=pltpu.CompilerParams(
            dimension_semantics=("arbitrary", "arbitrary"), vmem_limit_bytes=VMEM_LIMIT),
        name="routed_experts",
    )(off, tok_s, w_s, h2.reshape(G, TG, D), exp_gate, exp_up, exp_down)
    return out.reshape(N, D)


def _final_kernel(x1_ref, h2_ref, rt_ref, mod_ref, wg_ref, wu_ref, wd_ref, *rest, final):
    if final:
        fg_ref, o_ref = rest
    else:
        (o_ref,) = rest
    hb = h2_ref[0].astype(BF16)
    hmid = _silu(_dot(hb, wg_ref[...])) * _dot(hb, wu_ref[...])
    shared = _dot(hmid.astype(BF16), wd_ref[...])
    x2 = x1_ref[0] + mod_ref[0, 5:6, :] * (rt_ref[0] + shared)
    if final:
        x2 = _rms(x2, fg_ref[...])
    o_ref[0] = x2


def shared_and_residual(x1, h2, routed, mod, sh_gate, sh_up, sh_down, final_gain=None):
    B, T, D = x1.shape
    F = sh_gate.shape[1]
    tm = min(512, T)
    final = final_gain is not None
    tile = pl.BlockSpec((1, tm, D), lambda b, i: (b, i, 0))
    in_specs = [tile, tile, tile,
                pl.BlockSpec((1, 6, D), lambda b, i: (b, 0, 0)),
                pl.BlockSpec((D, F), lambda b, i: (0, 0)),
                pl.BlockSpec((D, F), lambda b, i: (0, 0)),
                pl.BlockSpec((F, D), lambda b, i: (0, 0))]
    args = [x1, h2, routed, mod, sh_gate, sh_up, sh_down]
    if final:
        in_specs.append(pl.BlockSpec((1, D), lambda b, i: (0, 0)))
        args.append(final_gain.reshape(1, D))
    return pl.pallas_call(
        functools.partial(_final_kernel, final=final),
        out_shape=jax.ShapeDtypeStruct((B, T, D), F32),
        grid=(B, T // tm),
        in_specs=in_specs,
        out_specs=tile,
        compiler_params=pltpu.CompilerParams(
            dimension_semantics=("arbitrary", "arbitrary"), vmem_limit_bytes=VMEM_LIMIT),
        name="shared_and_residual",
    )(*args)


def moe_block(x1, h2, logits, mod, router_bias, exp_gate, exp_up, exp_down, sh_gate, sh_up, sh_down,
              final_gain=None):
    B, T, D = x1.shape
    top_idx, top_w, counts = route(logits, router_bias, min(MOE_TOKEN_GROUP, B * T))
    routed = routed_experts(h2.reshape(B * T, D), top_idx, top_w, counts,
                            exp_gate.astype(BF16), exp_up.astype(BF16), exp_down.astype(BF16))
    return shared_and_residual(x1, h2, routed.reshape(B, T, D), mod,
                               sh_gate.astype(BF16), sh_up.astype(BF16), sh_down.astype(BF16), final_gain)


def kernel(x, c, hgrn_lb_logits, l0_norm1, l0_norm2, l0_w_mod, l0_b_mod, l0_w_in, l0_w_out, l0_hgrn_norm, l0_diff_lq1, l0_diff_lk1, l0_diff_lq2, l0_diff_lk2, l0_diff_subln, l0_router, l0_router_bias, l0_exp_gate, l0_exp_up, l0_exp_down, l0_sh_gate, l0_sh_up, l0_sh_down, l1_norm1, l1_norm2, l1_w_mod, l1_b_mod, l1_w_in, l1_w_out, l1_conv_w, l1_conv_b, l1_ml_i_bias, l1_ml_f_bias, l1_ml_norm, l1_na_rpb, l1_router, l1_router_bias, l1_exp_gate, l1_exp_up, l1_exp_down, l1_sh_gate, l1_sh_up, l1_sh_down, final_norm):
    G = GROUP_WIDTH
    lb_all = jnp.cumsum(jax.nn.softmax(hgrn_lb_logits.astype(F32), axis=0), axis=0)
    layer_idx = 0
    lambda_init = 0.8 - 0.6 * math.exp(-0.3 * layer_idx)
    lam = (jnp.exp(jnp.sum(l0_diff_lq1.astype(F32) * l0_diff_lk1.astype(F32)))
           - jnp.exp(jnp.sum(l0_diff_lq2.astype(F32) * l0_diff_lk2.astype(F32))) + lambda_init)

    mod0 = ada_mod(c, l0_w_mod, l0_b_mod)
    y0 = in_proj(x, mod0, l0_norm1, l0_w_in.astype(BF16))
    a_out = hgrn2(y0, lb_all[0], l0_hgrn_norm)
    b_out = diff_attention(y0, lam, l0_diff_subln, layer_idx)
    x1, h2, logits = out_proj(a_out, b_out, x, mod0, l0_norm2, l0_w_out.astype(BF16), l0_router)
    xa = moe_block(x1, h2, logits, mod0, l0_router_bias, l0_exp_gate, l0_exp_up, l0_exp_down,
                   l0_sh_gate, l0_sh_up, l0_sh_down)

    mod1 = ada_mod(c, l1_w_mod, l1_b_mod)
    n_gate = l1_w_in.shape[1] - 7 * G
    w_main = jnp.concatenate([l1_w_in[:, :4 * G], l1_w_in[:, 4 * G + n_gate:]], axis=1).astype(BF16)
    w_gate = jnp.pad(l1_w_in[:, 4 * G:4 * G + n_gate], ((0, 0), (0, HEAD_LANES - n_gate)))
    y1, gates = in_proj(xa, mod1, l1_norm1, w_main, w_gate)
    c_out = mlstm(y1, gates, l1_conv_w, l1_conv_b, l1_ml_i_bias, l1_ml_f_bias, l1_ml_norm)
    d_out = neighbourhood_attention(y1, l1_na_rpb)
    x1, h2, logits = out_proj(c_out, d_out, xa, mod1, l1_norm2, l1_w_out.astype(BF16), l1_router)
    return moe_block(x1, h2, logits, mod1, l1_router_bias, l1_exp_gate, l1_exp_up, l1_exp_down,
                     l1_sh_gate, l1_sh_up, l1_sh_down, final_gain=final_norm)
```

```python
import functools
import math

import numpy as np
import jax
import jax.numpy as jnp
from jax import lax
from jax.experimental import pallas as pl
from jax.experimental.pallas import tpu as pltpu

F32 = jnp.float32
BF16 = jnp.bfloat16
HIGHEST = lax.Precision.HIGHEST
EPS = 1e-6

GRID_W = 64
GROUP_WIDTH = 512
HEAD_LANES = 128
HG_CHUNK = 64
ML_CHUNK = 128
ML_CONV = 5
DA_DIM = 64
ROPE_DIM = 16
ROPE_THETA = 500000.0
NA_ROWS = 8
NA_COLS = 16
NA_DIM = 64
N_EXPERTS = 128
TOP_K = 8
N_GROUPS = 8
TOPK_GROUPS = 4
ROUTED_SCALE = 2.5
MOE_ROWS = 256
MOE_TOKEN_GROUP = 4096
SUBLANES = 8
NEG_BIG = -1e30
VMEM_LIMIT = 48 * 1024 * 1024


def _dot(a, b, **kw):
    return jnp.dot(a, b, preferred_element_type=F32, **kw)


def _dot_nt(a, b):
    return lax.dot_general(a, b, (((1,), (1,)), ((), ())), preferred_element_type=F32)


def _dot_tn(a, b):
    return lax.dot_general(a, b, (((0,), (0,)), ((), ())), preferred_element_type=F32)


def _sigmoid(x):
    return jax.nn.sigmoid(x)


def _silu(x):
    return x * jax.nn.sigmoid(x)


def _log_sigmoid(x):
    return jnp.minimum(x, 0.0) - jnp.log(1.0 + jnp.exp(-jnp.abs(x)))


def _rms(x, gain):
    return x * lax.rsqrt(jnp.mean(x * x, axis=-1, keepdims=True) + EPS) * gain


def _store_row_tiles(ref, val):
    for c in range(ref.shape[1]):
        ref[:, c, :] = val[:, c * HEAD_LANES:(c + 1) * HEAD_LANES]


def _load_row_tiles(ref):
    return jnp.concatenate([ref[:, c, :] for c in range(ref.shape[1])], axis=1)


def _mod_kernel(c_ref, w_ref, b_ref, o_ref):
    o_ref[...] = _dot(_silu(c_ref[...]), w_ref[...], precision=HIGHEST) + b_ref[...]


def ada_mod(c, w_mod, b_mod):
    B, D = c.shape
    N = w_mod.shape[1]
    tn = 1024
    out = pl.pallas_call(
        _mod_kernel,
        out_shape=jax.ShapeDtypeStruct((B, N), F32),
        grid=(N // tn,),
        in_specs=[pl.BlockSpec((B, D), lambda j: (0, 0)),
                  pl.BlockSpec((D, tn), lambda j: (0, j)),
                  pl.BlockSpec((1, tn), lambda j: (0, j))],
        out_specs=pl.BlockSpec((B, tn), lambda j: (0, j)),
        name="ada_mod",
    )(c, w_mod, b_mod.reshape(1, N))
    return out.reshape(B, 6, D)


def _in_kernel(x_ref, mod_ref, gain_ref, w_ref, *rest, has_gate):
    if has_gate:
        wg_ref, o_ref, og_ref, h_ref = rest
    else:
        o_ref, h_ref = rest

    @pl.when(pl.program_id(2) == 0)
    def _():
        h = _rms(x_ref[0], gain_ref[...]) * (1.0 + mod_ref[0, 1:2, :]) + mod_ref[0, 0:1, :]
        h_ref[...] = h.astype(BF16)
        if has_gate:
            og_ref[0] = _dot(h, wg_ref[...], precision=HIGHEST)

    o_ref[0] = _dot(h_ref[...], w_ref[...])


def in_proj(x, mod, gain, w_bf16, w_gate=None):
    B, T, D = x.shape
    N = w_bf16.shape[1]
    tm = min(512, T)
    tn = 512
    has_gate = w_gate is not None
    in_specs = [pl.BlockSpec((1, tm, D), lambda b, i, j: (b, i, 0)),
                pl.BlockSpec((1, 6, D), lambda b, i, j: (b, 0, 0)),
                pl.BlockSpec((1, D), lambda b, i, j: (0, 0)),
                pl.BlockSpec((D, tn), lambda b, i, j: (0, j))]
    out_shape = [jax.ShapeDtypeStruct((B, T, N), F32)]
    out_specs = [pl.BlockSpec((1, tm, tn), lambda b, i, j: (b, i, j))]
    args = [x, mod, gain.reshape(1, D), w_bf16]
    if has_gate:
        in_specs.append(pl.BlockSpec((D, HEAD_LANES), lambda b, i, j: (0, 0)))
        out_shape.append(jax.ShapeDtypeStruct((B, T, HEAD_LANES), F32))
        out_specs.append(pl.BlockSpec((1, tm, HEAD_LANES), lambda b, i, j: (b, i, 0)))
        args.append(w_gate)
    res = pl.pallas_call(
        functools.partial(_in_kernel, has_gate=has_gate),
        out_shape=out_shape,
        grid=(B, T // tm, N // tn),
        in_specs=in_specs,
        out_specs=out_specs,
        scratch_shapes=[pltpu.VMEM((tm, D), BF16)],
        compiler_params=pltpu.CompilerParams(
            dimension_semantics=("arbitrary", "arbitrary", "arbitrary")),
        name="in_proj",
    )(*args)
    return res if has_gate else res[0]


def _hgrn_consts(C):
    t = np.arange(C)
    tri = (t[None, :] <= t[:, None]).astype(np.float32)
    triT = np.ascontiguousarray(tri.T)
    wf, wb, mf = [tri], [triT], []
    levels = int(round(math.log2(C)))
    for l in range(levels):
        size = C >> l
        blk = t // size
        r = blk * size + size // 2
        wf.append(tri - tri[r - 1])
        wb.append(triT - triT[r])
        upper = (t % size) >= size // 2
        mf.append(((blk[:, None] == blk[None, :]) & upper[:, None] & (~upper)[None, :]).astype(np.float32))
    mf.append(np.eye(C, dtype=np.float32))
    ones = np.ones((8, C), np.float32)
    wf.append(ones)
    wb.append(ones)
    mf = np.stack(mf)
    mb = np.ascontiguousarray(np.transpose(mf, (0, 2, 1)))
    return np.concatenate(wf), np.concatenate(wb), mf, mb


def _split3(x):
    hi = x.astype(BF16)
    r1 = x - hi.astype(F32)
    mid = r1.astype(BF16)
    lo = (r1 - mid.astype(F32)).astype(BF16)
    return hi, mid, lo


def _hgrn_kernel(q_ref, i_ref, ff_ref, fb_ref, g_ref, lb_ref, gain_ref, wf_ref, wb_ref, mf_ref, mb_ref,
                 o_ref, of_ref, ob_ref, *, C, T):
    n = T // C
    levels = int(round(math.log2(C)))
    dv = q_ref.shape[-1]

    def chunk(c, st, f_ref, lbd, w_ref, m_ref):
        sl = pl.ds(pl.multiple_of(c * C, C), C)
        q = _silu(q_ref[0, sl, :])
        v = i_ref[0, sl, :]
        fg = lbd + (1.0 - lbd) * _sigmoid(f_ref[0, sl, :])
        k = 1.0 - fg
        lf = jnp.log(fg)
        d3 = _dot(w_ref[...], jnp.concatenate(_split3(lf), axis=1))
        dall = d3[:, 0:dv] + d3[:, dv:2 * dv] + d3[:, 2 * dv:3 * dv]
        cum = dall[0:C]
        tot = dall[(levels + 1) * C:(levels + 1) * C + 1]
        attn = m_ref[levels] * _dot_nt(q.astype(BF16), k.astype(BF16))
        for l in range(levels):
            e = jnp.exp(-jnp.abs(dall[(l + 1) * C:(l + 2) * C]))
            attn = attn + m_ref[l] * _dot_nt((q * e).astype(BF16), (k * e).astype(BF16))
        o = _dot(attn.astype(BF16), v.astype(BF16))
        o = o + _dot_nt((q * jnp.exp(cum)).astype(BF16), st.astype(BF16))
        kt = k * jnp.exp(tot - cum)
        st = st * jnp.exp(tot) + _dot_tn(v.astype(BF16), kt.astype(BF16))
        return sl, o, st

    st0 = jnp.zeros((dv, dv), F32)

    def step(j, carry):
        st_f, st_b = carry
        sl, o, st_f = chunk(j, st_f, ff_ref, lb_ref[0:1, :], wf_ref, mf_ref)
        of_ref[sl, :] = o
        sl, o, st_b = chunk(n - 1 - j, st_b, fb_ref, lb_ref[1:2, :], wb_ref, mb_ref)
        ob_ref[sl, :] = o
        return st_f, st_b

    lax.fori_loop(0, n, step, (st0, st0))

    rt = min(512, T)
    for r0 in range(0, T, rt):
        sl = slice(r0, r0 + rt)
        o_ref[0, sl, :] = _rms(of_ref[sl, :] + ob_ref[sl, :], gain_ref[...]) * _silu(g_ref[0, sl, :])


def hgrn2(y, lb, norm_gain):
    B, T, _ = y.shape
    H = GROUP_WIDTH // HEAD_LANES
    C = HG_CHUNK
    wf, wb, mf, mb = _hgrn_consts(C)
    wf, wb = jnp.asarray(wf, BF16), jnp.asarray(wb, BF16)
    mf, mb = jnp.asarray(mf), jnp.asarray(mb)

    def col(group):
        return pl.BlockSpec((1, T, HEAD_LANES), lambda b, h, group=group: (b, 0, group * H + h))

    def const(a):
        nd = a.ndim
        return pl.BlockSpec(a.shape, lambda b, h, nd=nd: (0,) * nd)

    return pl.pallas_call(
        functools.partial(_hgrn_kernel, C=C, T=T),
        out_shape=jax.ShapeDtypeStruct((B, T, GROUP_WIDTH), F32),
        grid=(B, H),
        in_specs=[col(0), col(1), col(2), col(3), col(4),
                  pl.BlockSpec((2, HEAD_LANES), lambda b, h: (0, h)),
                  pl.BlockSpec((1, HEAD_LANES), lambda b, h: (0, 0)),
                  const(wf), const(wb), const(mf), const(mb)],
        out_specs=pl.BlockSpec((1, T, HEAD_LANES), lambda b, h: (b, 0, h)),
        scratch_shapes=[pltpu.VMEM((T, HEAD_LANES), F32), pltpu.VMEM((T, HEAD_LANES), F32)],
        compiler_params=pltpu.CompilerParams(
            dimension_semantics=("arbitrary", "arbitrary"), vmem_limit_bytes=VMEM_LIMIT),
        name="hgrn2",
    )(y, y, y, y, y, lb, norm_gain.reshape(1, HEAD_LANES), wf, wb, mf, mb)


def _rope_tables(T):
    pos = np.arange(T, dtype=np.float32)
    inv_freq = (ROPE_THETA ** (-np.arange(0, ROPE_DIM, 2, dtype=np.float32) / ROPE_DIM)).astype(np.float32)
    ang = pos[:, None] * inv_freq[None, :]
    cos, sin = np.cos(ang), np.sin(ang)
    half = ROPE_DIM // 2
    c = np.ones((T, HEAD_LANES), np.float32)
    s_prev = np.zeros((T, HEAD_LANES), np.float32)
    s_next = np.zeros((T, HEAD_LANES), np.float32)
    for base in range(0, HEAD_LANES, DA_DIM):
        c[:, base:base + half] = cos
        c[:, base + half:base + ROPE_DIM] = cos
        s_next[:, base:base + half] = -sin
        s_prev[:, base + half:base + ROPE_DIM] = sin
    return jnp.asarray(c), jnp.asarray(s_prev), jnp.asarray(s_next)


def _rope(x, c, s_prev, s_next):
    half = ROPE_DIM // 2
    lanes = x.shape[-1]
    return (x * c + pltpu.roll(x, half, axis=1) * s_prev
            + pltpu.roll(x, lanes - half, axis=1) * s_next)


def _diff_kernel(lam_ref, q_ref, k_ref, v_ref, cq_ref, spq_ref, snq_ref, ck_ref, spk_ref, snk_ref,
                 subln_ref, o_ref, kr_ref, vb_ref, *, T, out_scale):
    rt = min(512, T)

    @pl.when(pl.program_id(2) == 0)
    def _():
        for r0 in range(0, T, rt):
            sl = slice(r0, r0 + rt)
            kr_ref[sl, :] = _rope(k_ref[0, sl, :], ck_ref[sl, :], spk_ref[sl, :], snk_ref[sl, :]).astype(BF16)
            vb_ref[sl, :] = v_ref[0, sl, :].astype(BF16)

    q = (_rope(q_ref[0], cq_ref[...], spq_ref[...], snq_ref[...]) * (DA_DIM ** -0.5)).astype(BF16)
    lam = lam_ref[0, 0]
    v = vb_ref[...]

    def softmax_pv(m):
        sl = slice(m * DA_DIM, (m + 1) * DA_DIM)
        s = _dot_nt(q[:, sl], kr_ref[:, sl])
        e = jnp.exp(s - jnp.max(s, axis=-1, keepdims=True))
        return _dot(e.astype(BF16), v) / jnp.sum(e, axis=-1, keepdims=True)

    o = softmax_pv(0) - lam * softmax_pv(1)
    o_ref[0] = _rms(o, subln_ref[...]) * out_scale


def diff_attention(y, lam, subln, layer_idx):
    B, T, _ = y.shape
    H = GROUP_WIDTH // HEAD_LANES
    tq = min(256, T)
    lambda_init = 0.8 - 0.6 * math.exp(-0.3 * layer_idx)
    c, sp, sn = _rope_tables(T)

    def col(group, rows):
        if rows == T:
            return pl.BlockSpec((1, T, HEAD_LANES), lambda b, h, i, group=group: (b, 0, group * H + h))
        return pl.BlockSpec((1, rows, HEAD_LANES), lambda b, h, i, group=group: (b, i, group * H + h))

    tab_q = pl.BlockSpec((tq, HEAD_LANES), lambda b, h, i: (i, 0))
    tab_k = pl.BlockSpec((T, HEAD_LANES), lambda b, h, i: (0, 0))
    return pl.pallas_call(
        functools.partial(_diff_kernel, T=T, out_scale=1.0 - lambda_init),
        out_shape=jax.ShapeDtypeStruct((B, T, GROUP_WIDTH), F32),
        grid=(B, H, T // tq),
        in_specs=[pl.BlockSpec(memory_space=pltpu.SMEM),
                  col(5, tq), col(6, T), col(7, T),
                  tab_q, tab_q, tab_q, tab_k, tab_k, tab_k,
                  pl.BlockSpec((1, HEAD_LANES), lambda b, h, i: (0, 0))],
        out_specs=pl.BlockSpec((1, tq, HEAD_LANES), lambda b, h, i: (b, i, h)),
        scratch_shapes=[pltpu.VMEM((T, HEAD_LANES), BF16), pltpu.VMEM((T, HEAD_LANES), BF16)],
        compiler_params=pltpu.CompilerParams(
            dimension_semantics=("arbitrary", "arbitrary", "arbitrary"), vmem_limit_bytes=VMEM_LIMIT),
        name="diff_attention",
    )(lam.reshape(1, 1), y, y, y, c, sp, sn, c, sp, sn, subln.reshape(1, HEAD_LANES))


def _mlstm_kernel(q_ref, k_ref, v_ref, og_ref, gt_ref, cwq_ref, cwk_ref, cbq_ref, cbk_ref, gbias_ref,
                  gain_ref, tri_ref, o_ref, qc_ref, kc_ref, xp_ref, gx_ref, hf_ref, hb_ref, *, C, T, dk):
    n = T // C
    head = pl.program_id(1)
    pad = 8
    half = ML_CONV // 2
    rt = min(512, T)

    xp_ref[0:pad, :] = jnp.zeros((pad, dk), F32)
    xp_ref[pad + T:pad + T + pad, :] = jnp.zeros((pad, dk), F32)
    for src, cw_ref, cb_ref, dst, scale in ((q_ref, cwq_ref, cbq_ref, qc_ref, 1.0),
                                            (k_ref, cwk_ref, cbk_ref, kc_ref, dk ** -0.5)):
        xp_ref[pad:pad + T, :] = src[0]
        for r0 in range(0, T, rt):
            acc = jnp.zeros((rt, dk), F32) + cb_ref[...]
            for j in range(ML_CONV):
                acc = acc + xp_ref[pad + r0 + j - half:pad + r0 + j - half + rt, :] * cw_ref[j:j + 1, :]
            dst[r0:r0 + rt, :] = _silu(acc) * scale

    lane = lax.broadcasted_iota(jnp.int32, (rt, HEAD_LANES), 1)
    is_f = (lane % 8) >= 4
    for r0 in range(0, T, rt):
        g = gt_ref[0, r0:r0 + rt, :] + gbias_ref[...]
        p = jnp.where(is_f, _log_sigmoid(g), g)
        x = jnp.zeros((rt, HEAD_LANES), F32)
        for j, src_lane in enumerate((0, 4, 8, 12)):
            colv = jnp.sum(jnp.where(lane == src_lane + head, p, 0.0), axis=1, keepdims=True)
            x = jnp.where(lane == j, colv, x)
        gx_ref[r0:r0 + rt, :] = x

    row = lax.broadcasted_iota(jnp.int32, (C, C), 0)
    colm = lax.broadcasted_iota(jnp.int32, (C, C), 1)

    def chunk(c, carry, d):
        s_state, nvec, m = carry
        sl = pl.ds(pl.multiple_of(c * C, C), C)
        q = qc_ref[sl, :]
        k = kc_ref[sl, :]
        v = v_ref[0, sl, :]
        x = gx_ref[sl, :]
        tri = tri_ref[d]
        mask = (colm <= row) if d == 0 else (colm >= row)
        cumx = _dot(tri, x, precision=HIGHEST)
        xt = x.T
        cumxt = cumx.T
        ig_c = x[:, 2 * d:2 * d + 1]
        ig_r = xt[2 * d:2 * d + 1, :]
        cum_c = cumx[:, 2 * d + 1:2 * d + 2]
        cum_r = cumxt[2 * d + 1:2 * d + 2, :]
        tot = jnp.sum(x[:, 2 * d + 1:2 * d + 2], axis=0, keepdims=True)
        dmat = jnp.where(mask, cum_c - cum_r + ig_r, -jnp.inf)
        g = cum_c + m
        mt = jnp.maximum(g, jnp.max(dmat, axis=1, keepdims=True))
        a = _dot_nt(q.astype(BF16), k.astype(BF16)) * jnp.exp(dmat - mt)
        inter = jnp.exp(g - mt)
        num = inter * _dot(q.astype(BF16), s_state.astype(BF16)) + _dot(a.astype(BF16), v.astype(BF16))
        den = inter * jnp.sum(q * nvec, axis=1, keepdims=True) + jnp.sum(a, axis=1, keepdims=True)
        h = num / jnp.maximum(jnp.abs(den), jnp.exp(-mt))
        ds = tot - cum_c + ig_c
        m_new = jnp.maximum(tot + m, jnp.max(ds, axis=0, keepdims=True))
        decay = jnp.exp(tot + m - m_new)
        kw = k * jnp.exp(ds - m_new)
        s_state = decay * s_state + _dot_tn(kw.astype(BF16), v.astype(BF16))
        nvec = decay * nvec + jnp.sum(kw, axis=0, keepdims=True)
        return sl, h, (s_state, nvec, m_new)

    init = (jnp.zeros((dk, dk), F32), jnp.zeros((1, dk), F32), jnp.full((1, 1), NEG_BIG, F32))

    def step(j, carry):
        carry_f, carry_b = carry
        sl, h, carry_f = chunk(j, carry_f, 0)
        hf_ref[sl, :] = h
        sl, h, carry_b = chunk(n - 1 - j, carry_b, 1)
        hb_ref[sl, :] = h
        return carry_f, carry_b

    lax.fori_loop(0, n, step, (init, init))

    for r0 in range(0, T, rt):
        sl = slice(r0, r0 + rt)
        o_ref[0, sl, :] = _rms(hf_ref[sl, :] + hb_ref[sl, :], gain_ref[...]) * _sigmoid(og_ref[0, sl, :])


def mlstm(y, gates, conv_w, conv_b, i_bias, f_bias, norm_gain):
    B, T, _ = y.shape
    H = GROUP_WIDTH // HEAD_LANES
    C = min(ML_CHUNK, T)
    t = np.arange(C)
    tri = np.stack([(t[None, :] <= t[:, None]), (t[None, :] >= t[:, None])]).astype(np.float32)
    gbias = jnp.zeros((1, HEAD_LANES), F32)
    gbias = gbias.at[0, 0:4].set(i_bias[0]).at[0, 4:8].set(f_bias[0])
    gbias = gbias.at[0, 8:12].set(i_bias[1]).at[0, 12:16].set(f_bias[1])

    def col(group):
        return pl.BlockSpec((1, T, HEAD_LANES), lambda b, h, group=group: (b, 0, group * H + h))

    conv_q = pl.BlockSpec((ML_CONV, HEAD_LANES), lambda b, h: (0, h))
    conv_k = pl.BlockSpec((ML_CONV, HEAD_LANES), lambda b, h: (0, H + h))
    bias_q = pl.BlockSpec((1, HEAD_LANES), lambda b, h: (0, h))
    bias_k = pl.BlockSpec((1, HEAD_LANES), lambda b, h: (0, H + h))
    cb = conv_b.reshape(1, -1)
    return pl.pallas_call(
        functools.partial(_mlstm_kernel, C=C, T=T, dk=HEAD_LANES),
        out_shape=jax.ShapeDtypeStruct((B, T, GROUP_WIDTH), F32),
        grid=(B, H),
        in_specs=[col(0), col(1), col(2), col(3),
                  pl.BlockSpec((1, T, HEAD_LANES), lambda b, h: (b, 0, 0)),
                  conv_q, conv_k, bias_q, bias_k,
                  pl.BlockSpec((1, HEAD_LANES), lambda b, h: (0, 0)),
                  pl.BlockSpec((1, HEAD_LANES), lambda b, h: (0, 0)),
                  pl.BlockSpec((2, C, C), lambda b, h: (0, 0, 0))],
        out_specs=pl.BlockSpec((1, T, HEAD_LANES), lambda b, h: (b, 0, h)),
        scratch_shapes=[pltpu.VMEM((T, HEAD_LANES), F32), pltpu.VMEM((T, HEAD_LANES), F32),
                        pltpu.VMEM((T + 16, HEAD_LANES), F32), pltpu.VMEM((T, HEAD_LANES), F32),
                        pltpu.VMEM((T, HEAD_LANES), F32), pltpu.VMEM((T, HEAD_LANES), F32)],
        compiler_params=pltpu.CompilerParams(
            dimension_semantics=("arbitrary", "arbitrary"), vmem_limit_bytes=VMEM_LIMIT),
        name="mlstm",
    )(y, y, y, y, gates, conv_w, conv_w, cb, cb, gbias, norm_gain.reshape(1, HEAD_LANES), jnp.asarray(tri))


def _na_bias_table(rpb, rows):
    kr = min(NA_ROWS, rows)
    c = np.arange(GRID_W)
    cstart = np.clip(c - NA_COLS // 2, 0, GRID_W - NA_COLS)
    kc = np.arange(GRID_W)
    valid = (kc[None, :] >= cstart[:, None]) & (kc[None, :] < cstart[:, None] + NA_COLS)
    coff = np.clip(kc[None, :] - c[:, None] + NA_COLS - 1, 0, 2 * NA_COLS - 2)
    di = np.arange(kr)
    i = np.arange(kr)
    roff = i[None, :] - di[:, None] + NA_ROWS - 1
    heads = rpb.shape[0]
    cols = jnp.where(jnp.asarray(valid)[None, None], rpb.astype(F32)[:, :, coff], NEG_BIG)
    tab = jnp.take(cols, jnp.asarray(roff.reshape(-1)), axis=1)
    tab = tab.reshape(heads, kr, kr, GRID_W, GRID_W).transpose(0, 1, 3, 2, 4)
    return tab.reshape(heads, kr, GRID_W, kr * GRID_W)


def _na_kernel(q_ref, k_ref, v_ref, bm_ref, o_ref, *, rows, kr):
    W = GRID_W
    heads_per_block = HEAD_LANES // NA_DIM

    def one_row(r):
        rs = jnp.clip(r - kr // 2, 0, rows - kr)
        di = r - rs
        qs = pl.ds(pl.multiple_of(r * W, W), W)
        ks = pl.ds(pl.multiple_of(rs * W, W), kr * W)
        q = (q_ref[0, qs, :] * (NA_DIM ** -0.5)).astype(BF16)
        kw = k_ref[0, ks, :].astype(BF16)
        vw = v_ref[0, ks, :].astype(BF16)
        outs = []
        for hh in range(heads_per_block):
            sl = slice(hh * NA_DIM, (hh + 1) * NA_DIM)
            s = _dot_nt(q[:, sl], kw[:, sl]) + bm_ref[hh, di]
            e = jnp.exp(s - jnp.max(s, axis=-1, keepdims=True))
            outs.append(_dot(e.astype(BF16), vw[:, sl]) / jnp.sum(e, axis=-1, keepdims=True))
        o_ref[0, qs, :] = jnp.concatenate(outs, axis=1)

    def body(j, _):
        one_row(2 * j)
        one_row(2 * j + 1)
        return 0

    lax.fori_loop(0, rows // 2, body, 0)


def neighbourhood_attention(y, rpb):
    B, T, _ = y.shape
    rows = T // GRID_W
    kr = min(NA_ROWS, rows)
    HB = GROUP_WIDTH // HEAD_LANES
    hpb = HEAD_LANES // NA_DIM
    bm = _na_bias_table(rpb, rows)

    def col(group):
        return pl.BlockSpec((1, T, HEAD_LANES), lambda b, h, group=group: (b, 0, group * HB + h))

    return pl.pallas_call(
        functools.partial(_na_kernel, rows=rows, kr=kr),
        out_shape=jax.ShapeDtypeStruct((B, T, GROUP_WIDTH), F32),
        grid=(B, HB),
        in_specs=[col(4), col(5), col(6),
                  pl.BlockSpec((hpb, kr, GRID_W, kr * GRID_W), lambda b, h: (h, 0, 0, 0))],
        out_specs=pl.BlockSpec((1, T, HEAD_LANES), lambda b, h: (b, 0, h)),
        compiler_params=pltpu.CompilerParams(
            dimension_semantics=("arbitrary", "arbitrary"), vmem_limit_bytes=VMEM_LIMIT),
        name="neighbourhood_attention",
    )(y, y, y, bm)


def _out_kernel(a_ref, b_ref, x_ref, mod_ref, gain_ref, w_ref, r_ref, x1_ref, h2_ref, lg_ref):
    G = a_ref.shape[-1]
    y = _dot(a_ref[0].astype(BF16), w_ref[0:G, :]) + _dot(b_ref[0].astype(BF16), w_ref[G:2 * G, :])
    x1 = x_ref[0] + mod_ref[0, 2:3, :] * y
    x1_ref[0] = x1
    h2 = _rms(x1, gain_ref[...]) * (1.0 + mod_ref[0, 4:5, :]) + mod_ref[0, 3:4, :]
    _store_row_tiles(h2_ref.at[0], h2)
    lg_ref[...] = lax.dot_general(r_ref[...], h2, (((1,), (1,)), ((), ())), precision=HIGHEST,
                                  preferred_element_type=F32)


def out_proj(a_out, b_out, x, mod, gain2, w_out_bf16, router):
    B, T, D = x.shape
    G = a_out.shape[-1]
    E = router.shape[1]
    tm = min(512, T)
    nt = T // tm
    return pl.pallas_call(
        _out_kernel,
        out_shape=[jax.ShapeDtypeStruct((B, T, D), F32),
                   jax.ShapeDtypeStruct((B, T, D // HEAD_LANES, HEAD_LANES), F32),
                   jax.ShapeDtypeStruct((E, B * T), F32)],
        grid=(B, T // tm),
        in_specs=[pl.BlockSpec((1, tm, G), lambda b, i: (b, i, 0)),
                  pl.BlockSpec((1, tm, G), lambda b, i: (b, i, 0)),
                  pl.BlockSpec((1, tm, D), lambda b, i: (b, i, 0)),
                  pl.BlockSpec((1, 6, D), lambda b, i: (b, 0, 0)),
                  pl.BlockSpec((1, D), lambda b, i: (0, 0)),
                  pl.BlockSpec((2 * G, D), lambda b, i: (0, 0)),
                  pl.BlockSpec((E, D), lambda b, i: (0, 0))],
        out_specs=[pl.BlockSpec((1, tm, D), lambda b, i: (b, i, 0)),
                   pl.BlockSpec((1, tm, D // HEAD_LANES, HEAD_LANES), lambda b, i: (b, i, 0, 0)),
                   pl.BlockSpec((E, tm), lambda b, i, nt=nt: (0, b * nt + i))],
        compiler_params=pltpu.CompilerParams(
            dimension_semantics=("arbitrary", "arbitrary"), vmem_limit_bytes=VMEM_LIMIT),
        name="out_proj",
    )(a_out, b_out, x, mod, gain2.reshape(1, D), w_out_bf16, router.T)


def _route_kernel(lg_ref, bias_ref, idx_ref, w_ref, cnt_ref, *, tiles_per_group):
    @pl.when(pl.program_id(0) % tiles_per_group == 0)
    def _():
        cnt_ref[...] = jnp.zeros(cnt_ref.shape, F32)

    scores = _sigmoid(lg_ref[...])
    sel = scores + bias_ref[...]
    E, tm = sel.shape
    per_group = E // N_GROUPS
    neg = -jnp.inf
    eid = lax.broadcasted_iota(jnp.int32, (E, tm), 0).astype(F32)
    eid_g = lax.broadcasted_iota(jnp.int32, (per_group, tm), 0).astype(F32)

    def first_argmax(x, ids, sentinel):
        m = jnp.max(x, axis=0, keepdims=True)
        i = jnp.min(jnp.where(x == m, ids, sentinel), axis=0, keepdims=True)
        return m, i

    parts, gscore = [], []
    for g in range(N_GROUPS):
        x = sel[g * per_group:(g + 1) * per_group]
        parts.append(x)
        m1, i1 = first_argmax(x, eid_g, float(per_group))
        m2 = jnp.max(jnp.where(eid_g == i1, neg, x), axis=0, keepdims=True)
        gscore.append(m1 + m2)
    kept = []
    for g in range(N_GROUPS):
        beaten = jnp.zeros((1, tm), F32)
        for o in range(N_GROUPS):
            if o == g:
                continue
            wins = (gscore[o] >= gscore[g]) if o < g else (gscore[o] > gscore[g])
            beaten = beaten + wins.astype(F32)
        kept.append(jnp.where(beaten < TOPK_GROUPS, parts[g], neg))
    sel = jnp.concatenate(kept, axis=0)

    ids, vals = [], []
    w_sum = jnp.zeros((1, tm), F32)
    chosen = jnp.zeros((E, tm), F32)
    for k in range(TOP_K):
        _, i = first_argmax(sel, eid, float(E))
        hit = eid == i
        val = jnp.sum(jnp.where(hit, scores, 0.0), axis=0, keepdims=True)
        sel = jnp.where(hit, neg, sel)
        chosen = jnp.where(hit, 1.0, chosen)
        ids.append(i)
        vals.append(val)
        w_sum = w_sum + val
    idx_ref[...] = jnp.concatenate(ids, axis=0).astype(jnp.int32)
    w_ref[...] = jnp.concatenate(vals, axis=0) / w_sum * ROUTED_SCALE
    part = chosen[:, 0:HEAD_LANES]
    for l0 in range(HEAD_LANES, tm, HEAD_LANES):
        part = part + chosen[:, l0:l0 + HEAD_LANES]
    cnt_ref[0] = cnt_ref[0] + part


def route(logits_t, router_bias, group_tokens):
    E, N = logits_t.shape
    tm = min(512, N)
    tpg = group_tokens // tm
    idx, w, cnt = pl.pallas_call(
        functools.partial(_route_kernel, tiles_per_group=tpg),
        out_shape=[jax.ShapeDtypeStruct((TOP_K, N), jnp.int32), jax.ShapeDtypeStruct((TOP_K, N), F32),
                   jax.ShapeDtypeStruct((N // group_tokens, E, HEAD_LANES), F32)],
        grid=(N // tm,),
        in_specs=[pl.BlockSpec((E, tm), lambda i: (0, i)), pl.BlockSpec((E, 1), lambda i: (0, 0))],
        out_specs=[pl.BlockSpec((TOP_K, tm), lambda i: (0, i)), pl.BlockSpec((TOP_K, tm), lambda i: (0, i)),
                   pl.BlockSpec((1, E, HEAD_LANES), lambda i, tpg=tpg: (i // tpg, 0, 0))],
        compiler_params=pltpu.CompilerParams(dimension_semantics=("arbitrary",)),
        name="route",
    )(logits_t, router_bias.reshape(E, 1))
    return idx, w, jnp.sum(cnt, axis=-1).astype(jnp.int32)


def _moe_kernel(off_ref, tok_ref, wl_ref, x_ref, wg_ref, wu_ref, wd_ref, acc_ref, xg_ref, yb_ref,
                *, R, E, per_group):
    g = pl.program_id(0)
    e = pl.program_id(1)

    @pl.when(e == 0)
    def _():
        acc_ref[...] = jnp.zeros(acc_ref.shape, F32)

    seg = off_ref[g * E + e]
    cnt = off_ref[g * E + e + 1] - seg
    start = seg - g * per_group
    U = SUBLANES
    _, _, chunks, lanes = x_ref.shape
    last = per_group - 1

    def sub(sb, _):
        s0 = start + sb * R
        nr = jnp.minimum(R, cnt - sb * R)

        def gather(j, _):
            r0 = pl.multiple_of(j * U, U)
            for i in range(U):
                t = tok_ref[jnp.minimum(s0 + r0 + i, last)]
                xg_ref[j, pl.ds(i, chunks, stride=U), :] = x_ref[0, t]
            return 0

        lax.fori_loop(0, R // U, gather, 0)
        xb = jnp.concatenate([xg_ref[:, c * U:(c + 1) * U, :].reshape(R, lanes) for c in range(chunks)],
                             axis=1).astype(BF16)
        hmid = _silu(_dot(xb, wg_ref[0])) * _dot(xb, wu_ref[0])
        y = _dot(hmid.astype(BF16), wd_ref[0])
        for c in range(chunks):
            yb_ref[:, c * U:(c + 1) * U, :] = y[:, c * lanes:(c + 1) * lanes].reshape(R // U, U, lanes)

        def scatter_group(j, _):
            r0 = pl.multiple_of(j * U, U)
            toks = [tok_ref[s0 + r0 + i] for i in range(U)]
            wts = [wl_ref[s0 + r0 + i] for i in range(U)]
            new = [acc_ref[0, toks[i]] + wts[i] * yb_ref[j, pl.ds(i, chunks, stride=U), :] for i in range(U)]
            for i in range(U):
                acc_ref[0, toks[i]] = new[i]
            return 0

        groups = nr // U
        lax.fori_loop(0, groups, scatter_group, 0)

        def scatter_row(r, _):
            t = tok_ref[s0 + r]
            row = yb_ref[r // U, pl.ds(r % U, chunks, stride=U), :]
            acc_ref[0, t] = acc_ref[0, t] + wl_ref[s0 + r] * row
            return 0

        lax.fori_loop(groups * U, nr, scatter_row, 0)
        return 0

    lax.fori_loop(0, (cnt + R - 1) // R, sub, 0)


def routed_experts(h2, top_idx, top_w, counts, exp_gate, exp_up, exp_down):
    N, chunks, lanes = h2.shape
    D = chunks * lanes
    E, _, F = exp_gate.shape
    TG = min(MOE_TOKEN_GROUP, N)
    G = N // TG
    per_group = TG * TOP_K
    key = (jnp.arange(N, dtype=jnp.int32)[None, :] // TG) * E + top_idx
    order = jnp.argsort(key.reshape(-1))
    tok_s = ((order % N) % TG).astype(jnp.int32)
    w_s = top_w.reshape(-1)[order]
    off = jnp.concatenate([jnp.zeros((1,), jnp.int32), jnp.cumsum(counts.reshape(-1)).astype(jnp.int32)])

    grid_spec = pltpu.PrefetchScalarGridSpec(
        num_scalar_prefetch=1,
        grid=(G, E),
        in_specs=[pl.BlockSpec((per_group,), lambda g, e, off: (g,), memory_space=pltpu.SMEM),
                  pl.BlockSpec((per_group,), lambda g, e, off: (g,), memory_space=pltpu.SMEM),
                  pl.BlockSpec((1, TG, chunks, lanes), lambda g, e, off: (g, 0, 0, 0),
                               pipeline_mode=pl.Buffered(1)),
                  pl.BlockSpec((1, D, F), lambda g, e, off: (e, 0, 0)),
                  pl.BlockSpec((1, D, F), lambda g, e, off: (e, 0, 0)),
                  pl.BlockSpec((1, F, D), lambda g, e, off: (e, 0, 0))],
        out_specs=pl.BlockSpec((1, TG, chunks, lanes), lambda g, e, off: (g, 0, 0, 0),
                               pipeline_mode=pl.Buffered(1)),
        scratch_shapes=[pltpu.VMEM((MOE_ROWS // SUBLANES, chunks * SUBLANES, lanes), F32),
                        pltpu.VMEM((MOE_ROWS // SUBLANES, chunks * SUBLANES, lanes), F32)],
    )
    out = pl.pallas_call(
        functools.partial(_moe_kernel, R=MOE_ROWS, E=E, per_group=per_group),
        out_shape=jax.ShapeDtypeStruct((G, TG, chunks, lanes), F32),
        grid_spec=grid_spec,
        compiler_params=pltpu.CompilerParams(
            dimension_semantics=("arbitrary", "arbitrary"), vmem_limit_bytes=VMEM_LIMIT),
        name="routed_experts",
    )(off, tok_s, w_s, h2.reshape(G, TG, chunks, lanes), exp_gate, exp_up, exp_down)
    return out.reshape(N, chunks, lanes)


def _final_kernel(x1_ref, h2_ref, rt_ref, mod_ref, wg_ref, wu_ref, wd_ref, *rest, final):
    if final:
        fg_ref, o_ref = rest
    else:
        (o_ref,) = rest
    hb = _load_row_tiles(h2_ref.at[0]).astype(BF16)
    hmid = _silu(_dot(hb, wg_ref[...])) * _dot(hb, wu_ref[...])
    shared = _dot(hmid.astype(BF16), wd_ref[...])
    x2 = x1_ref[0] + mod_ref[0, 5:6, :] * (_load_row_tiles(rt_ref.at[0]) + shared)
    if final:
        x2 = _rms(x2, fg_ref[...])
    o_ref[0] = x2


def shared_and_residual(x1, h2, routed, mod, sh_gate, sh_up, sh_down, final_gain=None):
    B, T, D = x1.shape
    F = sh_gate.shape[1]
    tm = min(512, T)
    final = final_gain is not None
    tile = pl.BlockSpec((1, tm, D), lambda b, i: (b, i, 0))
    row_tiles = pl.BlockSpec((1, tm, D // HEAD_LANES, HEAD_LANES), lambda b, i: (b, i, 0, 0))
    in_specs = [tile, row_tiles, row_tiles,
                pl.BlockSpec((1, 6, D), lambda b, i: (b, 0, 0)),
                pl.BlockSpec((D, F), lambda b, i: (0, 0)),
                pl.BlockSpec((D, F), lambda b, i: (0, 0)),
                pl.BlockSpec((F, D), lambda b, i: (0, 0))]
    args = [x1, h2, routed, mod, sh_gate, sh_up, sh_down]
    if final:
        in_specs.append(pl.BlockSpec((1, D), lambda b, i: (0, 0)))
        args.append(final_gain.reshape(1, D))
    return pl.pallas_call(
        functools.partial(_final_kernel, final=final),
        out_shape=jax.ShapeDtypeStruct((B, T, D), F32),
        grid=(B, T // tm),
        in_specs=in_specs,
        out_specs=tile,
        compiler_params=pltpu.CompilerParams(
            dimension_semantics=("arbitrary", "arbitrary"), vmem_limit_bytes=VMEM_LIMIT),
        name="shared_and_residual",
    )(*args)


def moe_block(x1, h2, logits, mod, router_bias, exp_gate, exp_up, exp_down, sh_gate, sh_up, sh_down,
              final_gain=None):
    B, T, D = x1.shape
    top_idx, top_w, counts = route(logits, router_bias, min(MOE_TOKEN_GROUP, B * T))
    chunks = D // HEAD_LANES
    routed = routed_experts(h2.reshape(B * T, chunks, HEAD_LANES), top_idx, top_w, counts,
                            exp_gate.astype(BF16), exp_up.astype(BF16), exp_down.astype(BF16))
    return shared_and_residual(x1, h2, routed.reshape(B, T, chunks, HEAD_LANES), mod,
                               sh_gate.astype(BF16), sh_up.astype(BF16), sh_down.astype(BF16), final_gain)


def kernel(x, c, hgrn_lb_logits, l0_norm1, l0_norm2, l0_w_mod, l0_b_mod, l0_w_in, l0_w_out, l0_hgrn_norm, l0_diff_lq1, l0_diff_lk1, l0_diff_lq2, l0_diff_lk2, l0_diff_subln, l0_router, l0_router_bias, l0_exp_gate, l0_exp_up, l0_exp_down, l0_sh_gate, l0_sh_up, l0_sh_down, l1_norm1, l1_norm2, l1_w_mod, l1_b_mod, l1_w_in, l1_w_out, l1_conv_w, l1_conv_b, l1_ml_i_bias, l1_ml_f_bias, l1_ml_norm, l1_na_rpb, l1_router, l1_router_bias, l1_exp_gate, l1_exp_up, l1_exp_down, l1_sh_gate, l1_sh_up, l1_sh_down, final_norm):
    G = GROUP_WIDTH
    lb_all = jnp.cumsum(jax.nn.softmax(hgrn_lb_logits.astype(F32), axis=0), axis=0)
    layer_idx = 0
    lambda_init = 0.8 - 0.6 * math.exp(-0.3 * layer_idx)
    lam = (jnp.exp(jnp.sum(l0_diff_lq1.astype(F32) * l0_diff_lk1.astype(F32)))
           - jnp.exp(jnp.sum(l0_diff_lq2.astype(F32) * l0_diff_lk2.astype(F32))) + lambda_init)

    mod0 = ada_mod(c, l0_w_mod, l0_b_mod)
    y0 = in_proj(x, mod0, l0_norm1, l0_w_in.astype(BF16))
    a_out = hgrn2(y0, lb_all[0], l0_hgrn_norm)
    b_out = diff_attention(y0, lam, l0_diff_subln, layer_idx)
    x1, h2, logits = out_proj(a_out, b_out, x, mod0, l0_norm2, l0_w_out.astype(BF16), l0_router)
    xa = moe_block(x1, h2, logits, mod0, l0_router_bias, l0_exp_gate, l0_exp_up, l0_exp_down,
                   l0_sh_gate, l0_sh_up, l0_sh_down)

    mod1 = ada_mod(c, l1_w_mod, l1_b_mod)
    n_gate = l1_w_in.shape[1] - 7 * G
    w_main = jnp.concatenate([l1_w_in[:, :4 * G], l1_w_in[:, 4 * G + n_gate:]], axis=1).astype(BF16)
    w_gate = jnp.pad(l1_w_in[:, 4 * G:4 * G + n_gate], ((0, 0), (0, HEAD_LANES - n_gate)))
    y1, gates = in_proj(xa, mod1, l1_norm1, w_main, w_gate)
    c_out = mlstm(y1, gates, l1_conv_w, l1_conv_b, l1_ml_i_bias, l1_ml_f_bias, l1_ml_norm)
    d_out = neighbourhood_attention(y1, l1_na_rpb)
    x1, h2, logits = out_proj(c_out, d_out, xa, mod1, l1_norm2, l1_w_out.astype(BF16), l1_router)
    return moe_block(x1, h2, logits, mod1, l1_router_bias, l1_exp_gate, l1_exp_up, l1_exp_down,
                     l1_sh_gate, l1_sh_up, l1_sh_down, final_gain=final_norm)
```

```python
import functools
import math

import numpy as np
import jax
import jax.numpy as jnp
from jax import lax
from jax.experimental import pallas as pl
from jax.experimental.pallas import tpu as pltpu

F32 = jnp.float32
BF16 = jnp.bfloat16
HIGHEST = lax.Precision.HIGHEST
EPS = 1e-6

GRID_W = 64
GROUP_WIDTH = 512
HEAD_LANES = 128
HG_CHUNK = 64
ML_CHUNK = 128
STEP_CHUNKS = 2
ML_CONV = 5
DA_DIM = 64
ROPE_DIM = 16
ROPE_THETA = 500000.0
NA_ROWS = 8
NA_COLS = 16
NA_DIM = 64
NA_STEP_ROWS = 4
N_EXPERTS = 128
TOP_K = 8
N_GROUPS = 8
TOPK_GROUPS = 4
ROUTED_SCALE = 2.5
MOE_ROWS = 256
MOE_TOKEN_GROUP = 4096
SUBLANES = 8
NEG_BIG = -1e30
VMEM_LIMIT = 48 * 1024 * 1024


def _dot(a, b, **kw):
    return jnp.dot(a, b, preferred_element_type=F32, **kw)


def _dot_nt(a, b):
    return lax.dot_general(a, b, (((1,), (1,)), ((), ())), preferred_element_type=F32)


def _dot_tn(a, b):
    return lax.dot_general(a, b, (((0,), (0,)), ((), ())), preferred_element_type=F32)


def _sigmoid(x):
    return jax.nn.sigmoid(x)


def _silu(x):
    return x * jax.nn.sigmoid(x)


def _log_sigmoid(x):
    return jnp.minimum(x, 0.0) - jnp.log(1.0 + jnp.exp(-jnp.abs(x)))


def _rms(x, gain):
    return x * lax.rsqrt(jnp.mean(x * x, axis=-1, keepdims=True) + EPS) * gain


def _store_row_tiles(ref, val):
    for c in range(ref.shape[1]):
        ref[:, c, :] = val[:, c * HEAD_LANES:(c + 1) * HEAD_LANES]


def _load_row_tiles(ref):
    return jnp.concatenate([ref[:, c, :] for c in range(ref.shape[1])], axis=1)


def _mod_kernel(c_ref, w_ref, b_ref, o_ref):
    o_ref[...] = _dot(_silu(c_ref[...]), w_ref[...], precision=HIGHEST) + b_ref[...]


def ada_mod(c, w_mod, b_mod):
    B, D = c.shape
    N = w_mod.shape[1]
    tn = 1024
    out = pl.pallas_call(
        _mod_kernel,
        out_shape=jax.ShapeDtypeStruct((B, N), F32),
        grid=(N // tn,),
        in_specs=[pl.BlockSpec((B, D), lambda j: (0, 0)),
                  pl.BlockSpec((D, tn), lambda j: (0, j)),
                  pl.BlockSpec((1, tn), lambda j: (0, j))],
        out_specs=pl.BlockSpec((B, tn), lambda j: (0, j)),
        name="ada_mod",
    )(c, w_mod, b_mod.reshape(1, N))
    return out.reshape(B, 6, D)


def _in_kernel(x_ref, mod_ref, gain_ref, w_ref, *rest, has_gate):
    if has_gate:
        wg_ref, o_ref, og_ref, h_ref = rest
    else:
        o_ref, h_ref = rest

    @pl.when(pl.program_id(2) == 0)
    def _():
        h = _rms(x_ref[0], gain_ref[...]) * (1.0 + mod_ref[0, 1:2, :]) + mod_ref[0, 0:1, :]
        h_ref[...] = h.astype(BF16)
        if has_gate:
            og_ref[0] = _dot(h, wg_ref[...], precision=HIGHEST)

    o_ref[0] = _dot(h_ref[...], w_ref[...])


def in_proj(x, mod, gain, w_bf16, w_gate=None):
    B, T, D = x.shape
    N = w_bf16.shape[1]
    tm = min(512, T)
    tn = 512
    has_gate = w_gate is not None
    in_specs = [pl.BlockSpec((1, tm, D), lambda b, i, j: (b, i, 0)),
                pl.BlockSpec((1, 6, D), lambda b, i, j: (b, 0, 0)),
                pl.BlockSpec((1, D), lambda b, i, j: (0, 0)),
                pl.BlockSpec((D, tn), lambda b, i, j: (0, j))]
    out_shape = [jax.ShapeDtypeStruct((B, T, N), F32)]
    out_specs = [pl.BlockSpec((1, tm, tn), lambda b, i, j: (b, i, j))]
    args = [x, mod, gain.reshape(1, D), w_bf16]
    if has_gate:
        in_specs.append(pl.BlockSpec((D, HEAD_LANES), lambda b, i, j: (0, 0)))
        out_shape.append(jax.ShapeDtypeStruct((B, T, HEAD_LANES), F32))
        out_specs.append(pl.BlockSpec((1, tm, HEAD_LANES), lambda b, i, j: (b, i, 0)))
        args.append(w_gate)
    res = pl.pallas_call(
        functools.partial(_in_kernel, has_gate=has_gate),
        out_shape=out_shape,
        grid=(B, T // tm, N // tn),
        in_specs=in_specs,
        out_specs=out_specs,
        scratch_shapes=[pltpu.VMEM((tm, D), BF16)],
        compiler_params=pltpu.CompilerParams(
            dimension_semantics=("arbitrary", "arbitrary", "arbitrary")),
        name="in_proj",
    )(*args)
    return res if has_gate else res[0]


def _hgrn_consts(C):
    t = np.arange(C)
    tri = (t[None, :] <= t[:, None]).astype(np.float32)
    triT = np.ascontiguousarray(tri.T)
    wf, wb, mf = [tri], [triT], []
    levels = int(round(math.log2(C)))
    for l in range(levels):
        size = C >> l
        blk = t // size
        r = blk * size + size // 2
        wf.append(tri - tri[r - 1])
        wb.append(triT - triT[r])
        upper = (t % size) >= size // 2
        mf.append(((blk[:, None] == blk[None, :]) & upper[:, None] & (~upper)[None, :]).astype(np.float32))
    mf.append(np.eye(C, dtype=np.float32))
    ones = np.ones((8, C), np.float32)
    wf.append(ones)
    wb.append(ones)
    mf = np.stack(mf)
    mb = np.ascontiguousarray(np.transpose(mf, (0, 2, 1)))
    return np.concatenate(wf), np.concatenate(wb), mf, mb


def _split3(x):
    hi = x.astype(BF16)
    r1 = x - hi.astype(F32)
    mid = r1.astype(BF16)
    lo = (r1 - mid.astype(F32)).astype(BF16)
    return hi, mid, lo


def _hgrn_kernel(q_ref, i_ref, ff_ref, fb_ref, g_ref, lb_ref, gain_ref, wf_ref, wb_ref, mf_ref, mb_ref,
                 o_ref, of_ref, ob_ref, *, C, T):
    n = T // C
    levels = int(round(math.log2(C)))
    dv = q_ref.shape[-1]

    def prepare(c, f_ref, lbd, w_ref):
        sl = pl.ds(pl.multiple_of(c * C, C), C)
        q = _silu(q_ref[0, sl, :])
        v = i_ref[0, sl, :]
        fg = lbd + (1.0 - lbd) * _sigmoid(f_ref[0, sl, :])
        lf = jnp.log(fg)
        d3 = _dot(w_ref[...], jnp.concatenate(_split3(lf), axis=1))
        dall = d3[:, 0:dv] + d3[:, dv:2 * dv] + d3[:, 2 * dv:3 * dv]
        return dict(sl=sl, q=q, k=1.0 - fg, v=v.astype(BF16), dall=dall)

    def scores(p, m_ref):
        q, k, dall = p["q"], p["k"], p["dall"]
        attn = m_ref[levels] * _dot_nt(q.astype(BF16), k.astype(BF16))
        for l in range(levels):
            e = jnp.exp(-jnp.abs(dall[(l + 1) * C:(l + 2) * C]))
            attn = attn + m_ref[l] * _dot_nt((q * e).astype(BF16), (k * e).astype(BF16))
        return attn.astype(BF16)

    st0 = jnp.zeros((dv, dv), F32)

    def step(j, carry):
        states = list(carry)
        work = []
        for u in range(STEP_CHUNKS):
            work.append((0, prepare(STEP_CHUNKS * j + u, ff_ref, lb_ref[0:1, :], wf_ref), mf_ref, of_ref))
            work.append((1, prepare(n - 1 - STEP_CHUNKS * j - u, fb_ref, lb_ref[1:2, :], wb_ref), mb_ref, ob_ref))
        attn = [scores(p, m_ref) for _, p, m_ref, _ in work]
        local = []
        for (_, p, _, _), a in zip(work, attn):
            cum = p["dall"][0:C]
            tot = p["dall"][(levels + 1) * C:(levels + 1) * C + 1]
            kt = p["k"] * jnp.exp(tot - cum)
            local.append((_dot(a, p["v"]), _dot_tn(p["v"], kt.astype(BF16)), cum, tot))
        for (d, p, _, out_ref), (o_in, incr, cum, tot) in zip(work, local):
            st = states[d]
            out_ref[p["sl"], :] = o_in + _dot_nt((p["q"] * jnp.exp(cum)).astype(BF16), st.astype(BF16))
            states[d] = st * jnp.exp(tot) + incr
        return tuple(states)

    lax.fori_loop(0, n // STEP_CHUNKS, step, (st0, st0))

    rt = min(512, T)
    for r0 in range(0, T, rt):
        sl = slice(r0, r0 + rt)
        o_ref[0, sl, :] = _rms(of_ref[sl, :] + ob_ref[sl, :], gain_ref[...]) * _silu(g_ref[0, sl, :])


def hgrn2(y, lb, norm_gain):
    B, T, _ = y.shape
    H = GROUP_WIDTH // HEAD_LANES
    C = HG_CHUNK
    wf, wb, mf, mb = _hgrn_consts(C)
    wf, wb = jnp.asarray(wf, BF16), jnp.asarray(wb, BF16)
    mf, mb = jnp.asarray(mf), jnp.asarray(mb)

    def col(group):
        return pl.BlockSpec((1, T, HEAD_LANES), lambda b, h, group=group: (b, 0, group * H + h))

    def const(a):
        nd = a.ndim
        return pl.BlockSpec(a.shape, lambda b, h, nd=nd: (0,) * nd)

    return pl.pallas_call(
        functools.partial(_hgrn_kernel, C=C, T=T),
        out_shape=jax.ShapeDtypeStruct((B, T, GROUP_WIDTH), F32),
        grid=(B, H),
        in_specs=[col(0), col(1), col(2), col(3), col(4),
                  pl.BlockSpec((2, HEAD_LANES), lambda b, h: (0, h)),
                  pl.BlockSpec((1, HEAD_LANES), lambda b, h: (0, 0)),
                  const(wf), const(wb), const(mf), const(mb)],
        out_specs=pl.BlockSpec((1, T, HEAD_LANES), lambda b, h: (b, 0, h)),
        scratch_shapes=[pltpu.VMEM((T, HEAD_LANES), F32), pltpu.VMEM((T, HEAD_LANES), F32)],
        compiler_params=pltpu.CompilerParams(
            dimension_semantics=("arbitrary", "arbitrary"), vmem_limit_bytes=VMEM_LIMIT),
        name="hgrn2",
    )(y, y, y, y, y, lb, norm_gain.reshape(1, HEAD_LANES), wf, wb, mf, mb)


def _rope_tables(T):
    pos = np.arange(T, dtype=np.float32)
    inv_freq = (ROPE_THETA ** (-np.arange(0, ROPE_DIM, 2, dtype=np.float32) / ROPE_DIM)).astype(np.float32)
    ang = pos[:, None] * inv_freq[None, :]
    cos, sin = np.cos(ang), np.sin(ang)
    half = ROPE_DIM // 2
    c = np.ones((T, HEAD_LANES), np.float32)
    s_prev = np.zeros((T, HEAD_LANES), np.float32)
    s_next = np.zeros((T, HEAD_LANES), np.float32)
    for base in range(0, HEAD_LANES, DA_DIM):
        c[:, base:base + half] = cos
        c[:, base + half:base + ROPE_DIM] = cos
        s_next[:, base:base + half] = -sin
        s_prev[:, base + half:base + ROPE_DIM] = sin
    return jnp.asarray(c), jnp.asarray(s_prev), jnp.asarray(s_next)


def _rope(x, c, s_prev, s_next):
    half = ROPE_DIM // 2
    lanes = x.shape[-1]
    return (x * c + pltpu.roll(x, half, axis=1) * s_prev
            + pltpu.roll(x, lanes - half, axis=1) * s_next)


def _diff_kernel(lam_ref, q_ref, k_ref, v_ref, cq_ref, spq_ref, snq_ref, ck_ref, spk_ref, snk_ref,
                 subln_ref, o_ref, kr_ref, vb_ref, *, T, out_scale):
    rt = min(512, T)

    @pl.when(pl.program_id(2) == 0)
    def _():
        for r0 in range(0, T, rt):
            sl = slice(r0, r0 + rt)
            kr_ref[sl, :] = _rope(k_ref[0, sl, :], ck_ref[sl, :], spk_ref[sl, :], snk_ref[sl, :]).astype(BF16)
            vb_ref[sl, :] = v_ref[0, sl, :].astype(BF16)

    q = _rope(q_ref[0], cq_ref[...], spq_ref[...], snq_ref[...]) * (DA_DIM ** -0.5 * math.log2(math.e))
    lam = lam_ref[0, 0]
    v = vb_ref[...]
    map_of_lane = lax.broadcasted_iota(jnp.int32, q.shape, 1) // DA_DIM

    scores = [_dot_nt(jnp.where(map_of_lane == m, q, 0.0).astype(BF16), kr_ref[...]) for m in range(2)]
    probs = []
    for s in scores:
        e = jnp.exp2(s - jnp.max(s, axis=-1, keepdims=True))
        probs.append((e.astype(BF16), jnp.sum(e, axis=-1, keepdims=True)))
    pv = [_dot(e, v) / l for e, l in probs]
    o = pv[0] - lam * pv[1]
    o_ref[0] = _rms(o, subln_ref[...]) * out_scale


def diff_attention(y, lam, subln, layer_idx):
    B, T, _ = y.shape
    H = GROUP_WIDTH // HEAD_LANES
    tq = min(256, T)
    lambda_init = 0.8 - 0.6 * math.exp(-0.3 * layer_idx)
    c, sp, sn = _rope_tables(T)

    def col(group, rows):
        if rows == T:
            return pl.BlockSpec((1, T, HEAD_LANES), lambda b, h, i, group=group: (b, 0, group * H + h))
        return pl.BlockSpec((1, rows, HEAD_LANES), lambda b, h, i, group=group: (b, i, group * H + h))

    tab_q = pl.BlockSpec((tq, HEAD_LANES), lambda b, h, i: (i, 0))
    tab_k = pl.BlockSpec((T, HEAD_LANES), lambda b, h, i: (0, 0))
    return pl.pallas_call(
        functools.partial(_diff_kernel, T=T, out_scale=1.0 - lambda_init),
        out_shape=jax.ShapeDtypeStruct((B, T, GROUP_WIDTH), F32),
        grid=(B, H, T // tq),
        in_specs=[pl.BlockSpec(memory_space=pltpu.SMEM),
                  col(5, tq), col(6, T), col(7, T),
                  tab_q, tab_q, tab_q, tab_k, tab_k, tab_k,
                  pl.BlockSpec((1, HEAD_LANES), lambda b, h, i: (0, 0))],
        out_specs=pl.BlockSpec((1, tq, HEAD_LANES), lambda b, h, i: (b, i, h)),
        scratch_shapes=[pltpu.VMEM((T, HEAD_LANES), BF16), pltpu.VMEM((T, HEAD_LANES), BF16)],
        compiler_params=pltpu.CompilerParams(
            dimension_semantics=("arbitrary", "arbitrary", "arbitrary"), vmem_limit_bytes=VMEM_LIMIT),
        name="diff_attention",
    )(lam.reshape(1, 1), y, y, y, c, sp, sn, c, sp, sn, subln.reshape(1, HEAD_LANES))


def _mlstm_kernel(q_ref, k_ref, v_ref, og_ref, gt_ref, cwq_ref, cwk_ref, cbq_ref, cbk_ref, gbias_ref,
                  gain_ref, tri_ref, o_ref, qc_ref, kc_ref, xp_ref, gx_ref, hf_ref, hb_ref, *, C, T, dk):
    n = T // C
    head = pl.program_id(1)
    pad = 8
    half = ML_CONV // 2
    rt = min(512, T)

    xp_ref[0:pad, :] = jnp.zeros((pad, dk), F32)
    xp_ref[pad + T:pad + T + pad, :] = jnp.zeros((pad, dk), F32)
    for src, cw_ref, cb_ref, dst, scale in ((q_ref, cwq_ref, cbq_ref, qc_ref, 1.0),
                                            (k_ref, cwk_ref, cbk_ref, kc_ref, dk ** -0.5)):
        xp_ref[pad:pad + T, :] = src[0]
        for r0 in range(0, T, rt):
            acc = jnp.zeros((rt, dk), F32) + cb_ref[...]
            for j in range(ML_CONV):
                acc = acc + xp_ref[pad + r0 + j - half:pad + r0 + j - half + rt, :] * cw_ref[j:j + 1, :]
            dst[r0:r0 + rt, :] = _silu(acc) * scale

    lane = lax.broadcasted_iota(jnp.int32, (rt, HEAD_LANES), 1)
    is_f = (lane % 8) >= 4
    for r0 in range(0, T, rt):
        g = gt_ref[0, r0:r0 + rt, :] + gbias_ref[...]
        p = jnp.where(is_f, _log_sigmoid(g), g)
        x = jnp.zeros((rt, HEAD_LANES), F32)
        for j, src_lane in enumerate((0, 4, 8, 12)):
            colv = jnp.sum(jnp.where(lane == src_lane + head, p, 0.0), axis=1, keepdims=True)
            x = jnp.where(lane == j, colv, x)
        gx_ref[r0:r0 + rt, :] = x

    row = lax.broadcasted_iota(jnp.int32, (C, C), 0)
    colm = lax.broadcasted_iota(jnp.int32, (C, C), 1)

    init = (jnp.zeros((dk, dk), F32), jnp.zeros((1, dk), F32), jnp.full((1, 1), NEG_BIG, F32))

    def step(j, carry):
        carries = list(carry)
        work = []
        for u in range(STEP_CHUNKS):
            work.append((0, STEP_CHUNKS * j + u, hf_ref))
            work.append((1, n - 1 - STEP_CHUNKS * j - u, hb_ref))
        chunks = []
        for d, c, out_ref in work:
            sl = pl.ds(pl.multiple_of(c * C, C), C)
            x = gx_ref[sl, :]
            chunks.append(dict(d=d, sl=sl, out=out_ref, x=x, q=qc_ref[sl, :], k=kc_ref[sl, :],
                               v=v_ref[0, sl, :].astype(BF16),
                               cumx=_dot(tri_ref[d], x, precision=HIGHEST)))
        for p in chunks:
            p["qk"] = _dot_nt(p["q"].astype(BF16), p["k"].astype(BF16))
        for p in chunks:
            d, x, cumx = p["d"], p["x"], p["cumx"]
            mask = (colm <= row) if d == 0 else (colm >= row)
            xt = x.T
            cumxt = cumx.T
            ig_c = x[:, 2 * d:2 * d + 1]
            ig_r = xt[2 * d:2 * d + 1, :]
            cum_c = cumx[:, 2 * d + 1:2 * d + 2]
            cum_r = cumxt[2 * d + 1:2 * d + 2, :]
            tot = jnp.sum(x[:, 2 * d + 1:2 * d + 2], axis=0, keepdims=True)
            dmat = jnp.where(mask, cum_c - cum_r + ig_r, -jnp.inf)
            dmax = jnp.max(dmat, axis=1, keepdims=True)
            a = p["qk"] * jnp.exp(dmat - dmax)
            ds = tot - cum_c + ig_c
            dsmax = jnp.max(ds, axis=0, keepdims=True)
            kw = p["k"] * jnp.exp(ds - dsmax)
            p.update(cum_c=cum_c, tot=tot, dmax=dmax, dsmax=dsmax, a=a.astype(BF16), kw=kw.astype(BF16),
                     den=jnp.sum(a, axis=1, keepdims=True), nsum=jnp.sum(kw, axis=0, keepdims=True))
        for p in chunks:
            p["num"] = _dot(p["a"], p["v"])
            p["upd"] = _dot_tn(p["kw"], p["v"])
        for p in chunks:
            s_state, nvec, m = carries[p["d"]]
            q = p["q"]
            g = p["cum_c"] + m
            mt = jnp.maximum(g, p["dmax"])
            inter = jnp.exp(g - mt)
            local = jnp.exp(p["dmax"] - mt)
            num = inter * _dot(q.astype(BF16), s_state.astype(BF16)) + local * p["num"]
            den = inter * jnp.sum(q * nvec, axis=1, keepdims=True) + local * p["den"]
            p["out"][p["sl"], :] = num / jnp.maximum(jnp.abs(den), jnp.exp(-mt))
            m_new = jnp.maximum(p["tot"] + m, p["dsmax"])
            decay = jnp.exp(p["tot"] + m - m_new)
            scale = jnp.exp(p["dsmax"] - m_new)
            carries[p["d"]] = (decay * s_state + scale * p["upd"], decay * nvec + scale * p["nsum"], m_new)
        return tuple(carries)

    lax.fori_loop(0, n // STEP_CHUNKS, step, (init, init))

    for r0 in range(0, T, rt):
        sl = slice(r0, r0 + rt)
        o_ref[0, sl, :] = _rms(hf_ref[sl, :] + hb_ref[sl, :], gain_ref[...]) * _sigmoid(og_ref[0, sl, :])


def mlstm(y, gates, conv_w, conv_b, i_bias, f_bias, norm_gain):
    B, T, _ = y.shape
    H = GROUP_WIDTH // HEAD_LANES
    C = min(ML_CHUNK, T)
    t = np.arange(C)
    tri = np.stack([(t[None, :] <= t[:, None]), (t[None, :] >= t[:, None])]).astype(np.float32)
    gbias = jnp.zeros((1, HEAD_LANES), F32)
    gbias = gbias.at[0, 0:4].set(i_bias[0]).at[0, 4:8].set(f_bias[0])
    gbias = gbias.at[0, 8:12].set(i_bias[1]).at[0, 12:16].set(f_bias[1])

    def col(group):
        return pl.BlockSpec((1, T, HEAD_LANES), lambda b, h, group=group: (b, 0, group * H + h))

    conv_q = pl.BlockSpec((ML_CONV, HEAD_LANES), lambda b, h: (0, h))
    conv_k = pl.BlockSpec((ML_CONV, HEAD_LANES), lambda b, h: (0, H + h))
    bias_q = pl.BlockSpec((1, HEAD_LANES), lambda b, h: (0, h))
    bias_k = pl.BlockSpec((1, HEAD_LANES), lambda b, h: (0, H + h))
    cb = conv_b.reshape(1, -1)
    return pl.pallas_call(
        functools.partial(_mlstm_kernel, C=C, T=T, dk=HEAD_LANES),
        out_shape=jax.ShapeDtypeStruct((B, T, GROUP_WIDTH), F32),
        grid=(B, H),
        in_specs=[col(0), col(1), col(2), col(3),
                  pl.BlockSpec((1, T, HEAD_LANES), lambda b, h: (b, 0, 0)),
                  conv_q, conv_k, bias_q, bias_k,
                  pl.BlockSpec((1, HEAD_LANES), lambda b, h: (0, 0)),
                  pl.BlockSpec((1, HEAD_LANES), lambda b, h: (0, 0)),
                  pl.BlockSpec((2, C, C), lambda b, h: (0, 0, 0))],
        out_specs=pl.BlockSpec((1, T, HEAD_LANES), lambda b, h: (b, 0, h)),
        scratch_shapes=[pltpu.VMEM((T, HEAD_LANES), F32), pltpu.VMEM((T, HEAD_LANES), F32),
                        pltpu.VMEM((T + 16, HEAD_LANES), F32), pltpu.VMEM((T, HEAD_LANES), F32),
                        pltpu.VMEM((T, HEAD_LANES), F32), pltpu.VMEM((T, HEAD_LANES), F32)],
        compiler_params=pltpu.CompilerParams(
            dimension_semantics=("arbitrary", "arbitrary"), vmem_limit_bytes=VMEM_LIMIT),
        name="mlstm",
    )(y, y, y, y, gates, conv_w, conv_w, cb, cb, gbias, norm_gain.reshape(1, HEAD_LANES), jnp.asarray(tri))


def _na_bias_table(rpb, rows):
    kr = min(NA_ROWS, rows)
    c = np.arange(GRID_W)
    cstart = np.clip(c - NA_COLS // 2, 0, GRID_W - NA_COLS)
    kc = np.arange(GRID_W)
    valid = (kc[None, :] >= cstart[:, None]) & (kc[None, :] < cstart[:, None] + NA_COLS)
    coff = np.clip(kc[None, :] - c[:, None] + NA_COLS - 1, 0, 2 * NA_COLS - 2)
    di = np.arange(kr)
    i = np.arange(kr)
    roff = i[None, :] - di[:, None] + NA_ROWS - 1
    heads = rpb.shape[0]
    cols = jnp.where(jnp.asarray(valid)[None, None], rpb.astype(F32)[:, :, coff], NEG_BIG)
    tab = jnp.take(cols, jnp.asarray(roff.reshape(-1)), axis=1)
    tab = tab.reshape(heads, kr, kr, GRID_W, GRID_W).transpose(0, 1, 3, 2, 4)
    return tab.reshape(heads, kr, GRID_W, kr * GRID_W)


def _na_kernel(q_ref, k_ref, v_ref, bm_ref, o_ref, *, rows, kr):
    W = GRID_W
    heads_per_block = HEAD_LANES // NA_DIM

    head_of_lane = lax.broadcasted_iota(jnp.int32, (W, HEAD_LANES), 1) // NA_DIM

    def body(j, _):
        work = []
        for u in range(NA_STEP_ROWS):
            r = NA_STEP_ROWS * j + u
            rs = jnp.clip(r - kr // 2, 0, rows - kr)
            di = r - rs
            qs = pl.ds(pl.multiple_of(r * W, W), W)
            ks = pl.ds(pl.multiple_of(rs * W, W), kr * W)
            q = q_ref[0, qs, :] * (NA_DIM ** -0.5)
            kw = k_ref[0, ks, :].astype(BF16)
            for hh in range(heads_per_block):
                s = _dot_nt(jnp.where(head_of_lane == hh, q, 0.0).astype(BF16), kw) + bm_ref[hh, di]
                work.append((u, hh, qs, ks, s))
        probs = []
        for u, hh, qs, ks, s in work:
            e = jnp.exp(s - jnp.max(s, axis=-1, keepdims=True))
            probs.append((e.astype(BF16), jnp.sum(e, axis=-1, keepdims=True)))
        outs = {}
        for (u, hh, qs, ks, s), (e, l) in zip(work, probs):
            o = _dot(e, v_ref[0, ks, :].astype(BF16)) / l
            outs[u] = o if hh == 0 else jnp.where(head_of_lane == hh, o, outs[u])
            if hh == heads_per_block - 1:
                o_ref[0, qs, :] = outs[u]
        return 0

    lax.fori_loop(0, rows // NA_STEP_ROWS, body, 0)


def neighbourhood_attention(y, rpb):
    B, T, _ = y.shape
    rows = T // GRID_W
    kr = min(NA_ROWS, rows)
    HB = GROUP_WIDTH // HEAD_LANES
    hpb = HEAD_LANES // NA_DIM
    bm = _na_bias_table(rpb, rows)

    def col(group):
        return pl.BlockSpec((1, T, HEAD_LANES), lambda b, h, group=group: (b, 0, group * HB + h))

    return pl.pallas_call(
        functools.partial(_na_kernel, rows=rows, kr=kr),
        out_shape=jax.ShapeDtypeStruct((B, T, GROUP_WIDTH), F32),
        grid=(B, HB),
        in_specs=[col(4), col(5), col(6),
                  pl.BlockSpec((hpb, kr, GRID_W, kr * GRID_W), lambda b, h: (h, 0, 0, 0))],
        out_specs=pl.BlockSpec((1, T, HEAD_LANES), lambda b, h: (b, 0, h)),
        compiler_params=pltpu.CompilerParams(
            dimension_semantics=("arbitrary", "arbitrary"), vmem_limit_bytes=VMEM_LIMIT),
        name="neighbourhood_attention",
    )(y, y, y, bm)


def _out_kernel(a_ref, b_ref, x_ref, mod_ref, gain_ref, w_ref, r_ref, x1_ref, h2_ref, lg_ref):
    G = a_ref.shape[-1]
    y = _dot(a_ref[0].astype(BF16), w_ref[0:G, :]) + _dot(b_ref[0].astype(BF16), w_ref[G:2 * G, :])
    x1 = x_ref[0] + mod_ref[0, 2:3, :] * y
    x1_ref[0] = x1
    h2 = _rms(x1, gain_ref[...]) * (1.0 + mod_ref[0, 4:5, :]) + mod_ref[0, 3:4, :]
    _store_row_tiles(h2_ref.at[0], h2)
    lg_ref[...] = lax.dot_general(r_ref[...], h2, (((1,), (1,)), ((), ())), precision=HIGHEST,
                                  preferred_element_type=F32)


def out_proj(a_out, b_out, x, mod, gain2, w_out_bf16, router):
    B, T, D = x.shape
    G = a_out.shape[-1]
    E = router.shape[1]
    tm = min(512, T)
    nt = T // tm
    return pl.pallas_call(
        _out_kernel,
        out_shape=[jax.ShapeDtypeStruct((B, T, D), F32),
                   jax.ShapeDtypeStruct((B, T, D // HEAD_LANES, HEAD_LANES), F32),
                   jax.ShapeDtypeStruct((E, B * T), F32)],
        grid=(B, T // tm),
        in_specs=[pl.BlockSpec((1, tm, G), lambda b, i: (b, i, 0)),
                  pl.BlockSpec((1, tm, G), lambda b, i: (b, i, 0)),
                  pl.BlockSpec((1, tm, D), lambda b, i: (b, i, 0)),
                  pl.BlockSpec((1, 6, D), lambda b, i: (b, 0, 0)),
                  pl.BlockSpec((1, D), lambda b, i: (0, 0)),
                  pl.BlockSpec((2 * G, D), lambda b, i: (0, 0)),
                  pl.BlockSpec((E, D), lambda b, i: (0, 0))],
        out_specs=[pl.BlockSpec((1, tm, D), lambda b, i: (b, i, 0)),
                   pl.BlockSpec((1, tm, D // HEAD_LANES, HEAD_LANES), lambda b, i: (b, i, 0, 0)),
                   pl.BlockSpec((E, tm), lambda b, i, nt=nt: (0, b * nt + i))],
        compiler_params=pltpu.CompilerParams(
            dimension_semantics=("arbitrary", "arbitrary"), vmem_limit_bytes=VMEM_LIMIT),
        name="out_proj",
    )(a_out, b_out, x, mod, gain2.reshape(1, D), w_out_bf16, router.T)


def _route_kernel(lg_ref, bias_ref, idx_ref, w_ref, cnt_ref, *, tiles_per_group):
    @pl.when(pl.program_id(0) % tiles_per_group == 0)
    def _():
        cnt_ref[...] = jnp.zeros(cnt_ref.shape, F32)

    scores = _sigmoid(lg_ref[...])
    sel = scores + bias_ref[...]
    E, tm = sel.shape
    per_group = E // N_GROUPS
    neg = -jnp.inf
    eid = lax.broadcasted_iota(jnp.int32, (E, tm), 0).astype(F32)
    eid_g = lax.broadcasted_iota(jnp.int32, (per_group, tm), 0).astype(F32)

    def first_argmax(x, ids, sentinel):
        m = jnp.max(x, axis=0, keepdims=True)
        i = jnp.min(jnp.where(x == m, ids, sentinel), axis=0, keepdims=True)
        return m, i

    parts, gscore = [], []
    for g in range(N_GROUPS):
        x = sel[g * per_group:(g + 1) * per_group]
        parts.append(x)
        m1, i1 = first_argmax(x, eid_g, float(per_group))
        m2 = jnp.max(jnp.where(eid_g == i1, neg, x), axis=0, keepdims=True)
        gscore.append(m1 + m2)
    kept = []
    for g in range(N_GROUPS):
        beaten = jnp.zeros((1, tm), F32)
        for o in range(N_GROUPS):
            if o == g:
                continue
            wins = (gscore[o] >= gscore[g]) if o < g else (gscore[o] > gscore[g])
            beaten = beaten + wins.astype(F32)
        kept.append(jnp.where(beaten < TOPK_GROUPS, parts[g], neg))
    sel = jnp.concatenate(kept, axis=0)

    ids, vals = [], []
    w_sum = jnp.zeros((1, tm), F32)
    chosen = jnp.zeros((E, tm), F32)
    for k in range(TOP_K):
        _, i = first_argmax(sel, eid, float(E))
        hit = eid == i
        val = jnp.sum(jnp.where(hit, scores, 0.0), axis=0, keepdims=True)
        sel = jnp.where(hit, neg, sel)
        chosen = jnp.where(hit, 1.0, chosen)
        ids.append(i)
        vals.append(val)
        w_sum = w_sum + val
    idx_ref[...] = jnp.concatenate(ids, axis=0).astype(jnp.int32)
    w_ref[...] = jnp.concatenate(vals, axis=0) / w_sum * ROUTED_SCALE
    part = chosen[:, 0:HEAD_LANES]
    for l0 in range(HEAD_LANES, tm, HEAD_LANES):
        part = part + chosen[:, l0:l0 + HEAD_LANES]
    cnt_ref[0] = cnt_ref[0] + part


def route(logits_t, router_bias, group_tokens):
    E, N = logits_t.shape
    tm = min(512, N)
    tpg = group_tokens // tm
    idx, w, cnt = pl.pallas_call(
        functools.partial(_route_kernel, tiles_per_group=tpg),
        out_shape=[jax.ShapeDtypeStruct((TOP_K, N), jnp.int32), jax.ShapeDtypeStruct((TOP_K, N), F32),
                   jax.ShapeDtypeStruct((N // group_tokens, E, HEAD_LANES), F32)],
        grid=(N // tm,),
        in_specs=[pl.BlockSpec((E, tm), lambda i: (0, i)), pl.BlockSpec((E, 1), lambda i: (0, 0))],
        out_specs=[pl.BlockSpec((TOP_K, tm), lambda i: (0, i)), pl.BlockSpec((TOP_K, tm), lambda i: (0, i)),
                   pl.BlockSpec((1, E, HEAD_LANES), lambda i, tpg=tpg: (i // tpg, 0, 0))],
        compiler_params=pltpu.CompilerParams(dimension_semantics=("arbitrary",)),
        name="route",
    )(logits_t, router_bias.reshape(E, 1))
    return idx, w, jnp.sum(cnt, axis=-1).astype(jnp.int32)


def _moe_kernel(off_ref, tok_ref, wl_ref, x_ref, wg_ref, wu_ref, wd_ref, acc_ref, xg_ref, yb_ref,
                *, R, E, per_group):
    g = pl.program_id(0)
    e = pl.program_id(1)

    @pl.when(e == 0)
    def _():
        acc_ref[...] = jnp.zeros(acc_ref.shape, F32)

    seg = off_ref[g * E + e]
    cnt = off_ref[g * E + e + 1] - seg
    start = seg - g * per_group
    U = SUBLANES
    _, _, chunks, lanes = x_ref.shape
    last = per_group - 1

    def sub(sb, _):
        s0 = start + sb * R
        nr = jnp.minimum(R, cnt - sb * R)

        def gather(j, _):
            r0 = pl.multiple_of(j * U, U)
            for i in range(U):
                t = tok_ref[jnp.minimum(s0 + r0 + i, last)]
                xg_ref[j, pl.ds(i, chunks, stride=U), :] = x_ref[0, t]
            return 0

        lax.fori_loop(0, R // U, gather, 0)
        xb = jnp.concatenate([xg_ref[:, c * U:(c + 1) * U, :].reshape(R, lanes) for c in range(chunks)],
                             axis=1).astype(BF16)
        hmid = _silu(_dot(xb, wg_ref[0])) * _dot(xb, wu_ref[0])
        y = _dot(hmid.astype(BF16), wd_ref[0])
        for c in range(chunks):
            yb_ref[:, c * U:(c + 1) * U, :] = y[:, c * lanes:(c + 1) * lanes].reshape(R // U, U, lanes)

        def scatter_group(j, _):
            r0 = pl.multiple_of(j * U, U)
            toks = [tok_ref[s0 + r0 + i] for i in range(U)]
            wts = [wl_ref[s0 + r0 + i] for i in range(U)]
            new = [acc_ref[0, toks[i]] + wts[i] * yb_ref[j, pl.ds(i, chunks, stride=U), :] for i in range(U)]
            for i in range(U):
                acc_ref[0, toks[i]] = new[i]
            return 0

        groups = nr // U
        lax.fori_loop(0, groups, scatter_group, 0)

        def scatter_row(r, _):
            t = tok_ref[s0 + r]
            row = yb_ref[r // U, pl.ds(r % U, chunks, stride=U), :]
            acc_ref[0, t] = acc_ref[0, t] + wl_ref[s0 + r] * row
            return 0

        lax.fori_loop(groups * U, nr, scatter_row, 0)
        return 0

    lax.fori_loop(0, (cnt + R - 1) // R, sub, 0)


def routed_experts(h2, top_idx, top_w, counts, exp_gate, exp_up, exp_down):
    N, chunks, lanes = h2.shape
    D = chunks * lanes
    E, _, F = exp_gate.shape
    TG = min(MOE_TOKEN_GROUP, N)
    G = N // TG
    per_group = TG * TOP_K
    key = (jnp.arange(N, dtype=jnp.int32)[None, :] // TG) * E + top_idx
    order = jnp.argsort(key.reshape(-1))
    tok_s = ((order % N) % TG).astype(jnp.int32)
    w_s = top_w.reshape(-1)[order]
    off = jnp.concatenate([jnp.zeros((1,), jnp.int32), jnp.cumsum(counts.reshape(-1)).astype(jnp.int32)])

    grid_spec = pltpu.PrefetchScalarGridSpec(
        num_scalar_prefetch=1,
        grid=(G, E),
        in_specs=[pl.BlockSpec((per_group,), lambda g, e, off: (g,), memory_space=pltpu.SMEM),
                  pl.BlockSpec((per_group,), lambda g, e, off: (g,), memory_space=pltpu.SMEM),
                  pl.BlockSpec((1, TG, chunks, lanes), lambda g, e, off: (g, 0, 0, 0),
                               pipeline_mode=pl.Buffered(1)),
                  pl.BlockSpec((1, D, F), lambda g, e, off: (e, 0, 0)),
                  pl.BlockSpec((1, D, F), lambda g, e, off: (e, 0, 0)),
                  pl.BlockSpec((1, F, D), lambda g, e, off: (e, 0, 0))],
        out_specs=pl.BlockSpec((1, TG, chunks, lanes), lambda g, e, off: (g, 0, 0, 0),
                               pipeline_mode=pl.Buffered(1)),
        scratch_shapes=[pltpu.VMEM((MOE_ROWS // SUBLANES, chunks * SUBLANES, lanes), F32),
                        pltpu.VMEM((MOE_ROWS // SUBLANES, chunks * SUBLANES, lanes), F32)],
    )
    out = pl.pallas_call(
        functools.partial(_moe_kernel, R=MOE_ROWS, E=E, per_group=per_group),
        out_shape=jax.ShapeDtypeStruct((G, TG, chunks, lanes), F32),
        grid_spec=grid_spec,
        compiler_params=pltpu.CompilerParams(
            dimension_semantics=("arbitrary", "arbitrary"), vmem_limit_bytes=VMEM_LIMIT),
        name="routed_experts",
    )(off, tok_s, w_s, h2.reshape(G, TG, chunks, lanes), exp_gate, exp_up, exp_down)
    return out.reshape(N, chunks, lanes)


def _final_kernel(x1_ref, h2_ref, rt_ref, mod_ref, wg_ref, wu_ref, wd_ref, *rest, final):
    if final:
        fg_ref, o_ref = rest
    else:
        (o_ref,) = rest
    hb = _load_row_tiles(h2_ref.at[0]).astype(BF16)
    hmid = _silu(_dot(hb, wg_ref[...])) * _dot(hb, wu_ref[...])
    shared = _dot(hmid.astype(BF16), wd_ref[...])
    x2 = x1_ref[0] + mod_ref[0, 5:6, :] * (_load_row_tiles(rt_ref.at[0]) + shared)
    if final:
        x2 = _rms(x2, fg_ref[...])
    o_ref[0] = x2


def shared_and_residual(x1, h2, routed, mod, sh_gate, sh_up, sh_down, final_gain=None):
    B, T, D = x1.shape
    F = sh_gate.shape[1]
    tm = min(512, T)
    final = final_gain is not None
    tile = pl.BlockSpec((1, tm, D), lambda b, i: (b, i, 0))
    row_tiles = pl.BlockSpec((1, tm, D // HEAD_LANES, HEAD_LANES), lambda b, i: (b, i, 0, 0))
    in_specs = [tile, row_tiles, row_tiles,
                pl.BlockSpec((1, 6, D), lambda b, i: (b, 0, 0)),
                pl.BlockSpec((D, F), lambda b, i: (0, 0)),
                pl.BlockSpec((D, F), lambda b, i: (0, 0)),
                pl.BlockSpec((F, D), lambda b, i: (0, 0))]
    args = [x1, h2, routed, mod, sh_gate, sh_up, sh_down]
    if final:
        in_specs.append(pl.BlockSpec((1, D), lambda b, i: (0, 0)))
        args.append(final_gain.reshape(1, D))
    return pl.pallas_call(
        functools.partial(_final_kernel, final=final),
        out_shape=jax.ShapeDtypeStruct((B, T, D), F32),
        grid=(B, T // tm),
        in_specs=in_specs,
        out_specs=tile,
        compiler_params=pltpu.CompilerParams(
            dimension_semantics=("arbitrary", "arbitrary"), vmem_limit_bytes=VMEM_LIMIT),
        name="shared_and_residual",
    )(*args)


def moe_block(x1, h2, logits, mod, router_bias, exp_gate, exp_up, exp_down, sh_gate, sh_up, sh_down,
              final_gain=None):
    B, T, D = x1.shape
    top_idx, top_w, counts = route(logits, router_bias, min(MOE_TOKEN_GROUP, B * T))
    chunks = D // HEAD_LANES
    routed = routed_experts(h2.reshape(B * T, chunks, HEAD_LANES), top_idx, top_w, counts,
                            exp_gate.astype(BF16), exp_up.astype(BF16), exp_down.astype(BF16))
    return shared_and_residual(x1, h2, routed.reshape(B, T, chunks, HEAD_LANES), mod,
                               sh_gate.astype(BF16), sh_up.astype(BF16), sh_down.astype(BF16), final_gain)


def kernel(x, c, hgrn_lb_logits, l0_norm1, l0_norm2, l0_w_mod, l0_b_mod, l0_w_in, l0_w_out, l0_hgrn_norm, l0_diff_lq1, l0_diff_lk1, l0_diff_lq2, l0_diff_lk2, l0_diff_subln, l0_router, l0_router_bias, l0_exp_gate, l0_exp_up, l0_exp_down, l0_sh_gate, l0_sh_up, l0_sh_down, l1_norm1, l1_norm2, l1_w_mod, l1_b_mod, l1_w_in, l1_w_out, l1_conv_w, l1_conv_b, l1_ml_i_bias, l1_ml_f_bias, l1_ml_norm, l1_na_rpb, l1_router, l1_router_bias, l1_exp_gate, l1_exp_up, l1_exp_down, l1_sh_gate, l1_sh_up, l1_sh_down, final_norm):
    G = GROUP_WIDTH
    lb_all = jnp.cumsum(jax.nn.softmax(hgrn_lb_logits.astype(F32), axis=0), axis=0)
    layer_idx = 0
    lambda_init = 0.8 - 0.6 * math.exp(-0.3 * layer_idx)
    lam = (jnp.exp(jnp.sum(l0_diff_lq1.astype(F32) * l0_diff_lk1.astype(F32)))
           - jnp.exp(jnp.sum(l0_diff_lq2.astype(F32) * l0_diff_lk2.astype(F32))) + lambda_init)

    mod0 = ada_mod(c, l0_w_mod, l0_b_mod)
    y0 = in_proj(x, mod0, l0_norm1, l0_w_in.astype(BF16))
    a_out = hgrn2(y0, lb_all[0], l0_hgrn_norm)
    b_out = diff_attention(y0, lam, l0_diff_subln, layer_idx)
    x1, h2, logits = out_proj(a_out, b_out, x, mod0, l0_norm2, l0_w_out.astype(BF16), l0_router)
    xa = moe_block(x1, h2, logits, mod0, l0_router_bias, l0_exp_gate, l0_exp_up, l0_exp_down,
                   l0_sh_gate, l0_sh_up, l0_sh_down)

    mod1 = ada_mod(c, l1_w_mod, l1_b_mod)
    n_gate = l1_w_in.shape[1] - 7 * G
    w_main = jnp.concatenate([l1_w_in[:, :4 * G], l1_w_in[:, 4 * G + n_gate:]], axis=1).astype(BF16)
    w_gate = jnp.pad(l1_w_in[:, 4 * G:4 * G + n_gate], ((0, 0), (0, HEAD_LANES - n_gate)))
    y1, gates = in_proj(xa, mod1, l1_norm1, w_main, w_gate)
    c_out = mlstm(y1, gates, l1_conv_w, l1_conv_b, l1_ml_i_bias, l1_ml_f_bias, l1_ml_norm)
    d_out = neighbourhood_attention(y1, l1_na_rpb)
    x1, h2, logits = out_proj(c_out, d_out, xa, mod1, l1_norm2, l1_w_out.astype(BF16), l1_router)
    return moe_block(x1, h2, logits, mod1, l1_router_bias, l1_exp_gate, l1_exp_up, l1_exp_down,
                     l1_sh_gate, l1_sh_up, l1_sh_down, final_gain=final_norm)
```

```python
import functools
import math

import numpy as np
import jax
import jax.numpy as jnp
from jax import lax
from jax.experimental import pallas as pl
from jax.experimental.pallas import tpu as pltpu

F32 = jnp.float32
BF16 = jnp.bfloat16
HIGHEST = lax.Precision.HIGHEST
EPS = 1e-6

GRID_W = 64
GROUP_WIDTH = 512
HEAD_LANES = 128
HG_CHUNK = 64
ML_CHUNK = 128
STEP_CHUNKS = 2
ML_CONV = 5
DA_DIM = 64
ROPE_DIM = 16
ROPE_THETA = 500000.0
NA_ROWS = 8
NA_COLS = 16
NA_DIM = 64
NA_STEP_ROWS = 4
N_EXPERTS = 128
TOP_K = 8
N_GROUPS = 8
TOPK_GROUPS = 4
ROUTED_SCALE = 2.5
MOE_ROWS = 256
MOE_TOKEN_GROUP = 4096
MOE_EXPERTS_PER_STEP = 2
SUBLANES = 8
NEG_BIG = -1e30
VMEM_LIMIT = 48 * 1024 * 1024


def _dot(a, b, **kw):
    return jnp.dot(a, b, preferred_element_type=F32, **kw)


def _dot_nt(a, b):
    return lax.dot_general(a, b, (((1,), (1,)), ((), ())), preferred_element_type=F32)


def _dot_tn(a, b):
    return lax.dot_general(a, b, (((0,), (0,)), ((), ())), preferred_element_type=F32)


def _sigmoid(x):
    return jax.nn.sigmoid(x)


def _silu(x):
    return x * jax.nn.sigmoid(x)


def _log_sigmoid(x):
    return jnp.minimum(x, 0.0) - jnp.log(1.0 + jnp.exp(-jnp.abs(x)))


def _rms(x, gain):
    return x * lax.rsqrt(jnp.mean(x * x, axis=-1, keepdims=True) + EPS) * gain


def _store_row_tiles(ref, val):
    for c in range(ref.shape[1]):
        ref[:, c, :] = val[:, c * HEAD_LANES:(c + 1) * HEAD_LANES]


def _load_row_tiles(ref):
    return jnp.concatenate([ref[:, c, :] for c in range(ref.shape[1])], axis=1)


def _mod_kernel(c_ref, w_ref, b_ref, o_ref):
    o_ref[...] = _dot(_silu(c_ref[...]), w_ref[...], precision=HIGHEST) + b_ref[...]


def ada_mod(c, w_mod, b_mod):
    B, D = c.shape
    N = w_mod.shape[1]
    tn = 1024
    out = pl.pallas_call(
        _mod_kernel,
        out_shape=jax.ShapeDtypeStruct((B, N), F32),
        grid=(N // tn,),
        in_specs=[pl.BlockSpec((B, D), lambda j: (0, 0)),
                  pl.BlockSpec((D, tn), lambda j: (0, j)),
                  pl.BlockSpec((1, tn), lambda j: (0, j))],
        out_specs=pl.BlockSpec((B, tn), lambda j: (0, j)),
        name="ada_mod",
    )(c, w_mod, b_mod.reshape(1, N))
    return out.reshape(B, 6, D)


def _in_kernel(x_ref, mod_ref, gain_ref, w_ref, *rest, has_gate):
    if has_gate:
        wg_ref, o_ref, og_ref, h_ref = rest
    else:
        o_ref, h_ref = rest

    @pl.when(pl.program_id(2) == 0)
    def _():
        h = _rms(x_ref[0], gain_ref[...]) * (1.0 + mod_ref[0, 1:2, :]) + mod_ref[0, 0:1, :]
        h_ref[...] = h.astype(BF16)
        if has_gate:
            og_ref[0] = _dot(h, wg_ref[...], precision=HIGHEST)

    o_ref[0] = _dot(h_ref[...], w_ref[...])


def in_proj(x, mod, gain, w_bf16, w_gate=None):
    B, T, D = x.shape
    N = w_bf16.shape[1]
    tm = min(1024, T)
    tn = 512
    has_gate = w_gate is not None
    in_specs = [pl.BlockSpec((1, tm, D), lambda b, i, j: (b, i, 0)),
                pl.BlockSpec((1, 6, D), lambda b, i, j: (b, 0, 0)),
                pl.BlockSpec((1, D), lambda b, i, j: (0, 0)),
                pl.BlockSpec((D, tn), lambda b, i, j: (0, j))]
    out_shape = [jax.ShapeDtypeStruct((B, T, N), F32)]
    out_specs = [pl.BlockSpec((1, tm, tn), lambda b, i, j: (b, i, j))]
    args = [x, mod, gain.reshape(1, D), w_bf16]
    if has_gate:
        in_specs.append(pl.BlockSpec((D, HEAD_LANES), lambda b, i, j: (0, 0)))
        out_shape.append(jax.ShapeDtypeStruct((B, T, HEAD_LANES), F32))
        out_specs.append(pl.BlockSpec((1, tm, HEAD_LANES), lambda b, i, j: (b, i, 0)))
        args.append(w_gate)
    res = pl.pallas_call(
        functools.partial(_in_kernel, has_gate=has_gate),
        out_shape=out_shape,
        grid=(B, T // tm, N // tn),
        in_specs=in_specs,
        out_specs=out_specs,
        scratch_shapes=[pltpu.VMEM((tm, D), BF16)],
        compiler_params=pltpu.CompilerParams(
            dimension_semantics=("arbitrary", "arbitrary", "arbitrary")),
        name="in_proj",
    )(*args)
    return res if has_gate else res[0]


def _hgrn_consts(C):
    t = np.arange(C)
    tri = (t[None, :] <= t[:, None]).astype(np.float32)
    triT = np.ascontiguousarray(tri.T)
    wf, wb, mf = [tri], [triT], []
    levels = int(round(math.log2(C)))
    for l in range(levels):
        size = C >> l
        blk = t // size
        r = blk * size + size // 2
        wf.append(tri - tri[r - 1])
        wb.append(triT - triT[r])
        upper = (t % size) >= size // 2
        mf.append(((blk[:, None] == blk[None, :]) & upper[:, None] & (~upper)[None, :]).astype(np.float32))
    mf.append(np.eye(C, dtype=np.float32))
    ones = np.ones((8, C), np.float32)
    wf.append(ones)
    wb.append(ones)
    mf = np.stack(mf)
    mb = np.ascontiguousarray(np.transpose(mf, (0, 2, 1)))
    return np.concatenate(wf), np.concatenate(wb), mf, mb


def _split3(x):
    hi = x.astype(BF16)
    r1 = x - hi.astype(F32)
    mid = r1.astype(BF16)
    lo = (r1 - mid.astype(F32)).astype(BF16)
    return hi, mid, lo


def _hgrn_kernel(q_ref, i_ref, ff_ref, fb_ref, g_ref, lb_ref, gain_ref, wf_ref, wb_ref, mf_ref, mb_ref,
                 o_ref, of_ref, ob_ref, *, C, T):
    n = T // C
    levels = int(round(math.log2(C)))
    dv = q_ref.shape[-1]

    def prepare(c, f_ref, lbd, w_ref):
        sl = pl.ds(pl.multiple_of(c * C, C), C)
        q = _silu(q_ref[0, sl, :])
        v = i_ref[0, sl, :]
        fg = lbd + (1.0 - lbd) * _sigmoid(f_ref[0, sl, :])
        lf = jnp.log(fg)
        d3 = _dot(w_ref[...], jnp.concatenate(_split3(lf), axis=1))
        dall = d3[:, 0:dv] + d3[:, dv:2 * dv] + d3[:, 2 * dv:3 * dv]
        return dict(sl=sl, q=q, k=1.0 - fg, v=v.astype(BF16), dall=dall)

    def scores(p, m_ref):
        q, k, dall = p["q"], p["k"], p["dall"]
        attn = m_ref[levels] * _dot_nt(q.astype(BF16), k.astype(BF16))
        for l in range(levels):
            e = jnp.exp(-jnp.abs(dall[(l + 1) * C:(l + 2) * C]))
            attn = attn + m_ref[l] * _dot_nt((q * e).astype(BF16), (k * e).astype(BF16))
        return attn.astype(BF16)

    st0 = jnp.zeros((dv, dv), F32)

    def step(j, carry):
        states = list(carry)
        work = []
        for u in range(STEP_CHUNKS):
            work.append((0, prepare(STEP_CHUNKS * j + u, ff_ref, lb_ref[0:1, :], wf_ref), mf_ref, of_ref))
            work.append((1, prepare(n - 1 - STEP_CHUNKS * j - u, fb_ref, lb_ref[1:2, :], wb_ref), mb_ref, ob_ref))
        attn = [scores(p, m_ref) for _, p, m_ref, _ in work]
        local = []
        for (_, p, _, _), a in zip(work, attn):
            cum = p["dall"][0:C]
            tot = p["dall"][(levels + 1) * C:(levels + 1) * C + 1]
            kt = p["k"] * jnp.exp(tot - cum)
            local.append((_dot(a, p["v"]), _dot_tn(p["v"], kt.astype(BF16)), cum, tot))
        for (d, p, _, out_ref), (o_in, incr, cum, tot) in zip(work, local):
            st = states[d]
            out_ref[p["sl"], :] = o_in + _dot_nt((p["q"] * jnp.exp(cum)).astype(BF16), st.astype(BF16))
            states[d] = st * jnp.exp(tot) + incr
        return tuple(states)

    lax.fori_loop(0, n // STEP_CHUNKS, step, (st0, st0))

    rt = min(512, T)
    for r0 in range(0, T, rt):
        sl = slice(r0, r0 + rt)
        o_ref[0, sl, :] = _rms(of_ref[sl, :] + ob_ref[sl, :], gain_ref[...]) * _silu(g_ref[0, sl, :])


def hgrn2(y, lb, norm_gain):
    B, T, _ = y.shape
    H = GROUP_WIDTH // HEAD_LANES
    C = HG_CHUNK
    wf, wb, mf, mb = _hgrn_consts(C)
    wf, wb = jnp.asarray(wf, BF16), jnp.asarray(wb, BF16)
    mf, mb = jnp.asarray(mf), jnp.asarray(mb)

    def col(group):
        return pl.BlockSpec((1, T, HEAD_LANES), lambda b, h, group=group: (b, 0, group * H + h))

    def const(a):
        nd = a.ndim
        return pl.BlockSpec(a.shape, lambda b, h, nd=nd: (0,) * nd)

    return pl.pallas_call(
        functools.partial(_hgrn_kernel, C=C, T=T),
        out_shape=jax.ShapeDtypeStruct((B, T, GROUP_WIDTH), F32),
        grid=(B, H),
        in_specs=[col(0), col(1), col(2), col(3), col(4),
                  pl.BlockSpec((2, HEAD_LANES), lambda b, h: (0, h)),
                  pl.BlockSpec((1, HEAD_LANES), lambda b, h: (0, 0)),
                  const(wf), const(wb), const(mf), const(mb)],
        out_specs=pl.BlockSpec((1, T, HEAD_LANES), lambda b, h: (b, 0, h)),
        scratch_shapes=[pltpu.VMEM((T, HEAD_LANES), F32), pltpu.VMEM((T, HEAD_LANES), F32)],
        compiler_params=pltpu.CompilerParams(
            dimension_semantics=("arbitrary", "arbitrary"), vmem_limit_bytes=VMEM_LIMIT),
        name="hgrn2",
    )(y, y, y, y, y, lb, norm_gain.reshape(1, HEAD_LANES), wf, wb, mf, mb)


def _rope_tables(T):
    pos = np.arange(T, dtype=np.float32)
    inv_freq = (ROPE_THETA ** (-np.arange(0, ROPE_DIM, 2, dtype=np.float32) / ROPE_DIM)).astype(np.float32)
    ang = pos[:, None] * inv_freq[None, :]
    cos, sin = np.cos(ang), np.sin(ang)
    half = ROPE_DIM // 2
    c = np.ones((T, HEAD_LANES), np.float32)
    s_prev = np.zeros((T, HEAD_LANES), np.float32)
    s_next = np.zeros((T, HEAD_LANES), np.float32)
    for base in range(0, HEAD_LANES, DA_DIM):
        c[:, base:base + half] = cos
        c[:, base + half:base + ROPE_DIM] = cos
        s_next[:, base:base + half] = -sin
        s_prev[:, base + half:base + ROPE_DIM] = sin
    return jnp.asarray(c), jnp.asarray(s_prev), jnp.asarray(s_next)


def _rope(x, c, s_prev, s_next):
    half = ROPE_DIM // 2
    lanes = x.shape[-1]
    return (x * c + pltpu.roll(x, half, axis=1) * s_prev
            + pltpu.roll(x, lanes - half, axis=1) * s_next)


def _diff_kernel(lam_ref, q_ref, k_ref, v_ref, cq_ref, spq_ref, snq_ref, ck_ref, spk_ref, snk_ref,
                 subln_ref, o_ref, kr_ref, vb_ref, *, T, out_scale):
    rt = min(512, T)

    @pl.when(pl.program_id(2) == 0)
    def _():
        for r0 in range(0, T, rt):
            sl = slice(r0, r0 + rt)
            kr_ref[sl, :] = _rope(k_ref[0, sl, :], ck_ref[sl, :], spk_ref[sl, :], snk_ref[sl, :]).astype(BF16)
            vb_ref[sl, :] = v_ref[0, sl, :].astype(BF16)

    q = _rope(q_ref[0], cq_ref[...], spq_ref[...], snq_ref[...]) * (DA_DIM ** -0.5 * math.log2(math.e))
    lam = lam_ref[0, 0]
    v = vb_ref[...]
    map_of_lane = lax.broadcasted_iota(jnp.int32, q.shape, 1) // DA_DIM

    scores = [_dot_nt(jnp.where(map_of_lane == m, q, 0.0).astype(BF16), kr_ref[...]) for m in range(2)]
    probs = []
    for s in scores:
        e = jnp.exp2(s - jnp.max(s, axis=-1, keepdims=True))
        probs.append((e.astype(BF16), jnp.sum(e, axis=-1, keepdims=True)))
    pv = [_dot(e, v) / l for e, l in probs]
    o = pv[0] - lam * pv[1]
    o_ref[0] = _rms(o, subln_ref[...]) * out_scale


def diff_attention(y, lam, subln, layer_idx):
    B, T, _ = y.shape
    H = GROUP_WIDTH // HEAD_LANES
    tq = min(256, T)
    lambda_init = 0.8 - 0.6 * math.exp(-0.3 * layer_idx)
    c, sp, sn = _rope_tables(T)

    def col(group, rows):
        if rows == T:
            return pl.BlockSpec((1, T, HEAD_LANES), lambda b, h, i, group=group: (b, 0, group * H + h))
        return pl.BlockSpec((1, rows, HEAD_LANES), lambda b, h, i, group=group: (b, i, group * H + h))

    tab_q = pl.BlockSpec((tq, HEAD_LANES), lambda b, h, i: (i, 0))
    tab_k = pl.BlockSpec((T, HEAD_LANES), lambda b, h, i: (0, 0))
    return pl.pallas_call(
        functools.partial(_diff_kernel, T=T, out_scale=1.0 - lambda_init),
        out_shape=jax.ShapeDtypeStruct((B, T, GROUP_WIDTH), F32),
        grid=(B, H, T // tq),
        in_specs=[pl.BlockSpec(memory_space=pltpu.SMEM),
                  col(5, tq), col(6, T), col(7, T),
                  tab_q, tab_q, tab_q, tab_k, tab_k, tab_k,
                  pl.BlockSpec((1, HEAD_LANES), lambda b, h, i: (0, 0))],
        out_specs=pl.BlockSpec((1, tq, HEAD_LANES), lambda b, h, i: (b, i, h)),
        scratch_shapes=[pltpu.VMEM((T, HEAD_LANES), BF16), pltpu.VMEM((T, HEAD_LANES), BF16)],
        compiler_params=pltpu.CompilerParams(
            dimension_semantics=("arbitrary", "arbitrary", "arbitrary"), vmem_limit_bytes=VMEM_LIMIT),
        name="diff_attention",
    )(lam.reshape(1, 1), y, y, y, c, sp, sn, c, sp, sn, subln.reshape(1, HEAD_LANES))


def _mlstm_kernel(q_ref, k_ref, v_ref, og_ref, gt_ref, cwq_ref, cwk_ref, cbq_ref, cbk_ref, gbias_ref,
                  gain_ref, tri_ref, o_ref, qc_ref, kc_ref, xp_ref, gx_ref, hf_ref, hb_ref, gxt_ref, va_ref,
                  vta_ref, *, C, T, dk):
    n = T // C
    head = pl.program_id(1)
    pad = 8
    half = ML_CONV // 2
    rt = min(512, T)

    xp_ref[0:pad, :] = jnp.zeros((pad, dk), F32)
    xp_ref[pad + T:pad + T + pad, :] = jnp.zeros((pad, dk), F32)
    for src, cw_ref, cb_ref, dst, scale in ((q_ref, cwq_ref, cbq_ref, qc_ref, 1.0),
                                            (k_ref, cwk_ref, cbk_ref, kc_ref, dk ** -0.5)):
        xp_ref[pad:pad + T, :] = src[0]
        for r0 in range(0, T, rt):
            acc = jnp.zeros((rt, dk), F32) + cb_ref[...]
            for j in range(ML_CONV):
                acc = acc + xp_ref[pad + r0 + j - half:pad + r0 + j - half + rt, :] * cw_ref[j:j + 1, :]
            dst[r0:r0 + rt, :] = _silu(acc) * scale

    lane = lax.broadcasted_iota(jnp.int32, (rt, HEAD_LANES), 1)
    is_f = (lane % 8) >= 4
    for r0 in range(0, T, rt):
        g = gt_ref[0, r0:r0 + rt, :] + gbias_ref[...]
        p = jnp.where(is_f, _log_sigmoid(g), g)
        x = jnp.zeros((rt, HEAD_LANES), F32)
        for j, src_lane in enumerate((0, 4, 8, 12)):
            colv = jnp.sum(jnp.where(lane == src_lane + head, p, 0.0), axis=1, keepdims=True)
            x = jnp.where(lane == j, colv, x)
        gx_ref[r0:r0 + rt, :] = x
        gxt_ref[:, r0:r0 + rt] = x.T[0:SUBLANES, :]
        v = v_ref[0, r0:r0 + rt, :]
        va_ref[r0:r0 + rt, 0:dk] = v.astype(BF16)
        va_ref[r0:r0 + rt, dk:2 * dk] = jnp.where(lane == 0, 1.0, 0.0).astype(BF16)
        vta_ref[0:dk, r0:r0 + rt] = v.T.astype(BF16)
        sub = lax.broadcasted_iota(jnp.int32, (dk, rt), 0)
        vta_ref[dk:2 * dk, r0:r0 + rt] = jnp.where(sub == 0, 1.0, 0.0).astype(BF16)

    row = lax.broadcasted_iota(jnp.int32, (C, C), 0)
    colm = lax.broadcasted_iota(jnp.int32, (C, C), 1)

    init = (jnp.zeros((2 * dk, dk), F32), jnp.full((1, 1), NEG_BIG, F32))

    def step(j, carry):
        carries = list(carry)
        work = []
        for u in range(STEP_CHUNKS):
            work.append((0, STEP_CHUNKS * j + u, hf_ref))
            work.append((1, n - 1 - STEP_CHUNKS * j - u, hb_ref))
        chunks = []
        for d, c, out_ref in work:
            sl = pl.ds(pl.multiple_of(c * C, C), C)
            x = gx_ref[sl, :]
            xr = gxt_ref[:, sl]
            chunks.append(dict(d=d, sl=sl, out=out_ref, x=x, xr=xr, q=qc_ref[sl, :], k=kc_ref[sl, :],
                               cumx=_dot(tri_ref[d], x, precision=HIGHEST),
                               cumr=_dot(xr, tri_ref[1 - d], precision=HIGHEST)))
        for p in chunks:
            p["qk"] = _dot_nt(p["q"].astype(BF16), p["k"].astype(BF16))
        for p in chunks:
            d, x, xr, cumx, cumr = p["d"], p["x"], p["xr"], p["cumx"], p["cumr"]
            mask = (colm <= row) if d == 0 else (colm >= row)
            ig_c = x[:, 2 * d:2 * d + 1]
            ig_r = xr[2 * d:2 * d + 1, :]
            cum_c = cumx[:, 2 * d + 1:2 * d + 2]
            cum_r = cumr[2 * d + 1:2 * d + 2, :]
            tot = jnp.sum(x[:, 2 * d + 1:2 * d + 2], axis=0, keepdims=True)
            dmat = jnp.where(mask, cum_c - cum_r + ig_r, -jnp.inf)
            dmax = jnp.max(dmat, axis=1, keepdims=True)
            a = p["qk"] * jnp.exp(dmat - dmax)
            ds = tot - cum_c + ig_c
            dsmax = jnp.max(ds, axis=0, keepdims=True)
            kw = p["k"] * jnp.exp(ds - dsmax)
            p.update(cum_c=cum_c, tot=tot, dmax=dmax, dsmax=dsmax, a=a.astype(BF16), kw=kw.astype(BF16))
        for p in chunks:
            p["num"] = _dot(p["a"], va_ref[p["sl"], :])
            p["upd"] = _dot(vta_ref[:, p["sl"]], p["kw"])
        for p in chunks:
            state, m = carries[p["d"]]
            g = p["cum_c"] + m
            mt = jnp.maximum(g, p["dmax"])
            full = (jnp.exp(g - mt) * _dot_nt(p["q"].astype(BF16), state.astype(BF16))
                    + jnp.exp(p["dmax"] - mt) * p["num"])
            den = full[:, dk:dk + 1]
            p["out"][p["sl"], :] = full[:, 0:dk] / jnp.maximum(jnp.abs(den), jnp.exp(-mt))
            m_new = jnp.maximum(p["tot"] + m, p["dsmax"])
            decay = jnp.exp(p["tot"] + m - m_new)
            scale = jnp.exp(p["dsmax"] - m_new)
            carries[p["d"]] = (decay * state + scale * p["upd"], m_new)
        return tuple(carries)

    lax.fori_loop(0, n // STEP_CHUNKS, step, (init, init))

    for r0 in range(0, T, rt):
        sl = slice(r0, r0 + rt)
        o_ref[0, sl, :] = _rms(hf_ref[sl, :] + hb_ref[sl, :], gain_ref[...]) * _sigmoid(og_ref[0, sl, :])


def mlstm(y, gates, conv_w, conv_b, i_bias, f_bias, norm_gain):
    B, T, _ = y.shape
    H = GROUP_WIDTH // HEAD_LANES
    C = min(ML_CHUNK, T)
    t = np.arange(C)
    tri = np.stack([(t[None, :] <= t[:, None]), (t[None, :] >= t[:, None])]).astype(np.float32)
    gbias = jnp.zeros((1, HEAD_LANES), F32)
    gbias = gbias.at[0, 0:4].set(i_bias[0]).at[0, 4:8].set(f_bias[0])
    gbias = gbias.at[0, 8:12].set(i_bias[1]).at[0, 12:16].set(f_bias[1])

    def col(group):
        return pl.BlockSpec((1, T, HEAD_LANES), lambda b, h, group=group: (b, 0, group * H + h))

    conv_q = pl.BlockSpec((ML_CONV, HEAD_LANES), lambda b, h: (0, h))
    conv_k = pl.BlockSpec((ML_CONV, HEAD_LANES), lambda b, h: (0, H + h))
    bias_q = pl.BlockSpec((1, HEAD_LANES), lambda b, h: (0, h))
    bias_k = pl.BlockSpec((1, HEAD_LANES), lambda b, h: (0, H + h))
    cb = conv_b.reshape(1, -1)
    return pl.pallas_call(
        functools.partial(_mlstm_kernel, C=C, T=T, dk=HEAD_LANES),
        out_shape=jax.ShapeDtypeStruct((B, T, GROUP_WIDTH), F32),
        grid=(B, H),
        in_specs=[col(0), col(1), col(2), col(3),
                  pl.BlockSpec((1, T, HEAD_LANES), lambda b, h: (b, 0, 0)),
                  conv_q, conv_k, bias_q, bias_k,
                  pl.BlockSpec((1, HEAD_LANES), lambda b, h: (0, 0)),
                  pl.BlockSpec((1, HEAD_LANES), lambda b, h: (0, 0)),
                  pl.BlockSpec((2, C, C), lambda b, h: (0, 0, 0))],
        out_specs=pl.BlockSpec((1, T, HEAD_LANES), lambda b, h: (b, 0, h)),
        scratch_shapes=[pltpu.VMEM((T, HEAD_LANES), F32), pltpu.VMEM((T, HEAD_LANES), F32),
                        pltpu.VMEM((T + 16, HEAD_LANES), F32), pltpu.VMEM((T, HEAD_LANES), F32),
                        pltpu.VMEM((T, HEAD_LANES), F32), pltpu.VMEM((T, HEAD_LANES), F32),
                        pltpu.VMEM((SUBLANES, T), F32), pltpu.VMEM((T, 2 * HEAD_LANES), BF16),
                        pltpu.VMEM((2 * HEAD_LANES, T), BF16)],
        compiler_params=pltpu.CompilerParams(
            dimension_semantics=("arbitrary", "arbitrary"), vmem_limit_bytes=VMEM_LIMIT),
        name="mlstm",
    )(y, y, y, y, gates, conv_w, conv_w, cb, cb, gbias, norm_gain.reshape(1, HEAD_LANES), jnp.asarray(tri))


def _na_bias_table(rpb, rows):
    kr = min(NA_ROWS, rows)
    c = np.arange(GRID_W)
    cstart = np.clip(c - NA_COLS // 2, 0, GRID_W - NA_COLS)
    kc = np.arange(GRID_W)
    valid = (kc[None, :] >= cstart[:, None]) & (kc[None, :] < cstart[:, None] + NA_COLS)
    coff = np.clip(kc[None, :] - c[:, None] + NA_COLS - 1, 0, 2 * NA_COLS - 2)
    di = np.arange(kr)
    i = np.arange(kr)
    roff = i[None, :] - di[:, None] + NA_ROWS - 1
    heads = rpb.shape[0]
    cols = jnp.where(jnp.asarray(valid)[None, None], rpb.astype(F32)[:, :, coff], NEG_BIG)
    tab = jnp.take(cols, jnp.asarray(roff.reshape(-1)), axis=1)
    tab = tab.reshape(heads, kr, kr, GRID_W, GRID_W).transpose(0, 1, 3, 2, 4)
    return tab.reshape(heads, kr, GRID_W, kr * GRID_W)


def _na_kernel(q_ref, k_ref, v_ref, bm_ref, o_ref, *, rows, kr):
    W = GRID_W
    heads_per_block = HEAD_LANES // NA_DIM

    head_of_lane = lax.broadcasted_iota(jnp.int32, (W, HEAD_LANES), 1) // NA_DIM

    def body(j, _):
        work = []
        for u in range(NA_STEP_ROWS):
            r = NA_STEP_ROWS * j + u
            rs = jnp.clip(r - kr // 2, 0, rows - kr)
            di = r - rs
            qs = pl.ds(pl.multiple_of(r * W, W), W)
            ks = pl.ds(pl.multiple_of(rs * W, W), kr * W)
            q = q_ref[0, qs, :] * (NA_DIM ** -0.5)
            kw = k_ref[0, ks, :].astype(BF16)
            for hh in range(heads_per_block):
                s = _dot_nt(jnp.where(head_of_lane == hh, q, 0.0).astype(BF16), kw) + bm_ref[hh, di]
                work.append((u, hh, qs, ks, s))
        probs = []
        for u, hh, qs, ks, s in work:
            e = jnp.exp(s - jnp.max(s, axis=-1, keepdims=True))
            probs.append((e.astype(BF16), jnp.sum(e, axis=-1, keepdims=True)))
        outs = {}
        for (u, hh, qs, ks, s), (e, l) in zip(work, probs):
            o = _dot(e, v_ref[0, ks, :].astype(BF16)) / l
            outs[u] = o if hh == 0 else jnp.where(head_of_lane == hh, o, outs[u])
            if hh == heads_per_block - 1:
                o_ref[0, qs, :] = outs[u]
        return 0

    lax.fori_loop(0, rows // NA_STEP_ROWS, body, 0)


def neighbourhood_attention(y, rpb):
    B, T, _ = y.shape
    rows = T // GRID_W
    kr = min(NA_ROWS, rows)
    HB = GROUP_WIDTH // HEAD_LANES
    hpb = HEAD_LANES // NA_DIM
    bm = _na_bias_table(rpb, rows)

    def col(group):
        return pl.BlockSpec((1, T, HEAD_LANES), lambda b, h, group=group: (b, 0, group * HB + h))

    return pl.pallas_call(
        functools.partial(_na_kernel, rows=rows, kr=kr),
        out_shape=jax.ShapeDtypeStruct((B, T, GROUP_WIDTH), F32),
        grid=(B, HB),
        in_specs=[col(4), col(5), col(6),
                  pl.BlockSpec((hpb, kr, GRID_W, kr * GRID_W), lambda b, h: (h, 0, 0, 0))],
        out_specs=pl.BlockSpec((1, T, HEAD_LANES), lambda b, h: (b, 0, h)),
        compiler_params=pltpu.CompilerParams(
            dimension_semantics=("arbitrary", "arbitrary"), vmem_limit_bytes=VMEM_LIMIT),
        name="neighbourhood_attention",
    )(y, y, y, bm)


def _out_kernel(a_ref, b_ref, x_ref, mod_ref, gain_ref, w_ref, r_ref, x1_ref, h2_ref, lg_ref):
    G = a_ref.shape[-1]
    y = _dot(a_ref[0].astype(BF16), w_ref[0:G, :]) + _dot(b_ref[0].astype(BF16), w_ref[G:2 * G, :])
    x1 = x_ref[0] + mod_ref[0, 2:3, :] * y
    x1_ref[0] = x1
    h2 = _rms(x1, gain_ref[...]) * (1.0 + mod_ref[0, 4:5, :]) + mod_ref[0, 3:4, :]
    _store_row_tiles(h2_ref.at[0], h2)
    lg_ref[...] = lax.dot_general(r_ref[...], h2, (((1,), (1,)), ((), ())), precision=HIGHEST,
                                  preferred_element_type=F32)


def out_proj(a_out, b_out, x, mod, gain2, w_out_bf16, router):
    B, T, D = x.shape
    G = a_out.shape[-1]
    E = router.shape[1]
    tm = min(512, T)
    nt = T // tm
    return pl.pallas_call(
        _out_kernel,
        out_shape=[jax.ShapeDtypeStruct((B, T, D), F32),
                   jax.ShapeDtypeStruct((B, T, D // HEAD_LANES, HEAD_LANES), F32),
                   jax.ShapeDtypeStruct((E, B * T), F32)],
        grid=(B, T // tm),
        in_specs=[pl.BlockSpec((1, tm, G), lambda b, i: (b, i, 0)),
                  pl.BlockSpec((1, tm, G), lambda b, i: (b, i, 0)),
                  pl.BlockSpec((1, tm, D), lambda b, i: (b, i, 0)),
                  pl.BlockSpec((1, 6, D), lambda b, i: (b, 0, 0)),
                  pl.BlockSpec((1, D), lambda b, i: (0, 0)),
                  pl.BlockSpec((2 * G, D), lambda b, i: (0, 0)),
                  pl.BlockSpec((E, D), lambda b, i: (0, 0))],
        out_specs=[pl.BlockSpec((1, tm, D), lambda b, i: (b, i, 0)),
                   pl.BlockSpec((1, tm, D // HEAD_LANES, HEAD_LANES), lambda b, i: (b, i, 0, 0)),
                   pl.BlockSpec((E, tm), lambda b, i, nt=nt: (0, b * nt + i))],
        compiler_params=pltpu.CompilerParams(
            dimension_semantics=("arbitrary", "arbitrary"), vmem_limit_bytes=VMEM_LIMIT),
        name="out_proj",
    )(a_out, b_out, x, mod, gain2.reshape(1, D), w_out_bf16, router.T)


def _route_kernel(lg_ref, bias_ref, idx_ref, w_ref, cnt_ref, *, tiles_per_group):
    @pl.when(pl.program_id(0) % tiles_per_group == 0)
    def _():
        cnt_ref[...] = jnp.zeros(cnt_ref.shape, F32)

    scores = _sigmoid(lg_ref[...])
    sel = scores + bias_ref[...]
    E, tm = sel.shape
    per_group = E // N_GROUPS
    neg = -jnp.inf
    eid = lax.broadcasted_iota(jnp.int32, (E, tm), 0).astype(F32)
    eid_g = lax.broadcasted_iota(jnp.int32, (per_group, tm), 0).astype(F32)

    def first_argmax(x, ids, sentinel):
        m = jnp.max(x, axis=0, keepdims=True)
        i = jnp.min(jnp.where(x == m, ids, sentinel), axis=0, keepdims=True)
        return m, i

    parts, gscore = [], []
    for g in range(N_GROUPS):
        x = sel[g * per_group:(g + 1) * per_group]
        parts.append(x)
        m1, i1 = first_argmax(x, eid_g, float(per_group))
        m2 = jnp.max(jnp.where(eid_g == i1, neg, x), axis=0, keepdims=True)
        gscore.append(m1 + m2)
    kept = []
    for g in range(N_GROUPS):
        beaten = jnp.zeros((1, tm), F32)
        for o in range(N_GROUPS):
            if o == g:
                continue
            wins = (gscore[o] >= gscore[g]) if o < g else (gscore[o] > gscore[g])
            beaten = beaten + wins.astype(F32)
        kept.append(jnp.where(beaten < TOPK_GROUPS, parts[g], neg))
    sel = jnp.concatenate(kept, axis=0)

    ids, vals = [], []
    w_sum = jnp.zeros((1, tm), F32)
    chosen = jnp.zeros((E, tm), F32)
    for k in range(TOP_K):
        _, i = first_argmax(sel, eid, float(E))
        hit = eid == i
        val = jnp.sum(jnp.where(hit, scores, 0.0), axis=0, keepdims=True)
        sel = jnp.where(hit, neg, sel)
        chosen = jnp.where(hit, 1.0, chosen)
        ids.append(i)
        vals.append(val)
        w_sum = w_sum + val
    idx_ref[...] = jnp.concatenate(ids, axis=0).astype(jnp.int32)
    w_ref[...] = jnp.concatenate(vals, axis=0) / w_sum * ROUTED_SCALE
    part = chosen[:, 0:HEAD_LANES]
    for l0 in range(HEAD_LANES, tm, HEAD_LANES):
        part = part + chosen[:, l0:l0 + HEAD_LANES]
    cnt_ref[0] = cnt_ref[0] + part


def route(logits_t, router_bias, group_tokens):
    E, N = logits_t.shape
    tm = min(512, N)
    tpg = group_tokens // tm
    idx, w, cnt = pl.pallas_call(
        functools.partial(_route_kernel, tiles_per_group=tpg),
        out_shape=[jax.ShapeDtypeStruct((TOP_K, N), jnp.int32), jax.ShapeDtypeStruct((TOP_K, N), F32),
                   jax.ShapeDtypeStruct((N // group_tokens, E, HEAD_LANES), F32)],
        grid=(N // tm,),
        in_specs=[pl.BlockSpec((E, tm), lambda i: (0, i)), pl.BlockSpec((E, 1), lambda i: (0, 0))],
        out_specs=[pl.BlockSpec((TOP_K, tm), lambda i: (0, i)), pl.BlockSpec((TOP_K, tm), lambda i: (0, i)),
                   pl.BlockSpec((1, E, HEAD_LANES), lambda i, tpg=tpg: (i // tpg, 0, 0))],
        compiler_params=pltpu.CompilerParams(dimension_semantics=("arbitrary",)),
        name="route",
    )(logits_t, router_bias.reshape(E, 1))
    return idx, w, jnp.sum(cnt, axis=-1).astype(jnp.int32)


def _moe_kernel(off_ref, tok_ref, wl_ref, x_ref, wg_ref, wu_ref, wd_ref, acc_ref, xg_ref, yb_ref,
                *, R, E, per_group, experts_per_step):
    g = pl.program_id(0)
    step = pl.program_id(1)

    @pl.when(step == 0)
    def _():
        acc_ref[...] = jnp.zeros(acc_ref.shape, F32)

    @pl.when((g == 0) & (step == 0))
    def _():
        xg_ref[...] = jnp.zeros(xg_ref.shape, F32)

    for le in range(experts_per_step):
        _moe_expert(off_ref, tok_ref, wl_ref, x_ref, wg_ref.at[le], wu_ref.at[le], wd_ref.at[le], acc_ref,
                    xg_ref, yb_ref, g * E + step * experts_per_step + le, g * per_group, R, per_group)


def _moe_expert(off_ref, tok_ref, wl_ref, x_ref, wg_ref, wu_ref, wd_ref, acc_ref, xg_ref, yb_ref,
                segment, group_base, R, per_group):
    seg = off_ref[segment]
    cnt = off_ref[segment + 1] - seg
    start = seg - group_base
    U = SUBLANES
    _, _, chunks, lanes = x_ref.shape
    last = per_group - 1

    def sub(sb, _):
        s0 = start + sb * R
        nr = jnp.minimum(R, cnt - sb * R)

        def gather(j, _):
            r0 = pl.multiple_of(j * U, U)
            for i in range(U):
                t = tok_ref[jnp.minimum(s0 + r0 + i, last)]
                xg_ref[j, pl.ds(i, chunks, stride=U), :] = x_ref[0, t]
            return 0

        lax.fori_loop(0, (nr + U - 1) // U, gather, 0)
        xb = jnp.concatenate([xg_ref[:, c * U:(c + 1) * U, :].reshape(R, lanes) for c in range(chunks)],
                             axis=1).astype(BF16)
        hmid = _silu(_dot(xb, wg_ref[...])) * _dot(xb, wu_ref[...])
        y = _dot(hmid.astype(BF16), wd_ref[...])
        for c in range(chunks):
            yb_ref[:, c * U:(c + 1) * U, :] = y[:, c * lanes:(c + 1) * lanes].reshape(R // U, U, lanes)

        def scatter_group(j, _):
            r0 = pl.multiple_of(j * U, U)
            toks = [tok_ref[s0 + r0 + i] for i in range(U)]
            wts = [wl_ref[s0 + r0 + i] for i in range(U)]
            new = [acc_ref[0, toks[i]] + wts[i] * yb_ref[j, pl.ds(i, chunks, stride=U), :] for i in range(U)]
            for i in range(U):
                acc_ref[0, toks[i]] = new[i]
            return 0

        groups = nr // U
        lax.fori_loop(0, groups, scatter_group, 0)

        def scatter_row(r, _):
            t = tok_ref[s0 + r]
            row = yb_ref[r // U, pl.ds(r % U, chunks, stride=U), :]
            acc_ref[0, t] = acc_ref[0, t] + wl_ref[s0 + r] * row
            return 0

        lax.fori_loop(groups * U, nr, scatter_row, 0)
        return 0

    lax.fori_loop(0, (cnt + R - 1) // R, sub, 0)


def routed_experts(h2, top_idx, top_w, counts, exp_gate, exp_up, exp_down):
    N, chunks, lanes = h2.shape
    D = chunks * lanes
    E, _, F = exp_gate.shape
    TG = min(MOE_TOKEN_GROUP, N)
    G = N // TG
    per_group = TG * TOP_K
    key = (jnp.arange(N, dtype=jnp.int32)[None, :] // TG) * E + top_idx
    order = jnp.argsort(key.reshape(-1))
    tok_s = ((order % N) % TG).astype(jnp.int32)
    w_s = top_w.reshape(-1)[order]
    off = jnp.concatenate([jnp.zeros((1,), jnp.int32), jnp.cumsum(counts.reshape(-1)).astype(jnp.int32)])

    EPS = MOE_EXPERTS_PER_STEP
    grid_spec = pltpu.PrefetchScalarGridSpec(
        num_scalar_prefetch=1,
        grid=(G, E // EPS),
        in_specs=[pl.BlockSpec((per_group,), lambda g, e, off: (g,), memory_space=pltpu.SMEM),
                  pl.BlockSpec((per_group,), lambda g, e, off: (g,), memory_space=pltpu.SMEM),
                  pl.BlockSpec((1, TG, chunks, lanes), lambda g, e, off: (g, 0, 0, 0),
                               pipeline_mode=pl.Buffered(1)),
                  pl.BlockSpec((EPS, D, F), lambda g, e, off: (e, 0, 0)),
                  pl.BlockSpec((EPS, D, F), lambda g, e, off: (e, 0, 0)),
                  pl.BlockSpec((EPS, F, D), lambda g, e, off: (e, 0, 0))],
        out_specs=pl.BlockSpec((1, TG, chunks, lanes), lambda g, e, off: (g, 0, 0, 0),
                               pipeline_mode=pl.Buffered(1)),
        scratch_shapes=[pltpu.VMEM((MOE_ROWS // SUBLANES, chunks * SUBLANES, lanes), F32),
                        pltpu.VMEM((MOE_ROWS // SUBLANES, chunks * SUBLANES, lanes), F32)],
    )
    out = pl.pallas_call(
        functools.partial(_moe_kernel, R=MOE_ROWS, E=E, per_group=per_group, experts_per_step=EPS),
        out_shape=jax.ShapeDtypeStruct((G, TG, chunks, lanes), F32),
        grid_spec=grid_spec,
        compiler_params=pltpu.CompilerParams(
            dimension_semantics=("arbitrary", "arbitrary"), vmem_limit_bytes=VMEM_LIMIT),
        name="routed_experts",
    )(off, tok_s, w_s, h2.reshape(G, TG, chunks, lanes), exp_gate, exp_up, exp_down)
    return out.reshape(N, chunks, lanes)


def _final_kernel(x1_ref, h2_ref, rt_ref, mod_ref, wg_ref, wu_ref, wd_ref, *rest, final):
    if final:
        fg_ref, o_ref = rest
    else:
        (o_ref,) = rest
    hb = _load_row_tiles(h2_ref.at[0]).astype(BF16)
    hmid = _silu(_dot(hb, wg_ref[...])) * _dot(hb, wu_ref[...])
    shared = _dot(hmid.astype(BF16), wd_ref[...])
    x2 = x1_ref[0] + mod_ref[0, 5:6, :] * (_load_row_tiles(rt_ref.at[0]) + shared)
    if final:
        x2 = _rms(x2, fg_ref[...])
    o_ref[0] = x2


def shared_and_residual(x1, h2, routed, mod, sh_gate, sh_up, sh_down, final_gain=None):
    B, T, D = x1.shape
    F = sh_gate.shape[1]
    tm = min(512, T)
    final = final_gain is not None
    tile = pl.BlockSpec((1, tm, D), lambda b, i: (b, i, 0))
    row_tiles = pl.BlockSpec((1, tm, D // HEAD_LANES, HEAD_LANES), lambda b, i: (b, i, 0, 0))
    in_specs = [tile, row_tiles, row_tiles,
                pl.BlockSpec((1, 6, D), lambda b, i: (b, 0, 0)),
                pl.BlockSpec((D, F), lambda b, i: (0, 0)),
                pl.BlockSpec((D, F), lambda b, i: (0, 0)),
                pl.BlockSpec((F, D), lambda b, i: (0, 0))]
    args = [x1, h2, routed, mod, sh_gate, sh_up, sh_down]
    if final:
        in_specs.append(pl.BlockSpec((1, D), lambda b, i: (0, 0)))
        args.append(final_gain.reshape(1, D))
    return pl.pallas_call(
        functools.partial(_final_kernel, final=final),
        out_shape=jax.ShapeDtypeStruct((B, T, D), F32),
        grid=(B, T // tm),
        in_specs=in_specs,
        out_specs=tile,
        compiler_params=pltpu.CompilerParams(
            dimension_semantics=("arbitrary", "arbitrary"), vmem_limit_bytes=VMEM_LIMIT),
        name="shared_and_residual",
    )(*args)


def moe_block(x1, h2, logits, mod, router_bias, exp_gate, exp_up, exp_down, sh_gate, sh_up, sh_down,
              final_gain=None):
    B, T, D = x1.shape
    top_idx, top_w, counts = route(logits, router_bias, min(MOE_TOKEN_GROUP, B * T))
    chunks = D // HEAD_LANES
    routed = routed_experts(h2.reshape(B * T, chunks, HEAD_LANES), top_idx, top_w, counts,
                            exp_gate.astype(BF16), exp_up.astype(BF16), exp_down.astype(BF16))
    return shared_and_residual(x1, h2, routed.reshape(B, T, chunks, HEAD_LANES), mod,
                               sh_gate.astype(BF16), sh_up.astype(BF16), sh_down.astype(BF16), final_gain)


def kernel(x, c, hgrn_lb_logits, l0_norm1, l0_norm2, l0_w_mod, l0_b_mod, l0_w_in, l0_w_out, l0_hgrn_norm, l0_diff_lq1, l0_diff_lk1, l0_diff_lq2, l0_diff_lk2, l0_diff_subln, l0_router, l0_router_bias, l0_exp_gate, l0_exp_up, l0_exp_down, l0_sh_gate, l0_sh_up, l0_sh_down, l1_norm1, l1_norm2, l1_w_mod, l1_b_mod, l1_w_in, l1_w_out, l1_conv_w, l1_conv_b, l1_ml_i_bias, l1_ml_f_bias, l1_ml_norm, l1_na_rpb, l1_router, l1_router_bias, l1_exp_gate, l1_exp_up, l1_exp_down, l1_sh_gate, l1_sh_up, l1_sh_down, final_norm):
    G = GROUP_WIDTH
    lb_all = jnp.cumsum(jax.nn.softmax(hgrn_lb_logits.astype(F32), axis=0), axis=0)
    layer_idx = 0
    lambda_init = 0.8 - 0.6 * math.exp(-0.3 * layer_idx)
    lam = (jnp.exp(jnp.sum(l0_diff_lq1.astype(F32) * l0_diff_lk1.astype(F32)))
           - jnp.exp(jnp.sum(l0_diff_lq2.astype(F32) * l0_diff_lk2.astype(F32))) + lambda_init)

    mod0 = ada_mod(c, l0_w_mod, l0_b_mod)
    y0 = in_proj(x, mod0, l0_norm1, l0_w_in.astype(BF16))
    a_out = hgrn2(y0, lb_all[0], l0_hgrn_norm)
    b_out = diff_attention(y0, lam, l0_diff_subln, layer_idx)
    x1, h2, logits = out_proj(a_out, b_out, x, mod0, l0_norm2, l0_w_out.astype(BF16), l0_router)
    xa = moe_block(x1, h2, logits, mod0, l0_router_bias, l0_exp_gate, l0_exp_up, l0_exp_down,
                   l0_sh_gate, l0_sh_up, l0_sh_down)

    mod1 = ada_mod(c, l1_w_mod, l1_b_mod)
    n_gate = l1_w_in.shape[1] - 7 * G
    w_main = jnp.concatenate([l1_w_in[:, :4 * G], l1_w_in[:, 4 * G + n_gate:]], axis=1).astype(BF16)
    w_gate = jnp.pad(l1_w_in[:, 4 * G:4 * G + n_gate], ((0, 0), (0, HEAD_LANES - n_gate)))
    y1, gates = in_proj(xa, mod1, l1_norm1, w_main, w_gate)
    c_out = mlstm(y1, gates, l1_conv_w, l1_conv_b, l1_ml_i_bias, l1_ml_f_bias, l1_ml_norm)
    d_out = neighbourhood_attention(y1, l1_na_rpb)
    x1, h2, logits = out_proj(c_out, d_out, xa, mod1, l1_norm2, l1_w_out.astype(BF16), l1_router)
    return moe_block(x1, h2, logits, mod1, l1_router_bias, l1_exp_gate, l1_exp_up, l1_exp_down,
                     l1_sh_gate, l1_sh_up, l1_sh_down, final_gain=final_norm)
```

```python
import functools
import math

import numpy as np
import jax
import jax.numpy as jnp
from jax import lax
from jax.experimental import pallas as pl
from jax.experimental.pallas import tpu as pltpu

F32 = jnp.float32
BF16 = jnp.bfloat16
HIGHEST = lax.Precision.HIGHEST
EPS = 1e-6

GRID_W = 64
GROUP_WIDTH = 512
HEAD_LANES = 128
HG_CHUNK = 64
ML_CHUNK = 128
STEP_CHUNKS = 2
ML_CONV = 5
DA_DIM = 64
ROPE_DIM = 16
ROPE_THETA = 500000.0
NA_ROWS = 8
NA_COLS = 16
NA_DIM = 64
NA_STEP_ROWS = 4
N_EXPERTS = 128
TOP_K = 8
N_GROUPS = 8
TOPK_GROUPS = 4
ROUTED_SCALE = 2.5
MOE_ROWS = 320
MOE_TOKEN_GROUP = 4096
MOE_EXPERTS_PER_STEP = 2
SUBLANES = 8
NEG_BIG = -1e30
VMEM_LIMIT = 48 * 1024 * 1024


def _dot(a, b, **kw):
    return jnp.dot(a, b, preferred_element_type=F32, **kw)


def _dot_nt(a, b):
    return lax.dot_general(a, b, (((1,), (1,)), ((), ())), preferred_element_type=F32)


def _dot_tn(a, b):
    return lax.dot_general(a, b, (((0,), (0,)), ((), ())), preferred_element_type=F32)


def _sigmoid(x):
    return jax.nn.sigmoid(x)


def _silu(x):
    return x * jax.nn.sigmoid(x)


def _log_sigmoid(x):
    return jnp.minimum(x, 0.0) - jnp.log(1.0 + jnp.exp(-jnp.abs(x)))


def _rms(x, gain):
    return x * lax.rsqrt(jnp.mean(x * x, axis=-1, keepdims=True) + EPS) * gain


def _store_row_tiles(ref, val):
    for c in range(ref.shape[1]):
        ref[:, c, :] = val[:, c * HEAD_LANES:(c + 1) * HEAD_LANES]


def _load_row_tiles(ref):
    return jnp.concatenate([ref[:, c, :] for c in range(ref.shape[1])], axis=1)


def _mod_kernel(c_ref, w_ref, b_ref, o_ref):
    o_ref[...] = _dot(_silu(c_ref[...]), w_ref[...], precision=HIGHEST) + b_ref[...]


def ada_mod(c, w_mod, b_mod):
    B, D = c.shape
    N = w_mod.shape[1]
    tn = 1024
    out = pl.pallas_call(
        _mod_kernel,
        out_shape=jax.ShapeDtypeStruct((B, N), F32),
        grid=(N // tn,),
        in_specs=[pl.BlockSpec((B, D), lambda j: (0, 0)),
                  pl.BlockSpec((D, tn), lambda j: (0, j)),
                  pl.BlockSpec((1, tn), lambda j: (0, j))],
        out_specs=pl.BlockSpec((B, tn), lambda j: (0, j)),
        name="ada_mod",
    )(c, w_mod, b_mod.reshape(1, N))
    return out.reshape(B, 6, D)


def _in_kernel(x_ref, mod_ref, gain_ref, w_ref, *rest, has_gate):
    if has_gate:
        wg_ref, o_ref, og_ref, h_ref = rest
    else:
        o_ref, h_ref = rest

    @pl.when(pl.program_id(2) == 0)
    def _():
        h = _rms(x_ref[0], gain_ref[...]) * (1.0 + mod_ref[0, 1:2, :]) + mod_ref[0, 0:1, :]
        h_ref[...] = h.astype(BF16)
        if has_gate:
            og_ref[0] = _dot(h, wg_ref[...], precision=HIGHEST)

    o_ref[0] = _dot(h_ref[...], w_ref[...])


def in_proj(x, mod, gain, w_bf16, w_gate=None):
    B, T, D = x.shape
    N = w_bf16.shape[1]
    tm = min(1024, T)
    tn = 512
    has_gate = w_gate is not None
    in_specs = [pl.BlockSpec((1, tm, D), lambda b, i, j: (b, i, 0)),
                pl.BlockSpec((1, 6, D), lambda b, i, j: (b, 0, 0)),
                pl.BlockSpec((1, D), lambda b, i, j: (0, 0)),
                pl.BlockSpec((D, tn), lambda b, i, j: (0, j))]
    out_shape = [jax.ShapeDtypeStruct((B, T, N), F32)]
    out_specs = [pl.BlockSpec((1, tm, tn), lambda b, i, j: (b, i, j))]
    args = [x, mod, gain.reshape(1, D), w_bf16]
    if has_gate:
        in_specs.append(pl.BlockSpec((D, HEAD_LANES), lambda b, i, j: (0, 0)))
        out_shape.append(jax.ShapeDtypeStruct((B, T, HEAD_LANES), F32))
        out_specs.append(pl.BlockSpec((1, tm, HEAD_LANES), lambda b, i, j: (b, i, 0)))
        args.append(w_gate)
    res = pl.pallas_call(
        functools.partial(_in_kernel, has_gate=has_gate),
        out_shape=out_shape,
        grid=(B, T // tm, N // tn),
        in_specs=in_specs,
        out_specs=out_specs,
        scratch_shapes=[pltpu.VMEM((tm, D), BF16)],
        compiler_params=pltpu.CompilerParams(
            dimension_semantics=("arbitrary", "arbitrary", "arbitrary")),
        name="in_proj",
    )(*args)
    return res if has_gate else res[0]


def _hgrn_consts(C):
    t = np.arange(C)
    tri = (t[None, :] <= t[:, None]).astype(np.float32)
    triT = np.ascontiguousarray(tri.T)
    wf, wb, mf = [tri], [triT], []
    levels = int(round(math.log2(C)))
    for l in range(levels):
        size = C >> l
        blk = t // size
        r = blk * size + size // 2
        wf.append(tri - tri[r - 1])
        wb.append(triT - triT[r])
        upper = (t % size) >= size // 2
        mf.append(((blk[:, None] == blk[None, :]) & upper[:, None] & (~upper)[None, :]).astype(np.float32))
    mf.append(np.eye(C, dtype=np.float32))
    ones = np.ones((8, C), np.float32)
    wf.append(ones)
    wb.append(ones)
    mf = np.stack(mf)
    mb = np.ascontiguousarray(np.transpose(mf, (0, 2, 1)))
    return np.concatenate(wf), np.concatenate(wb), mf, mb


def _split3(x):
    hi = x.astype(BF16)
    r1 = x - hi.astype(F32)
    mid = r1.astype(BF16)
    lo = (r1 - mid.astype(F32)).astype(BF16)
    return hi, mid, lo


def _hgrn_kernel(q_ref, i_ref, ff_ref, fb_ref, g_ref, lb_ref, gain_ref, wf_ref, wb_ref, mf_ref, mb_ref,
                 o_ref, of_ref, ob_ref, *, C, T):
    n = T // C
    levels = int(round(math.log2(C)))
    dv = q_ref.shape[-1]

    def prepare(c, f_ref, lbd, w_ref):
        sl = pl.ds(pl.multiple_of(c * C, C), C)
        q = _silu(q_ref[0, sl, :])
        v = i_ref[0, sl, :]
        fg = lbd + (1.0 - lbd) * _sigmoid(f_ref[0, sl, :])
        lf = jnp.log(fg)
        d3 = _dot(w_ref[...], jnp.concatenate(_split3(lf), axis=1))
        dall = d3[:, 0:dv] + d3[:, dv:2 * dv] + d3[:, 2 * dv:3 * dv]
        return dict(sl=sl, q=q, k=1.0 - fg, v=v.astype(BF16), dall=dall)

    def scores(p, m_ref):
        q, k, dall = p["q"], p["k"], p["dall"]
        attn = m_ref[levels] * _dot_nt(q.astype(BF16), k.astype(BF16))
        for l in range(levels):
            e = jnp.exp(-jnp.abs(dall[(l + 1) * C:(l + 2) * C]))
            attn = attn + m_ref[l] * _dot_nt((q * e).astype(BF16), (k * e).astype(BF16))
        return attn.astype(BF16)

    st0 = jnp.zeros((dv, dv), F32)

    def step(j, carry):
        states = list(carry)
        work = []
        for u in range(STEP_CHUNKS):
            work.append((0, prepare(STEP_CHUNKS * j + u, ff_ref, lb_ref[0:1, :], wf_ref), mf_ref, of_ref))
            work.append((1, prepare(n - 1 - STEP_CHUNKS * j - u, fb_ref, lb_ref[1:2, :], wb_ref), mb_ref, ob_ref))
        attn = [scores(p, m_ref) for _, p, m_ref, _ in work]
        local = []
        for (_, p, _, _), a in zip(work, attn):
            cum = p["dall"][0:C]
            tot = p["dall"][(levels + 1) * C:(levels + 1) * C + 1]
            kt = p["k"] * jnp.exp(tot - cum)
            local.append((_dot(a, p["v"]), _dot_tn(p["v"], kt.astype(BF16)), cum, tot))
        for (d, p, _, out_ref), (o_in, incr, cum, tot) in zip(work, local):
            st = states[d]
            out_ref[p["sl"], :] = o_in + _dot_nt((p["q"] * jnp.exp(cum)).astype(BF16), st.astype(BF16))
            states[d] = st * jnp.exp(tot) + incr
        return tuple(states)

    lax.fori_loop(0, n // STEP_CHUNKS, step, (st0, st0))

    rt = min(512, T)
    for r0 in range(0, T, rt):
        sl = slice(r0, r0 + rt)
        o_ref[0, sl, :] = _rms(of_ref[sl, :] + ob_ref[sl, :], gain_ref[...]) * _silu(g_ref[0, sl, :])


def hgrn2(y, lb, norm_gain):
    B, T, _ = y.shape
    H = GROUP_WIDTH // HEAD_LANES
    C = HG_CHUNK
    wf, wb, mf, mb = _hgrn_consts(C)
    wf, wb = jnp.asarray(wf, BF16), jnp.asarray(wb, BF16)
    mf, mb = jnp.asarray(mf), jnp.asarray(mb)

    def col(group):
        return pl.BlockSpec((1, T, HEAD_LANES), lambda b, h, group=group: (b, 0, group * H + h))

    def const(a):
        nd = a.ndim
        return pl.BlockSpec(a.shape, lambda b, h, nd=nd: (0,) * nd)

    return pl.pallas_call(
        functools.partial(_hgrn_kernel, C=C, T=T),
        out_shape=jax.ShapeDtypeStruct((B, T, GROUP_WIDTH), F32),
        grid=(B, H),
        in_specs=[col(0), col(1), col(2), col(3), col(4),
                  pl.BlockSpec((2, HEAD_LANES), lambda b, h: (0, h)),
                  pl.BlockSpec((1, HEAD_LANES), lambda b, h: (0, 0)),
                  const(wf), const(wb), const(mf), const(mb)],
        out_specs=pl.BlockSpec((1, T, HEAD_LANES), lambda b, h: (b, 0, h)),
        scratch_shapes=[pltpu.VMEM((T, HEAD_LANES), F32), pltpu.VMEM((T, HEAD_LANES), F32)],
        compiler_params=pltpu.CompilerParams(
            dimension_semantics=("arbitrary", "arbitrary"), vmem_limit_bytes=VMEM_LIMIT),
        name="hgrn2",
    )(y, y, y, y, y, lb, norm_gain.reshape(1, HEAD_LANES), wf, wb, mf, mb)


def _rope_tables(T):
    pos = np.arange(T, dtype=np.float32)
    inv_freq = (ROPE_THETA ** (-np.arange(0, ROPE_DIM, 2, dtype=np.float32) / ROPE_DIM)).astype(np.float32)
    ang = pos[:, None] * inv_freq[None, :]
    cos, sin = np.cos(ang), np.sin(ang)
    half = ROPE_DIM // 2
    c = np.ones((T, HEAD_LANES), np.float32)
    s_prev = np.zeros((T, HEAD_LANES), np.float32)
    s_next = np.zeros((T, HEAD_LANES), np.float32)
    for base in range(0, HEAD_LANES, DA_DIM):
        c[:, base:base + half] = cos
        c[:, base + half:base + ROPE_DIM] = cos
        s_next[:, base:base + half] = -sin
        s_prev[:, base + half:base + ROPE_DIM] = sin
    return jnp.asarray(c), jnp.asarray(s_prev), jnp.asarray(s_next)


def _rope(x, c, s_prev, s_next):
    half = ROPE_DIM // 2
    lanes = x.shape[-1]
    return (x * c + pltpu.roll(x, half, axis=1) * s_prev
            + pltpu.roll(x, lanes - half, axis=1) * s_next)


def _diff_kernel(lam_ref, q_ref, k_ref, v_ref, cq_ref, spq_ref, snq_ref, ck_ref, spk_ref, snk_ref,
                 subln_ref, o_ref, kr_ref, vb_ref, *, T, out_scale):
    rt = min(512, T)

    @pl.when(pl.program_id(2) == 0)
    def _():
        for r0 in range(0, T, rt):
            sl = slice(r0, r0 + rt)
            kr_ref[sl, :] = _rope(k_ref[0, sl, :], ck_ref[sl, :], spk_ref[sl, :], snk_ref[sl, :]).astype(BF16)
            vb_ref[sl, :] = v_ref[0, sl, :].astype(BF16)

    q = _rope(q_ref[0], cq_ref[...], spq_ref[...], snq_ref[...]) * (DA_DIM ** -0.5 * math.log2(math.e))
    lam = lam_ref[0, 0]
    v = vb_ref[...]
    map_of_lane = lax.broadcasted_iota(jnp.int32, q.shape, 1) // DA_DIM

    scores = [_dot_nt(jnp.where(map_of_lane == m, q, 0.0).astype(BF16), kr_ref[...]) for m in range(2)]
    probs = []
    for s in scores:
        e = jnp.exp2(s - jnp.max(s, axis=-1, keepdims=True))
        probs.append((e.astype(BF16), jnp.sum(e, axis=-1, keepdims=True)))
    pv = [_dot(e, v) / l for e, l in probs]
    o = pv[0] - lam * pv[1]
    o_ref[0] = _rms(o, subln_ref[...]) * out_scale


def diff_attention(y, lam, subln, layer_idx):
    B, T, _ = y.shape
    H = GROUP_WIDTH // HEAD_LANES
    tq = min(256, T)
    lambda_init = 0.8 - 0.6 * math.exp(-0.3 * layer_idx)
    c, sp, sn = _rope_tables(T)

    def col(group, rows):
        if rows == T:
            return pl.BlockSpec((1, T, HEAD_LANES), lambda b, h, i, group=group: (b, 0, group * H + h))
        return pl.BlockSpec((1, rows, HEAD_LANES), lambda b, h, i, group=group: (b, i, group * H + h))

    tab_q = pl.BlockSpec((tq, HEAD_LANES), lambda b, h, i: (i, 0))
    tab_k = pl.BlockSpec((T, HEAD_LANES), lambda b, h, i: (0, 0))
    return pl.pallas_call(
        functools.partial(_diff_kernel, T=T, out_scale=1.0 - lambda_init),
        out_shape=jax.ShapeDtypeStruct((B, T, GROUP_WIDTH), F32),
        grid=(B, H, T // tq),
        in_specs=[pl.BlockSpec(memory_space=pltpu.SMEM),
                  col(5, tq), col(6, T), col(7, T),
                  tab_q, tab_q, tab_q, tab_k, tab_k, tab_k,
                  pl.BlockSpec((1, HEAD_LANES), lambda b, h, i: (0, 0))],
        out_specs=pl.BlockSpec((1, tq, HEAD_LANES), lambda b, h, i: (b, i, h)),
        scratch_shapes=[pltpu.VMEM((T, HEAD_LANES), BF16), pltpu.VMEM((T, HEAD_LANES), BF16)],
        compiler_params=pltpu.CompilerParams(
            dimension_semantics=("arbitrary", "arbitrary", "arbitrary"), vmem_limit_bytes=VMEM_LIMIT),
        name="diff_attention",
    )(lam.reshape(1, 1), y, y, y, c, sp, sn, c, sp, sn, subln.reshape(1, HEAD_LANES))


def _mlstm_kernel(q_ref, k_ref, v_ref, og_ref, gt_ref, cwq_ref, cwk_ref, cbq_ref, cbk_ref, gbias_ref,
                  gain_ref, tri_ref, o_ref, qc_ref, kc_ref, xp_ref, gx_ref, hf_ref, hb_ref, gxt_ref, va_ref,
                  vta_ref, *, C, T, dk):
    n = T // C
    head = pl.program_id(1)
    pad = 8
    half = ML_CONV // 2
    rt = min(512, T)

    xp_ref[0:pad, :] = jnp.zeros((pad, dk), F32)
    xp_ref[pad + T:pad + T + pad, :] = jnp.zeros((pad, dk), F32)
    for src, cw_ref, cb_ref, dst, scale in ((q_ref, cwq_ref, cbq_ref, qc_ref, 1.0),
                                            (k_ref, cwk_ref, cbk_ref, kc_ref, dk ** -0.5)):
        xp_ref[pad:pad + T, :] = src[0]
        for r0 in range(0, T, rt):
            acc = jnp.zeros((rt, dk), F32) + cb_ref[...]
            for j in range(ML_CONV):
                acc = acc + xp_ref[pad + r0 + j - half:pad + r0 + j - half + rt, :] * cw_ref[j:j + 1, :]
            dst[r0:r0 + rt, :] = _silu(acc) * scale

    lane = lax.broadcasted_iota(jnp.int32, (rt, HEAD_LANES), 1)
    is_f = (lane % 8) >= 4
    for r0 in range(0, T, rt):
        g = gt_ref[0, r0:r0 + rt, :] + gbias_ref[...]
        p = jnp.where(is_f, _log_sigmoid(g), g)
        x = jnp.zeros((rt, HEAD_LANES), F32)
        for j, src_lane in enumerate((0, 4, 8, 12)):
            colv = jnp.sum(jnp.where(lane == src_lane + head, p, 0.0), axis=1, keepdims=True)
            x = jnp.where(lane == j, colv, x)
        gx_ref[r0:r0 + rt, :] = x
        gxt_ref[:, r0:r0 + rt] = x.T[0:SUBLANES, :]
        v = v_ref[0, r0:r0 + rt, :]
        va_ref[r0:r0 + rt, 0:dk] = v.astype(BF16)
        va_ref[r0:r0 + rt, dk:2 * dk] = jnp.where(lane == 0, 1.0, 0.0).astype(BF16)
        vta_ref[0:dk, r0:r0 + rt] = v.T.astype(BF16)
        sub = lax.broadcasted_iota(jnp.int32, (dk, rt), 0)
        vta_ref[dk:2 * dk, r0:r0 + rt] = jnp.where(sub == 0, 1.0, 0.0).astype(BF16)

    row = lax.broadcasted_iota(jnp.int32, (C, C), 0)
    colm = lax.broadcasted_iota(jnp.int32, (C, C), 1)

    init = (jnp.zeros((2 * dk, dk), F32), jnp.full((1, 1), NEG_BIG, F32))

    def step(j, carry):
        carries = list(carry)
        work = []
        for u in range(STEP_CHUNKS):
            work.append((0, STEP_CHUNKS * j + u, hf_ref))
            work.append((1, n - 1 - STEP_CHUNKS * j - u, hb_ref))
        chunks = []
        for d, c, out_ref in work:
            sl = pl.ds(pl.multiple_of(c * C, C), C)
            x = gx_ref[sl, :]
            xr = gxt_ref[:, sl]
            chunks.append(dict(d=d, sl=sl, out=out_ref, x=x, xr=xr, q=qc_ref[sl, :], k=kc_ref[sl, :],
                               cumx=_dot(tri_ref[d], x, precision=HIGHEST),
                               cumr=_dot(xr, tri_ref[1 - d], precision=HIGHEST)))
        for p in chunks:
            p["qk"] = _dot_nt(p["q"].astype(BF16), p["k"].astype(BF16))
        for p in chunks:
            d, x, xr, cumx, cumr = p["d"], p["x"], p["xr"], p["cumx"], p["cumr"]
            mask = (colm <= row) if d == 0 else (colm >= row)
            ig_c = x[:, 2 * d:2 * d + 1]
            ig_r = xr[2 * d:2 * d + 1, :]
            cum_c = cumx[:, 2 * d + 1:2 * d + 2]
            cum_r = cumr[2 * d + 1:2 * d + 2, :]
            tot = jnp.sum(x[:, 2 * d + 1:2 * d + 2], axis=0, keepdims=True)
            dmat = jnp.where(mask, cum_c - cum_r + ig_r, -jnp.inf)
            dmax = jnp.max(dmat, axis=1, keepdims=True)
            a = p["qk"] * jnp.exp(dmat - dmax)
            ds = tot - cum_c + ig_c
            dsmax = jnp.max(ds, axis=0, keepdims=True)
            kw = p["k"] * jnp.exp(ds - dsmax)
            p.update(cum_c=cum_c, tot=tot, dmax=dmax, dsmax=dsmax, a=a.astype(BF16), kw=kw.astype(BF16))
        for p in chunks:
            p["num"] = _dot(p["a"], va_ref[p["sl"], :])
            p["upd"] = _dot(vta_ref[:, p["sl"]], p["kw"])
        for p in chunks:
            state, m = carries[p["d"]]
            g = p["cum_c"] + m
            mt = jnp.maximum(g, p["dmax"])
            full = (jnp.exp(g - mt) * _dot_nt(p["q"].astype(BF16), state.astype(BF16))
                    + jnp.exp(p["dmax"] - mt) * p["num"])
            den = full[:, dk:dk + 1]
            p["out"][p["sl"], :] = full[:, 0:dk] / jnp.maximum(jnp.abs(den), jnp.exp(-mt))
            m_new = jnp.maximum(p["tot"] + m, p["dsmax"])
            decay = jnp.exp(p["tot"] + m - m_new)
            scale = jnp.exp(p["dsmax"] - m_new)
            carries[p["d"]] = (decay * state + scale * p["upd"], m_new)
        return tuple(carries)

    lax.fori_loop(0, n // STEP_CHUNKS, step, (init, init))

    for r0 in range(0, T, rt):
        sl = slice(r0, r0 + rt)
        o_ref[0, sl, :] = _rms(hf_ref[sl, :] + hb_ref[sl, :], gain_ref[...]) * _sigmoid(og_ref[0, sl, :])


def mlstm(y, gates, conv_w, conv_b, i_bias, f_bias, norm_gain):
    B, T, _ = y.shape
    H = GROUP_WIDTH // HEAD_LANES
    C = min(ML_CHUNK, T)
    t = np.arange(C)
    tri = np.stack([(t[None, :] <= t[:, None]), (t[None, :] >= t[:, None])]).astype(np.float32)
    gbias = jnp.zeros((1, HEAD_LANES), F32)
    gbias = gbias.at[0, 0:4].set(i_bias[0]).at[0, 4:8].set(f_bias[0])
    gbias = gbias.at[0, 8:12].set(i_bias[1]).at[0, 12:16].set(f_bias[1])

    def col(group):
        return pl.BlockSpec((1, T, HEAD_LANES), lambda b, h, group=group: (b, 0, group * H + h))

    conv_q = pl.BlockSpec((ML_CONV, HEAD_LANES), lambda b, h: (0, h))
    conv_k = pl.BlockSpec((ML_CONV, HEAD_LANES), lambda b, h: (0, H + h))
    bias_q = pl.BlockSpec((1, HEAD_LANES), lambda b, h: (0, h))
    bias_k = pl.BlockSpec((1, HEAD_LANES), lambda b, h: (0, H + h))
    cb = conv_b.reshape(1, -1)
    return pl.pallas_call(
        functools.partial(_mlstm_kernel, C=C, T=T, dk=HEAD_LANES),
        out_shape=jax.ShapeDtypeStruct((B, T, GROUP_WIDTH), F32),
        grid=(B, H),
        in_specs=[col(0), col(1), col(2), col(3),
                  pl.BlockSpec((1, T, HEAD_LANES), lambda b, h: (b, 0, 0)),
                  conv_q, conv_k, bias_q, bias_k,
                  pl.BlockSpec((1, HEAD_LANES), lambda b, h: (0, 0)),
                  pl.BlockSpec((1, HEAD_LANES), lambda b, h: (0, 0)),
                  pl.BlockSpec((2, C, C), lambda b, h: (0, 0, 0))],
        out_specs=pl.BlockSpec((1, T, HEAD_LANES), lambda b, h: (b, 0, h)),
        scratch_shapes=[pltpu.VMEM((T, HEAD_LANES), F32), pltpu.VMEM((T, HEAD_LANES), F32),
                        pltpu.VMEM((T + 16, HEAD_LANES), F32), pltpu.VMEM((T, HEAD_LANES), F32),
                        pltpu.VMEM((T, HEAD_LANES), F32), pltpu.VMEM((T, HEAD_LANES), F32),
                        pltpu.VMEM((SUBLANES, T), F32), pltpu.VMEM((T, 2 * HEAD_LANES), BF16),
                        pltpu.VMEM((2 * HEAD_LANES, T), BF16)],
        compiler_params=pltpu.CompilerParams(
            dimension_semantics=("arbitrary", "arbitrary"), vmem_limit_bytes=VMEM_LIMIT),
        name="mlstm",
    )(y, y, y, y, gates, conv_w, conv_w, cb, cb, gbias, norm_gain.reshape(1, HEAD_LANES), jnp.asarray(tri))


def _na_bias_table(rpb, rows):
    kr = min(NA_ROWS, rows)
    c = np.arange(GRID_W)
    cstart = np.clip(c - NA_COLS // 2, 0, GRID_W - NA_COLS)
    kc = np.arange(GRID_W)
    valid = (kc[None, :] >= cstart[:, None]) & (kc[None, :] < cstart[:, None] + NA_COLS)
    coff = np.clip(kc[None, :] - c[:, None] + NA_COLS - 1, 0, 2 * NA_COLS - 2)
    di = np.arange(kr)
    i = np.arange(kr)
    roff = i[None, :] - di[:, None] + NA_ROWS - 1
    heads = rpb.shape[0]
    cols = jnp.where(jnp.asarray(valid)[None, None], rpb.astype(F32)[:, :, coff], NEG_BIG)
    tab = jnp.take(cols, jnp.asarray(roff.reshape(-1)), axis=1)
    tab = tab.reshape(heads, kr, kr, GRID_W, GRID_W).transpose(0, 1, 3, 2, 4)
    return tab.reshape(heads, kr, GRID_W, kr * GRID_W)


def _na_kernel(q_ref, k_ref, v_ref, bm_ref, o_ref, *, rows, kr):
    W = GRID_W
    heads_per_block = HEAD_LANES // NA_DIM

    head_of_lane = lax.broadcasted_iota(jnp.int32, (W, HEAD_LANES), 1) // NA_DIM

    def body(j, _):
        work = []
        for u in range(NA_STEP_ROWS):
            r = NA_STEP_ROWS * j + u
            rs = jnp.clip(r - kr // 2, 0, rows - kr)
            di = r - rs
            qs = pl.ds(pl.multiple_of(r * W, W), W)
            ks = pl.ds(pl.multiple_of(rs * W, W), kr * W)
            q = q_ref[0, qs, :] * (NA_DIM ** -0.5)
            kw = k_ref[0, ks, :].astype(BF16)
            for hh in range(heads_per_block):
                s = _dot_nt(jnp.where(head_of_lane == hh, q, 0.0).astype(BF16), kw) + bm_ref[hh, di]
                work.append((u, hh, qs, ks, s))
        probs = []
        for u, hh, qs, ks, s in work:
            e = jnp.exp(s - jnp.max(s, axis=-1, keepdims=True))
            probs.append((e.astype(BF16), jnp.sum(e, axis=-1, keepdims=True)))
        outs = {}
        for (u, hh, qs, ks, s), (e, l) in zip(work, probs):
            o = _dot(e, v_ref[0, ks, :].astype(BF16)) / l
            outs[u] = o if hh == 0 else jnp.where(head_of_lane == hh, o, outs[u])
            if hh == heads_per_block - 1:
                o_ref[0, qs, :] = outs[u]
        return 0

    lax.fori_loop(0, rows // NA_STEP_ROWS, body, 0)


def neighbourhood_attention(y, rpb):
    B, T, _ = y.shape
    rows = T // GRID_W
    kr = min(NA_ROWS, rows)
    HB = GROUP_WIDTH // HEAD_LANES
    hpb = HEAD_LANES // NA_DIM
    bm = _na_bias_table(rpb, rows)

    def col(group):
        return pl.BlockSpec((1, T, HEAD_LANES), lambda b, h, group=group: (b, 0, group * HB + h))

    return pl.pallas_call(
        functools.partial(_na_kernel, rows=rows, kr=kr),
        out_shape=jax.ShapeDtypeStruct((B, T, GROUP_WIDTH), F32),
        grid=(B, HB),
        in_specs=[col(4), col(5), col(6),
                  pl.BlockSpec((hpb, kr, GRID_W, kr * GRID_W), lambda b, h: (h, 0, 0, 0))],
        out_specs=pl.BlockSpec((1, T, HEAD_LANES), lambda b, h: (b, 0, h)),
        compiler_params=pltpu.CompilerParams(
            dimension_semantics=("arbitrary", "arbitrary"), vmem_limit_bytes=VMEM_LIMIT),
        name="neighbourhood_attention",
    )(y, y, y, bm)


def _out_kernel(a_ref, b_ref, x_ref, mod_ref, gain_ref, w_ref, r_ref, x1_ref, h2_ref, lg_ref):
    G = a_ref.shape[-1]
    y = _dot(a_ref[0].astype(BF16), w_ref[0:G, :]) + _dot(b_ref[0].astype(BF16), w_ref[G:2 * G, :])
    x1 = x_ref[0] + mod_ref[0, 2:3, :] * y
    x1_ref[0] = x1
    h2 = _rms(x1, gain_ref[...]) * (1.0 + mod_ref[0, 4:5, :]) + mod_ref[0, 3:4, :]
    _store_row_tiles(h2_ref.at[0], h2)
    lg_ref[...] = lax.dot_general(r_ref[...], h2, (((1,), (1,)), ((), ())), precision=HIGHEST,
                                  preferred_element_type=F32)


def out_proj(a_out, b_out, x, mod, gain2, w_out_bf16, router):
    B, T, D = x.shape
    G = a_out.shape[-1]
    E = router.shape[1]
    tm = min(512, T)
    nt = T // tm
    return pl.pallas_call(
        _out_kernel,
        out_shape=[jax.ShapeDtypeStruct((B, T, D), F32),
                   jax.ShapeDtypeStruct((B, T, D // HEAD_LANES, HEAD_LANES), F32),
                   jax.ShapeDtypeStruct((E, B * T), F32)],
        grid=(B, T // tm),
        in_specs=[pl.BlockSpec((1, tm, G), lambda b, i: (b, i, 0)),
                  pl.BlockSpec((1, tm, G), lambda b, i: (b, i, 0)),
                  pl.BlockSpec((1, tm, D), lambda b, i: (b, i, 0)),
                  pl.BlockSpec((1, 6, D), lambda b, i: (b, 0, 0)),
                  pl.BlockSpec((1, D), lambda b, i: (0, 0)),
                  pl.BlockSpec((2 * G, D), lambda b, i: (0, 0)),
                  pl.BlockSpec((E, D), lambda b, i: (0, 0))],
        out_specs=[pl.BlockSpec((1, tm, D), lambda b, i: (b, i, 0)),
                   pl.BlockSpec((1, tm, D // HEAD_LANES, HEAD_LANES), lambda b, i: (b, i, 0, 0)),
                   pl.BlockSpec((E, tm), lambda b, i, nt=nt: (0, b * nt + i))],
        compiler_params=pltpu.CompilerParams(
            dimension_semantics=("arbitrary", "arbitrary"), vmem_limit_bytes=VMEM_LIMIT),
        name="out_proj",
    )(a_out, b_out, x, mod, gain2.reshape(1, D), w_out_bf16, router.T)


def _route_kernel(lg_ref, bias_ref, idx_ref, w_ref, cnt_ref, *, tiles_per_group):
    @pl.when(pl.program_id(0) % tiles_per_group == 0)
    def _():
        cnt_ref[...] = jnp.zeros(cnt_ref.shape, F32)

    scores = _sigmoid(lg_ref[...])
    sel = scores + bias_ref[...]
    E, tm = sel.shape
    per_group = E // N_GROUPS
    neg = -jnp.inf
    eid = lax.broadcasted_iota(jnp.int32, (E, tm), 0).astype(F32)
    eid_g = lax.broadcasted_iota(jnp.int32, (per_group, tm), 0).astype(F32)

    def first_argmax(x, ids, sentinel):
        m = jnp.max(x, axis=0, keepdims=True)
        i = jnp.min(jnp.where(x == m, ids, sentinel), axis=0, keepdims=True)
        return m, i

    parts, gscore = [], []
    for g in range(N_GROUPS):
        x = sel[g * per_group:(g + 1) * per_group]
        parts.append(x)
        m1, i1 = first_argmax(x, eid_g, float(per_group))
        m2 = jnp.max(jnp.where(eid_g == i1, neg, x), axis=0, keepdims=True)
        gscore.append(m1 + m2)
    kept = []
    for g in range(N_GROUPS):
        beaten = jnp.zeros((1, tm), F32)
        for o in range(N_GROUPS):
            if o == g:
                continue
            wins = (gscore[o] >= gscore[g]) if o < g else (gscore[o] > gscore[g])
            beaten = beaten + wins.astype(F32)
        kept.append(jnp.where(beaten < TOPK_GROUPS, parts[g], neg))
    sel = jnp.concatenate(kept, axis=0)

    ids, vals = [], []
    w_sum = jnp.zeros((1, tm), F32)
    chosen = jnp.zeros((E, tm), F32)
    for k in range(TOP_K):
        _, i = first_argmax(sel, eid, float(E))
        hit = eid == i
        val = jnp.sum(jnp.where(hit, scores, 0.0), axis=0, keepdims=True)
        sel = jnp.where(hit, neg, sel)
        chosen = jnp.where(hit, 1.0, chosen)
        ids.append(i)
        vals.append(val)
        w_sum = w_sum + val
    idx_ref[...] = jnp.concatenate(ids, axis=0).astype(jnp.int32)
    w_ref[...] = jnp.concatenate(vals, axis=0) / w_sum * ROUTED_SCALE
    part = chosen[:, 0:HEAD_LANES]
    for l0 in range(HEAD_LANES, tm, HEAD_LANES):
        part = part + chosen[:, l0:l0 + HEAD_LANES]
    cnt_ref[0] = cnt_ref[0] + part


def route(logits_t, router_bias, group_tokens):
    E, N = logits_t.shape
    tm = min(512, N)
    tpg = group_tokens // tm
    idx, w, cnt = pl.pallas_call(
        functools.partial(_route_kernel, tiles_per_group=tpg),
        out_shape=[jax.ShapeDtypeStruct((TOP_K, N), jnp.int32), jax.ShapeDtypeStruct((TOP_K, N), F32),
                   jax.ShapeDtypeStruct((N // group_tokens, E, HEAD_LANES), F32)],
        grid=(N // tm,),
        in_specs=[pl.BlockSpec((E, tm), lambda i: (0, i)), pl.BlockSpec((E, 1), lambda i: (0, 0))],
        out_specs=[pl.BlockSpec((TOP_K, tm), lambda i: (0, i)), pl.BlockSpec((TOP_K, tm), lambda i: (0, i)),
                   pl.BlockSpec((1, E, HEAD_LANES), lambda i, tpg=tpg: (i // tpg, 0, 0))],
        compiler_params=pltpu.CompilerParams(dimension_semantics=("arbitrary",)),
        name="route",
    )(logits_t, router_bias.reshape(E, 1))
    return idx, w, jnp.sum(cnt, axis=-1).astype(jnp.int32)


def _moe_kernel(off_ref, tok_ref, wl_ref, x_ref, wg_ref, wu_ref, wd_ref, acc_ref, xg_ref, yb_ref,
                *, R, E, per_group, experts_per_step):
    g = pl.program_id(0)
    step = pl.program_id(1)

    @pl.when(step == 0)
    def _():
        acc_ref[...] = jnp.zeros(acc_ref.shape, F32)

    @pl.when((g == 0) & (step == 0))
    def _():
        xg_ref[...] = jnp.zeros(xg_ref.shape, F32)

    U = SUBLANES
    _, _, chunks, lanes = x_ref.shape
    last = per_group - 1

    def gather(le, s0, nr):
        def body(j, _):
            r0 = pl.multiple_of(j * U, U)
            for i in range(U):
                t = tok_ref[jnp.minimum(s0 + r0 + i, last)]
                xg_ref[le, j, pl.ds(i, chunks, stride=U), :] = x_ref[0, t]
            return 0

        lax.fori_loop(0, (nr + U - 1) // U, body, 0)

    def load_block(le):
        return jnp.concatenate([xg_ref[le, :, c * U:(c + 1) * U, :].reshape(R, lanes) for c in range(chunks)],
                               axis=1).astype(BF16)

    def store_block(le, y):
        for c in range(chunks):
            yb_ref[le, :, c * U:(c + 1) * U, :] = y[:, c * lanes:(c + 1) * lanes].reshape(R // U, U, lanes)

    def scatter(le, s0, nr):
        def group(j, _):
            r0 = pl.multiple_of(j * U, U)
            toks = [tok_ref[s0 + r0 + i] for i in range(U)]
            wts = [wl_ref[s0 + r0 + i] for i in range(U)]
            new = [acc_ref[0, toks[i]] + wts[i] * yb_ref[le, j, pl.ds(i, chunks, stride=U), :] for i in range(U)]
            for i in range(U):
                acc_ref[0, toks[i]] = new[i]
            return 0

        groups = nr // U
        lax.fori_loop(0, groups, group, 0)

        def single(r, _):
            t = tok_ref[s0 + r]
            row = yb_ref[le, r // U, pl.ds(r % U, chunks, stride=U), :]
            acc_ref[0, t] = acc_ref[0, t] + wl_ref[s0 + r] * row
            return 0

        lax.fori_loop(groups * U, nr, single, 0)

    segments = []
    for le in range(experts_per_step):
        segment = g * E + step * experts_per_step + le
        seg = off_ref[segment]
        segments.append((seg - g * per_group, off_ref[segment + 1] - seg))

    for le, (start, cnt) in enumerate(segments):
        gather(le, start, jnp.minimum(R, cnt))
    xs = [load_block(le) for le in range(experts_per_step)]
    gu = [(_dot(xs[le], wg_ref[le]), _dot(xs[le], wu_ref[le])) for le in range(experts_per_step)]
    hs = [(_silu(gate) * up).astype(BF16) for gate, up in gu]
    for le in range(experts_per_step):
        store_block(le, _dot(hs[le], wd_ref[le]))
    for le, (start, cnt) in enumerate(segments):
        scatter(le, start, jnp.minimum(R, cnt))

    for le, (start, cnt) in enumerate(segments):
        def more(sb, _, le=le, start=start, cnt=cnt):
            s0 = start + sb * R
            nr = jnp.minimum(R, cnt - sb * R)
            gather(le, s0, nr)
            x = load_block(le)
            store_block(le, _dot((_silu(_dot(x, wg_ref[le])) * _dot(x, wu_ref[le])).astype(BF16), wd_ref[le]))
            scatter(le, s0, nr)
            return 0

        lax.fori_loop(1, (cnt + R - 1) // R, more, 0)


def routed_experts(h2, top_idx, top_w, counts, exp_gate, exp_up, exp_down):
    N, chunks, lanes = h2.shape
    D = chunks * lanes
    E, _, F = exp_gate.shape
    TG = min(MOE_TOKEN_GROUP, N)
    G = N // TG
    per_group = TG * TOP_K
    key = (jnp.arange(N, dtype=jnp.int32)[None, :] // TG) * E + top_idx
    order = jnp.argsort(key.reshape(-1))
    tok_s = ((order % N) % TG).astype(jnp.int32)
    w_s = top_w.reshape(-1)[order]
    off = jnp.concatenate([jnp.zeros((1,), jnp.int32), jnp.cumsum(counts.reshape(-1)).astype(jnp.int32)])

    EPS = MOE_EXPERTS_PER_STEP
    grid_spec = pltpu.PrefetchScalarGridSpec(
        num_scalar_prefetch=1,
        grid=(G, E // EPS),
        in_specs=[pl.BlockSpec((per_group,), lambda g, e, off: (g,), memory_space=pltpu.SMEM),
                  pl.BlockSpec((per_group,), lambda g, e, off: (g,), memory_space=pltpu.SMEM),
                  pl.BlockSpec((1, TG, chunks, lanes), lambda g, e, off: (g, 0, 0, 0),
                               pipeline_mode=pl.Buffered(1)),
                  pl.BlockSpec((EPS, D, F), lambda g, e, off: (e, 0, 0)),
                  pl.BlockSpec((EPS, D, F), lambda g, e, off: (e, 0, 0)),
                  pl.BlockSpec((EPS, F, D), lambda g, e, off: (e, 0, 0))],
        out_specs=pl.BlockSpec((1, TG, chunks, lanes), lambda g, e, off: (g, 0, 0, 0),
                               pipeline_mode=pl.Buffered(1)),
        scratch_shapes=[pltpu.VMEM((EPS, MOE_ROWS // SUBLANES, chunks * SUBLANES, lanes), F32),
                        pltpu.VMEM((EPS, MOE_ROWS // SUBLANES, chunks * SUBLANES, lanes), F32)],
    )
    out = pl.pallas_call(
        functools.partial(_moe_kernel, R=MOE_ROWS, E=E, per_group=per_group, experts_per_step=EPS),
        out_shape=jax.ShapeDtypeStruct((G, TG, chunks, lanes), F32),
        grid_spec=grid_spec,
        compiler_params=pltpu.CompilerParams(
            dimension_semantics=("arbitrary", "arbitrary"), vmem_limit_bytes=VMEM_LIMIT),
        name="routed_experts",
    )(off, tok_s, w_s, h2.reshape(G, TG, chunks, lanes), exp_gate, exp_up, exp_down)
    return out.reshape(N, chunks, lanes)


def _final_kernel(x1_ref, h2_ref, rt_ref, mod_ref, wg_ref, wu_ref, wd_ref, *rest, final):
    if final:
        fg_ref, o_ref = rest
    else:
        (o_ref,) = rest
    hb = _load_row_tiles(h2_ref.at[0]).astype(BF16)
    hmid = _silu(_dot(hb, wg_ref[...])) * _dot(hb, wu_ref[...])
    shared = _dot(hmid.astype(BF16), wd_ref[...])
    x2 = x1_ref[0] + mod_ref[0, 5:6, :] * (_load_row_tiles(rt_ref.at[0]) + shared)
    if final:
        x2 = _rms(x2, fg_ref[...])
    o_ref[0] = x2


def shared_and_residual(x1, h2, routed, mod, sh_gate, sh_up, sh_down, final_gain=None):
    B, T, D = x1.shape
    F = sh_gate.shape[1]
    tm = min(512, T)
    final = final_gain is not None
    tile = pl.BlockSpec((1, tm, D), lambda b, i: (b, i, 0))
    row_tiles = pl.BlockSpec((1, tm, D // HEAD_LANES, HEAD_LANES), lambda b, i: (b, i, 0, 0))
    in_specs = [tile, row_tiles, row_tiles,
                pl.BlockSpec((1, 6, D), lambda b, i: (b, 0, 0)),
                pl.BlockSpec((D, F), lambda b, i: (0, 0)),
                pl.BlockSpec((D, F), lambda b, i: (0, 0)),
                pl.BlockSpec((F, D), lambda b, i: (0, 0))]
    args = [x1, h2, routed, mod, sh_gate, sh_up, sh_down]
    if final:
        in_specs.append(pl.BlockSpec((1, D), lambda b, i: (0, 0)))
        args.append(final_gain.reshape(1, D))
    return pl.pallas_call(
        functools.partial(_final_kernel, final=final),
        out_shape=jax.ShapeDtypeStruct((B, T, D), F32),
        grid=(B, T // tm),
        in_specs=in_specs,
        out_specs=tile,
        compiler_params=pltpu.CompilerParams(
            dimension_semantics=("arbitrary", "arbitrary"), vmem_limit_bytes=VMEM_LIMIT),
        name="shared_and_residual",
    )(*args)


def moe_block(x1, h2, logits, mod, router_bias, exp_gate, exp_up, exp_down, sh_gate, sh_up, sh_down,
              final_gain=None):
    B, T, D = x1.shape
    top_idx, top_w, counts = route(logits, router_bias, min(MOE_TOKEN_GROUP, B * T))
    chunks = D // HEAD_LANES
    routed = routed_experts(h2.reshape(B * T, chunks, HEAD_LANES), top_idx, top_w, counts,
                            exp_gate.astype(BF16), exp_up.astype(BF16), exp_down.astype(BF16))
    return shared_and_residual(x1, h2, routed.reshape(B, T, chunks, HEAD_LANES), mod,
                               sh_gate.astype(BF16), sh_up.astype(BF16), sh_down.astype(BF16), final_gain)


def kernel(x, c, hgrn_lb_logits, l0_norm1, l0_norm2, l0_w_mod, l0_b_mod, l0_w_in, l0_w_out, l0_hgrn_norm, l0_diff_lq1, l0_diff_lk1, l0_diff_lq2, l0_diff_lk2, l0_diff_subln, l0_router, l0_router_bias, l0_exp_gate, l0_exp_up, l0_exp_down, l0_sh_gate, l0_sh_up, l0_sh_down, l1_norm1, l1_norm2, l1_w_mod, l1_b_mod, l1_w_in, l1_w_out, l1_conv_w, l1_conv_b, l1_ml_i_bias, l1_ml_f_bias, l1_ml_norm, l1_na_rpb, l1_router, l1_router_bias, l1_exp_gate, l1_exp_up, l1_exp_down, l1_sh_gate, l1_sh_up, l1_sh_down, final_norm):
    G = GROUP_WIDTH
    lb_all = jnp.cumsum(jax.nn.softmax(hgrn_lb_logits.astype(F32), axis=0), axis=0)
    layer_idx = 0
    lambda_init = 0.8 - 0.6 * math.exp(-0.3 * layer_idx)
    lam = (jnp.exp(jnp.sum(l0_diff_lq1.astype(F32) * l0_diff_lk1.astype(F32)))
           - jnp.exp(jnp.sum(l0_diff_lq2.astype(F32) * l0_diff_lk2.astype(F32))) + lambda_init)

    mod0 = ada_mod(c, l0_w_mod, l0_b_mod)
    y0 = in_proj(x, mod0, l0_norm1, l0_w_in.astype(BF16))
    a_out = hgrn2(y0, lb_all[0], l0_hgrn_norm)
    b_out = diff_attention(y0, lam, l0_diff_subln, layer_idx)
    x1, h2, logits = out_proj(a_out, b_out, x, mod0, l0_norm2, l0_w_out.astype(BF16), l0_router)
    xa = moe_block(x1, h2, logits, mod0, l0_router_bias, l0_exp_gate, l0_exp_up, l0_exp_down,
                   l0_sh_gate, l0_sh_up, l0_sh_down)

    mod1 = ada_mod(c, l1_w_mod, l1_b_mod)
    n_gate = l1_w_in.shape[1] - 7 * G
    w_main = jnp.concatenate([l1_w_in[:, :4 * G], l1_w_in[:, 4 * G + n_gate:]], axis=1).astype(BF16)
    w_gate = jnp.pad(l1_w_in[:, 4 * G:4 * G + n_gate], ((0, 0), (0, HEAD_LANES - n_gate)))
    y1, gates = in_proj(xa, mod1, l1_norm1, w_main, w_gate)
    c_out = mlstm(y1, gates, l1_conv_w, l1_conv_b, l1_ml_i_bias, l1_ml_f_bias, l1_ml_norm)
    d_out = neighbourhood_attention(y1, l1_na_rpb)
    x1, h2, logits = out_proj(c_out, d_out, xa, mod1, l1_norm2, l1_w_out.astype(BF16), l1_router)
    return moe_block(x1, h2, logits, mod1, l1_router_bias, l1_exp_gate, l1_exp_up, l1_exp_down,
                     l1_sh_gate, l1_sh_up, l1_sh_down, final_gain=final_norm)
```

```python
import functools
import math

import numpy as np
import jax
import jax.numpy as jnp
from jax import lax
from jax.experimental import pallas as pl
from jax.experimental.pallas import tpu as pltpu

F32 = jnp.float32
BF16 = jnp.bfloat16
HIGHEST = lax.Precision.HIGHEST
EPS = 1e-6

GRID_W = 64
GROUP_WIDTH = 512
HEAD_LANES = 128
HG_CHUNK = 64
ML_CHUNK = 128
STEP_CHUNKS = 2
ML_CONV = 5
DA_DIM = 64
ROPE_DIM = 16
ROPE_THETA = 500000.0
NA_ROWS = 8
NA_COLS = 16
NA_DIM = 64
NA_STEP_ROWS = 4
N_EXPERTS = 128
TOP_K = 8
N_GROUPS = 8
TOPK_GROUPS = 4
ROUTED_SCALE = 2.5
MOE_ROWS = 320
MOE_TOKEN_GROUP = 4096
MOE_EXPERTS_PER_STEP = 2
MOE_LOOP_UNROLL = 2
SUBLANES = 8
NEG_BIG = -1e30
VMEM_LIMIT = 48 * 1024 * 1024


def _dot(a, b, **kw):
    return jnp.dot(a, b, preferred_element_type=F32, **kw)


def _dot_nt(a, b):
    return lax.dot_general(a, b, (((1,), (1,)), ((), ())), preferred_element_type=F32)


def _dot_tn(a, b):
    return lax.dot_general(a, b, (((0,), (0,)), ((), ())), preferred_element_type=F32)


def _sigmoid(x):
    return jax.nn.sigmoid(x)


def _silu(x):
    return x * jax.nn.sigmoid(x)


def _log_sigmoid(x):
    return jnp.minimum(x, 0.0) - jnp.log(1.0 + jnp.exp(-jnp.abs(x)))


def _rms(x, gain):
    return x * lax.rsqrt(jnp.mean(x * x, axis=-1, keepdims=True) + EPS) * gain


def _store_row_tiles(ref, val):
    for c in range(ref.shape[1]):
        ref[:, c, :] = val[:, c * HEAD_LANES:(c + 1) * HEAD_LANES]


def _load_row_tiles(ref):
    return jnp.concatenate([ref[:, c, :] for c in range(ref.shape[1])], axis=1)


def _mod_kernel(c_ref, w_ref, b_ref, o_ref):
    o_ref[...] = _dot(_silu(c_ref[...]), w_ref[...], precision=HIGHEST) + b_ref[...]


def ada_mod(c, w_mod, b_mod):
    B, D = c.shape
    N = w_mod.shape[1]
    tn = 1024
    out = pl.pallas_call(
        _mod_kernel,
        out_shape=jax.ShapeDtypeStruct((B, N), F32),
        grid=(N // tn,),
        in_specs=[pl.BlockSpec((B, D), lambda j: (0, 0)),
                  pl.BlockSpec((D, tn), lambda j: (0, j)),
                  pl.BlockSpec((1, tn), lambda j: (0, j))],
        out_specs=pl.BlockSpec((B, tn), lambda j: (0, j)),
        name="ada_mod",
    )(c, w_mod, b_mod.reshape(1, N))
    return out.reshape(B, 6, D)


def _in_kernel(x_ref, mod_ref, gain_ref, w_ref, *rest, has_gate):
    if has_gate:
        wg_ref, o_ref, og_ref, h_ref = rest
    else:
        o_ref, h_ref = rest

    @pl.when(pl.program_id(2) == 0)
    def _():
        h = _rms(x_ref[0], gain_ref[...]) * (1.0 + mod_ref[0, 1:2, :]) + mod_ref[0, 0:1, :]
        h_ref[...] = h.astype(BF16)
        if has_gate:
            og_ref[0] = _dot(h, wg_ref[...], precision=HIGHEST)

    o_ref[0] = _dot(h_ref[...], w_ref[...])


def in_proj(x, mod, gain, w_bf16, w_gate=None):
    B, T, D = x.shape
    N = w_bf16.shape[1]
    tm = min(1024, T)
    tn = 512
    has_gate = w_gate is not None
    in_specs = [pl.BlockSpec((1, tm, D), lambda b, i, j: (b, i, 0)),
                pl.BlockSpec((1, 6, D), lambda b, i, j: (b, 0, 0)),
                pl.BlockSpec((1, D), lambda b, i, j: (0, 0)),
                pl.BlockSpec((D, tn), lambda b, i, j: (0, j))]
    out_shape = [jax.ShapeDtypeStruct((B, T, N), F32)]
    out_specs = [pl.BlockSpec((1, tm, tn), lambda b, i, j: (b, i, j))]
    args = [x, mod, gain.reshape(1, D), w_bf16]
    if has_gate:
        in_specs.append(pl.BlockSpec((D, HEAD_LANES), lambda b, i, j: (0, 0)))
        out_shape.append(jax.ShapeDtypeStruct((B, T, HEAD_LANES), F32))
        out_specs.append(pl.BlockSpec((1, tm, HEAD_LANES), lambda b, i, j: (b, i, 0)))
        args.append(w_gate)
    res = pl.pallas_call(
        functools.partial(_in_kernel, has_gate=has_gate),
        out_shape=out_shape,
        grid=(B, T // tm, N // tn),
        in_specs=in_specs,
        out_specs=out_specs,
        scratch_shapes=[pltpu.VMEM((tm, D), BF16)],
        compiler_params=pltpu.CompilerParams(
            dimension_semantics=("arbitrary", "arbitrary", "arbitrary")),
        name="in_proj",
    )(*args)
    return res if has_gate else res[0]


def _hgrn_consts(C):
    t = np.arange(C)
    tri = (t[None, :] <= t[:, None]).astype(np.float32)
    triT = np.ascontiguousarray(tri.T)
    wf, wb, mf = [tri], [triT], []
    levels = int(round(math.log2(C)))
    for l in range(levels):
        size = C >> l
        blk = t // size
        r = blk * size + size // 2
        wf.append(tri - tri[r - 1])
        wb.append(triT - triT[r])
        upper = (t % size) >= size // 2
        mf.append(((blk[:, None] == blk[None, :]) & upper[:, None] & (~upper)[None, :]).astype(np.float32))
    mf.append(np.eye(C, dtype=np.float32))
    ones = np.ones((8, C), np.float32)
    wf.append(ones)
    wb.append(ones)
    mf = np.stack(mf)
    mb = np.ascontiguousarray(np.transpose(mf, (0, 2, 1)))
    return np.concatenate(wf), np.concatenate(wb), mf, mb


def _split3(x):
    hi = x.astype(BF16)
    r1 = x - hi.astype(F32)
    mid = r1.astype(BF16)
    lo = (r1 - mid.astype(F32)).astype(BF16)
    return hi, mid, lo


def _hgrn_kernel(q_ref, i_ref, ff_ref, fb_ref, g_ref, lb_ref, gain_ref, wf_ref, wb_ref, mf_ref, mb_ref,
                 o_ref, of_ref, ob_ref, *, C, T):
    n = T // C
    levels = int(round(math.log2(C)))
    dv = q_ref.shape[-1]

    def prepare(c, f_ref, lbd, w_ref):
        sl = pl.ds(pl.multiple_of(c * C, C), C)
        q = _silu(q_ref[0, sl, :])
        v = i_ref[0, sl, :]
        fg = lbd + (1.0 - lbd) * _sigmoid(f_ref[0, sl, :])
        lf = jnp.log(fg)
        d3 = _dot(w_ref[...], jnp.concatenate(_split3(lf), axis=1))
        dall = d3[:, 0:dv] + d3[:, dv:2 * dv] + d3[:, 2 * dv:3 * dv]
        return dict(sl=sl, q=q, k=1.0 - fg, v=v.astype(BF16), dall=dall)

    def scores(p, m_ref):
        q, k, dall = p["q"], p["k"], p["dall"]
        attn = m_ref[levels] * _dot_nt(q.astype(BF16), k.astype(BF16))
        for l in range(levels):
            e = jnp.exp(-jnp.abs(dall[(l + 1) * C:(l + 2) * C]))
            attn = attn + m_ref[l] * _dot_nt((q * e).astype(BF16), (k * e).astype(BF16))
        return attn.astype(BF16)

    st0 = jnp.zeros((dv, dv), F32)

    def step(j, carry):
        states = list(carry)
        work = []
        for u in range(STEP_CHUNKS):
            work.append((0, prepare(STEP_CHUNKS * j + u, ff_ref, lb_ref[0:1, :], wf_ref), mf_ref, of_ref))
            work.append((1, prepare(n - 1 - STEP_CHUNKS * j - u, fb_ref, lb_ref[1:2, :], wb_ref), mb_ref, ob_ref))
        attn = [scores(p, m_ref) for _, p, m_ref, _ in work]
        local = []
        for (_, p, _, _), a in zip(work, attn):
            cum = p["dall"][0:C]
            tot = p["dall"][(levels + 1) * C:(levels + 1) * C + 1]
            kt = p["k"] * jnp.exp(tot - cum)
            local.append((_dot(a, p["v"]), _dot_tn(p["v"], kt.astype(BF16)), cum, tot))
        for (d, p, _, out_ref), (o_in, incr, cum, tot) in zip(work, local):
            st = states[d]
            out_ref[p["sl"], :] = o_in + _dot_nt((p["q"] * jnp.exp(cum)).astype(BF16), st.astype(BF16))
            states[d] = st * jnp.exp(tot) + incr
        return tuple(states)

    lax.fori_loop(0, n // STEP_CHUNKS, step, (st0, st0))

    rt = min(512, T)
    for r0 in range(0, T, rt):
        sl = slice(r0, r0 + rt)
        o_ref[0, sl, :] = _rms(of_ref[sl, :] + ob_ref[sl, :], gain_ref[...]) * _silu(g_ref[0, sl, :])


def hgrn2(y, lb, norm_gain):
    B, T, _ = y.shape
    H = GROUP_WIDTH // HEAD_LANES
    C = HG_CHUNK
    wf, wb, mf, mb = _hgrn_consts(C)
    wf, wb = jnp.asarray(wf, BF16), jnp.asarray(wb, BF16)
    mf, mb = jnp.asarray(mf), jnp.asarray(mb)

    def col(group):
        return pl.BlockSpec((1, T, HEAD_LANES), lambda b, h, group=group: (b, 0, group * H + h))

    def const(a):
        nd = a.ndim
        return pl.BlockSpec(a.shape, lambda b, h, nd=nd: (0,) * nd)

    return pl.pallas_call(
        functools.partial(_hgrn_kernel, C=C, T=T),
        out_shape=jax.ShapeDtypeStruct((B, T, GROUP_WIDTH), F32),
        grid=(B, H),
        in_specs=[col(0), col(1), col(2), col(3), col(4),
                  pl.BlockSpec((2, HEAD_LANES), lambda b, h: (0, h)),
                  pl.BlockSpec((1, HEAD_LANES), lambda b, h: (0, 0)),
                  const(wf), const(wb), const(mf), const(mb)],
        out_specs=pl.BlockSpec((1, T, HEAD_LANES), lambda b, h: (b, 0, h)),
        scratch_shapes=[pltpu.VMEM((T, HEAD_LANES), F32), pltpu.VMEM((T, HEAD_LANES), F32)],
        compiler_params=pltpu.CompilerParams(
            dimension_semantics=("arbitrary", "arbitrary"), vmem_limit_bytes=VMEM_LIMIT),
        name="hgrn2",
    )(y, y, y, y, y, lb, norm_gain.reshape(1, HEAD_LANES), wf, wb, mf, mb)


def _rope_tables(T):
    pos = np.arange(T, dtype=np.float32)
    inv_freq = (ROPE_THETA ** (-np.arange(0, ROPE_DIM, 2, dtype=np.float32) / ROPE_DIM)).astype(np.float32)
    ang = pos[:, None] * inv_freq[None, :]
    cos, sin = np.cos(ang), np.sin(ang)
    half = ROPE_DIM // 2
    c = np.ones((T, HEAD_LANES), np.float32)
    s_prev = np.zeros((T, HEAD_LANES), np.float32)
    s_next = np.zeros((T, HEAD_LANES), np.float32)
    for base in range(0, HEAD_LANES, DA_DIM):
        c[:, base:base + half] = cos
        c[:, base + half:base + ROPE_DIM] = cos
        s_next[:, base:base + half] = -sin
        s_prev[:, base + half:base + ROPE_DIM] = sin
    return jnp.asarray(c), jnp.asarray(s_prev), jnp.asarray(s_next)


def _rope(x, c, s_prev, s_next):
    half = ROPE_DIM // 2
    lanes = x.shape[-1]
    return (x * c + pltpu.roll(x, half, axis=1) * s_prev
            + pltpu.roll(x, lanes - half, axis=1) * s_next)


def _diff_kernel(lam_ref, q_ref, k_ref, v_ref, cq_ref, spq_ref, snq_ref, ck_ref, spk_ref, snk_ref,
                 subln_ref, o_ref, kr_ref, vb_ref, *, T, out_scale):
    rt = min(512, T)

    @pl.when(pl.program_id(2) == 0)
    def _():
        for r0 in range(0, T, rt):
            sl = slice(r0, r0 + rt)
            kr_ref[sl, :] = _rope(k_ref[0, sl, :], ck_ref[sl, :], spk_ref[sl, :], snk_ref[sl, :]).astype(BF16)
            vb_ref[sl, 0:HEAD_LANES] = v_ref[0, sl, :].astype(BF16)
            lane = lax.broadcasted_iota(jnp.int32, (rt, HEAD_LANES), 1)
            vb_ref[sl, HEAD_LANES:2 * HEAD_LANES] = jnp.where(lane == 0, 1.0, 0.0).astype(BF16)

    q = _rope(q_ref[0], cq_ref[...], spq_ref[...], snq_ref[...]) * (DA_DIM ** -0.5 * math.log2(math.e))
    lam = lam_ref[0, 0]
    v = vb_ref[...]
    map_of_lane = lax.broadcasted_iota(jnp.int32, q.shape, 1) // DA_DIM

    scores = [_dot_nt(jnp.where(map_of_lane == m, q, 0.0).astype(BF16), kr_ref[...]) for m in range(2)]
    probs = [jnp.exp2(s - jnp.max(s, axis=-1, keepdims=True)).astype(BF16) for s in scores]
    pv = []
    for e in probs:
        full = _dot(e, v)
        pv.append(full[:, 0:HEAD_LANES] / full[:, HEAD_LANES:HEAD_LANES + 1])
    o = pv[0] - lam * pv[1]
    o_ref[0] = _rms(o, subln_ref[...]) * out_scale


def diff_attention(y, lam, subln, layer_idx):
    B, T, _ = y.shape
    H = GROUP_WIDTH // HEAD_LANES
    tq = min(256, T)
    lambda_init = 0.8 - 0.6 * math.exp(-0.3 * layer_idx)
    c, sp, sn = _rope_tables(T)

    def col(group, rows):
        if rows == T:
            return pl.BlockSpec((1, T, HEAD_LANES), lambda b, h, i, group=group: (b, 0, group * H + h))
        return pl.BlockSpec((1, rows, HEAD_LANES), lambda b, h, i, group=group: (b, i, group * H + h))

    tab_q = pl.BlockSpec((tq, HEAD_LANES), lambda b, h, i: (i, 0))
    tab_k = pl.BlockSpec((T, HEAD_LANES), lambda b, h, i: (0, 0))
    return pl.pallas_call(
        functools.partial(_diff_kernel, T=T, out_scale=1.0 - lambda_init),
        out_shape=jax.ShapeDtypeStruct((B, T, GROUP_WIDTH), F32),
        grid=(B, H, T // tq),
        in_specs=[pl.BlockSpec(memory_space=pltpu.SMEM),
                  col(5, tq), col(6, T), col(7, T),
                  tab_q, tab_q, tab_q, tab_k, tab_k, tab_k,
                  pl.BlockSpec((1, HEAD_LANES), lambda b, h, i: (0, 0))],
        out_specs=pl.BlockSpec((1, tq, HEAD_LANES), lambda b, h, i: (b, i, h)),
        scratch_shapes=[pltpu.VMEM((T, HEAD_LANES), BF16), pltpu.VMEM((T, 2 * HEAD_LANES), BF16)],
        compiler_params=pltpu.CompilerParams(
            dimension_semantics=("arbitrary", "arbitrary", "arbitrary"), vmem_limit_bytes=VMEM_LIMIT),
        name="diff_attention",
    )(lam.reshape(1, 1), y, y, y, c, sp, sn, c, sp, sn, subln.reshape(1, HEAD_LANES))


def _mlstm_kernel(q_ref, k_ref, v_ref, og_ref, gt_ref, cwq_ref, cwk_ref, cbq_ref, cbk_ref, gbias_ref,
                  gain_ref, tri_ref, o_ref, qc_ref, kc_ref, xp_ref, gx_ref, hf_ref, hb_ref, gxt_ref, va_ref,
                  vta_ref, *, C, T, dk):
    n = T // C
    head = pl.program_id(1)
    pad = 8
    half = ML_CONV // 2
    rt = min(512, T)

    xp_ref[0:pad, :] = jnp.zeros((pad, dk), F32)
    xp_ref[pad + T:pad + T + pad, :] = jnp.zeros((pad, dk), F32)
    for src, cw_ref, cb_ref, dst, scale in ((q_ref, cwq_ref, cbq_ref, qc_ref, 1.0),
                                            (k_ref, cwk_ref, cbk_ref, kc_ref, dk ** -0.5)):
        xp_ref[pad:pad + T, :] = src[0]
        for r0 in range(0, T, rt):
            acc = jnp.zeros((rt, dk), F32) + cb_ref[...]
            for j in range(ML_CONV):
                acc = acc + xp_ref[pad + r0 + j - half:pad + r0 + j - half + rt, :] * cw_ref[j:j + 1, :]
            dst[r0:r0 + rt, :] = _silu(acc) * scale

    lane = lax.broadcasted_iota(jnp.int32, (rt, HEAD_LANES), 1)
    is_f = (lane % 8) >= 4
    for r0 in range(0, T, rt):
        g = gt_ref[0, r0:r0 + rt, :] + gbias_ref[...]
        p = jnp.where(is_f, _log_sigmoid(g), g)
        x = jnp.zeros((rt, HEAD_LANES), F32)
        for j, src_lane in enumerate((0, 4, 8, 12)):
            colv = jnp.sum(jnp.where(lane == src_lane + head, p, 0.0), axis=1, keepdims=True)
            x = jnp.where(lane == j, colv, x)
        gx_ref[r0:r0 + rt, :] = x
        gxt_ref[:, r0:r0 + rt] = x.T[0:SUBLANES, :]
        v = v_ref[0, r0:r0 + rt, :]
        va_ref[r0:r0 + rt, 0:dk] = v.astype(BF16)
        va_ref[r0:r0 + rt, dk:2 * dk] = jnp.where(lane == 0, 1.0, 0.0).astype(BF16)
        vta_ref[0:dk, r0:r0 + rt] = v.T.astype(BF16)
        sub = lax.broadcasted_iota(jnp.int32, (dk, rt), 0)
        vta_ref[dk:2 * dk, r0:r0 + rt] = jnp.where(sub == 0, 1.0, 0.0).astype(BF16)

    row = lax.broadcasted_iota(jnp.int32, (C, C), 0)
    colm = lax.broadcasted_iota(jnp.int32, (C, C), 1)

    init = (jnp.zeros((2 * dk, dk), F32), jnp.full((1, 1), NEG_BIG, F32))

    def step(j, carry):
        carries = list(carry)
        work = []
        for u in range(STEP_CHUNKS):
            work.append((0, STEP_CHUNKS * j + u, hf_ref))
            work.append((1, n - 1 - STEP_CHUNKS * j - u, hb_ref))
        chunks = []
        for d, c, out_ref in work:
            sl = pl.ds(pl.multiple_of(c * C, C), C)
            x = gx_ref[sl, :]
            xr = gxt_ref[:, sl]
            chunks.append(dict(d=d, sl=sl, out=out_ref, x=x, xr=xr, q=qc_ref[sl, :], k=kc_ref[sl, :],
                               cumx=_dot(tri_ref[d], x, precision=HIGHEST),
                               cumr=_dot(xr, tri_ref[1 - d], precision=HIGHEST)))
        for p in chunks:
            p["qk"] = _dot_nt(p["q"].astype(BF16), p["k"].astype(BF16))
        for p in chunks:
            d, x, xr, cumx, cumr = p["d"], p["x"], p["xr"], p["cumx"], p["cumr"]
            mask = (colm <= row) if d == 0 else (colm >= row)
            ig_c = x[:, 2 * d:2 * d + 1]
            ig_r = xr[2 * d:2 * d + 1, :]
            cum_c = cumx[:, 2 * d + 1:2 * d + 2]
            cum_r = cumr[2 * d + 1:2 * d + 2, :]
            tot = jnp.sum(x[:, 2 * d + 1:2 * d + 2], axis=0, keepdims=True)
            dmat = jnp.where(mask, cum_c - cum_r + ig_r, -jnp.inf)
            dmax = jnp.max(dmat, axis=1, keepdims=True)
            a = p["qk"] * jnp.exp(dmat - dmax)
            ds = tot - cum_c + ig_c
            dsmax = jnp.max(ds, axis=0, keepdims=True)
            kw = p["k"] * jnp.exp(ds - dsmax)
            p.update(cum_c=cum_c, tot=tot, dmax=dmax, dsmax=dsmax, a=a.astype(BF16), kw=kw.astype(BF16))
        for p in chunks:
            p["num"] = _dot(p["a"], va_ref[p["sl"], :])
            p["upd"] = _dot(vta_ref[:, p["sl"]], p["kw"])
        for p in chunks:
            state, m = carries[p["d"]]
            g = p["cum_c"] + m
            mt = jnp.maximum(g, p["dmax"])
            full = (jnp.exp(g - mt) * _dot_nt(p["q"].astype(BF16), state.astype(BF16))
                    + jnp.exp(p["dmax"] - mt) * p["num"])
            den = full[:, dk:dk + 1]
            p["out"][p["sl"], :] = full[:, 0:dk] / jnp.maximum(jnp.abs(den), jnp.exp(-mt))
            m_new = jnp.maximum(p["tot"] + m, p["dsmax"])
            decay = jnp.exp(p["tot"] + m - m_new)
            scale = jnp.exp(p["dsmax"] - m_new)
            carries[p["d"]] = (decay * state + scale * p["upd"], m_new)
        return tuple(carries)

    lax.fori_loop(0, n // STEP_CHUNKS, step, (init, init))

    for r0 in range(0, T, rt):
        sl = slice(r0, r0 + rt)
        o_ref[0, sl, :] = _rms(hf_ref[sl, :] + hb_ref[sl, :], gain_ref[...]) * _sigmoid(og_ref[0, sl, :])


def mlstm(y, gates, conv_w, conv_b, i_bias, f_bias, norm_gain):
    B, T, _ = y.shape
    H = GROUP_WIDTH // HEAD_LANES
    C = min(ML_CHUNK, T)
    t = np.arange(C)
    tri = np.stack([(t[None, :] <= t[:, None]), (t[None, :] >= t[:, None])]).astype(np.float32)
    gbias = jnp.zeros((1, HEAD_LANES), F32)
    gbias = gbias.at[0, 0:4].set(i_bias[0]).at[0, 4:8].set(f_bias[0])
    gbias = gbias.at[0, 8:12].set(i_bias[1]).at[0, 12:16].set(f_bias[1])

    def col(group):
        return pl.BlockSpec((1, T, HEAD_LANES), lambda b, h, group=group: (b, 0, group * H + h))

    conv_q = pl.BlockSpec((ML_CONV, HEAD_LANES), lambda b, h: (0, h))
    conv_k = pl.BlockSpec((ML_CONV, HEAD_LANES), lambda b, h: (0, H + h))
    bias_q = pl.BlockSpec((1, HEAD_LANES), lambda b, h: (0, h))
    bias_k = pl.BlockSpec((1, HEAD_LANES), lambda b, h: (0, H + h))
    cb = conv_b.reshape(1, -1)
    return pl.pallas_call(
        functools.partial(_mlstm_kernel, C=C, T=T, dk=HEAD_LANES),
        out_shape=jax.ShapeDtypeStruct((B, T, GROUP_WIDTH), F32),
        grid=(B, H),
        in_specs=[col(0), col(1), col(2), col(3),
                  pl.BlockSpec((1, T, HEAD_LANES), lambda b, h: (b, 0, 0)),
                  conv_q, conv_k, bias_q, bias_k,
                  pl.BlockSpec((1, HEAD_LANES), lambda b, h: (0, 0)),
                  pl.BlockSpec((1, HEAD_LANES), lambda b, h: (0, 0)),
                  pl.BlockSpec((2, C, C), lambda b, h: (0, 0, 0))],
        out_specs=pl.BlockSpec((1, T, HEAD_LANES), lambda b, h: (b, 0, h)),
        scratch_shapes=[pltpu.VMEM((T, HEAD_LANES), F32), pltpu.VMEM((T, HEAD_LANES), F32),
                        pltpu.VMEM((T + 16, HEAD_LANES), F32), pltpu.VMEM((T, HEAD_LANES), F32),
                        pltpu.VMEM((T, HEAD_LANES), F32), pltpu.VMEM((T, HEAD_LANES), F32),
                        pltpu.VMEM((SUBLANES, T), F32), pltpu.VMEM((T, 2 * HEAD_LANES), BF16),
                        pltpu.VMEM((2 * HEAD_LANES, T), BF16)],
        compiler_params=pltpu.CompilerParams(
            dimension_semantics=("arbitrary", "arbitrary"), vmem_limit_bytes=VMEM_LIMIT),
        name="mlstm",
    )(y, y, y, y, gates, conv_w, conv_w, cb, cb, gbias, norm_gain.reshape(1, HEAD_LANES), jnp.asarray(tri))


def _na_bias_table(rpb, rows):
    kr = min(NA_ROWS, rows)
    c = np.arange(GRID_W)
    cstart = np.clip(c - NA_COLS // 2, 0, GRID_W - NA_COLS)
    kc = np.arange(GRID_W)
    valid = (kc[None, :] >= cstart[:, None]) & (kc[None, :] < cstart[:, None] + NA_COLS)
    coff = np.clip(kc[None, :] - c[:, None] + NA_COLS - 1, 0, 2 * NA_COLS - 2)
    di = np.arange(kr)
    i = np.arange(kr)
    roff = i[None, :] - di[:, None] + NA_ROWS - 1
    heads = rpb.shape[0]
    cols = jnp.where(jnp.asarray(valid)[None, None], rpb.astype(F32)[:, :, coff], NEG_BIG)
    tab = jnp.take(cols, jnp.asarray(roff.reshape(-1)), axis=1)
    tab = tab.reshape(heads, kr, kr, GRID_W, GRID_W).transpose(0, 1, 3, 2, 4)
    return tab.reshape(heads, kr, GRID_W, kr * GRID_W)


def _na_kernel(q_ref, k_ref, v_ref, bm_ref, o_ref, *, rows, kr):
    W = GRID_W
    heads_per_block = HEAD_LANES // NA_DIM

    head_of_lane = lax.broadcasted_iota(jnp.int32, (W, HEAD_LANES), 1) // NA_DIM

    def body(j, _):
        work = []
        for u in range(NA_STEP_ROWS):
            r = NA_STEP_ROWS * j + u
            rs = jnp.clip(r - kr // 2, 0, rows - kr)
            di = r - rs
            qs = pl.ds(pl.multiple_of(r * W, W), W)
            ks = pl.ds(pl.multiple_of(rs * W, W), kr * W)
            q = q_ref[0, qs, :] * (NA_DIM ** -0.5)
            kw = k_ref[0, ks, :].astype(BF16)
            for hh in range(heads_per_block):
                s = _dot_nt(jnp.where(head_of_lane == hh, q, 0.0).astype(BF16), kw) + bm_ref[hh, di]
                work.append((u, hh, qs, ks, s))
        probs = []
        for u, hh, qs, ks, s in work:
            e = jnp.exp(s - jnp.max(s, axis=-1, keepdims=True))
            probs.append((e.astype(BF16), jnp.sum(e, axis=-1, keepdims=True)))
        outs = {}
        for (u, hh, qs, ks, s), (e, l) in zip(work, probs):
            o = _dot(e, v_ref[0, ks, :].astype(BF16)) / l
            outs[u] = o if hh == 0 else jnp.where(head_of_lane == hh, o, outs[u])
            if hh == heads_per_block - 1:
                o_ref[0, qs, :] = outs[u]
        return 0

    lax.fori_loop(0, rows // NA_STEP_ROWS, body, 0)


def neighbourhood_attention(y, rpb):
    B, T, _ = y.shape
    rows = T // GRID_W
    kr = min(NA_ROWS, rows)
    HB = GROUP_WIDTH // HEAD_LANES
    hpb = HEAD_LANES // NA_DIM
    bm = _na_bias_table(rpb, rows)

    def col(group):
        return pl.BlockSpec((1, T, HEAD_LANES), lambda b, h, group=group: (b, 0, group * HB + h))

    return pl.pallas_call(
        functools.partial(_na_kernel, rows=rows, kr=kr),
        out_shape=jax.ShapeDtypeStruct((B, T, GROUP_WIDTH), F32),
        grid=(B, HB),
        in_specs=[col(4), col(5), col(6),
                  pl.BlockSpec((hpb, kr, GRID_W, kr * GRID_W), lambda b, h: (h, 0, 0, 0))],
        out_specs=pl.BlockSpec((1, T, HEAD_LANES), lambda b, h: (b, 0, h)),
        compiler_params=pltpu.CompilerParams(
            dimension_semantics=("arbitrary", "arbitrary"), vmem_limit_bytes=VMEM_LIMIT),
        name="neighbourhood_attention",
    )(y, y, y, bm)


def _out_kernel(a_ref, b_ref, x_ref, mod_ref, gain_ref, w_ref, r_ref, x1_ref, h2_ref, lg_ref):
    G = a_ref.shape[-1]
    y = _dot(a_ref[0].astype(BF16), w_ref[0:G, :]) + _dot(b_ref[0].astype(BF16), w_ref[G:2 * G, :])
    x1 = x_ref[0] + mod_ref[0, 2:3, :] * y
    x1_ref[0] = x1
    h2 = _rms(x1, gain_ref[...]) * (1.0 + mod_ref[0, 4:5, :]) + mod_ref[0, 3:4, :]
    _store_row_tiles(h2_ref.at[0], h2)
    lg_ref[...] = lax.dot_general(r_ref[...], h2, (((1,), (1,)), ((), ())), precision=HIGHEST,
                                  preferred_element_type=F32)


def out_proj(a_out, b_out, x, mod, gain2, w_out_bf16, router):
    B, T, D = x.shape
    G = a_out.shape[-1]
    E = router.shape[1]
    tm = min(512, T)
    nt = T // tm
    return pl.pallas_call(
        _out_kernel,
        out_shape=[jax.ShapeDtypeStruct((B, T, D), F32),
                   jax.ShapeDtypeStruct((B, T, D // HEAD_LANES, HEAD_LANES), F32),
                   jax.ShapeDtypeStruct((E, B * T), F32)],
        grid=(B, T // tm),
        in_specs=[pl.BlockSpec((1, tm, G), lambda b, i: (b, i, 0)),
                  pl.BlockSpec((1, tm, G), lambda b, i: (b, i, 0)),
                  pl.BlockSpec((1, tm, D), lambda b, i: (b, i, 0)),
                  pl.BlockSpec((1, 6, D), lambda b, i: (b, 0, 0)),
                  pl.BlockSpec((1, D), lambda b, i: (0, 0)),
                  pl.BlockSpec((2 * G, D), lambda b, i: (0, 0)),
                  pl.BlockSpec((E, D), lambda b, i: (0, 0))],
        out_specs=[pl.BlockSpec((1, tm, D), lambda b, i: (b, i, 0)),
                   pl.BlockSpec((1, tm, D // HEAD_LANES, HEAD_LANES), lambda b, i: (b, i, 0, 0)),
                   pl.BlockSpec((E, tm), lambda b, i, nt=nt: (0, b * nt + i))],
        compiler_params=pltpu.CompilerParams(
            dimension_semantics=("arbitrary", "arbitrary"), vmem_limit_bytes=VMEM_LIMIT),
        name="out_proj",
    )(a_out, b_out, x, mod, gain2.reshape(1, D), w_out_bf16, router.T)


def _route_kernel(lg_ref, bias_ref, idx_ref, w_ref, cnt_ref, *, tiles_per_group):
    @pl.when(pl.program_id(0) % tiles_per_group == 0)
    def _():
        cnt_ref[...] = jnp.zeros(cnt_ref.shape, F32)

    scores = _sigmoid(lg_ref[...])
    sel = scores + bias_ref[...]
    E, tm = sel.shape
    per_group = E // N_GROUPS
    neg = -jnp.inf
    eid = lax.broadcasted_iota(jnp.int32, (E, tm), 0).astype(F32)
    eid_g = lax.broadcasted_iota(jnp.int32, (per_group, tm), 0).astype(F32)

    def first_argmax(x, ids, sentinel):
        m = jnp.max(x, axis=0, keepdims=True)
        i = jnp.min(jnp.where(x == m, ids, sentinel), axis=0, keepdims=True)
        return m, i

    parts, gscore = [], []
    for g in range(N_GROUPS):
        x = sel[g * per_group:(g + 1) * per_group]
        parts.append(x)
        m1, i1 = first_argmax(x, eid_g, float(per_group))
        m2 = jnp.max(jnp.where(eid_g == i1, neg, x), axis=0, keepdims=True)
        gscore.append(m1 + m2)
    kept = []
    for g in range(N_GROUPS):
        beaten = jnp.zeros((1, tm), F32)
        for o in range(N_GROUPS):
            if o == g:
                continue
            wins = (gscore[o] >= gscore[g]) if o < g else (gscore[o] > gscore[g])
            beaten = beaten + wins.astype(F32)
        kept.append(jnp.where(beaten < TOPK_GROUPS, parts[g], neg))
    sel = jnp.concatenate(kept, axis=0)

    ids, vals = [], []
    w_sum = jnp.zeros((1, tm), F32)
    chosen = jnp.zeros((E, tm), F32)
    for k in range(TOP_K):
        _, i = first_argmax(sel, eid, float(E))
        hit = eid == i
        val = jnp.sum(jnp.where(hit, scores, 0.0), axis=0, keepdims=True)
        sel = jnp.where(hit, neg, sel)
        chosen = jnp.where(hit, 1.0, chosen)
        ids.append(i)
        vals.append(val)
        w_sum = w_sum + val
    idx_ref[...] = jnp.concatenate(ids, axis=0).astype(jnp.int32)
    w_ref[...] = jnp.concatenate(vals, axis=0) / w_sum * ROUTED_SCALE
    part = chosen[:, 0:HEAD_LANES]
    for l0 in range(HEAD_LANES, tm, HEAD_LANES):
        part = part + chosen[:, l0:l0 + HEAD_LANES]
    cnt_ref[0] = cnt_ref[0] + part


def route(logits_t, router_bias, group_tokens):
    E, N = logits_t.shape
    tm = min(512, N)
    tpg = group_tokens // tm
    idx, w, cnt = pl.pallas_call(
        functools.partial(_route_kernel, tiles_per_group=tpg),
        out_shape=[jax.ShapeDtypeStruct((TOP_K, N), jnp.int32), jax.ShapeDtypeStruct((TOP_K, N), F32),
                   jax.ShapeDtypeStruct((N // group_tokens, E, HEAD_LANES), F32)],
        grid=(N // tm,),
        in_specs=[pl.BlockSpec((E, tm), lambda i: (0, i)), pl.BlockSpec((E, 1), lambda i: (0, 0))],
        out_specs=[pl.BlockSpec((TOP_K, tm), lambda i: (0, i)), pl.BlockSpec((TOP_K, tm), lambda i: (0, i)),
                   pl.BlockSpec((1, E, HEAD_LANES), lambda i, tpg=tpg: (i // tpg, 0, 0))],
        compiler_params=pltpu.CompilerParams(dimension_semantics=("arbitrary",)),
        name="route",
    )(logits_t, router_bias.reshape(E, 1))
    return idx, w, jnp.sum(cnt, axis=-1).astype(jnp.int32)


def _loop_unrolled(n, body):
    u = MOE_LOOP_UNROLL
    lax.fori_loop(0, n // u, functools.partial(body, count=u), 0)
    lax.fori_loop(n // u * u, n, functools.partial(body, count=1), 0)


def _moe_kernel(off_ref, tok_ref, wl_ref, x_ref, wg_ref, wu_ref, wd_ref, acc_ref, xg_ref, yb_ref,
                *, R, E, per_group, experts_per_step):
    g = pl.program_id(0)
    step = pl.program_id(1)

    @pl.when(step == 0)
    def _():
        acc_ref[...] = jnp.zeros(acc_ref.shape, F32)

    @pl.when((g == 0) & (step == 0))
    def _():
        xg_ref[...] = jnp.zeros(xg_ref.shape, F32)

    U = SUBLANES
    chunks = U
    lanes = x_ref.shape[-1]

    def token_tile(ref, t8):
        return ref.at[0, pl.ds(pl.multiple_of(t8, U), U), :]

    def gather(le, s0, nr):
        def body(j, _, count):
            for gg in range(count):
                base = s0 + (j * count + gg) * U
                for i in range(U):
                    xg_ref[le, j * count + gg, pl.ds(i, chunks, stride=U), :] = (
                        token_tile(x_ref, tok_ref[base + i])[...])
            return 0

        whole = jnp.minimum((nr + U - 1) // U, (per_group - s0) // U)
        _loop_unrolled(whole, body)

        def single(r, _):
            xg_ref[le, r // U, pl.ds(r % U, chunks, stride=U), :] = token_tile(x_ref, tok_ref[s0 + r])[...]
            return 0

        lax.fori_loop(whole * U, nr, single, 0)

    def load_block(le):
        return jnp.concatenate([xg_ref[le, :, c * U:(c + 1) * U, :].reshape(R, lanes) for c in range(chunks)],
                               axis=1).astype(BF16)

    def store_block(le, y):
        for c in range(chunks):
            yb_ref[le, :, c * U:(c + 1) * U, :] = y[:, c * lanes:(c + 1) * lanes].reshape(R // U, U, lanes)

    def scatter(le, s0, nr):
        def group(j, _, count):
            base = s0 + j * count * U
            rows = range(count * U)
            tiles = [token_tile(acc_ref, tok_ref[base + i]) for i in rows]
            wts = [wl_ref[base + i] for i in rows]
            new = [tiles[i][...] + wts[i] * yb_ref[le, j * count + i // U, pl.ds(i % U, chunks, stride=U), :]
                   for i in rows]
            for i in rows:
                tiles[i][...] = new[i]
            return 0

        groups = nr // U
        _loop_unrolled(groups, group)

        def single(r, _):
            tile = token_tile(acc_ref, tok_ref[s0 + r])
            row = yb_ref[le, r // U, pl.ds(r % U, chunks, stride=U), :]
            tile[...] = tile[...] + wl_ref[s0 + r] * row
            return 0

        lax.fori_loop(groups * U, nr, single, 0)

    segments = []
    for le in range(experts_per_step):
        segment = g * E + step * experts_per_step + le
        seg = off_ref[segment]
        segments.append((seg - g * per_group, off_ref[segment + 1] - seg))

    for le, (start, cnt) in enumerate(segments):
        gather(le, start, jnp.minimum(R, cnt))
    xs = [load_block(le) for le in range(experts_per_step)]
    gu = [(_dot(xs[le], wg_ref[le]), _dot(xs[le], wu_ref[le])) for le in range(experts_per_step)]
    hs = [(_silu(gate) * up).astype(BF16) for gate, up in gu]
    for le in range(experts_per_step):
        store_block(le, _dot(hs[le], wd_ref[le]))
    for le, (start, cnt) in enumerate(segments):
        scatter(le, start, jnp.minimum(R, cnt))

    for le, (start, cnt) in enumerate(segments):
        def more(sb, _, le=le, start=start, cnt=cnt):
            s0 = start + sb * R
            nr = jnp.minimum(R, cnt - sb * R)
            gather(le, s0, nr)
            x = load_block(le)
            store_block(le, _dot((_silu(_dot(x, wg_ref[le])) * _dot(x, wu_ref[le])).astype(BF16), wd_ref[le]))
            scatter(le, s0, nr)
            return 0

        lax.fori_loop(1, (cnt + R - 1) // R, more, 0)


def routed_experts(h2, top_idx, top_w, counts, exp_gate, exp_up, exp_down):
    N, chunks, lanes = h2.shape
    D = chunks * lanes
    E, _, F = exp_gate.shape
    TG = min(MOE_TOKEN_GROUP, N)
    G = N // TG
    per_group = TG * TOP_K
    key = (jnp.arange(N, dtype=jnp.int32)[None, :] // TG) * E + top_idx
    order = jnp.argsort(key.reshape(-1))
    assert chunks == SUBLANES
    tok_s = (((order % N) % TG) * chunks).astype(jnp.int32)
    w_s = top_w.reshape(-1)[order]
    off = jnp.concatenate([jnp.zeros((1,), jnp.int32), jnp.cumsum(counts.reshape(-1)).astype(jnp.int32)])

    EPS = MOE_EXPERTS_PER_STEP
    grid_spec = pltpu.PrefetchScalarGridSpec(
        num_scalar_prefetch=1,
        grid=(G, E // EPS),
        in_specs=[pl.BlockSpec((per_group,), lambda g, e, off: (g,), memory_space=pltpu.SMEM),
                  pl.BlockSpec((per_group,), lambda g, e, off: (g,), memory_space=pltpu.SMEM),
                  pl.BlockSpec((1, TG * chunks, lanes), lambda g, e, off: (g, 0, 0),
                               pipeline_mode=pl.Buffered(1)),
                  pl.BlockSpec((EPS, D, F), lambda g, e, off: (e, 0, 0)),
                  pl.BlockSpec((EPS, D, F), lambda g, e, off: (e, 0, 0)),
                  pl.BlockSpec((EPS, F, D), lambda g, e, off: (e, 0, 0))],
        out_specs=pl.BlockSpec((1, TG * chunks, lanes), lambda g, e, off: (g, 0, 0),
                               pipeline_mode=pl.Buffered(1)),
        scratch_shapes=[pltpu.VMEM((EPS, MOE_ROWS // SUBLANES, chunks * SUBLANES, lanes), F32),
                        pltpu.VMEM((EPS, MOE_ROWS // SUBLANES, chunks * SUBLANES, lanes), F32)],
    )
    out = pl.pallas_call(
        functools.partial(_moe_kernel, R=MOE_ROWS, E=E, per_group=per_group, experts_per_step=EPS),
        out_shape=jax.ShapeDtypeStruct((G, TG * chunks, lanes), F32),
        grid_spec=grid_spec,
        compiler_params=pltpu.CompilerParams(
            dimension_semantics=("arbitrary", "arbitrary"), vmem_limit_bytes=VMEM_LIMIT),
        name="routed_experts",
    )(off, tok_s, w_s, h2.reshape(G, TG * chunks, lanes), exp_gate, exp_up, exp_down)
    return out.reshape(N, chunks, lanes)


def _final_kernel(x1_ref, h2_ref, rt_ref, mod_ref, wg_ref, wu_ref, wd_ref, *rest, final):
    if final:
        fg_ref, o_ref = rest
    else:
        (o_ref,) = rest
    hb = _load_row_tiles(h2_ref.at[0]).astype(BF16)
    hmid = _silu(_dot(hb, wg_ref[...])) * _dot(hb, wu_ref[...])
    shared = _dot(hmid.astype(BF16), wd_ref[...])
    x2 = x1_ref[0] + mod_ref[0, 5:6, :] * (_load_row_tiles(rt_ref.at[0]) + shared)
    if final:
        x2 = _rms(x2, fg_ref[...])
    o_ref[0] = x2


def shared_and_residual(x1, h2, routed, mod, sh_gate, sh_up, sh_down, final_gain=None):
    B, T, D = x1.shape
    F = sh_gate.shape[1]
    tm = min(512, T)
    final = final_gain is not None
    tile = pl.BlockSpec((1, tm, D), lambda b, i: (b, i, 0))
    row_tiles = pl.BlockSpec((1, tm, D // HEAD_LANES, HEAD_LANES), lambda b, i: (b, i, 0, 0))
    in_specs = [tile, row_tiles, row_tiles,
                pl.BlockSpec((1, 6, D), lambda b, i: (b, 0, 0)),
                pl.BlockSpec((D, F), lambda b, i: (0, 0)),
                pl.BlockSpec((D, F), lambda b, i: (0, 0)),
                pl.BlockSpec((F, D), lambda b, i: (0, 0))]
    args = [x1, h2, routed, mod, sh_gate, sh_up, sh_down]
    if final:
        in_specs.append(pl.BlockSpec((1, D), lambda b, i: (0, 0)))
        args.append(final_gain.reshape(1, D))
    return pl.pallas_call(
        functools.partial(_final_kernel, final=final),
        out_shape=jax.ShapeDtypeStruct((B, T, D), F32),
        grid=(B, T // tm),
        in_specs=in_specs,
        out_specs=tile,
        compiler_params=pltpu.CompilerParams(
            dimension_semantics=("arbitrary", "arbitrary"), vmem_limit_bytes=VMEM_LIMIT),
        name="shared_and_residual",
    )(*args)


def moe_block(x1, h2, logits, mod, router_bias, exp_gate, exp_up, exp_down, sh_gate, sh_up, sh_down,
              final_gain=None):
    B, T, D = x1.shape
    top_idx, top_w, counts = route(logits, router_bias, min(MOE_TOKEN_GROUP, B * T))
    chunks = D // HEAD_LANES
    routed = routed_experts(h2.reshape(B * T, chunks, HEAD_LANES), top_idx, top_w, counts,
                            exp_gate.astype(BF16), exp_up.astype(BF16), exp_down.astype(BF16))
    return shared_and_residual(x1, h2, routed.reshape(B, T, chunks, HEAD_LANES), mod,
                               sh_gate.astype(BF16), sh_up.astype(BF16), sh_down.astype(BF16), final_gain)


def kernel(x, c, hgrn_lb_logits, l0_norm1, l0_norm2, l0_w_mod, l0_b_mod, l0_w_in, l0_w_out, l0_hgrn_norm, l0_diff_lq1, l0_diff_lk1, l0_diff_lq2, l0_diff_lk2, l0_diff_subln, l0_router, l0_router_bias, l0_exp_gate, l0_exp_up, l0_exp_down, l0_sh_gate, l0_sh_up, l0_sh_down, l1_norm1, l1_norm2, l1_w_mod, l1_b_mod, l1_w_in, l1_w_out, l1_conv_w, l1_conv_b, l1_ml_i_bias, l1_ml_f_bias, l1_ml_norm, l1_na_rpb, l1_router, l1_router_bias, l1_exp_gate, l1_exp_up, l1_exp_down, l1_sh_gate, l1_sh_up, l1_sh_down, final_norm):
    G = GROUP_WIDTH
    lb_all = jnp.cumsum(jax.nn.softmax(hgrn_lb_logits.astype(F32), axis=0), axis=0)
    layer_idx = 0
    lambda_init = 0.8 - 0.6 * math.exp(-0.3 * layer_idx)
    lam = (jnp.exp(jnp.sum(l0_diff_lq1.astype(F32) * l0_diff_lk1.astype(F32)))
           - jnp.exp(jnp.sum(l0_diff_lq2.astype(F32) * l0_diff_lk2.astype(F32))) + lambda_init)

    mod0 = ada_mod(c, l0_w_mod, l0_b_mod)
    y0 = in_proj(x, mod0, l0_norm1, l0_w_in.astype(BF16))
    a_out = hgrn2(y0, lb_all[0], l0_hgrn_norm)
    b_out = diff_attention(y0, lam, l0_diff_subln, layer_idx)
    x1, h2, logits = out_proj(a_out, b_out, x, mod0, l0_norm2, l0_w_out.astype(BF16), l0_router)
    xa = moe_block(x1, h2, logits, mod0, l0_router_bias, l0_exp_gate, l0_exp_up, l0_exp_down,
                   l0_sh_gate, l0_sh_up, l0_sh_down)

    mod1 = ada_mod(c, l1_w_mod, l1_b_mod)
    n_gate = l1_w_in.shape[1] - 7 * G
    w_main = jnp.concatenate([l1_w_in[:, :4 * G], l1_w_in[:, 4 * G + n_gate:]], axis=1).astype(BF16)
    w_gate = jnp.pad(l1_w_in[:, 4 * G:4 * G + n_gate], ((0, 0), (0, HEAD_LANES - n_gate)))
    y1, gates = in_proj(xa, mod1, l1_norm1, w_main, w_gate)
    c_out = mlstm(y1, gates, l1_conv_w, l1_conv_b, l1_ml_i_bias, l1_ml_f_bias, l1_ml_norm)
    d_out = neighbourhood_attention(y1, l1_na_rpb)
    x1, h2, logits = out_proj(c_out, d_out, xa, mod1, l1_norm2, l1_w_out.astype(BF16), l1_router)
    return moe_block(x1, h2, logits, mod1, l1_router_bias, l1_exp_gate, l1_exp_up, l1_exp_down,
                     l1_sh_gate, l1_sh_up, l1_sh_down, final_gain=final_norm)
```

```python
import functools
import math

import numpy as np
import jax
import jax.numpy as jnp
from jax import lax
from jax.experimental import pallas as pl
from jax.experimental.pallas import tpu as pltpu

F32 = jnp.float32
BF16 = jnp.bfloat16
HIGHEST = lax.Precision.HIGHEST
EPS = 1e-6

GRID_W = 64
GROUP_WIDTH = 512
HEAD_LANES = 128
HG_CHUNK = 64
ML_CHUNK = 128
STEP_CHUNKS = 2
ML_CONV = 5
DA_DIM = 64
ROPE_DIM = 16
ROPE_THETA = 500000.0
NA_ROWS = 8
NA_COLS = 16
NA_DIM = 64
NA_STEP_ROWS = 4
N_EXPERTS = 128
TOP_K = 8
N_GROUPS = 8
TOPK_GROUPS = 4
ROUTED_SCALE = 2.5
MOE_ROWS = 320
MOE_TOKEN_GROUP = 4096
MOE_EXPERTS_PER_STEP = 2
MOE_LOOP_UNROLL = 2
SUBLANES = 8
NEG_BIG = -1e30
VMEM_LIMIT = 48 * 1024 * 1024
MOE_VMEM_LIMIT = 56 * 1024 * 1024


def _dot(a, b, **kw):
    return jnp.dot(a, b, preferred_element_type=F32, **kw)


def _dot_nt(a, b):
    return lax.dot_general(a, b, (((1,), (1,)), ((), ())), preferred_element_type=F32)


def _dot_tn(a, b):
    return lax.dot_general(a, b, (((0,), (0,)), ((), ())), preferred_element_type=F32)


def _sigmoid(x):
    return jax.nn.sigmoid(x)


def _silu(x):
    return x * jax.nn.sigmoid(x)


def _log_sigmoid(x):
    return jnp.minimum(x, 0.0) - jnp.log(1.0 + jnp.exp(-jnp.abs(x)))


def _rms(x, gain):
    return x * lax.rsqrt(jnp.mean(x * x, axis=-1, keepdims=True) + EPS) * gain


def _store_row_tiles(ref, val):
    for c in range(ref.shape[1]):
        ref[:, c, :] = val[:, c * HEAD_LANES:(c + 1) * HEAD_LANES]


def _load_row_tiles(ref):
    return jnp.concatenate([ref[:, c, :] for c in range(ref.shape[1])], axis=1)


def _mod_kernel(c_ref, w_ref, b_ref, o_ref):
    o_ref[...] = _dot(_silu(c_ref[...]), w_ref[...], precision=HIGHEST) + b_ref[...]


def ada_mod(c, w_mod, b_mod):
    B, D = c.shape
    N = w_mod.shape[1]
    tn = 1024
    out = pl.pallas_call(
        _mod_kernel,
        out_shape=jax.ShapeDtypeStruct((B, N), F32),
        grid=(N // tn,),
        in_specs=[pl.BlockSpec((B, D), lambda j: (0, 0)),
                  pl.BlockSpec((D, tn), lambda j: (0, j)),
                  pl.BlockSpec((1, tn), lambda j: (0, j))],
        out_specs=pl.BlockSpec((B, tn), lambda j: (0, j)),
        name="ada_mod",
    )(c, w_mod, b_mod.reshape(1, N))
    return out.reshape(B, 6, D)


def _in_kernel(x_ref, mod_ref, gain_ref, w_ref, *rest, has_gate):
    if has_gate:
        wg_ref, o_ref, og_ref, h_ref = rest
    else:
        o_ref, h_ref = rest

    @pl.when(pl.program_id(2) == 0)
    def _():
        h = _rms(x_ref[0], gain_ref[...]) * (1.0 + mod_ref[0, 1:2, :]) + mod_ref[0, 0:1, :]
        h_ref[...] = h.astype(BF16)
        if has_gate:
            og_ref[0] = _dot(h, wg_ref[...], precision=HIGHEST)

    o_ref[0] = _dot(h_ref[...], w_ref[...])


def in_proj(x, mod, gain, w_bf16, w_gate=None):
    B, T, D = x.shape
    N = w_bf16.shape[1]
    tm = min(1024, T)
    tn = 512
    has_gate = w_gate is not None
    in_specs = [pl.BlockSpec((1, tm, D), lambda b, i, j: (b, i, 0)),
                pl.BlockSpec((1, 6, D), lambda b, i, j: (b, 0, 0)),
                pl.BlockSpec((1, D), lambda b, i, j: (0, 0)),
                pl.BlockSpec((D, tn), lambda b, i, j: (0, j))]
    out_shape = [jax.ShapeDtypeStruct((B, T, N), F32)]
    out_specs = [pl.BlockSpec((1, tm, tn), lambda b, i, j: (b, i, j))]
    args = [x, mod, gain.reshape(1, D), w_bf16]
    if has_gate:
        in_specs.append(pl.BlockSpec((D, HEAD_LANES), lambda b, i, j: (0, 0)))
        out_shape.append(jax.ShapeDtypeStruct((B, T, HEAD_LANES), F32))
        out_specs.append(pl.BlockSpec((1, tm, HEAD_LANES), lambda b, i, j: (b, i, 0)))
        args.append(w_gate)
    res = pl.pallas_call(
        functools.partial(_in_kernel, has_gate=has_gate),
        out_shape=out_shape,
        grid=(B, T // tm, N // tn),
        in_specs=in_specs,
        out_specs=out_specs,
        scratch_shapes=[pltpu.VMEM((tm, D), BF16)],
        compiler_params=pltpu.CompilerParams(
            dimension_semantics=("arbitrary", "arbitrary", "arbitrary")),
        name="in_proj",
    )(*args)
    return res if has_gate else res[0]


def _hgrn_consts(C):
    t = np.arange(C)
    tri = (t[None, :] <= t[:, None]).astype(np.float32)
    triT = np.ascontiguousarray(tri.T)
    wf, wb, mf = [tri], [triT], []
    levels = int(round(math.log2(C)))
    for l in range(levels):
        size = C >> l
        blk = t // size
        r = blk * size + size // 2
        wf.append(tri - tri[r - 1])
        wb.append(triT - triT[r])
        upper = (t % size) >= size // 2
        mf.append(((blk[:, None] == blk[None, :]) & upper[:, None] & (~upper)[None, :]).astype(np.float32))
    mf.append(np.eye(C, dtype=np.float32))
    ones = np.ones((8, C), np.float32)
    wf.append(ones)
    wb.append(ones)
    mf = np.stack(mf)
    mb = np.ascontiguousarray(np.transpose(mf, (0, 2, 1)))
    return np.concatenate(wf), np.concatenate(wb), mf, mb


def _split2(x):
    hi = x.astype(BF16)
    lo = (x - hi.astype(F32)).astype(BF16)
    return hi, lo


def _hgrn_kernel(q_ref, i_ref, ff_ref, fb_ref, g_ref, lb_ref, gain_ref, wf_ref, wb_ref, mf_ref, mb_ref,
                 o_ref, of_ref, ob_ref, *, C, T):
    n = T // C
    levels = int(round(math.log2(C)))
    dv = q_ref.shape[-1]

    def prepare(c, f_ref, lbd, w_ref):
        sl = pl.ds(pl.multiple_of(c * C, C), C)
        q = _silu(q_ref[0, sl, :])
        v = i_ref[0, sl, :]
        fg = lbd + (1.0 - lbd) * _sigmoid(f_ref[0, sl, :])
        lf = jnp.log(fg)
        d2 = _dot(w_ref[...], jnp.concatenate(_split2(lf), axis=1))
        dall = d2[:, 0:dv] + d2[:, dv:2 * dv]
        return dict(sl=sl, q=q, k=1.0 - fg, v=v.astype(BF16), dall=dall)

    def scores(p, m_ref):
        q, k, dall = p["q"], p["k"], p["dall"]
        attn = m_ref[levels] * _dot_nt(q.astype(BF16), k.astype(BF16))
        for l in range(levels):
            e = jnp.exp(-jnp.abs(dall[(l + 1) * C:(l + 2) * C]))
            attn = attn + m_ref[l] * _dot_nt((q * e).astype(BF16), (k * e).astype(BF16))
        return attn.astype(BF16)

    st0 = jnp.zeros((dv, dv), F32)

    def step(j, carry):
        states = list(carry)
        work = []
        for u in range(STEP_CHUNKS):
            work.append((0, prepare(STEP_CHUNKS * j + u, ff_ref, lb_ref[0:1, :], wf_ref), mf_ref, of_ref))
            work.append((1, prepare(n - 1 - STEP_CHUNKS * j - u, fb_ref, lb_ref[1:2, :], wb_ref), mb_ref, ob_ref))
        attn = [scores(p, m_ref) for _, p, m_ref, _ in work]
        local = []
        for (_, p, _, _), a in zip(work, attn):
            cum = p["dall"][0:C]
            tot = p["dall"][(levels + 1) * C:(levels + 1) * C + 1]
            kt = p["k"] * jnp.exp(tot - cum)
            local.append((_dot(a, p["v"]), _dot_tn(p["v"], kt.astype(BF16)), cum, tot))
        for (d, p, _, out_ref), (o_in, incr, cum, tot) in zip(work, local):
            st = states[d]
            out_ref[p["sl"], :] = o_in + _dot_nt((p["q"] * jnp.exp(cum)).astype(BF16), st.astype(BF16))
            states[d] = st * jnp.exp(tot) + incr
        return tuple(states)

    lax.fori_loop(0, n // STEP_CHUNKS, step, (st0, st0))

    rt = min(512, T)
    for r0 in range(0, T, rt):
        sl = slice(r0, r0 + rt)
        o_ref[0, sl, :] = _rms(of_ref[sl, :] + ob_ref[sl, :], gain_ref[...]) * _silu(g_ref[0, sl, :])


def hgrn2(y, lb, norm_gain):
    B, T, _ = y.shape
    H = GROUP_WIDTH // HEAD_LANES
    C = HG_CHUNK
    wf, wb, mf, mb = _hgrn_consts(C)
    wf, wb = jnp.asarray(wf, BF16), jnp.asarray(wb, BF16)
    mf, mb = jnp.asarray(mf), jnp.asarray(mb)

    def col(group):
        return pl.BlockSpec((1, T, HEAD_LANES), lambda b, h, group=group: (b, 0, group * H + h))

    def const(a):
        nd = a.ndim
        return pl.BlockSpec(a.shape, lambda b, h, nd=nd: (0,) * nd)

    return pl.pallas_call(
        functools.partial(_hgrn_kernel, C=C, T=T),
        out_shape=jax.ShapeDtypeStruct((B, T, GROUP_WIDTH), F32),
        grid=(B, H),
        in_specs=[col(0), col(1), col(2), col(3), col(4),
                  pl.BlockSpec((2, HEAD_LANES), lambda b, h: (0, h)),
                  pl.BlockSpec((1, HEAD_LANES), lambda b, h: (0, 0)),
                  const(wf), const(wb), const(mf), const(mb)],
        out_specs=pl.BlockSpec((1, T, HEAD_LANES), lambda b, h: (b, 0, h)),
        scratch_shapes=[pltpu.VMEM((T, HEAD_LANES), F32), pltpu.VMEM((T, HEAD_LANES), F32)],
        compiler_params=pltpu.CompilerParams(
            dimension_semantics=("arbitrary", "arbitrary"), vmem_limit_bytes=VMEM_LIMIT),
        name="hgrn2",
    )(y, y, y, y, y, lb, norm_gain.reshape(1, HEAD_LANES), wf, wb, mf, mb)


def _rope_tables(T):
    pos = np.arange(T, dtype=np.float32)
    inv_freq = (ROPE_THETA ** (-np.arange(0, ROPE_DIM, 2, dtype=np.float32) / ROPE_DIM)).astype(np.float32)
    ang = pos[:, None] * inv_freq[None, :]
    cos, sin = np.cos(ang), np.sin(ang)
    half = ROPE_DIM // 2
    c = np.ones((T, HEAD_LANES), np.float32)
    s_prev = np.zeros((T, HEAD_LANES), np.float32)
    s_next = np.zeros((T, HEAD_LANES), np.float32)
    for base in range(0, HEAD_LANES, DA_DIM):
        c[:, base:base + half] = cos
        c[:, base + half:base + ROPE_DIM] = cos
        s_next[:, base:base + half] = -sin
        s_prev[:, base + half:base + ROPE_DIM] = sin
    return jnp.asarray(c), jnp.asarray(s_prev), jnp.asarray(s_next)


def _rope(x, c, s_prev, s_next):
    half = ROPE_DIM // 2
    lanes = x.shape[-1]
    return (x * c + pltpu.roll(x, half, axis=1) * s_prev
            + pltpu.roll(x, lanes - half, axis=1) * s_next)


def _diff_kernel(lam_ref, q_ref, k_ref, v_ref, cq_ref, spq_ref, snq_ref, ck_ref, spk_ref, snk_ref,
                 subln_ref, o_ref, kr_ref, vb_ref, *, T, out_scale):
    rt = min(512, T)

    @pl.when(pl.program_id(2) == 0)
    def _():
        for r0 in range(0, T, rt):
            sl = slice(r0, r0 + rt)
            kr_ref[sl, :] = _rope(k_ref[0, sl, :], ck_ref[sl, :], spk_ref[sl, :], snk_ref[sl, :]).astype(BF16)
            vb_ref[sl, 0:HEAD_LANES] = v_ref[0, sl, :].astype(BF16)
            lane = lax.broadcasted_iota(jnp.int32, (rt, HEAD_LANES), 1)
            vb_ref[sl, HEAD_LANES:2 * HEAD_LANES] = jnp.where(lane == 0, 1.0, 0.0).astype(BF16)

    q = _rope(q_ref[0], cq_ref[...], spq_ref[...], snq_ref[...]) * (DA_DIM ** -0.5 * math.log2(math.e))
    lam = lam_ref[0, 0]
    v = vb_ref[...]
    map_of_lane = lax.broadcasted_iota(jnp.int32, q.shape, 1) // DA_DIM

    scores = [_dot_nt(jnp.where(map_of_lane == m, q, 0.0).astype(BF16), kr_ref[...]) for m in range(2)]
    probs = [jnp.exp2(s - jnp.max(s, axis=-1, keepdims=True)).astype(BF16) for s in scores]
    pv = []
    for e in probs:
        full = _dot(e, v)
        pv.append(full[:, 0:HEAD_LANES] / full[:, HEAD_LANES:HEAD_LANES + 1])
    o = pv[0] - lam * pv[1]
    o_ref[0] = _rms(o, subln_ref[...]) * out_scale


def diff_attention(y, lam, subln, layer_idx):
    B, T, _ = y.shape
    H = GROUP_WIDTH // HEAD_LANES
    tq = min(256, T)
    lambda_init = 0.8 - 0.6 * math.exp(-0.3 * layer_idx)
    c, sp, sn = _rope_tables(T)

    def col(group, rows):
        if rows == T:
            return pl.BlockSpec((1, T, HEAD_LANES), lambda b, h, i, group=group: (b, 0, group * H + h))
        return pl.BlockSpec((1, rows, HEAD_LANES), lambda b, h, i, group=group: (b, i, group * H + h))

    tab_q = pl.BlockSpec((tq, HEAD_LANES), lambda b, h, i: (i, 0))
    tab_k = pl.BlockSpec((T, HEAD_LANES), lambda b, h, i: (0, 0))
    return pl.pallas_call(
        functools.partial(_diff_kernel, T=T, out_scale=1.0 - lambda_init),
        out_shape=jax.ShapeDtypeStruct((B, T, GROUP_WIDTH), F32),
        grid=(B, H, T // tq),
        in_specs=[pl.BlockSpec(memory_space=pltpu.SMEM),
                  col(5, tq), col(6, T), col(7, T),
                  tab_q, tab_q, tab_q, tab_k, tab_k, tab_k,
                  pl.BlockSpec((1, HEAD_LANES), lambda b, h, i: (0, 0))],
        out_specs=pl.BlockSpec((1, tq, HEAD_LANES), lambda b, h, i: (b, i, h)),
        scratch_shapes=[pltpu.VMEM((T, HEAD_LANES), BF16), pltpu.VMEM((T, 2 * HEAD_LANES), BF16)],
        compiler_params=pltpu.CompilerParams(
            dimension_semantics=("arbitrary", "arbitrary", "arbitrary"), vmem_limit_bytes=VMEM_LIMIT),
        name="diff_attention",
    )(lam.reshape(1, 1), y, y, y, c, sp, sn, c, sp, sn, subln.reshape(1, HEAD_LANES))


def _mlstm_kernel(q_ref, k_ref, v_ref, og_ref, gt_ref, cwq_ref, cwk_ref, cbq_ref, cbk_ref, gbias_ref,
                  gain_ref, tri_ref, o_ref, qc_ref, kc_ref, xp_ref, gx_ref, hf_ref, hb_ref, gxt_ref, va_ref,
                  vta_ref, *, C, T, dk):
    n = T // C
    head = pl.program_id(1)
    pad = 8
    half = ML_CONV // 2
    rt = min(512, T)

    xp_ref[0:pad, :] = jnp.zeros((pad, dk), F32)
    xp_ref[pad + T:pad + T + pad, :] = jnp.zeros((pad, dk), F32)
    for src, cw_ref, cb_ref, dst, scale in ((q_ref, cwq_ref, cbq_ref, qc_ref, 1.0),
                                            (k_ref, cwk_ref, cbk_ref, kc_ref, dk ** -0.5)):
        xp_ref[pad:pad + T, :] = src[0]
        for r0 in range(0, T, rt):
            acc = jnp.zeros((rt, dk), F32) + cb_ref[...]
            for j in range(ML_CONV):
                acc = acc + xp_ref[pad + r0 + j - half:pad + r0 + j - half + rt, :] * cw_ref[j:j + 1, :]
            dst[r0:r0 + rt, :] = _silu(acc) * scale

    lane = lax.broadcasted_iota(jnp.int32, (rt, HEAD_LANES), 1)
    is_f = (lane % 8) >= 4
    for r0 in range(0, T, rt):
        g = gt_ref[0, r0:r0 + rt, :] + gbias_ref[...]
        p = jnp.where(is_f, _log_sigmoid(g), g)
        x = jnp.zeros((rt, HEAD_LANES), F32)
        for j, src_lane in enumerate((0, 4, 8, 12)):
            colv = jnp.sum(jnp.where(lane == src_lane + head, p, 0.0), axis=1, keepdims=True)
            x = jnp.where(lane == j, colv, x)
        gx_ref[r0:r0 + rt, :] = x
        gxt_ref[:, r0:r0 + rt] = x.T[0:SUBLANES, :]
        v = v_ref[0, r0:r0 + rt, :]
        va_ref[r0:r0 + rt, 0:dk] = v.astype(BF16)
        va_ref[r0:r0 + rt, dk:2 * dk] = jnp.where(lane == 0, 1.0, 0.0).astype(BF16)
        vta_ref[0:dk, r0:r0 + rt] = v.T.astype(BF16)
        sub = lax.broadcasted_iota(jnp.int32, (dk, rt), 0)
        vta_ref[dk:2 * dk, r0:r0 + rt] = jnp.where(sub == 0, 1.0, 0.0).astype(BF16)

    row = lax.broadcasted_iota(jnp.int32, (C, C), 0)
    colm = lax.broadcasted_iota(jnp.int32, (C, C), 1)

    init = (jnp.zeros((2 * dk, dk), F32), jnp.full((1, 1), NEG_BIG, F32))

    def step(j, carry):
        carries = list(carry)
        work = []
        for u in range(STEP_CHUNKS):
            work.append((0, STEP_CHUNKS * j + u, hf_ref))
            work.append((1, n - 1 - STEP_CHUNKS * j - u, hb_ref))
        chunks = []
        for d, c, out_ref in work:
            sl = pl.ds(pl.multiple_of(c * C, C), C)
            x = gx_ref[sl, :]
            xr = gxt_ref[:, sl]
            chunks.append(dict(d=d, sl=sl, out=out_ref, x=x, xr=xr, q=qc_ref[sl, :], k=kc_ref[sl, :],
                               cumx=_dot(tri_ref[d], x, precision=HIGHEST),
                               cumr=_dot(xr, tri_ref[1 - d], precision=HIGHEST)))
        for p in chunks:
            p["qk"] = _dot_nt(p["q"].astype(BF16), p["k"].astype(BF16))
        for p in chunks:
            d, x, xr, cumx, cumr = p["d"], p["x"], p["xr"], p["cumx"], p["cumr"]
            mask = (colm <= row) if d == 0 else (colm >= row)
            ig_c = x[:, 2 * d:2 * d + 1]
            ig_r = xr[2 * d:2 * d + 1, :]
            cum_c = cumx[:, 2 * d + 1:2 * d + 2]
            cum_r = cumr[2 * d + 1:2 * d + 2, :]
            tot = jnp.sum(x[:, 2 * d + 1:2 * d + 2], axis=0, keepdims=True)
            dmat = jnp.where(mask, cum_c - cum_r + ig_r, -jnp.inf)
            dmax = jnp.max(dmat, axis=1, keepdims=True)
            a = p["qk"] * jnp.exp(dmat - dmax)
            ds = tot - cum_c + ig_c
            dsmax = jnp.max(ds, axis=0, keepdims=True)
            kw = p["k"] * jnp.exp(ds - dsmax)
            p.update(cum_c=cum_c, tot=tot, dmax=dmax, dsmax=dsmax, a=a.astype(BF16), kw=kw.astype(BF16))
        for p in chunks:
            p["num"] = _dot(p["a"], va_ref[p["sl"], :])
            p["upd"] = _dot(vta_ref[:, p["sl"]], p["kw"])
        for p in chunks:
            state, m = carries[p["d"]]
            g = p["cum_c"] + m
            mt = jnp.maximum(g, p["dmax"])
            full = (jnp.exp(g - mt) * _dot_nt(p["q"].astype(BF16), state.astype(BF16))
                    + jnp.exp(p["dmax"] - mt) * p["num"])
            den = full[:, dk:dk + 1]
            p["out"][p["sl"], :] = full[:, 0:dk] / jnp.maximum(jnp.abs(den), jnp.exp(-mt))
            m_new = jnp.maximum(p["tot"] + m, p["dsmax"])
            decay = jnp.exp(p["tot"] + m - m_new)
            scale = jnp.exp(p["dsmax"] - m_new)
            carries[p["d"]] = (decay * state + scale * p["upd"], m_new)
        return tuple(carries)

    lax.fori_loop(0, n // STEP_CHUNKS, step, (init, init))

    for r0 in range(0, T, rt):
        sl = slice(r0, r0 + rt)
        o_ref[0, sl, :] = _rms(hf_ref[sl, :] + hb_ref[sl, :], gain_ref[...]) * _sigmoid(og_ref[0, sl, :])


def mlstm(y, gates, conv_w, conv_b, i_bias, f_bias, norm_gain):
    B, T, _ = y.shape
    H = GROUP_WIDTH // HEAD_LANES
    C = min(ML_CHUNK, T)
    t = np.arange(C)
    tri = np.stack([(t[None, :] <= t[:, None]), (t[None, :] >= t[:, None])]).astype(np.float32)
    gbias = jnp.zeros((1, HEAD_LANES), F32)
    gbias = gbias.at[0, 0:4].set(i_bias[0]).at[0, 4:8].set(f_bias[0])
    gbias = gbias.at[0, 8:12].set(i_bias[1]).at[0, 12:16].set(f_bias[1])

    def col(group):
        return pl.BlockSpec((1, T, HEAD_LANES), lambda b, h, group=group: (b, 0, group * H + h))

    conv_q = pl.BlockSpec((ML_CONV, HEAD_LANES), lambda b, h: (0, h))
    conv_k = pl.BlockSpec((ML_CONV, HEAD_LANES), lambda b, h: (0, H + h))
    bias_q = pl.BlockSpec((1, HEAD_LANES), lambda b, h: (0, h))
    bias_k = pl.BlockSpec((1, HEAD_LANES), lambda b, h: (0, H + h))
    cb = conv_b.reshape(1, -1)
    return pl.pallas_call(
        functools.partial(_mlstm_kernel, C=C, T=T, dk=HEAD_LANES),
        out_shape=jax.ShapeDtypeStruct((B, T, GROUP_WIDTH), F32),
        grid=(B, H),
        in_specs=[col(0), col(1), col(2), col(3),
                  pl.BlockSpec((1, T, HEAD_LANES), lambda b, h: (b, 0, 0)),
                  conv_q, conv_k, bias_q, bias_k,
                  pl.BlockSpec((1, HEAD_LANES), lambda b, h: (0, 0)),
                  pl.BlockSpec((1, HEAD_LANES), lambda b, h: (0, 0)),
                  pl.BlockSpec((2, C, C), lambda b, h: (0, 0, 0))],
        out_specs=pl.BlockSpec((1, T, HEAD_LANES), lambda b, h: (b, 0, h)),
        scratch_shapes=[pltpu.VMEM((T, HEAD_LANES), F32), pltpu.VMEM((T, HEAD_LANES), F32),
                        pltpu.VMEM((T + 16, HEAD_LANES), F32), pltpu.VMEM((T, HEAD_LANES), F32),
                        pltpu.VMEM((T, HEAD_LANES), F32), pltpu.VMEM((T, HEAD_LANES), F32),
                        pltpu.VMEM((SUBLANES, T), F32), pltpu.VMEM((T, 2 * HEAD_LANES), BF16),
                        pltpu.VMEM((2 * HEAD_LANES, T), BF16)],
        compiler_params=pltpu.CompilerParams(
            dimension_semantics=("arbitrary", "arbitrary"), vmem_limit_bytes=VMEM_LIMIT),
        name="mlstm",
    )(y, y, y, y, gates, conv_w, conv_w, cb, cb, gbias, norm_gain.reshape(1, HEAD_LANES), jnp.asarray(tri))


def _na_bias_table(rpb, rows):
    kr = min(NA_ROWS, rows)
    c = np.arange(GRID_W)
    cstart = np.clip(c - NA_COLS // 2, 0, GRID_W - NA_COLS)
    kc = np.arange(GRID_W)
    valid = (kc[None, :] >= cstart[:, None]) & (kc[None, :] < cstart[:, None] + NA_COLS)
    coff = np.clip(kc[None, :] - c[:, None] + NA_COLS - 1, 0, 2 * NA_COLS - 2)
    di = np.arange(kr)
    i = np.arange(kr)
    roff = i[None, :] - di[:, None] + NA_ROWS - 1
    heads = rpb.shape[0]
    cols = jnp.where(jnp.asarray(valid)[None, None], rpb.astype(F32)[:, :, coff], NEG_BIG)
    tab = jnp.take(cols, jnp.asarray(roff.reshape(-1)), axis=1)
    tab = tab.reshape(heads, kr, kr, GRID_W, GRID_W).transpose(0, 1, 3, 2, 4)
    return tab.reshape(heads, kr, GRID_W, kr * GRID_W)


def _na_kernel(q_ref, k_ref, v_ref, bm_ref, o_ref, *, rows, kr):
    W = GRID_W
    heads_per_block = HEAD_LANES // NA_DIM

    head_of_lane = lax.broadcasted_iota(jnp.int32, (W, HEAD_LANES), 1) // NA_DIM

    def body(j, _):
        work = []
        for u in range(NA_STEP_ROWS):
            r = NA_STEP_ROWS * j + u
            rs = jnp.clip(r - kr // 2, 0, rows - kr)
            di = r - rs
            qs = pl.ds(pl.multiple_of(r * W, W), W)
            ks = pl.ds(pl.multiple_of(rs * W, W), kr * W)
            q = q_ref[0, qs, :] * (NA_DIM ** -0.5)
            kw = k_ref[0, ks, :].astype(BF16)
            for hh in range(heads_per_block):
                s = _dot_nt(jnp.where(head_of_lane == hh, q, 0.0).astype(BF16), kw) + bm_ref[hh, di]
                work.append((u, hh, qs, ks, s))
        probs = []
        for u, hh, qs, ks, s in work:
            e = jnp.exp(s - jnp.max(s, axis=-1, keepdims=True))
            probs.append((e.astype(BF16), jnp.sum(e, axis=-1, keepdims=True)))
        outs = {}
        for (u, hh, qs, ks, s), (e, l) in zip(work, probs):
            o = _dot(e, v_ref[0, ks, :].astype(BF16)) / l
            outs[u] = o if hh == 0 else jnp.where(head_of_lane == hh, o, outs[u])
            if hh == heads_per_block - 1:
                o_ref[0, qs, :] = outs[u]
        return 0

    lax.fori_loop(0, rows // NA_STEP_ROWS, body, 0)


def neighbourhood_attention(y, rpb):
    B, T, _ = y.shape
    rows = T // GRID_W
    kr = min(NA_ROWS, rows)
    HB = GROUP_WIDTH // HEAD_LANES
    hpb = HEAD_LANES // NA_DIM
    bm = _na_bias_table(rpb, rows)

    def col(group):
        return pl.BlockSpec((1, T, HEAD_LANES), lambda b, h, group=group: (b, 0, group * HB + h))

    return pl.pallas_call(
        functools.partial(_na_kernel, rows=rows, kr=kr),
        out_shape=jax.ShapeDtypeStruct((B, T, GROUP_WIDTH), F32),
        grid=(B, HB),
        in_specs=[col(4), col(5), col(6),
                  pl.BlockSpec((hpb, kr, GRID_W, kr * GRID_W), lambda b, h: (h, 0, 0, 0))],
        out_specs=pl.BlockSpec((1, T, HEAD_LANES), lambda b, h: (b, 0, h)),
        compiler_params=pltpu.CompilerParams(
            dimension_semantics=("arbitrary", "arbitrary"), vmem_limit_bytes=VMEM_LIMIT),
        name="neighbourhood_attention",
    )(y, y, y, bm)


def _out_kernel(a_ref, b_ref, x_ref, mod_ref, gain_ref, w_ref, r_ref, x1_ref, h2_ref, lg_ref):
    G = a_ref.shape[-1]
    y = _dot(a_ref[0].astype(BF16), w_ref[0:G, :]) + _dot(b_ref[0].astype(BF16), w_ref[G:2 * G, :])
    x1 = x_ref[0] + mod_ref[0, 2:3, :] * y
    x1_ref[0] = x1
    h2 = _rms(x1, gain_ref[...]) * (1.0 + mod_ref[0, 4:5, :]) + mod_ref[0, 3:4, :]
    _store_row_tiles(h2_ref.at[0], h2)
    lg_ref[...] = lax.dot_general(r_ref[...], h2, (((1,), (1,)), ((), ())), precision=HIGHEST,
                                  preferred_element_type=F32)


def out_proj(a_out, b_out, x, mod, gain2, w_out_bf16, router):
    B, T, D = x.shape
    G = a_out.shape[-1]
    E = router.shape[1]
    tm = min(512, T)
    nt = T // tm
    return pl.pallas_call(
        _out_kernel,
        out_shape=[jax.ShapeDtypeStruct((B, T, D), F32),
                   jax.ShapeDtypeStruct((B, T, D // HEAD_LANES, HEAD_LANES), F32),
                   jax.ShapeDtypeStruct((E, B * T), F32)],
        grid=(B, T // tm),
        in_specs=[pl.BlockSpec((1, tm, G), lambda b, i: (b, i, 0)),
                  pl.BlockSpec((1, tm, G), lambda b, i: (b, i, 0)),
                  pl.BlockSpec((1, tm, D), lambda b, i: (b, i, 0)),
                  pl.BlockSpec((1, 6, D), lambda b, i: (b, 0, 0)),
                  pl.BlockSpec((1, D), lambda b, i: (0, 0)),
                  pl.BlockSpec((2 * G, D), lambda b, i: (0, 0)),
                  pl.BlockSpec((E, D), lambda b, i: (0, 0))],
        out_specs=[pl.BlockSpec((1, tm, D), lambda b, i: (b, i, 0)),
                   pl.BlockSpec((1, tm, D // HEAD_LANES, HEAD_LANES), lambda b, i: (b, i, 0, 0)),
                   pl.BlockSpec((E, tm), lambda b, i, nt=nt: (0, b * nt + i))],
        compiler_params=pltpu.CompilerParams(
            dimension_semantics=("arbitrary", "arbitrary"), vmem_limit_bytes=VMEM_LIMIT),
        name="out_proj",
    )(a_out, b_out, x, mod, gain2.reshape(1, D), w_out_bf16, router.T)


def _route_kernel(lg_ref, bias_ref, idx_ref, w_ref, cnt_ref, *, tiles_per_group):
    @pl.when(pl.program_id(0) % tiles_per_group == 0)
    def _():
        cnt_ref[...] = jnp.zeros(cnt_ref.shape, F32)

    scores = _sigmoid(lg_ref[...])
    sel = scores + bias_ref[...]
    E, tm = sel.shape
    per_group = E // N_GROUPS
    neg = -jnp.inf
    eid = lax.broadcasted_iota(jnp.int32, (E, tm), 0).astype(F32)
    eid_g = lax.broadcasted_iota(jnp.int32, (per_group, tm), 0).astype(F32)

    def first_argmax(x, ids, sentinel):
        m = jnp.max(x, axis=0, keepdims=True)
        i = jnp.min(jnp.where(x == m, ids, sentinel), axis=0, keepdims=True)
        return m, i

    parts, gscore = [], []
    for g in range(N_GROUPS):
        x = sel[g * per_group:(g + 1) * per_group]
        parts.append(x)
        m1, i1 = first_argmax(x, eid_g, float(per_group))
        m2 = jnp.max(jnp.where(eid_g == i1, neg, x), axis=0, keepdims=True)
        gscore.append(m1 + m2)
    kept = []
    for g in range(N_GROUPS):
        beaten = jnp.zeros((1, tm), F32)
        for o in range(N_GROUPS):
            if o == g:
                continue
            wins = (gscore[o] >= gscore[g]) if o < g else (gscore[o] > gscore[g])
            beaten = beaten + wins.astype(F32)
        kept.append(jnp.where(beaten < TOPK_GROUPS, parts[g], neg))
    sel = jnp.concatenate(kept, axis=0)

    ids, vals = [], []
    w_sum = jnp.zeros((1, tm), F32)
    chosen = jnp.zeros((E, tm), F32)
    for k in range(TOP_K):
        _, i = first_argmax(sel, eid, float(E))
        hit = eid == i
        val = jnp.sum(jnp.where(hit, scores, 0.0), axis=0, keepdims=True)
        sel = jnp.where(hit, neg, sel)
        chosen = jnp.where(hit, 1.0, chosen)
        ids.append(i)
        vals.append(val)
        w_sum = w_sum + val
    idx_ref[...] = jnp.concatenate(ids, axis=0).astype(jnp.int32)
    w_ref[...] = jnp.concatenate(vals, axis=0) / w_sum * ROUTED_SCALE
    part = chosen[:, 0:HEAD_LANES]
    for l0 in range(HEAD_LANES, tm, HEAD_LANES):
        part = part + chosen[:, l0:l0 + HEAD_LANES]
    cnt_ref[0] = cnt_ref[0] + part


def route(logits_t, router_bias, group_tokens):
    E, N = logits_t.shape
    tm = min(512, N)
    tpg = group_tokens // tm
    idx, w, cnt = pl.pallas_call(
        functools.partial(_route_kernel, tiles_per_group=tpg),
        out_shape=[jax.ShapeDtypeStruct((TOP_K, N), jnp.int32), jax.ShapeDtypeStruct((TOP_K, N), F32),
                   jax.ShapeDtypeStruct((N // group_tokens, E, HEAD_LANES), F32)],
        grid=(N // tm,),
        in_specs=[pl.BlockSpec((E, tm), lambda i: (0, i)), pl.BlockSpec((E, 1), lambda i: (0, 0))],
        out_specs=[pl.BlockSpec((TOP_K, tm), lambda i: (0, i)), pl.BlockSpec((TOP_K, tm), lambda i: (0, i)),
                   pl.BlockSpec((1, E, HEAD_LANES), lambda i, tpg=tpg: (i // tpg, 0, 0))],
        compiler_params=pltpu.CompilerParams(dimension_semantics=("arbitrary",)),
        name="route",
    )(logits_t, router_bias.reshape(E, 1))
    return idx, w, jnp.sum(cnt, axis=-1).astype(jnp.int32)


def _loop_unrolled(n, body):
    u = MOE_LOOP_UNROLL
    lax.fori_loop(0, n // u, functools.partial(body, count=u), 0)
    lax.fori_loop(n // u * u, n, functools.partial(body, count=1), 0)


def _moe_kernel(off_ref, tok_ref, wl_ref, x_ref, wg_ref, wu_ref, wd_ref, acc_ref, xg_ref, yb_ref,
                *, R, E, per_group, experts_per_step):
    g = pl.program_id(0)
    step = pl.program_id(1)

    @pl.when(step == 0)
    def _():
        acc_ref[...] = jnp.zeros(acc_ref.shape, F32)

    @pl.when((g == 0) & (step == 0))
    def _():
        xg_ref[...] = jnp.zeros(xg_ref.shape, F32)

    U = SUBLANES
    chunks = U
    lanes = x_ref.shape[-1]

    def token_tile(ref, t8):
        return ref.at[0, pl.ds(pl.multiple_of(t8, U), U), :]

    def gather(le, s0, nr):
        def body(j, _, count):
            for gg in range(count):
                base = s0 + (j * count + gg) * U
                for i in range(U):
                    xg_ref[le, j * count + gg, pl.ds(i, chunks, stride=U), :] = (
                        token_tile(x_ref, tok_ref[base + i])[...])
            return 0

        whole = jnp.minimum((nr + U - 1) // U, (per_group - s0) // U)
        _loop_unrolled(whole, body)

        def single(r, _):
            xg_ref[le, r // U, pl.ds(r % U, chunks, stride=U), :] = token_tile(x_ref, tok_ref[s0 + r])[...]
            return 0

        lax.fori_loop(whole * U, nr, single, 0)

    def load_block(le):
        return jnp.concatenate([xg_ref[le, :, c * U:(c + 1) * U, :].reshape(R, lanes) for c in range(chunks)],
                               axis=1).astype(BF16)

    def store_block(le, y):
        for c in range(chunks):
            yb_ref[le, :, c * U:(c + 1) * U, :] = y[:, c * lanes:(c + 1) * lanes].reshape(R // U, U, lanes)

    def scatter(le, s0, nr):
        def group(j, _, count):
            base = s0 + j * count * U
            rows = range(count * U)
            tiles = [token_tile(acc_ref, tok_ref[base + i]) for i in rows]
            wts = [wl_ref[base + i] for i in rows]
            new = [tiles[i][...] + wts[i] * yb_ref[le, j * count + i // U, pl.ds(i % U, chunks, stride=U), :]
                   for i in rows]
            for i in rows:
                tiles[i][...] = new[i]
            return 0

        groups = nr // U
        _loop_unrolled(groups, group)

        def single(r, _):
            tile = token_tile(acc_ref, tok_ref[s0 + r])
            row = yb_ref[le, r // U, pl.ds(r % U, chunks, stride=U), :]
            tile[...] = tile[...] + wl_ref[s0 + r] * row
            return 0

        lax.fori_loop(groups * U, nr, single, 0)

    segments = []
    for le in range(experts_per_step):
        segment = g * E + step * experts_per_step + le
        seg = off_ref[segment]
        segments.append((seg - g * per_group, off_ref[segment + 1] - seg))

    for le, (start, cnt) in enumerate(segments):
        gather(le, start, jnp.minimum(R, cnt))
    xs = [load_block(le) for le in range(experts_per_step)]
    gu = [(_dot(xs[le], wg_ref[le].astype(BF16)), _dot(xs[le], wu_ref[le].astype(BF16)))
          for le in range(experts_per_step)]
    hs = [(_silu(gate) * up).astype(BF16) for gate, up in gu]
    for le in range(experts_per_step):
        store_block(le, _dot(hs[le], wd_ref[le].astype(BF16)))
    for le, (start, cnt) in enumerate(segments):
        scatter(le, start, jnp.minimum(R, cnt))

    for le, (start, cnt) in enumerate(segments):
        def more(sb, _, le=le, start=start, cnt=cnt):
            s0 = start + sb * R
            nr = jnp.minimum(R, cnt - sb * R)
            gather(le, s0, nr)
            x = load_block(le)
            hmid = _silu(_dot(x, wg_ref[le].astype(BF16))) * _dot(x, wu_ref[le].astype(BF16))
            store_block(le, _dot(hmid.astype(BF16), wd_ref[le].astype(BF16)))
            scatter(le, s0, nr)
            return 0

        lax.fori_loop(1, (cnt + R - 1) // R, more, 0)


def routed_experts(h2, top_idx, top_w, counts, exp_gate, exp_up, exp_down):
    N, chunks, lanes = h2.shape
    D = chunks * lanes
    E, _, F = exp_gate.shape
    TG = min(MOE_TOKEN_GROUP, N)
    G = N // TG
    per_group = TG * TOP_K
    key = (jnp.arange(N, dtype=jnp.int32)[None, :] // TG) * E + top_idx
    order = jnp.argsort(key.reshape(-1))
    assert chunks == SUBLANES
    tok_s = (((order % N) % TG) * chunks).astype(jnp.int32)
    w_s = top_w.reshape(-1)[order]
    off = jnp.concatenate([jnp.zeros((1,), jnp.int32), jnp.cumsum(counts.reshape(-1)).astype(jnp.int32)])

    EPS = MOE_EXPERTS_PER_STEP
    grid_spec = pltpu.PrefetchScalarGridSpec(
        num_scalar_prefetch=1,
        grid=(G, E // EPS),
        in_specs=[pl.BlockSpec((per_group,), lambda g, e, off: (g,), memory_space=pltpu.SMEM),
                  pl.BlockSpec((per_group,), lambda g, e, off: (g,), memory_space=pltpu.SMEM),
                  pl.BlockSpec((1, TG * chunks, lanes), lambda g, e, off: (g, 0, 0),
                               pipeline_mode=pl.Buffered(1)),
                  pl.BlockSpec((EPS, D, F), lambda g, e, off: (e, 0, 0)),
                  pl.BlockSpec((EPS, D, F), lambda g, e, off: (e, 0, 0)),
                  pl.BlockSpec((EPS, F, D), lambda g, e, off: (e, 0, 0))],
        out_specs=pl.BlockSpec((1, TG * chunks, lanes), lambda g, e, off: (g, 0, 0),
                               pipeline_mode=pl.Buffered(1)),
        scratch_shapes=[pltpu.VMEM((EPS, MOE_ROWS // SUBLANES, chunks * SUBLANES, lanes), F32),
                        pltpu.VMEM((EPS, MOE_ROWS // SUBLANES, chunks * SUBLANES, lanes), F32)],
    )
    out = pl.pallas_call(
        functools.partial(_moe_kernel, R=MOE_ROWS, E=E, per_group=per_group, experts_per_step=EPS),
        out_shape=jax.ShapeDtypeStruct((G, TG * chunks, lanes), F32),
        grid_spec=grid_spec,
        compiler_params=pltpu.CompilerParams(
            dimension_semantics=("arbitrary", "arbitrary"), vmem_limit_bytes=MOE_VMEM_LIMIT),
        name="routed_experts",
    )(off, tok_s, w_s, h2.reshape(G, TG * chunks, lanes), exp_gate, exp_up, exp_down)
    return out.reshape(N, chunks, lanes)


def _final_kernel(x1_ref, h2_ref, rt_ref, mod_ref, wg_ref, wu_ref, wd_ref, *rest, final):
    if final:
        fg_ref, o_ref = rest
    else:
        (o_ref,) = rest
    hb = _load_row_tiles(h2_ref.at[0]).astype(BF16)
    hmid = _silu(_dot(hb, wg_ref[...])) * _dot(hb, wu_ref[...])
    shared = _dot(hmid.astype(BF16), wd_ref[...])
    x2 = x1_ref[0] + mod_ref[0, 5:6, :] * (_load_row_tiles(rt_ref.at[0]) + shared)
    if final:
        x2 = _rms(x2, fg_ref[...])
    o_ref[0] = x2


def shared_and_residual(x1, h2, routed, mod, sh_gate, sh_up, sh_down, final_gain=None):
    B, T, D = x1.shape
    F = sh_gate.shape[1]
    tm = min(512, T)
    final = final_gain is not None
    tile = pl.BlockSpec((1, tm, D), lambda b, i: (b, i, 0))
    row_tiles = pl.BlockSpec((1, tm, D // HEAD_LANES, HEAD_LANES), lambda b, i: (b, i, 0, 0))
    in_specs = [tile, row_tiles, row_tiles,
                pl.BlockSpec((1, 6, D), lambda b, i: (b, 0, 0)),
                pl.BlockSpec((D, F), lambda b, i: (0, 0)),
                pl.BlockSpec((D, F), lambda b, i: (0, 0)),
                pl.BlockSpec((F, D), lambda b, i: (0, 0))]
    args = [x1, h2, routed, mod, sh_gate, sh_up, sh_down]
    if final:
        in_specs.append(pl.BlockSpec((1, D), lambda b, i: (0, 0)))
        args.append(final_gain.reshape(1, D))
    return pl.pallas_call(
        functools.partial(_final_kernel, final=final),
        out_shape=jax.ShapeDtypeStruct((B, T, D), F32),
        grid=(B, T // tm),
        in_specs=in_specs,
        out_specs=tile,
        compiler_params=pltpu.CompilerParams(
            dimension_semantics=("arbitrary", "arbitrary"), vmem_limit_bytes=VMEM_LIMIT),
        name="shared_and_residual",
    )(*args)


def moe_block(x1, h2, logits, mod, router_bias, exp_gate, exp_up, exp_down, sh_gate, sh_up, sh_down,
              final_gain=None):
    B, T, D = x1.shape
    top_idx, top_w, counts = route(logits, router_bias, min(MOE_TOKEN_GROUP, B * T))
    chunks = D // HEAD_LANES
    routed = routed_experts(h2.reshape(B * T, chunks, HEAD_LANES), top_idx, top_w, counts,
                            exp_gate, exp_up, exp_down)
    return shared_and_residual(x1, h2, routed.reshape(B, T, chunks, HEAD_LANES), mod,
                               sh_gate.astype(BF16), sh_up.astype(BF16), sh_down.astype(BF16), final_gain)


def kernel(x, c, hgrn_lb_logits, l0_norm1, l0_norm2, l0_w_mod, l0_b_mod, l0_w_in, l0_w_out, l0_hgrn_norm, l0_diff_lq1, l0_diff_lk1, l0_diff_lq2, l0_diff_lk2, l0_diff_subln, l0_router, l0_router_bias, l0_exp_gate, l0_exp_up, l0_exp_down, l0_sh_gate, l0_sh_up, l0_sh_down, l1_norm1, l1_norm2, l1_w_mod, l1_b_mod, l1_w_in, l1_w_out, l1_conv_w, l1_conv_b, l1_ml_i_bias, l1_ml_f_bias, l1_ml_norm, l1_na_rpb, l1_router, l1_router_bias, l1_exp_gate, l1_exp_up, l1_exp_down, l1_sh_gate, l1_sh_up, l1_sh_down, final_norm):
    G = GROUP_WIDTH
    lb_all = jnp.cumsum(jax.nn.softmax(hgrn_lb_logits.astype(F32), axis=0), axis=0)
    layer_idx = 0
    lambda_init = 0.8 - 0.6 * math.exp(-0.3 * layer_idx)
    lam = (jnp.exp(jnp.sum(l0_diff_lq1.astype(F32) * l0_diff_lk1.astype(F32)))
           - jnp.exp(jnp.sum(l0_diff_lq2.astype(F32) * l0_diff_lk2.astype(F32))) + lambda_init)

    mod0 = ada_mod(c, l0_w_mod, l0_b_mod)
    y0 = in_proj(x, mod0, l0_norm1, l0_w_in.astype(BF16))
    a_out = hgrn2(y0, lb_all[0], l0_hgrn_norm)
    b_out = diff_attention(y0, lam, l0_diff_subln, layer_idx)
    x1, h2, logits = out_proj(a_out, b_out, x, mod0, l0_norm2, l0_w_out.astype(BF16), l0_router)
    xa = moe_block(x1, h2, logits, mod0, l0_router_bias, l0_exp_gate, l0_exp_up, l0_exp_down,
                   l0_sh_gate, l0_sh_up, l0_sh_down)

    mod1 = ada_mod(c, l1_w_mod, l1_b_mod)
    n_gate = l1_w_in.shape[1] - 7 * G
    w_main = jnp.concatenate([l1_w_in[:, :4 * G], l1_w_in[:, 4 * G + n_gate:]], axis=1).astype(BF16)
    w_gate = jnp.pad(l1_w_in[:, 4 * G:4 * G + n_gate], ((0, 0), (0, HEAD_LANES - n_gate)))
    y1, gates = in_proj(xa, mod1, l1_norm1, w_main, w_gate)
    c_out = mlstm(y1, gates, l1_conv_w, l1_conv_b, l1_ml_i_bias, l1_ml_f_bias, l1_ml_norm)
    d_out = neighbourhood_attention(y1, l1_na_rpb)
    x1, h2, logits = out_proj(c_out, d_out, xa, mod1, l1_norm2, l1_w_out.astype(BF16), l1_router)
    return moe_block(x1, h2, logits, mod1, l1_router_bias, l1_exp_gate, l1_exp_up, l1_exp_down,
                     l1_sh_gate, l1_sh_up, l1_sh_down, final_gain=final_norm)
```

```python
import functools
import math

import numpy as np
import jax
import jax.numpy as jnp
from jax import lax
from jax.experimental import pallas as pl
from jax.experimental.pallas import tpu as pltpu

F32 = jnp.float32
BF16 = jnp.bfloat16
HIGHEST = lax.Precision.HIGHEST
EPS = 1e-6

GRID_W = 64
GROUP_WIDTH = 512
HEAD_LANES = 128
HG_CHUNK = 64
ML_CHUNK = 128
STEP_CHUNKS = 2
ML_CONV = 5
DA_DIM = 64
ROPE_DIM = 16
ROPE_THETA = 500000.0
NA_ROWS = 8
NA_COLS = 16
NA_DIM = 64
NA_STEP_ROWS = 4
N_EXPERTS = 128
TOP_K = 8
N_GROUPS = 8
TOPK_GROUPS = 4
ROUTED_SCALE = 2.5
MOE_ROWS = 320
MOE_TOKEN_GROUP = 4096
MOE_EXPERTS_PER_STEP = 2
MOE_LOOP_UNROLL = 2
SUBLANES = 8
NEG_BIG = -1e30
VMEM_LIMIT = 48 * 1024 * 1024
MOE_VMEM_LIMIT = 56 * 1024 * 1024


def _dot(a, b, **kw):
    return jnp.dot(a, b, preferred_element_type=F32, **kw)


def _dot_nt(a, b):
    return lax.dot_general(a, b, (((1,), (1,)), ((), ())), preferred_element_type=F32)


def _dot_tn(a, b):
    return lax.dot_general(a, b, (((0,), (0,)), ((), ())), preferred_element_type=F32)


def _sigmoid(x):
    return jax.nn.sigmoid(x)


def _silu(x):
    return x * jax.nn.sigmoid(x)


def _log_sigmoid(x):
    return jnp.minimum(x, 0.0) - jnp.log(1.0 + jnp.exp(-jnp.abs(x)))


def _rms(x, gain):
    return x * lax.rsqrt(jnp.mean(x * x, axis=-1, keepdims=True) + EPS) * gain


def _store_row_tiles(ref, val):
    for c in range(ref.shape[1]):
        ref[:, c, :] = val[:, c * HEAD_LANES:(c + 1) * HEAD_LANES]


def _load_row_tiles(ref):
    return jnp.concatenate([ref[:, c, :] for c in range(ref.shape[1])], axis=1)


def _mod_kernel(c_ref, w_ref, b_ref, o_ref):
    o_ref[...] = _dot(_silu(c_ref[...]), w_ref[...], precision=HIGHEST) + b_ref[...]


def ada_mod(c, w_mod, b_mod):
    B, D = c.shape
    N = w_mod.shape[1]
    tn = 1024
    out = pl.pallas_call(
        _mod_kernel,
        out_shape=jax.ShapeDtypeStruct((B, N), F32),
        grid=(N // tn,),
        in_specs=[pl.BlockSpec((B, D), lambda j: (0, 0)),
                  pl.BlockSpec((D, tn), lambda j: (0, j)),
                  pl.BlockSpec((1, tn), lambda j: (0, j))],
        out_specs=pl.BlockSpec((B, tn), lambda j: (0, j)),
        name="ada_mod",
    )(c, w_mod, b_mod.reshape(1, N))
    return out.reshape(B, 6, D)


def _in_kernel(x_ref, mod_ref, gain_ref, w_ref, *rest, has_gate):
    if has_gate:
        wg_ref, o_ref, og_ref, h_ref = rest
    else:
        o_ref, h_ref = rest

    @pl.when(pl.program_id(2) == 0)
    def _():
        h = _rms(x_ref[0], gain_ref[...]) * (1.0 + mod_ref[0, 1:2, :]) + mod_ref[0, 0:1, :]
        h_ref[...] = h.astype(BF16)
        if has_gate:
            og_ref[0] = _dot(h, wg_ref[...], precision=HIGHEST)

    o_ref[0] = _dot(h_ref[...], w_ref[...]).astype(o_ref.dtype)


def in_proj(x, mod, gain, w_bf16, w_gate=None):
    B, T, D = x.shape
    N = w_bf16.shape[1]
    tm = min(1024, T)
    tn = 512
    has_gate = w_gate is not None
    in_specs = [pl.BlockSpec((1, tm, D), lambda b, i, j: (b, i, 0)),
                pl.BlockSpec((1, 6, D), lambda b, i, j: (b, 0, 0)),
                pl.BlockSpec((1, D), lambda b, i, j: (0, 0)),
                pl.BlockSpec((D, tn), lambda b, i, j: (0, j))]
    out_shape = [jax.ShapeDtypeStruct((B, T, N), BF16)]
    out_specs = [pl.BlockSpec((1, tm, tn), lambda b, i, j: (b, i, j))]
    args = [x, mod, gain.reshape(1, D), w_bf16]
    if has_gate:
        in_specs.append(pl.BlockSpec((D, HEAD_LANES), lambda b, i, j: (0, 0)))
        out_shape.append(jax.ShapeDtypeStruct((B, T, HEAD_LANES), F32))
        out_specs.append(pl.BlockSpec((1, tm, HEAD_LANES), lambda b, i, j: (b, i, 0)))
        args.append(w_gate)
    res = pl.pallas_call(
        functools.partial(_in_kernel, has_gate=has_gate),
        out_shape=out_shape,
        grid=(B, T // tm, N // tn),
        in_specs=in_specs,
        out_specs=out_specs,
        scratch_shapes=[pltpu.VMEM((tm, D), BF16)],
        compiler_params=pltpu.CompilerParams(
            dimension_semantics=("arbitrary", "arbitrary", "arbitrary")),
        name="in_proj",
    )(*args)
    return res if has_gate else res[0]


def _hgrn_consts(C):
    t = np.arange(C)
    tri = (t[None, :] <= t[:, None]).astype(np.float32)
    triT = np.ascontiguousarray(tri.T)
    wf, wb, mf = [tri], [triT], []
    levels = int(round(math.log2(C)))
    for l in range(levels):
        size = C >> l
        blk = t // size
        r = blk * size + size // 2
        wf.append(tri - tri[r - 1])
        wb.append(triT - triT[r])
        upper = (t % size) >= size // 2
        mf.append(((blk[:, None] == blk[None, :]) & upper[:, None] & (~upper)[None, :]).astype(np.float32))
    mf.append(np.eye(C, dtype=np.float32))
    ones = np.ones((8, C), np.float32)
    wf.append(ones)
    wb.append(ones)
    mf = np.stack(mf)
    mb = np.ascontiguousarray(np.transpose(mf, (0, 2, 1)))
    return np.concatenate(wf), np.concatenate(wb), mf, mb


def _split2(x):
    hi = x.astype(BF16)
    lo = (x - hi.astype(F32)).astype(BF16)
    return hi, lo


def _hgrn_kernel(q_ref, i_ref, ff_ref, fb_ref, g_ref, lb_ref, gain_ref, wf_ref, wb_ref, mf_ref, mb_ref,
                 o_ref, of_ref, ob_ref, *, C, T):
    n = T // C
    levels = int(round(math.log2(C)))
    dv = q_ref.shape[-1]

    def prepare(c, f_ref, lbd, w_ref):
        sl = pl.ds(pl.multiple_of(c * C, C), C)
        q = _silu(q_ref[0, sl, :].astype(F32))
        v = i_ref[0, sl, :]
        fg = lbd + (1.0 - lbd) * _sigmoid(f_ref[0, sl, :].astype(F32))
        lf = jnp.log(fg)
        d2 = _dot(w_ref[...], jnp.concatenate(_split2(lf), axis=1))
        dall = d2[:, 0:dv] + d2[:, dv:2 * dv]
        return dict(sl=sl, q=q, k=1.0 - fg, v=v.astype(BF16), dall=dall)

    def scores(p, m_ref):
        q, k, dall = p["q"], p["k"], p["dall"]
        attn = m_ref[levels] * _dot_nt(q.astype(BF16), k.astype(BF16))
        for l in range(levels):
            e = jnp.exp(-jnp.abs(dall[(l + 1) * C:(l + 2) * C]))
            attn = attn + m_ref[l] * _dot_nt((q * e).astype(BF16), (k * e).astype(BF16))
        return attn.astype(BF16)

    st0 = jnp.zeros((dv, dv), F32)

    def step(j, carry):
        states = list(carry)
        work = []
        for u in range(STEP_CHUNKS):
            work.append((0, prepare(STEP_CHUNKS * j + u, ff_ref, lb_ref[0:1, :], wf_ref), mf_ref, of_ref))
            work.append((1, prepare(n - 1 - STEP_CHUNKS * j - u, fb_ref, lb_ref[1:2, :], wb_ref), mb_ref, ob_ref))
        attn = [scores(p, m_ref) for _, p, m_ref, _ in work]
        local = []
        for (_, p, _, _), a in zip(work, attn):
            cum = p["dall"][0:C]
            tot = p["dall"][(levels + 1) * C:(levels + 1) * C + 1]
            kt = p["k"] * jnp.exp(tot - cum)
            local.append((_dot(a, p["v"]), _dot_tn(p["v"], kt.astype(BF16)), cum, tot))
        for (d, p, _, out_ref), (o_in, incr, cum, tot) in zip(work, local):
            st = states[d]
            out_ref[p["sl"], :] = o_in + _dot_nt((p["q"] * jnp.exp(cum)).astype(BF16), st.astype(BF16))
            states[d] = st * jnp.exp(tot) + incr
        return tuple(states)

    lax.fori_loop(0, n // STEP_CHUNKS, step, (st0, st0))

    rt = min(512, T)
    for r0 in range(0, T, rt):
        sl = slice(r0, r0 + rt)
        o_ref[0, sl, :] = _rms(of_ref[sl, :] + ob_ref[sl, :], gain_ref[...]) * _silu(g_ref[0, sl, :].astype(F32))


def hgrn2(y, lb, norm_gain):
    B, T, _ = y.shape
    H = GROUP_WIDTH // HEAD_LANES
    C = HG_CHUNK
    wf, wb, mf, mb = _hgrn_consts(C)
    wf, wb = jnp.asarray(wf, BF16), jnp.asarray(wb, BF16)
    mf, mb = jnp.asarray(mf), jnp.asarray(mb)

    def col(group):
        return pl.BlockSpec((1, T, HEAD_LANES), lambda b, h, group=group: (b, 0, group * H + h))

    def const(a):
        nd = a.ndim
        return pl.BlockSpec(a.shape, lambda b, h, nd=nd: (0,) * nd)

    return pl.pallas_call(
        functools.partial(_hgrn_kernel, C=C, T=T),
        out_shape=jax.ShapeDtypeStruct((B, T, GROUP_WIDTH), F32),
        grid=(B, H),
        in_specs=[col(0), col(1), col(2), col(3), col(4),
                  pl.BlockSpec((2, HEAD_LANES), lambda b, h: (0, h)),
                  pl.BlockSpec((1, HEAD_LANES), lambda b, h: (0, 0)),
                  const(wf), const(wb), const(mf), const(mb)],
        out_specs=pl.BlockSpec((1, T, HEAD_LANES), lambda b, h: (b, 0, h)),
        scratch_shapes=[pltpu.VMEM((T, HEAD_LANES), F32), pltpu.VMEM((T, HEAD_LANES), F32)],
        compiler_params=pltpu.CompilerParams(
            dimension_semantics=("arbitrary", "arbitrary"), vmem_limit_bytes=VMEM_LIMIT),
        name="hgrn2",
    )(y, y, y, y, y, lb, norm_gain.reshape(1, HEAD_LANES), wf, wb, mf, mb)


def _rope_tables(T):
    pos = np.arange(T, dtype=np.float32)
    inv_freq = (ROPE_THETA ** (-np.arange(0, ROPE_DIM, 2, dtype=np.float32) / ROPE_DIM)).astype(np.float32)
    ang = pos[:, None] * inv_freq[None, :]
    cos, sin = np.cos(ang), np.sin(ang)
    half = ROPE_DIM // 2
    c = np.ones((T, HEAD_LANES), np.float32)
    s_prev = np.zeros((T, HEAD_LANES), np.float32)
    s_next = np.zeros((T, HEAD_LANES), np.float32)
    for base in range(0, HEAD_LANES, DA_DIM):
        c[:, base:base + half] = cos
        c[:, base + half:base + ROPE_DIM] = cos
        s_next[:, base:base + half] = -sin
        s_prev[:, base + half:base + ROPE_DIM] = sin
    return jnp.asarray(c), jnp.asarray(s_prev), jnp.asarray(s_next)


def _rope(x, c, s_prev, s_next):
    half = ROPE_DIM // 2
    lanes = x.shape[-1]
    return (x * c + pltpu.roll(x, half, axis=1) * s_prev
            + pltpu.roll(x, lanes - half, axis=1) * s_next)


def _diff_kernel(lam_ref, q_ref, k_ref, v_ref, cq_ref, spq_ref, snq_ref, ck_ref, spk_ref, snk_ref,
                 subln_ref, o_ref, kr_ref, vb_ref, *, T, out_scale):
    rt = min(512, T)

    @pl.when(pl.program_id(2) == 0)
    def _():
        for r0 in range(0, T, rt):
            sl = slice(r0, r0 + rt)
            kr_ref[sl, :] = _rope(k_ref[0, sl, :].astype(F32), ck_ref[sl, :], spk_ref[sl, :], snk_ref[sl, :]).astype(BF16)
            vb_ref[sl, 0:HEAD_LANES] = v_ref[0, sl, :].astype(BF16)
            lane = lax.broadcasted_iota(jnp.int32, (rt, HEAD_LANES), 1)
            vb_ref[sl, HEAD_LANES:2 * HEAD_LANES] = jnp.where(lane == 0, 1.0, 0.0).astype(BF16)

    q = _rope(q_ref[0].astype(F32), cq_ref[...], spq_ref[...], snq_ref[...]) * (DA_DIM ** -0.5 * math.log2(math.e))
    lam = lam_ref[0, 0]
    v = vb_ref[...]
    map_of_lane = lax.broadcasted_iota(jnp.int32, q.shape, 1) // DA_DIM

    scores = [_dot_nt(jnp.where(map_of_lane == m, q, 0.0).astype(BF16), kr_ref[...]) for m in range(2)]
    probs = [jnp.exp2(s - jnp.max(s, axis=-1, keepdims=True)).astype(BF16) for s in scores]
    pv = []
    for e in probs:
        full = _dot(e, v)
        pv.append(full[:, 0:HEAD_LANES] / full[:, HEAD_LANES:HEAD_LANES + 1])
    o = pv[0] - lam * pv[1]
    o_ref[0] = _rms(o, subln_ref[...]) * out_scale


def diff_attention(y, lam, subln, layer_idx):
    B, T, _ = y.shape
    H = GROUP_WIDTH // HEAD_LANES
    tq = min(256, T)
    lambda_init = 0.8 - 0.6 * math.exp(-0.3 * layer_idx)
    c, sp, sn = _rope_tables(T)

    def col(group, rows):
        if rows == T:
            return pl.BlockSpec((1, T, HEAD_LANES), lambda b, h, i, group=group: (b, 0, group * H + h))
        return pl.BlockSpec((1, rows, HEAD_LANES), lambda b, h, i, group=group: (b, i, group * H + h))

    tab_q = pl.BlockSpec((tq, HEAD_LANES), lambda b, h, i: (i, 0))
    tab_k = pl.BlockSpec((T, HEAD_LANES), lambda b, h, i: (0, 0))
    return pl.pallas_call(
        functools.partial(_diff_kernel, T=T, out_scale=1.0 - lambda_init),
        out_shape=jax.ShapeDtypeStruct((B, T, GROUP_WIDTH), F32),
        grid=(B, H, T // tq),
        in_specs=[pl.BlockSpec(memory_space=pltpu.SMEM),
                  col(5, tq), col(6, T), col(7, T),
                  tab_q, tab_q, tab_q, tab_k, tab_k, tab_k,
                  pl.BlockSpec((1, HEAD_LANES), lambda b, h, i: (0, 0))],
        out_specs=pl.BlockSpec((1, tq, HEAD_LANES), lambda b, h, i: (b, i, h)),
        scratch_shapes=[pltpu.VMEM((T, HEAD_LANES), BF16), pltpu.VMEM((T, 2 * HEAD_LANES), BF16)],
        compiler_params=pltpu.CompilerParams(
            dimension_semantics=("arbitrary", "arbitrary", "arbitrary"), vmem_limit_bytes=VMEM_LIMIT),
        name="diff_attention",
    )(lam.reshape(1, 1), y, y, y, c, sp, sn, c, sp, sn, subln.reshape(1, HEAD_LANES))


def _mlstm_kernel(q_ref, k_ref, v_ref, og_ref, gt_ref, cwq_ref, cwk_ref, cbq_ref, cbk_ref, gbias_ref,
                  gain_ref, tri_ref, o_ref, qc_ref, kc_ref, xp_ref, gx_ref, hf_ref, hb_ref, gxt_ref, va_ref,
                  vta_ref, *, C, T, dk):
    n = T // C
    head = pl.program_id(1)
    pad = 8
    half = ML_CONV // 2
    rt = min(512, T)

    xp_ref[0:pad, :] = jnp.zeros((pad, dk), F32)
    xp_ref[pad + T:pad + T + pad, :] = jnp.zeros((pad, dk), F32)
    for src, cw_ref, cb_ref, dst, scale in ((q_ref, cwq_ref, cbq_ref, qc_ref, 1.0),
                                            (k_ref, cwk_ref, cbk_ref, kc_ref, dk ** -0.5)):
        xp_ref[pad:pad + T, :] = src[0].astype(F32)
        for r0 in range(0, T, rt):
            acc = jnp.zeros((rt, dk), F32) + cb_ref[...]
            for j in range(ML_CONV):
                acc = acc + xp_ref[pad + r0 + j - half:pad + r0 + j - half + rt, :] * cw_ref[j:j + 1, :]
            dst[r0:r0 + rt, :] = _silu(acc) * scale

    lane = lax.broadcasted_iota(jnp.int32, (rt, HEAD_LANES), 1)
    is_f = (lane % 8) >= 4
    for r0 in range(0, T, rt):
        g = gt_ref[0, r0:r0 + rt, :] + gbias_ref[...]
        p = jnp.where(is_f, _log_sigmoid(g), g)
        x = jnp.zeros((rt, HEAD_LANES), F32)
        for j, src_lane in enumerate((0, 4, 8, 12)):
            colv = jnp.sum(jnp.where(lane == src_lane + head, p, 0.0), axis=1, keepdims=True)
            x = jnp.where(lane == j, colv, x)
        gx_ref[r0:r0 + rt, :] = x
        gxt_ref[:, r0:r0 + rt] = x.T[0:SUBLANES, :]
        v = v_ref[0, r0:r0 + rt, :].astype(F32)
        va_ref[r0:r0 + rt, 0:dk] = v.astype(BF16)
        va_ref[r0:r0 + rt, dk:2 * dk] = jnp.where(lane == 0, 1.0, 0.0).astype(BF16)
        vta_ref[0:dk, r0:r0 + rt] = v.T.astype(BF16)
        sub = lax.broadcasted_iota(jnp.int32, (dk, rt), 0)
        vta_ref[dk:2 * dk, r0:r0 + rt] = jnp.where(sub == 0, 1.0, 0.0).astype(BF16)

    row = lax.broadcasted_iota(jnp.int32, (C, C), 0)
    colm = lax.broadcasted_iota(jnp.int32, (C, C), 1)

    init = (jnp.zeros((2 * dk, dk), F32), jnp.full((1, 1), NEG_BIG, F32))

    def step(j, carry):
        carries = list(carry)
        work = []
        for u in range(STEP_CHUNKS):
            work.append((0, STEP_CHUNKS * j + u, hf_ref))
            work.append((1, n - 1 - STEP_CHUNKS * j - u, hb_ref))
        chunks = []
        for d, c, out_ref in work:
            sl = pl.ds(pl.multiple_of(c * C, C), C)
            x = gx_ref[sl, :]
            xr = gxt_ref[:, sl]
            chunks.append(dict(d=d, sl=sl, out=out_ref, x=x, xr=xr, q=qc_ref[sl, :], k=kc_ref[sl, :],
                               cumx=_dot(tri_ref[d], x, precision=HIGHEST),
                               cumr=_dot(xr, tri_ref[1 - d], precision=HIGHEST)))
        for p in chunks:
            p["qk"] = _dot_nt(p["q"].astype(BF16), p["k"].astype(BF16))
        for p in chunks:
            d, x, xr, cumx, cumr = p["d"], p["x"], p["xr"], p["cumx"], p["cumr"]
            mask = (colm <= row) if d == 0 else (colm >= row)
            ig_c = x[:, 2 * d:2 * d + 1]
            ig_r = xr[2 * d:2 * d + 1, :]
            cum_c = cumx[:, 2 * d + 1:2 * d + 2]
            cum_r = cumr[2 * d + 1:2 * d + 2, :]
            tot = jnp.sum(x[:, 2 * d + 1:2 * d + 2], axis=0, keepdims=True)
            dmat = jnp.where(mask, cum_c - cum_r + ig_r, -jnp.inf)
            dmax = jnp.max(dmat, axis=1, keepdims=True)
            a = p["qk"] * jnp.exp(dmat - dmax)
            ds = tot - cum_c + ig_c
            dsmax = jnp.max(ds, axis=0, keepdims=True)
            kw = p["k"] * jnp.exp(ds - dsmax)
            p.update(cum_c=cum_c, tot=tot, dmax=dmax, dsmax=dsmax, a=a.astype(BF16), kw=kw.astype(BF16))
        for p in chunks:
            p["num"] = _dot(p["a"], va_ref[p["sl"], :])
            p["upd"] = _dot(vta_ref[:, p["sl"]], p["kw"])
        for p in chunks:
            state, m = carries[p["d"]]
            g = p["cum_c"] + m
            mt = jnp.maximum(g, p["dmax"])
            full = (jnp.exp(g - mt) * _dot_nt(p["q"].astype(BF16), state.astype(BF16))
                    + jnp.exp(p["dmax"] - mt) * p["num"])
            den = full[:, dk:dk + 1]
            p["out"][p["sl"], :] = full[:, 0:dk] / jnp.maximum(jnp.abs(den), jnp.exp(-mt))
            m_new = jnp.maximum(p["tot"] + m, p["dsmax"])
            decay = jnp.exp(p["tot"] + m - m_new)
            scale = jnp.exp(p["dsmax"] - m_new)
            carries[p["d"]] = (decay * state + scale * p["upd"], m_new)
        return tuple(carries)

    lax.fori_loop(0, n // STEP_CHUNKS, step, (init, init))

    for r0 in range(0, T, rt):
        sl = slice(r0, r0 + rt)
        o_ref[0, sl, :] = _rms(hf_ref[sl, :] + hb_ref[sl, :], gain_ref[...]) * _sigmoid(og_ref[0, sl, :].astype(F32))


def mlstm(y, gates, conv_w, conv_b, i_bias, f_bias, norm_gain):
    B, T, _ = y.shape
    H = GROUP_WIDTH // HEAD_LANES
    C = min(ML_CHUNK, T)
    t = np.arange(C)
    tri = np.stack([(t[None, :] <= t[:, None]), (t[None, :] >= t[:, None])]).astype(np.float32)
    gbias = jnp.zeros((1, HEAD_LANES), F32)
    gbias = gbias.at[0, 0:4].set(i_bias[0]).at[0, 4:8].set(f_bias[0])
    gbias = gbias.at[0, 8:12].set(i_bias[1]).at[0, 12:16].set(f_bias[1])

    def col(group):
        return pl.BlockSpec((1, T, HEAD_LANES), lambda b, h, group=group: (b, 0, group * H + h))

    conv_q = pl.BlockSpec((ML_CONV, HEAD_LANES), lambda b, h: (0, h))
    conv_k = pl.BlockSpec((ML_CONV, HEAD_LANES), lambda b, h: (0, H + h))
    bias_q = pl.BlockSpec((1, HEAD_LANES), lambda b, h: (0, h))
    bias_k = pl.BlockSpec((1, HEAD_LANES), lambda b, h: (0, H + h))
    cb = conv_b.reshape(1, -1)
    return pl.pallas_call(
        functools.partial(_mlstm_kernel, C=C, T=T, dk=HEAD_LANES),
        out_shape=jax.ShapeDtypeStruct((B, T, GROUP_WIDTH), F32),
        grid=(B, H),
        in_specs=[col(0), col(1), col(2), col(3),
                  pl.BlockSpec((1, T, HEAD_LANES), lambda b, h: (b, 0, 0)),
                  conv_q, conv_k, bias_q, bias_k,
                  pl.BlockSpec((1, HEAD_LANES), lambda b, h: (0, 0)),
                  pl.BlockSpec((1, HEAD_LANES), lambda b, h: (0, 0)),
                  pl.BlockSpec((2, C, C), lambda b, h: (0, 0, 0))],
        out_specs=pl.BlockSpec((1, T, HEAD_LANES), lambda b, h: (b, 0, h)),
        scratch_shapes=[pltpu.VMEM((T, HEAD_LANES), F32), pltpu.VMEM((T, HEAD_LANES), F32),
                        pltpu.VMEM((T + 16, HEAD_LANES), F32), pltpu.VMEM((T, HEAD_LANES), F32),
                        pltpu.VMEM((T, HEAD_LANES), F32), pltpu.VMEM((T, HEAD_LANES), F32),
                        pltpu.VMEM((SUBLANES, T), F32), pltpu.VMEM((T, 2 * HEAD_LANES), BF16),
                        pltpu.VMEM((2 * HEAD_LANES, T), BF16)],
        compiler_params=pltpu.CompilerParams(
            dimension_semantics=("arbitrary", "arbitrary"), vmem_limit_bytes=VMEM_LIMIT),
        name="mlstm",
    )(y, y, y, y, gates, conv_w, conv_w, cb, cb, gbias, norm_gain.reshape(1, HEAD_LANES), jnp.asarray(tri))


def _na_bias_table(rpb, rows):
    kr = min(NA_ROWS, rows)
    c = np.arange(GRID_W)
    cstart = np.clip(c - NA_COLS // 2, 0, GRID_W - NA_COLS)
    kc = np.arange(GRID_W)
    valid = (kc[None, :] >= cstart[:, None]) & (kc[None, :] < cstart[:, None] + NA_COLS)
    coff = np.clip(kc[None, :] - c[:, None] + NA_COLS - 1, 0, 2 * NA_COLS - 2)
    di = np.arange(kr)
    i = np.arange(kr)
    roff = i[None, :] - di[:, None] + NA_ROWS - 1
    heads = rpb.shape[0]
    cols = jnp.where(jnp.asarray(valid)[None, None], rpb.astype(F32)[:, :, coff], NEG_BIG)
    tab = jnp.take(cols, jnp.asarray(roff.reshape(-1)), axis=1)
    tab = tab.reshape(heads, kr, kr, GRID_W, GRID_W).transpose(0, 1, 3, 2, 4)
    return tab.reshape(heads, kr, GRID_W, kr * GRID_W)


def _na_kernel(q_ref, k_ref, v_ref, bm_ref, o_ref, *, rows, kr):
    W = GRID_W
    heads_per_block = HEAD_LANES // NA_DIM

    head_of_lane = lax.broadcasted_iota(jnp.int32, (W, HEAD_LANES), 1) // NA_DIM

    def body(j, _):
        work = []
        for u in range(NA_STEP_ROWS):
            r = NA_STEP_ROWS * j + u
            rs = jnp.clip(r - kr // 2, 0, rows - kr)
            di = r - rs
            qs = pl.ds(pl.multiple_of(r * W, W), W)
            ks = pl.ds(pl.multiple_of(rs * W, W), kr * W)
            q = q_ref[0, qs, :].astype(F32) * (NA_DIM ** -0.5)
            kw = k_ref[0, ks, :].astype(BF16)
            for hh in range(heads_per_block):
                s = _dot_nt(jnp.where(head_of_lane == hh, q, 0.0).astype(BF16), kw) + bm_ref[hh, di]
                work.append((u, hh, qs, ks, s))
        probs = []
        for u, hh, qs, ks, s in work:
            e = jnp.exp(s - jnp.max(s, axis=-1, keepdims=True))
            probs.append((e.astype(BF16), jnp.sum(e, axis=-1, keepdims=True)))
        outs = {}
        for (u, hh, qs, ks, s), (e, l) in zip(work, probs):
            o = _dot(e, v_ref[0, ks, :].astype(BF16)) / l
            outs[u] = o if hh == 0 else jnp.where(head_of_lane == hh, o, outs[u])
            if hh == heads_per_block - 1:
                o_ref[0, qs, :] = outs[u]
        return 0

    lax.fori_loop(0, rows // NA_STEP_ROWS, body, 0)


def neighbourhood_attention(y, rpb):
    B, T, _ = y.shape
    rows = T // GRID_W
    kr = min(NA_ROWS, rows)
    HB = GROUP_WIDTH // HEAD_LANES
    hpb = HEAD_LANES // NA_DIM
    bm = _na_bias_table(rpb, rows)

    def col(group):
        return pl.BlockSpec((1, T, HEAD_LANES), lambda b, h, group=group: (b, 0, group * HB + h))

    return pl.pallas_call(
        functools.partial(_na_kernel, rows=rows, kr=kr),
        out_shape=jax.ShapeDtypeStruct((B, T, GROUP_WIDTH), F32),
        grid=(B, HB),
        in_specs=[col(4), col(5), col(6),
                  pl.BlockSpec((hpb, kr, GRID_W, kr * GRID_W), lambda b, h: (h, 0, 0, 0))],
        out_specs=pl.BlockSpec((1, T, HEAD_LANES), lambda b, h: (b, 0, h)),
        compiler_params=pltpu.CompilerParams(
            dimension_semantics=("arbitrary", "arbitrary"), vmem_limit_bytes=VMEM_LIMIT),
        name="neighbourhood_attention",
    )(y, y, y, bm)


def _out_kernel(a_ref, b_ref, x_ref, mod_ref, gain_ref, w_ref, r_ref, x1_ref, h2_ref, lg_ref):
    G = a_ref.shape[-1]
    y = _dot(a_ref[0].astype(BF16), w_ref[0:G, :]) + _dot(b_ref[0].astype(BF16), w_ref[G:2 * G, :])
    x1 = x_ref[0] + mod_ref[0, 2:3, :] * y
    x1_ref[0] = x1
    h2 = _rms(x1, gain_ref[...]) * (1.0 + mod_ref[0, 4:5, :]) + mod_ref[0, 3:4, :]
    _store_row_tiles(h2_ref.at[0], h2)
    lg_ref[...] = lax.dot_general(r_ref[...], h2, (((1,), (1,)), ((), ())), precision=HIGHEST,
                                  preferred_element_type=F32)


def out_proj(a_out, b_out, x, mod, gain2, w_out_bf16, router):
    B, T, D = x.shape
    G = a_out.shape[-1]
    E = router.shape[1]
    tm = min(512, T)
    nt = T // tm
    return pl.pallas_call(
        _out_kernel,
        out_shape=[jax.ShapeDtypeStruct((B, T, D), F32),
                   jax.ShapeDtypeStruct((B, T, D // HEAD_LANES, HEAD_LANES), F32),
                   jax.ShapeDtypeStruct((E, B * T), F32)],
        grid=(B, T // tm),
        in_specs=[pl.BlockSpec((1, tm, G), lambda b, i: (b, i, 0)),
                  pl.BlockSpec((1, tm, G), lambda b, i: (b, i, 0)),
                  pl.BlockSpec((1, tm, D), lambda b, i: (b, i, 0)),
                  pl.BlockSpec((1, 6, D), lambda b, i: (b, 0, 0)),
                  pl.BlockSpec((1, D), lambda b, i: (0, 0)),
                  pl.BlockSpec((2 * G, D), lambda b, i: (0, 0)),
                  pl.BlockSpec((E, D), lambda b, i: (0, 0))],
        out_specs=[pl.BlockSpec((1, tm, D), lambda b, i: (b, i, 0)),
                   pl.BlockSpec((1, tm, D // HEAD_LANES, HEAD_LANES), lambda b, i: (b, i, 0, 0)),
                   pl.BlockSpec((E, tm), lambda b, i, nt=nt: (0, b * nt + i))],
        compiler_params=pltpu.CompilerParams(
            dimension_semantics=("arbitrary", "arbitrary"), vmem_limit_bytes=VMEM_LIMIT),
        name="out_proj",
    )(a_out, b_out, x, mod, gain2.reshape(1, D), w_out_bf16, router.T)


def _route_kernel(lg_ref, bias_ref, idx_ref, w_ref, cnt_ref, *, tiles_per_group):
    @pl.when(pl.program_id(0) % tiles_per_group == 0)
    def _():
        cnt_ref[...] = jnp.zeros(cnt_ref.shape, F32)

    scores = _sigmoid(lg_ref[...])
    sel = scores + bias_ref[...]
    E, tm = sel.shape
    per_group = E // N_GROUPS
    neg = -jnp.inf
    eid = lax.broadcasted_iota(jnp.int32, (E, tm), 0).astype(F32)
    eid_g = lax.broadcasted_iota(jnp.int32, (per_group, tm), 0).astype(F32)

    def first_argmax(x, ids, sentinel):
        m = jnp.max(x, axis=0, keepdims=True)
        i = jnp.min(jnp.where(x == m, ids, sentinel), axis=0, keepdims=True)
        return m, i

    parts, gscore = [], []
    for g in range(N_GROUPS):
        x = sel[g * per_group:(g + 1) * per_group]
        parts.append(x)
        m1, i1 = first_argmax(x, eid_g, float(per_group))
        m2 = jnp.max(jnp.where(eid_g == i1, neg, x), axis=0, keepdims=True)
        gscore.append(m1 + m2)
    kept = []
    for g in range(N_GROUPS):
        beaten = jnp.zeros((1, tm), F32)
        for o in range(N_GROUPS):
            if o == g:
                continue
            wins = (gscore[o] >= gscore[g]) if o < g else (gscore[o] > gscore[g])
            beaten = beaten + wins.astype(F32)
        kept.append(jnp.where(beaten < TOPK_GROUPS, parts[g], neg))
    sel = jnp.concatenate(kept, axis=0)

    ids, vals = [], []
    w_sum = jnp.zeros((1, tm), F32)
    chosen = jnp.zeros((E, tm), F32)
    for k in range(TOP_K):
        _, i = first_argmax(sel, eid, float(E))
        hit = eid == i
        val = jnp.sum(jnp.where(hit, scores, 0.0), axis=0, keepdims=True)
        sel = jnp.where(hit, neg, sel)
        chosen = jnp.where(hit, 1.0, chosen)
        ids.append(i)
        vals.append(val)
        w_sum = w_sum + val
    idx_ref[...] = jnp.concatenate(ids, axis=0).astype(jnp.int32)
    w_ref[...] = jnp.concatenate(vals, axis=0) / w_sum * ROUTED_SCALE
    part = chosen[:, 0:HEAD_LANES]
    for l0 in range(HEAD_LANES, tm, HEAD_LANES):
        part = part + chosen[:, l0:l0 + HEAD_LANES]
    cnt_ref[0] = cnt_ref[0] + part


def route(logits_t, router_bias, group_tokens):
    E, N = logits_t.shape
    tm = min(512, N)
    tpg = group_tokens // tm
    idx, w, cnt = pl.pallas_call(
        functools.partial(_route_kernel, tiles_per_group=tpg),
        out_shape=[jax.ShapeDtypeStruct((TOP_K, N), jnp.int32), jax.ShapeDtypeStruct((TOP_K, N), F32),
                   jax.ShapeDtypeStruct((N // group_tokens, E, HEAD_LANES), F32)],
        grid=(N // tm,),
        in_specs=[pl.BlockSpec((E, tm), lambda i: (0, i)), pl.BlockSpec((E, 1), lambda i: (0, 0))],
        out_specs=[pl.BlockSpec((TOP_K, tm), lambda i: (0, i)), pl.BlockSpec((TOP_K, tm), lambda i: (0, i)),
                   pl.BlockSpec((1, E, HEAD_LANES), lambda i, tpg=tpg: (i // tpg, 0, 0))],
        compiler_params=pltpu.CompilerParams(dimension_semantics=("arbitrary",)),
        name="route",
    )(logits_t, router_bias.reshape(E, 1))
    return idx, w, jnp.sum(cnt, axis=-1).astype(jnp.int32)


def _loop_unrolled(n, body):
    u = MOE_LOOP_UNROLL
    lax.fori_loop(0, n // u, functools.partial(body, count=u), 0)
    lax.fori_loop(n // u * u, n, functools.partial(body, count=1), 0)


def _moe_kernel(off_ref, tok_ref, wl_ref, x_ref, wg_ref, wu_ref, wd_ref, acc_ref, xg_ref, yb_ref,
                *, R, E, per_group, experts_per_step):
    g = pl.program_id(0)
    step = pl.program_id(1)

    @pl.when(step == 0)
    def _():
        acc_ref[...] = jnp.zeros(acc_ref.shape, F32)

    @pl.when((g == 0) & (step == 0))
    def _():
        xg_ref[...] = jnp.zeros(xg_ref.shape, F32)

    U = SUBLANES
    chunks = U
    lanes = x_ref.shape[-1]

    def token_tile(ref, t8):
        return ref.at[0, pl.ds(pl.multiple_of(t8, U), U), :]

    def gather(le, s0, nr):
        def body(j, _, count):
            for gg in range(count):
                base = s0 + (j * count + gg) * U
                for i in range(U):
                    xg_ref[le, j * count + gg, pl.ds(i, chunks, stride=U), :] = (
                        token_tile(x_ref, tok_ref[base + i])[...])
            return 0

        whole = jnp.minimum((nr + U - 1) // U, (per_group - s0) // U)
        _loop_unrolled(whole, body)

        def single(r, _):
            xg_ref[le, r // U, pl.ds(r % U, chunks, stride=U), :] = token_tile(x_ref, tok_ref[s0 + r])[...]
            return 0

        lax.fori_loop(whole * U, nr, single, 0)

    def load_block(le):
        return jnp.concatenate([xg_ref[le, :, c * U:(c + 1) * U, :].reshape(R, lanes) for c in range(chunks)],
                               axis=1).astype(BF16)

    def store_block(le, y):
        for c in range(chunks):
            yb_ref[le, :, c * U:(c + 1) * U, :] = y[:, c * lanes:(c + 1) * lanes].reshape(R // U, U, lanes)

    def scatter(le, s0, nr):
        def group(j, _, count):
            base = s0 + j * count * U
            rows = range(count * U)
            tiles = [token_tile(acc_ref, tok_ref[base + i]) for i in rows]
            wts = [wl_ref[base + i] for i in rows]
            new = [tiles[i][...] + wts[i] * yb_ref[le, j * count + i // U, pl.ds(i % U, chunks, stride=U), :]
                   for i in rows]
            for i in rows:
                tiles[i][...] = new[i]
            return 0

        groups = nr // U
        _loop_unrolled(groups, group)

        def single(r, _):
            tile = token_tile(acc_ref, tok_ref[s0 + r])
            row = yb_ref[le, r // U, pl.ds(r % U, chunks, stride=U), :]
            tile[...] = tile[...] + wl_ref[s0 + r] * row
            return 0

        lax.fori_loop(groups * U, nr, single, 0)

    segments = []
    for le in range(experts_per_step):
        segment = g * E + step * experts_per_step + le
        seg = off_ref[segment]
        segments.append((seg - g * per_group, off_ref[segment + 1] - seg))

    for le, (start, cnt) in enumerate(segments):
        gather(le, start, jnp.minimum(R, cnt))
    xs = [load_block(le) for le in range(experts_per_step)]
    gu = [(_dot(xs[le], wg_ref[le].astype(BF16)), _dot(xs[le], wu_ref[le].astype(BF16)))
          for le in range(experts_per_step)]
    hs = [(_silu(gate) * up).astype(BF16) for gate, up in gu]
    for le in range(experts_per_step):
        store_block(le, _dot(hs[le], wd_ref[le].astype(BF16)))
    for le, (start, cnt) in enumerate(segments):
        scatter(le, start, jnp.minimum(R, cnt))

    for le, (start, cnt) in enumerate(segments):
        def more(sb, _, le=le, start=start, cnt=cnt):
            s0 = start + sb * R
            nr = jnp.minimum(R, cnt - sb * R)
            gather(le, s0, nr)
            x = load_block(le)
            hmid = _silu(_dot(x, wg_ref[le].astype(BF16))) * _dot(x, wu_ref[le].astype(BF16))
            store_block(le, _dot(hmid.astype(BF16), wd_ref[le].astype(BF16)))
            scatter(le, s0, nr)
            return 0

        lax.fori_loop(1, (cnt + R - 1) // R, more, 0)


def routed_experts(h2, top_idx, top_w, counts, exp_gate, exp_up, exp_down):
    N, chunks, lanes = h2.shape
    D = chunks * lanes
    E, _, F = exp_gate.shape
    TG = min(MOE_TOKEN_GROUP, N)
    G = N // TG
    per_group = TG * TOP_K
    key = (jnp.arange(N, dtype=jnp.int32)[None, :] // TG) * E + top_idx
    order = jnp.argsort(key.reshape(-1))
    assert chunks == SUBLANES
    tok_s = (((order % N) % TG) * chunks).astype(jnp.int32)
    w_s = top_w.reshape(-1)[order]
    off = jnp.concatenate([jnp.zeros((1,), jnp.int32), jnp.cumsum(counts.reshape(-1)).astype(jnp.int32)])

    EPS = MOE_EXPERTS_PER_STEP
    grid_spec = pltpu.PrefetchScalarGridSpec(
        num_scalar_prefetch=1,
        grid=(G, E // EPS),
        in_specs=[pl.BlockSpec((per_group,), lambda g, e, off: (g,), memory_space=pltpu.SMEM),
                  pl.BlockSpec((per_group,), lambda g, e, off: (g,), memory_space=pltpu.SMEM),
                  pl.BlockSpec((1, TG * chunks, lanes), lambda g, e, off: (g, 0, 0),
                               pipeline_mode=pl.Buffered(1)),
                  pl.BlockSpec((EPS, D, F), lambda g, e, off: (e, 0, 0)),
                  pl.BlockSpec((EPS, D, F), lambda g, e, off: (e, 0, 0)),
                  pl.BlockSpec((EPS, F, D), lambda g, e, off: (e, 0, 0))],
        out_specs=pl.BlockSpec((1, TG * chunks, lanes), lambda g, e, off: (g, 0, 0),
                               pipeline_mode=pl.Buffered(1)),
        scratch_shapes=[pltpu.VMEM((EPS, MOE_ROWS // SUBLANES, chunks * SUBLANES, lanes), F32),
                        pltpu.VMEM((EPS, MOE_ROWS // SUBLANES, chunks * SUBLANES, lanes), F32)],
    )
    out = pl.pallas_call(
        functools.partial(_moe_kernel, R=MOE_ROWS, E=E, per_group=per_group, experts_per_step=EPS),
        out_shape=jax.ShapeDtypeStruct((G, TG * chunks, lanes), F32),
        grid_spec=grid_spec,
        compiler_params=pltpu.CompilerParams(
            dimension_semantics=("arbitrary", "arbitrary"), vmem_limit_bytes=MOE_VMEM_LIMIT),
        name="routed_experts",
    )(off, tok_s, w_s, h2.reshape(G, TG * chunks, lanes), exp_gate, exp_up, exp_down)
    return out.reshape(N, chunks, lanes)


def _final_kernel(x1_ref, h2_ref, rt_ref, mod_ref, wg_ref, wu_ref, wd_ref, *rest, final):
    if final:
        fg_ref, o_ref = rest
    else:
        (o_ref,) = rest
    hb = _load_row_tiles(h2_ref.at[0]).astype(BF16)
    hmid = _silu(_dot(hb, wg_ref[...])) * _dot(hb, wu_ref[...])
    shared = _dot(hmid.astype(BF16), wd_ref[...])
    x2 = x1_ref[0] + mod_ref[0, 5:6, :] * (_load_row_tiles(rt_ref.at[0]) + shared)
    if final:
        x2 = _rms(x2, fg_ref[...])
    o_ref[0] = x2


def shared_and_residual(x1, h2, routed, mod, sh_gate, sh_up, sh_down, final_gain=None):
    B, T, D = x1.shape
    F = sh_gate.shape[1]
    tm = min(512, T)
    final = final_gain is not None
    tile = pl.BlockSpec((1, tm, D), lambda b, i: (b, i, 0))
    row_tiles = pl.BlockSpec((1, tm, D // HEAD_LANES, HEAD_LANES), lambda b, i: (b, i, 0, 0))
    in_specs = [tile, row_tiles, row_tiles,
                pl.BlockSpec((1, 6, D), lambda b, i: (b, 0, 0)),
                pl.BlockSpec((D, F), lambda b, i: (0, 0)),
                pl.BlockSpec((D, F), lambda b, i: (0, 0)),
                pl.BlockSpec((F, D), lambda b, i: (0, 0))]
    args = [x1, h2, routed, mod, sh_gate, sh_up, sh_down]
    if final:
        in_specs.append(pl.BlockSpec((1, D), lambda b, i: (0, 0)))
        args.append(final_gain.reshape(1, D))
    return pl.pallas_call(
        functools.partial(_final_kernel, final=final),
        out_shape=jax.ShapeDtypeStruct((B, T, D), F32),
        grid=(B, T // tm),
        in_specs=in_specs,
        out_specs=tile,
        compiler_params=pltpu.CompilerParams(
            dimension_semantics=("arbitrary", "arbitrary"), vmem_limit_bytes=VMEM_LIMIT),
        name="shared_and_residual",
    )(*args)


def moe_block(x1, h2, logits, mod, router_bias, exp_gate, exp_up, exp_down, sh_gate, sh_up, sh_down,
              final_gain=None):
    B, T, D = x1.shape
    top_idx, top_w, counts = route(logits, router_bias, min(MOE_TOKEN_GROUP, B * T))
    chunks = D // HEAD_LANES
    routed = routed_experts(h2.reshape(B * T, chunks, HEAD_LANES), top_idx, top_w, counts,
                            exp_gate, exp_up, exp_down)
    return shared_and_residual(x1, h2, routed.reshape(B, T, chunks, HEAD_LANES), mod,
                               sh_gate.astype(BF16), sh_up.astype(BF16), sh_down.astype(BF16), final_gain)


def kernel(x, c, hgrn_lb_logits, l0_norm1, l0_norm2, l0_w_mod, l0_b_mod, l0_w_in, l0_w_out, l0_hgrn_norm, l0_diff_lq1, l0_diff_lk1, l0_diff_lq2, l0_diff_lk2, l0_diff_subln, l0_router, l0_router_bias, l0_exp_gate, l0_exp_up, l0_exp_down, l0_sh_gate, l0_sh_up, l0_sh_down, l1_norm1, l1_norm2, l1_w_mod, l1_b_mod, l1_w_in, l1_w_out, l1_conv_w, l1_conv_b, l1_ml_i_bias, l1_ml_f_bias, l1_ml_norm, l1_na_rpb, l1_router, l1_router_bias, l1_exp_gate, l1_exp_up, l1_exp_down, l1_sh_gate, l1_sh_up, l1_sh_down, final_norm):
    G = GROUP_WIDTH
    lb_all = jnp.cumsum(jax.nn.softmax(hgrn_lb_logits.astype(F32), axis=0), axis=0)
    layer_idx = 0
    lambda_init = 0.8 - 0.6 * math.exp(-0.3 * layer_idx)
    lam = (jnp.exp(jnp.sum(l0_diff_lq1.astype(F32) * l0_diff_lk1.astype(F32)))
           - jnp.exp(jnp.sum(l0_diff_lq2.astype(F32) * l0_diff_lk2.astype(F32))) + lambda_init)

    mod0 = ada_mod(c, l0_w_mod, l0_b_mod)
    y0 = in_proj(x, mod0, l0_norm1, l0_w_in.astype(BF16))
    a_out = hgrn2(y0, lb_all[0], l0_hgrn_norm)
    b_out = diff_attention(y0, lam, l0_diff_subln, layer_idx)
    x1, h2, logits = out_proj(a_out, b_out, x, mod0, l0_norm2, l0_w_out.astype(BF16), l0_router)
    xa = moe_block(x1, h2, logits, mod0, l0_router_bias, l0_exp_gate, l0_exp_up, l0_exp_down,
                   l0_sh_gate, l0_sh_up, l0_sh_down)

    mod1 = ada_mod(c, l1_w_mod, l1_b_mod)
    n_gate = l1_w_in.shape[1] - 7 * G
    w_main = jnp.concatenate([l1_w_in[:, :4 * G], l1_w_in[:, 4 * G + n_gate:]], axis=1).astype(BF16)
    w_gate = jnp.pad(l1_w_in[:, 4 * G:4 * G + n_gate], ((0, 0), (0, HEAD_LANES - n_gate)))
    y1, gates = in_proj(xa, mod1, l1_norm1, w_main, w_gate)
    c_out = mlstm(y1, gates, l1_conv_w, l1_conv_b, l1_ml_i_bias, l1_ml_f_bias, l1_ml_norm)
    d_out = neighbourhood_attention(y1, l1_na_rpb)
    x1, h2, logits = out_proj(c_out, d_out, xa, mod1, l1_norm2, l1_w_out.astype(BF16), l1_router)
    return moe_block(x1, h2, logits, mod1, l1_router_bias, l1_exp_gate, l1_exp_up, l1_exp_down,
                     l1_sh_gate, l1_sh_up, l1_sh_down, final_gain=final_norm)
```

```python
import functools
import math

import numpy as np
import jax
import jax.numpy as jnp
from jax import lax
from jax.experimental import pallas as pl
from jax.experimental.pallas import tpu as pltpu

F32 = jnp.float32
BF16 = jnp.bfloat16
HIGHEST = lax.Precision.HIGHEST
EPS = 1e-6

GRID_W = 64
GROUP_WIDTH = 512
HEAD_LANES = 128
HG_CHUNK = 64
ML_CHUNK = 128
STEP_CHUNKS = 4
ML_CONV = 5
DA_DIM = 64
ROPE_DIM = 16
ROPE_THETA = 500000.0
NA_ROWS = 8
NA_COLS = 16
NA_DIM = 64
NA_STEP_ROWS = 4
N_EXPERTS = 128
TOP_K = 8
N_GROUPS = 8
TOPK_GROUPS = 4
ROUTED_SCALE = 2.5
MOE_ROWS = 320
MOE_TOKEN_GROUP = 4096
MOE_EXPERTS_PER_STEP = 2
MOE_LOOP_UNROLL = 2
SUBLANES = 8
NEG_BIG = -1e30
VMEM_LIMIT = 48 * 1024 * 1024
MOE_VMEM_LIMIT = 56 * 1024 * 1024


def _dot(a, b, **kw):
    return jnp.dot(a, b, preferred_element_type=F32, **kw)


def _dot_nt(a, b):
    return lax.dot_general(a, b, (((1,), (1,)), ((), ())), preferred_element_type=F32)


def _dot_tn(a, b):
    return lax.dot_general(a, b, (((0,), (0,)), ((), ())), preferred_element_type=F32)


def _sigmoid(x):
    return jax.nn.sigmoid(x)


def _silu(x):
    return x * jax.nn.sigmoid(x)


def _log_sigmoid(x):
    return jnp.minimum(x, 0.0) - jnp.log(1.0 + jnp.exp(-jnp.abs(x)))


def _rms(x, gain):
    return x * lax.rsqrt(jnp.mean(x * x, axis=-1, keepdims=True) + EPS) * gain


def _store_row_tiles(ref, val):
    for c in range(ref.shape[1]):
        ref[:, c, :] = val[:, c * HEAD_LANES:(c + 1) * HEAD_LANES]


def _load_row_tiles(ref):
    return jnp.concatenate([ref[:, c, :] for c in range(ref.shape[1])], axis=1)


def _mod_kernel(c_ref, w_ref, b_ref, o_ref):
    o_ref[...] = _dot(_silu(c_ref[...]), w_ref[...], precision=HIGHEST) + b_ref[...]


def ada_mod(c, w_mod, b_mod):
    B, D = c.shape
    N = w_mod.shape[1]
    tn = 1024
    out = pl.pallas_call(
        _mod_kernel,
        out_shape=jax.ShapeDtypeStruct((B, N), F32),
        grid=(N // tn,),
        in_specs=[pl.BlockSpec((B, D), lambda j: (0, 0)),
                  pl.BlockSpec((D, tn), lambda j: (0, j)),
                  pl.BlockSpec((1, tn), lambda j: (0, j))],
        out_specs=pl.BlockSpec((B, tn), lambda j: (0, j)),
        name="ada_mod",
    )(c, w_mod, b_mod.reshape(1, N))
    return out.reshape(B, 6, D)


def _in_kernel(x_ref, mod_ref, gain_ref, w_ref, *rest, has_gate):
    if has_gate:
        wg_ref, o_ref, og_ref, h_ref = rest
    else:
        o_ref, h_ref = rest

    @pl.when(pl.program_id(2) == 0)
    def _():
        h = _rms(x_ref[0], gain_ref[...]) * (1.0 + mod_ref[0, 1:2, :]) + mod_ref[0, 0:1, :]
        h_ref[...] = h.astype(BF16)
        if has_gate:
            og_ref[0] = _dot(h, wg_ref[...], precision=HIGHEST)

    o_ref[0] = _dot(h_ref[...], w_ref[...]).astype(o_ref.dtype)


def in_proj(x, mod, gain, w_bf16, w_gate=None):
    B, T, D = x.shape
    N = w_bf16.shape[1]
    tm = min(1024, T)
    tn = 512
    has_gate = w_gate is not None
    in_specs = [pl.BlockSpec((1, tm, D), lambda b, i, j: (b, i, 0)),
                pl.BlockSpec((1, 6, D), lambda b, i, j: (b, 0, 0)),
                pl.BlockSpec((1, D), lambda b, i, j: (0, 0)),
                pl.BlockSpec((D, tn), lambda b, i, j: (0, j))]
    out_shape = [jax.ShapeDtypeStruct((B, T, N), BF16)]
    out_specs = [pl.BlockSpec((1, tm, tn), lambda b, i, j: (b, i, j))]
    args = [x, mod, gain.reshape(1, D), w_bf16]
    if has_gate:
        in_specs.append(pl.BlockSpec((D, HEAD_LANES), lambda b, i, j: (0, 0)))
        out_shape.append(jax.ShapeDtypeStruct((B, T, HEAD_LANES), F32))
        out_specs.append(pl.BlockSpec((1, tm, HEAD_LANES), lambda b, i, j: (b, i, 0)))
        args.append(w_gate)
    res = pl.pallas_call(
        functools.partial(_in_kernel, has_gate=has_gate),
        out_shape=out_shape,
        grid=(B, T // tm, N // tn),
        in_specs=in_specs,
        out_specs=out_specs,
        scratch_shapes=[pltpu.VMEM((tm, D), BF16)],
        compiler_params=pltpu.CompilerParams(
            dimension_semantics=("arbitrary", "arbitrary", "arbitrary")),
        name="in_proj",
    )(*args)
    return res if has_gate else res[0]


def _hgrn_consts(C):
    t = np.arange(C)
    tri = (t[None, :] <= t[:, None]).astype(np.float32)
    triT = np.ascontiguousarray(tri.T)
    wf, wb, mf = [tri], [triT], []
    levels = int(round(math.log2(C)))
    for l in range(levels):
        size = C >> l
        blk = t // size
        r = blk * size + size // 2
        wf.append(tri - tri[r - 1])
        wb.append(triT - triT[r])
        upper = (t % size) >= size // 2
        mf.append(((blk[:, None] == blk[None, :]) & upper[:, None] & (~upper)[None, :]).astype(np.float32))
    mf.append(np.eye(C, dtype=np.float32))
    ones = np.ones((8, C), np.float32)
    wf.append(ones)
    wb.append(ones)
    mf = np.stack(mf)
    mb = np.ascontiguousarray(np.transpose(mf, (0, 2, 1)))
    return np.concatenate(wf), np.concatenate(wb), mf, mb


def _split2(x):
    hi = x.astype(BF16)
    lo = (x - hi.astype(F32)).astype(BF16)
    return hi, lo


def _hgrn_kernel(q_ref, i_ref, ff_ref, fb_ref, g_ref, lb_ref, gain_ref, wf_ref, wb_ref, mf_ref, mb_ref,
                 o_ref, of_ref, ob_ref, *, C, T):
    n = T // C
    levels = int(round(math.log2(C)))
    dv = q_ref.shape[-1]

    def prepare(c, f_ref, lbd, w_ref):
        sl = pl.ds(pl.multiple_of(c * C, C), C)
        q = _silu(q_ref[0, sl, :].astype(F32))
        v = i_ref[0, sl, :]
        fg = lbd + (1.0 - lbd) * _sigmoid(f_ref[0, sl, :].astype(F32))
        lf = jnp.log(fg)
        d2 = _dot(w_ref[...], jnp.concatenate(_split2(lf), axis=1))
        dall = d2[:, 0:dv] + d2[:, dv:2 * dv]
        return dict(sl=sl, q=q, k=1.0 - fg, v=v.astype(BF16), dall=dall)

    def scores(p, m_ref):
        q, k, dall = p["q"], p["k"], p["dall"]
        attn = m_ref[levels] * _dot_nt(q.astype(BF16), k.astype(BF16))
        for l in range(levels):
            e = jnp.exp(-jnp.abs(dall[(l + 1) * C:(l + 2) * C]))
            attn = attn + m_ref[l] * _dot_nt((q * e).astype(BF16), (k * e).astype(BF16))
        return attn.astype(BF16)

    st0 = jnp.zeros((dv, dv), F32)

    def step(j, carry):
        states = list(carry)
        work = []
        for u in range(STEP_CHUNKS):
            work.append((0, prepare(STEP_CHUNKS * j + u, ff_ref, lb_ref[0:1, :], wf_ref), mf_ref, of_ref))
            work.append((1, prepare(n - 1 - STEP_CHUNKS * j - u, fb_ref, lb_ref[1:2, :], wb_ref), mb_ref, ob_ref))
        attn = [scores(p, m_ref) for _, p, m_ref, _ in work]
        local = []
        for (_, p, _, _), a in zip(work, attn):
            cum = p["dall"][0:C]
            tot = p["dall"][(levels + 1) * C:(levels + 1) * C + 1]
            kt = p["k"] * jnp.exp(tot - cum)
            local.append((_dot(a, p["v"]), _dot_tn(p["v"], kt.astype(BF16)), cum, tot))
        for (d, p, _, out_ref), (o_in, incr, cum, tot) in zip(work, local):
            st = states[d]
            out_ref[p["sl"], :] = o_in + _dot_nt((p["q"] * jnp.exp(cum)).astype(BF16), st.astype(BF16))
            states[d] = st * jnp.exp(tot) + incr
        return tuple(states)

    lax.fori_loop(0, n // STEP_CHUNKS, step, (st0, st0))

    rt = min(512, T)
    for r0 in range(0, T, rt):
        sl = slice(r0, r0 + rt)
        o_ref[0, sl, :] = _rms(of_ref[sl, :] + ob_ref[sl, :], gain_ref[...]) * _silu(g_ref[0, sl, :].astype(F32))


def hgrn2(y, lb, norm_gain):
    B, T, _ = y.shape
    H = GROUP_WIDTH // HEAD_LANES
    C = HG_CHUNK
    wf, wb, mf, mb = _hgrn_consts(C)
    wf, wb = jnp.asarray(wf, BF16), jnp.asarray(wb, BF16)
    mf, mb = jnp.asarray(mf), jnp.asarray(mb)

    def col(group):
        return pl.BlockSpec((1, T, HEAD_LANES), lambda b, h, group=group: (b, 0, group * H + h))

    def const(a):
        nd = a.ndim
        return pl.BlockSpec(a.shape, lambda b, h, nd=nd: (0,) * nd)

    return pl.pallas_call(
        functools.partial(_hgrn_kernel, C=C, T=T),
        out_shape=jax.ShapeDtypeStruct((B, T, GROUP_WIDTH), F32),
        grid=(B, H),
        in_specs=[col(0), col(1), col(2), col(3), col(4),
                  pl.BlockSpec((2, HEAD_LANES), lambda b, h: (0, h)),
                  pl.BlockSpec((1, HEAD_LANES), lambda b, h: (0, 0)),
                  const(wf), const(wb), const(mf), const(mb)],
        out_specs=pl.BlockSpec((1, T, HEAD_LANES), lambda b, h: (b, 0, h)),
        scratch_shapes=[pltpu.VMEM((T, HEAD_LANES), F32), pltpu.VMEM((T, HEAD_LANES), F32)],
        compiler_params=pltpu.CompilerParams(
            dimension_semantics=("arbitrary", "arbitrary"), vmem_limit_bytes=VMEM_LIMIT),
        name="hgrn2",
    )(y, y, y, y, y, lb, norm_gain.reshape(1, HEAD_LANES), wf, wb, mf, mb)


def _rope_tables(T):
    pos = np.arange(T, dtype=np.float32)
    inv_freq = (ROPE_THETA ** (-np.arange(0, ROPE_DIM, 2, dtype=np.float32) / ROPE_DIM)).astype(np.float32)
    ang = pos[:, None] * inv_freq[None, :]
    cos, sin = np.cos(ang), np.sin(ang)
    half = ROPE_DIM // 2
    c = np.ones((T, HEAD_LANES), np.float32)
    s_prev = np.zeros((T, HEAD_LANES), np.float32)
    s_next = np.zeros((T, HEAD_LANES), np.float32)
    for base in range(0, HEAD_LANES, DA_DIM):
        c[:, base:base + half] = cos
        c[:, base + half:base + ROPE_DIM] = cos
        s_next[:, base:base + half] = -sin
        s_prev[:, base + half:base + ROPE_DIM] = sin
    return jnp.asarray(c), jnp.asarray(s_prev), jnp.asarray(s_next)


def _rope(x, c, s_prev, s_next):
    half = ROPE_DIM // 2
    lanes = x.shape[-1]
    return (x * c + pltpu.roll(x, half, axis=1) * s_prev
            + pltpu.roll(x, lanes - half, axis=1) * s_next)


def _diff_kernel(lam_ref, q_ref, k_ref, v_ref, cq_ref, spq_ref, snq_ref, ck_ref, spk_ref, snk_ref,
                 subln_ref, o_ref, kr_ref, vb_ref, *, T, out_scale):
    rt = min(512, T)

    @pl.when(pl.program_id(2) == 0)
    def _():
        for r0 in range(0, T, rt):
            sl = slice(r0, r0 + rt)
            kr_ref[sl, :] = _rope(k_ref[0, sl, :].astype(F32), ck_ref[sl, :], spk_ref[sl, :], snk_ref[sl, :]).astype(BF16)
            vb_ref[sl, 0:HEAD_LANES] = v_ref[0, sl, :].astype(BF16)
            lane = lax.broadcasted_iota(jnp.int32, (rt, HEAD_LANES), 1)
            vb_ref[sl, HEAD_LANES:2 * HEAD_LANES] = jnp.where(lane == 0, 1.0, 0.0).astype(BF16)

    q = _rope(q_ref[0].astype(F32), cq_ref[...], spq_ref[...], snq_ref[...]) * (DA_DIM ** -0.5 * math.log2(math.e))
    lam = lam_ref[0, 0]
    v = vb_ref[...]
    map_of_lane = lax.broadcasted_iota(jnp.int32, q.shape, 1) // DA_DIM

    scores = [_dot_nt(jnp.where(map_of_lane == m, q, 0.0).astype(BF16), kr_ref[...]) for m in range(2)]
    probs = [jnp.exp2(s - jnp.max(s, axis=-1, keepdims=True)).astype(BF16) for s in scores]
    pv = []
    for e in probs:
        full = _dot(e, v)
        pv.append(full[:, 0:HEAD_LANES] / full[:, HEAD_LANES:HEAD_LANES + 1])
    o = pv[0] - lam * pv[1]
    o_ref[0] = _rms(o, subln_ref[...]) * out_scale


def diff_attention(y, lam, subln, layer_idx):
    B, T, _ = y.shape
    H = GROUP_WIDTH // HEAD_LANES
    tq = min(256, T)
    lambda_init = 0.8 - 0.6 * math.exp(-0.3 * layer_idx)
    c, sp, sn = _rope_tables(T)

    def col(group, rows):
        if rows == T:
            return pl.BlockSpec((1, T, HEAD_LANES), lambda b, h, i, group=group: (b, 0, group * H + h))
        return pl.BlockSpec((1, rows, HEAD_LANES), lambda b, h, i, group=group: (b, i, group * H + h))

    tab_q = pl.BlockSpec((tq, HEAD_LANES), lambda b, h, i: (i, 0))
    tab_k = pl.BlockSpec((T, HEAD_LANES), lambda b, h, i: (0, 0))
    return pl.pallas_call(
        functools.partial(_diff_kernel, T=T, out_scale=1.0 - lambda_init),
        out_shape=jax.ShapeDtypeStruct((B, T, GROUP_WIDTH), F32),
        grid=(B, H, T // tq),
        in_specs=[pl.BlockSpec(memory_space=pltpu.SMEM),
                  col(5, tq), col(6, T), col(7, T),
                  tab_q, tab_q, tab_q, tab_k, tab_k, tab_k,
                  pl.BlockSpec((1, HEAD_LANES), lambda b, h, i: (0, 0))],
        out_specs=pl.BlockSpec((1, tq, HEAD_LANES), lambda b, h, i: (b, i, h)),
        scratch_shapes=[pltpu.VMEM((T, HEAD_LANES), BF16), pltpu.VMEM((T, 2 * HEAD_LANES), BF16)],
        compiler_params=pltpu.CompilerParams(
            dimension_semantics=("arbitrary", "arbitrary", "arbitrary"), vmem_limit_bytes=VMEM_LIMIT),
        name="diff_attention",
    )(lam.reshape(1, 1), y, y, y, c, sp, sn, c, sp, sn, subln.reshape(1, HEAD_LANES))


def _mlstm_kernel(q_ref, k_ref, v_ref, og_ref, gt_ref, cwq_ref, cwk_ref, cbq_ref, cbk_ref, gbias_ref,
                  gain_ref, tri_ref, o_ref, qc_ref, kc_ref, xp_ref, gx_ref, hf_ref, hb_ref, gxt_ref, va_ref,
                  vta_ref, *, C, T, dk):
    n = T // C
    head = pl.program_id(1)
    pad = 8
    half = ML_CONV // 2
    rt = min(512, T)

    xp_ref[0:pad, :] = jnp.zeros((pad, dk), F32)
    xp_ref[pad + T:pad + T + pad, :] = jnp.zeros((pad, dk), F32)
    for src, cw_ref, cb_ref, dst, scale in ((q_ref, cwq_ref, cbq_ref, qc_ref, 1.0),
                                            (k_ref, cwk_ref, cbk_ref, kc_ref, dk ** -0.5)):
        xp_ref[pad:pad + T, :] = src[0].astype(F32)
        for r0 in range(0, T, rt):
            acc = jnp.zeros((rt, dk), F32) + cb_ref[...]
            for j in range(ML_CONV):
                acc = acc + xp_ref[pad + r0 + j - half:pad + r0 + j - half + rt, :] * cw_ref[j:j + 1, :]
            dst[r0:r0 + rt, :] = _silu(acc) * scale

    lane = lax.broadcasted_iota(jnp.int32, (rt, HEAD_LANES), 1)
    is_f = (lane % 8) >= 4
    for r0 in range(0, T, rt):
        g = gt_ref[0, r0:r0 + rt, :] + gbias_ref[...]
        p = jnp.where(is_f, _log_sigmoid(g), g)
        x = jnp.zeros((rt, HEAD_LANES), F32)
        for j, src_lane in enumerate((0, 4, 8, 12)):
            colv = jnp.sum(jnp.where(lane == src_lane + head, p, 0.0), axis=1, keepdims=True)
            x = jnp.where(lane == j, colv, x)
        gx_ref[r0:r0 + rt, :] = x
        gxt_ref[:, r0:r0 + rt] = x.T[0:SUBLANES, :]
        v = v_ref[0, r0:r0 + rt, :].astype(F32)
        va_ref[r0:r0 + rt, 0:dk] = v.astype(BF16)
        va_ref[r0:r0 + rt, dk:2 * dk] = jnp.where(lane == 0, 1.0, 0.0).astype(BF16)
        vta_ref[0:dk, r0:r0 + rt] = v.T.astype(BF16)
        sub = lax.broadcasted_iota(jnp.int32, (dk, rt), 0)
        vta_ref[dk:2 * dk, r0:r0 + rt] = jnp.where(sub == 0, 1.0, 0.0).astype(BF16)

    row = lax.broadcasted_iota(jnp.int32, (C, C), 0)
    colm = lax.broadcasted_iota(jnp.int32, (C, C), 1)

    init = (jnp.zeros((2 * dk, dk), F32), jnp.full((1, 1), NEG_BIG, F32))

    def step(j, carry):
        carries = list(carry)
        work = []
        for u in range(STEP_CHUNKS):
            work.append((0, STEP_CHUNKS * j + u, hf_ref))
            work.append((1, n - 1 - STEP_CHUNKS * j - u, hb_ref))
        chunks = []
        for d, c, out_ref in work:
            sl = pl.ds(pl.multiple_of(c * C, C), C)
            x = gx_ref[sl, :]
            xr = gxt_ref[:, sl]
            chunks.append(dict(d=d, sl=sl, out=out_ref, x=x, xr=xr, q=qc_ref[sl, :], k=kc_ref[sl, :],
                               cumx=_dot(tri_ref[d], x, precision=HIGHEST),
                               cumr=_dot(xr, tri_ref[1 - d], precision=HIGHEST)))
        for p in chunks:
            p["qk"] = _dot_nt(p["q"].astype(BF16), p["k"].astype(BF16))
        for p in chunks:
            d, x, xr, cumx, cumr = p["d"], p["x"], p["xr"], p["cumx"], p["cumr"]
            mask = (colm <= row) if d == 0 else (colm >= row)
            ig_c = x[:, 2 * d:2 * d + 1]
            ig_r = xr[2 * d:2 * d + 1, :]
            cum_c = cumx[:, 2 * d + 1:2 * d + 2]
            cum_r = cumr[2 * d + 1:2 * d + 2, :]
            tot = jnp.sum(x[:, 2 * d + 1:2 * d + 2], axis=0, keepdims=True)
            dmat = jnp.where(mask, cum_c - cum_r + ig_r, -jnp.inf)
            dmax = jnp.max(dmat, axis=1, keepdims=True)
            a = p["qk"] * jnp.exp(dmat - dmax)
            ds = tot - cum_c + ig_c
            dsmax = jnp.max(ds, axis=0, keepdims=True)
            kw = p["k"] * jnp.exp(ds - dsmax)
            p.update(cum_c=cum_c, tot=tot, dmax=dmax, dsmax=dsmax, a=a.astype(BF16), kw=kw.astype(BF16))
        for p in chunks:
            p["num"] = _dot(p["a"], va_ref[p["sl"], :])
            p["upd"] = _dot(vta_ref[:, p["sl"]], p["kw"])
        for p in chunks:
            state, m = carries[p["d"]]
            g = p["cum_c"] + m
            mt = jnp.maximum(g, p["dmax"])
            full = (jnp.exp(g - mt) * _dot_nt(p["q"].astype(BF16), state.astype(BF16))
                    + jnp.exp(p["dmax"] - mt) * p["num"])
            den = full[:, dk:dk + 1]
            p["out"][p["sl"], :] = full[:, 0:dk] / jnp.maximum(jnp.abs(den), jnp.exp(-mt))
            m_new = jnp.maximum(p["tot"] + m, p["dsmax"])
            decay = jnp.exp(p["tot"] + m - m_new)
            scale = jnp.exp(p["dsmax"] - m_new)
            carries[p["d"]] = (decay * state + scale * p["upd"], m_new)
        return tuple(carries)

    lax.fori_loop(0, n // STEP_CHUNKS, step, (init, init))

    for r0 in range(0, T, rt):
        sl = slice(r0, r0 + rt)
        o_ref[0, sl, :] = _rms(hf_ref[sl, :] + hb_ref[sl, :], gain_ref[...]) * _sigmoid(og_ref[0, sl, :].astype(F32))


def mlstm(y, gates, conv_w, conv_b, i_bias, f_bias, norm_gain):
    B, T, _ = y.shape
    H = GROUP_WIDTH // HEAD_LANES
    C = min(ML_CHUNK, T)
    t = np.arange(C)
    tri = np.stack([(t[None, :] <= t[:, None]), (t[None, :] >= t[:, None])]).astype(np.float32)
    gbias = jnp.zeros((1, HEAD_LANES), F32)
    gbias = gbias.at[0, 0:4].set(i_bias[0]).at[0, 4:8].set(f_bias[0])
    gbias = gbias.at[0, 8:12].set(i_bias[1]).at[0, 12:16].set(f_bias[1])

    def col(group):
        return pl.BlockSpec((1, T, HEAD_LANES), lambda b, h, group=group: (b, 0, group * H + h))

    conv_q = pl.BlockSpec((ML_CONV, HEAD_LANES), lambda b, h: (0, h))
    conv_k = pl.BlockSpec((ML_CONV, HEAD_LANES), lambda b, h: (0, H + h))
    bias_q = pl.BlockSpec((1, HEAD_LANES), lambda b, h: (0, h))
    bias_k = pl.BlockSpec((1, HEAD_LANES), lambda b, h: (0, H + h))
    cb = conv_b.reshape(1, -1)
    return pl.pallas_call(
        functools.partial(_mlstm_kernel, C=C, T=T, dk=HEAD_LANES),
        out_shape=jax.ShapeDtypeStruct((B, T, GROUP_WIDTH), F32),
        grid=(B, H),
        in_specs=[col(0), col(1), col(2), col(3),
                  pl.BlockSpec((1, T, HEAD_LANES), lambda b, h: (b, 0, 0)),
                  conv_q, conv_k, bias_q, bias_k,
                  pl.BlockSpec((1, HEAD_LANES), lambda b, h: (0, 0)),
                  pl.BlockSpec((1, HEAD_LANES), lambda b, h: (0, 0)),
                  pl.BlockSpec((2, C, C), lambda b, h: (0, 0, 0))],
        out_specs=pl.BlockSpec((1, T, HEAD_LANES), lambda b, h: (b, 0, h)),
        scratch_shapes=[pltpu.VMEM((T, HEAD_LANES), F32), pltpu.VMEM((T, HEAD_LANES), F32),
                        pltpu.VMEM((T + 16, HEAD_LANES), F32), pltpu.VMEM((T, HEAD_LANES), F32),
                        pltpu.VMEM((T, HEAD_LANES), F32), pltpu.VMEM((T, HEAD_LANES), F32),
                        pltpu.VMEM((SUBLANES, T), F32), pltpu.VMEM((T, 2 * HEAD_LANES), BF16),
                        pltpu.VMEM((2 * HEAD_LANES, T), BF16)],
        compiler_params=pltpu.CompilerParams(
            dimension_semantics=("arbitrary", "arbitrary"), vmem_limit_bytes=VMEM_LIMIT),
        name="mlstm",
    )(y, y, y, y, gates, conv_w, conv_w, cb, cb, gbias, norm_gain.reshape(1, HEAD_LANES), jnp.asarray(tri))


def _na_bias_table(rpb, rows):
    kr = min(NA_ROWS, rows)
    c = np.arange(GRID_W)
    cstart = np.clip(c - NA_COLS // 2, 0, GRID_W - NA_COLS)
    kc = np.arange(GRID_W)
    valid = (kc[None, :] >= cstart[:, None]) & (kc[None, :] < cstart[:, None] + NA_COLS)
    coff = np.clip(kc[None, :] - c[:, None] + NA_COLS - 1, 0, 2 * NA_COLS - 2)
    di = np.arange(kr)
    i = np.arange(kr)
    roff = i[None, :] - di[:, None] + NA_ROWS - 1
    heads = rpb.shape[0]
    cols = jnp.where(jnp.asarray(valid)[None, None], rpb.astype(F32)[:, :, coff], NEG_BIG)
    tab = jnp.take(cols, jnp.asarray(roff.reshape(-1)), axis=1)
    tab = tab.reshape(heads, kr, kr, GRID_W, GRID_W).transpose(0, 1, 3, 2, 4)
    return tab.reshape(heads, kr, GRID_W, kr * GRID_W)


def _na_kernel(q_ref, k_ref, v_ref, bm_ref, o_ref, *, rows, kr):
    W = GRID_W
    heads_per_block = HEAD_LANES // NA_DIM

    head_of_lane = lax.broadcasted_iota(jnp.int32, (W, HEAD_LANES), 1) // NA_DIM

    def body(j, _):
        work = []
        for u in range(NA_STEP_ROWS):
            r = NA_STEP_ROWS * j + u
            rs = jnp.clip(r - kr // 2, 0, rows - kr)
            di = r - rs
            qs = pl.ds(pl.multiple_of(r * W, W), W)
            ks = pl.ds(pl.multiple_of(rs * W, W), kr * W)
            q = q_ref[0, qs, :].astype(F32) * (NA_DIM ** -0.5)
            kw = k_ref[0, ks, :].astype(BF16)
            for hh in range(heads_per_block):
                s = _dot_nt(jnp.where(head_of_lane == hh, q, 0.0).astype(BF16), kw) + bm_ref[hh, di]
                work.append((u, hh, qs, ks, s))
        probs = []
        for u, hh, qs, ks, s in work:
            e = jnp.exp(s - jnp.max(s, axis=-1, keepdims=True))
            probs.append((e.astype(BF16), jnp.sum(e, axis=-1, keepdims=True)))
        outs = {}
        for (u, hh, qs, ks, s), (e, l) in zip(work, probs):
            o = _dot(e, v_ref[0, ks, :].astype(BF16)) / l
            outs[u] = o if hh == 0 else jnp.where(head_of_lane == hh, o, outs[u])
            if hh == heads_per_block - 1:
                o_ref[0, qs, :] = outs[u]
        return 0

    lax.fori_loop(0, rows // NA_STEP_ROWS, body, 0)


def neighbourhood_attention(y, rpb):
    B, T, _ = y.shape
    rows = T // GRID_W
    kr = min(NA_ROWS, rows)
    HB = GROUP_WIDTH // HEAD_LANES
    hpb = HEAD_LANES // NA_DIM
    bm = _na_bias_table(rpb, rows)

    def col(group):
        return pl.BlockSpec((1, T, HEAD_LANES), lambda b, h, group=group: (b, 0, group * HB + h))

    return pl.pallas_call(
        functools.partial(_na_kernel, rows=rows, kr=kr),
        out_shape=jax.ShapeDtypeStruct((B, T, GROUP_WIDTH), F32),
        grid=(B, HB),
        in_specs=[col(4), col(5), col(6),
                  pl.BlockSpec((hpb, kr, GRID_W, kr * GRID_W), lambda b, h: (h, 0, 0, 0))],
        out_specs=pl.BlockSpec((1, T, HEAD_LANES), lambda b, h: (b, 0, h)),
        compiler_params=pltpu.CompilerParams(
            dimension_semantics=("arbitrary", "arbitrary"), vmem_limit_bytes=VMEM_LIMIT),
        name="neighbourhood_attention",
    )(y, y, y, bm)


def _out_kernel(a_ref, b_ref, x_ref, mod_ref, gain_ref, w_ref, r_ref, x1_ref, h2_ref, lg_ref):
    G = a_ref.shape[-1]
    y = _dot(a_ref[0].astype(BF16), w_ref[0:G, :]) + _dot(b_ref[0].astype(BF16), w_ref[G:2 * G, :])
    x1 = x_ref[0] + mod_ref[0, 2:3, :] * y
    x1_ref[0] = x1
    h2 = _rms(x1, gain_ref[...]) * (1.0 + mod_ref[0, 4:5, :]) + mod_ref[0, 3:4, :]
    _store_row_tiles(h2_ref.at[0], h2)
    lg_ref[...] = lax.dot_general(r_ref[...], h2, (((1,), (1,)), ((), ())), precision=HIGHEST,
                                  preferred_element_type=F32)


def out_proj(a_out, b_out, x, mod, gain2, w_out_bf16, router):
    B, T, D = x.shape
    G = a_out.shape[-1]
    E = router.shape[1]
    tm = min(512, T)
    nt = T // tm
    return pl.pallas_call(
        _out_kernel,
        out_shape=[jax.ShapeDtypeStruct((B, T, D), F32),
                   jax.ShapeDtypeStruct((B, T, D // HEAD_LANES, HEAD_LANES), F32),
                   jax.ShapeDtypeStruct((E, B * T), F32)],
        grid=(B, T // tm),
        in_specs=[pl.BlockSpec((1, tm, G), lambda b, i: (b, i, 0)),
                  pl.BlockSpec((1, tm, G), lambda b, i: (b, i, 0)),
                  pl.BlockSpec((1, tm, D), lambda b, i: (b, i, 0)),
                  pl.BlockSpec((1, 6, D), lambda b, i: (b, 0, 0)),
                  pl.BlockSpec((1, D), lambda b, i: (0, 0)),
                  pl.BlockSpec((2 * G, D), lambda b, i: (0, 0)),
                  pl.BlockSpec((E, D), lambda b, i: (0, 0))],
        out_specs=[pl.BlockSpec((1, tm, D), lambda b, i: (b, i, 0)),
                   pl.BlockSpec((1, tm, D // HEAD_LANES, HEAD_LANES), lambda b, i: (b, i, 0, 0)),
                   pl.BlockSpec((E, tm), lambda b, i, nt=nt: (0, b * nt + i))],
        compiler_params=pltpu.CompilerParams(
            dimension_semantics=("arbitrary", "arbitrary"), vmem_limit_bytes=VMEM_LIMIT),
        name="out_proj",
    )(a_out, b_out, x, mod, gain2.reshape(1, D), w_out_bf16, router.T)


def _route_kernel(lg_ref, bias_ref, idx_ref, w_ref, cnt_ref, *, tiles_per_group):
    @pl.when(pl.program_id(0) % tiles_per_group == 0)
    def _():
        cnt_ref[...] = jnp.zeros(cnt_ref.shape, F32)

    scores = _sigmoid(lg_ref[...])
    sel = scores + bias_ref[...]
    E, tm = sel.shape
    per_group = E // N_GROUPS
    neg = -jnp.inf
    eid = lax.broadcasted_iota(jnp.int32, (E, tm), 0).astype(F32)
    eid_g = lax.broadcasted_iota(jnp.int32, (per_group, tm), 0).astype(F32)

    def first_argmax(x, ids, sentinel):
        m = jnp.max(x, axis=0, keepdims=True)
        i = jnp.min(jnp.where(x == m, ids, sentinel), axis=0, keepdims=True)
        return m, i

    parts, gscore = [], []
    for g in range(N_GROUPS):
        x = sel[g * per_group:(g + 1) * per_group]
        parts.append(x)
        m1, i1 = first_argmax(x, eid_g, float(per_group))
        m2 = jnp.max(jnp.where(eid_g == i1, neg, x), axis=0, keepdims=True)
        gscore.append(m1 + m2)
    kept = []
    for g in range(N_GROUPS):
        beaten = jnp.zeros((1, tm), F32)
        for o in range(N_GROUPS):
            if o == g:
                continue
            wins = (gscore[o] >= gscore[g]) if o < g else (gscore[o] > gscore[g])
            beaten = beaten + wins.astype(F32)
        kept.append(jnp.where(beaten < TOPK_GROUPS, parts[g], neg))
    sel = jnp.concatenate(kept, axis=0)

    ids, vals = [], []
    w_sum = jnp.zeros((1, tm), F32)
    chosen = jnp.zeros((E, tm), F32)
    for k in range(TOP_K):
        _, i = first_argmax(sel, eid, float(E))
        hit = eid == i
        val = jnp.sum(jnp.where(hit, scores, 0.0), axis=0, keepdims=True)
        sel = jnp.where(hit, neg, sel)
        chosen = jnp.where(hit, 1.0, chosen)
        ids.append(i)
        vals.append(val)
        w_sum = w_sum + val
    idx_ref[...] = jnp.concatenate(ids, axis=0).astype(jnp.int32)
    w_ref[...] = jnp.concatenate(vals, axis=0) / w_sum * ROUTED_SCALE
    part = chosen[:, 0:HEAD_LANES]
    for l0 in range(HEAD_LANES, tm, HEAD_LANES):
        part = part + chosen[:, l0:l0 + HEAD_LANES]
    cnt_ref[0] = cnt_ref[0] + part


def route(logits_t, router_bias, group_tokens):
    E, N = logits_t.shape
    tm = min(512, N)
    tpg = group_tokens // tm
    idx, w, cnt = pl.pallas_call(
        functools.partial(_route_kernel, tiles_per_group=tpg),
        out_shape=[jax.ShapeDtypeStruct((TOP_K, N), jnp.int32), jax.ShapeDtypeStruct((TOP_K, N), F32),
                   jax.ShapeDtypeStruct((N // group_tokens, E, HEAD_LANES), F32)],
        grid=(N // tm,),
        in_specs=[pl.BlockSpec((E, tm), lambda i: (0, i)), pl.BlockSpec((E, 1), lambda i: (0, 0))],
        out_specs=[pl.BlockSpec((TOP_K, tm), lambda i: (0, i)), pl.BlockSpec((TOP_K, tm), lambda i: (0, i)),
                   pl.BlockSpec((1, E, HEAD_LANES), lambda i, tpg=tpg: (i // tpg, 0, 0))],
        compiler_params=pltpu.CompilerParams(dimension_semantics=("arbitrary",)),
        name="route",
    )(logits_t, router_bias.reshape(E, 1))
    return idx, w, jnp.sum(cnt, axis=-1).astype(jnp.int32)


def _loop_unrolled(n, body):
    u = MOE_LOOP_UNROLL
    lax.fori_loop(0, n // u, functools.partial(body, count=u), 0)
    lax.fori_loop(n // u * u, n, functools.partial(body, count=1), 0)


def _moe_kernel(off_ref, tok_ref, wl_ref, x_ref, wg_ref, wu_ref, wd_ref, acc_ref, xg_ref, yb_ref,
                *, R, E, per_group, experts_per_step):
    g = pl.program_id(0)
    step = pl.program_id(1)

    @pl.when(step == 0)
    def _():
        acc_ref[...] = jnp.zeros(acc_ref.shape, F32)

    @pl.when((g == 0) & (step == 0))
    def _():
        xg_ref[...] = jnp.zeros(xg_ref.shape, F32)

    U = SUBLANES
    chunks = U
    lanes = x_ref.shape[-1]

    def token_tile(ref, t8):
        return ref.at[0, pl.ds(pl.multiple_of(t8, U), U), :]

    def gather(le, s0, nr):
        def body(j, _, count):
            for gg in range(count):
                base = s0 + (j * count + gg) * U
                for i in range(U):
                    xg_ref[le, j * count + gg, pl.ds(i, chunks, stride=U), :] = (
                        token_tile(x_ref, tok_ref[base + i])[...])
            return 0

        whole = jnp.minimum((nr + U - 1) // U, (per_group - s0) // U)
        _loop_unrolled(whole, body)

        def single(r, _):
            xg_ref[le, r // U, pl.ds(r % U, chunks, stride=U), :] = token_tile(x_ref, tok_ref[s0 + r])[...]
            return 0

        lax.fori_loop(whole * U, nr, single, 0)

    def load_block(le):
        return jnp.concatenate([xg_ref[le, :, c * U:(c + 1) * U, :].reshape(R, lanes) for c in range(chunks)],
                               axis=1).astype(BF16)

    def store_block(le, y):
        for c in range(chunks):
            yb_ref[le, :, c * U:(c + 1) * U, :] = y[:, c * lanes:(c + 1) * lanes].reshape(R // U, U, lanes)

    def scatter(le, s0, nr):
        def group(j, _, count):
            base = s0 + j * count * U
            rows = range(count * U)
            tiles = [token_tile(acc_ref, tok_ref[base + i]) for i in rows]
            wts = [wl_ref[base + i] for i in rows]
            new = [tiles[i][...] + wts[i] * yb_ref[le, j * count + i // U, pl.ds(i % U, chunks, stride=U), :]
                   for i in rows]
            for i in rows:
                tiles[i][...] = new[i]
            return 0

        groups = nr // U
        _loop_unrolled(groups, group)

        def single(r, _):
            tile = token_tile(acc_ref, tok_ref[s0 + r])
            row = yb_ref[le, r // U, pl.ds(r % U, chunks, stride=U), :]
            tile[...] = tile[...] + wl_ref[s0 + r] * row
            return 0

        lax.fori_loop(groups * U, nr, single, 0)

    segments = []
    for le in range(experts_per_step):
        segment = g * E + step * experts_per_step + le
        seg = off_ref[segment]
        segments.append((seg - g * per_group, off_ref[segment + 1] - seg))

    for le, (start, cnt) in enumerate(segments):
        gather(le, start, jnp.minimum(R, cnt))
    xs = [load_block(le) for le in range(experts_per_step)]
    gu = [(_dot(xs[le], wg_ref[le].astype(BF16)), _dot(xs[le], wu_ref[le].astype(BF16)))
          for le in range(experts_per_step)]
    hs = [(_silu(gate) * up).astype(BF16) for gate, up in gu]
    for le in range(experts_per_step):
        store_block(le, _dot(hs[le], wd_ref[le].astype(BF16)))
    for le, (start, cnt) in enumerate(segments):
        scatter(le, start, jnp.minimum(R, cnt))

    for le, (start, cnt) in enumerate(segments):
        def more(sb, _, le=le, start=start, cnt=cnt):
            s0 = start + sb * R
            nr = jnp.minimum(R, cnt - sb * R)
            gather(le, s0, nr)
            x = load_block(le)
            hmid = _silu(_dot(x, wg_ref[le].astype(BF16))) * _dot(x, wu_ref[le].astype(BF16))
            store_block(le, _dot(hmid.astype(BF16), wd_ref[le].astype(BF16)))
            scatter(le, s0, nr)
            return 0

        lax.fori_loop(1, (cnt + R - 1) // R, more, 0)


def routed_experts(h2, top_idx, top_w, counts, exp_gate, exp_up, exp_down):
    N, chunks, lanes = h2.shape
    D = chunks * lanes
    E, _, F = exp_gate.shape
    TG = min(MOE_TOKEN_GROUP, N)
    G = N // TG
    per_group = TG * TOP_K
    key = (jnp.arange(N, dtype=jnp.int32)[None, :] // TG) * E + top_idx
    order = jnp.argsort(key.reshape(-1))
    assert chunks == SUBLANES
    tok_s = (((order % N) % TG) * chunks).astype(jnp.int32)
    w_s = top_w.reshape(-1)[order]
    off = jnp.concatenate([jnp.zeros((1,), jnp.int32), jnp.cumsum(counts.reshape(-1)).astype(jnp.int32)])

    EPS = MOE_EXPERTS_PER_STEP
    grid_spec = pltpu.PrefetchScalarGridSpec(
        num_scalar_prefetch=1,
        grid=(G, E // EPS),
        in_specs=[pl.BlockSpec((per_group,), lambda g, e, off: (g,), memory_space=pltpu.SMEM),
                  pl.BlockSpec((per_group,), lambda g, e, off: (g,), memory_space=pltpu.SMEM),
                  pl.BlockSpec((1, TG * chunks, lanes), lambda g, e, off: (g, 0, 0),
                               pipeline_mode=pl.Buffered(1)),
                  pl.BlockSpec((EPS, D, F), lambda g, e, off: (e, 0, 0)),
                  pl.BlockSpec((EPS, D, F), lambda g, e, off: (e, 0, 0)),
                  pl.BlockSpec((EPS, F, D), lambda g, e, off: (e, 0, 0))],
        out_specs=pl.BlockSpec((1, TG * chunks, lanes), lambda g, e, off: (g, 0, 0),
                               pipeline_mode=pl.Buffered(1)),
        scratch_shapes=[pltpu.VMEM((EPS, MOE_ROWS // SUBLANES, chunks * SUBLANES, lanes), F32),
                        pltpu.VMEM((EPS, MOE_ROWS // SUBLANES, chunks * SUBLANES, lanes), F32)],
    )
    out = pl.pallas_call(
        functools.partial(_moe_kernel, R=MOE_ROWS, E=E, per_group=per_group, experts_per_step=EPS),
        out_shape=jax.ShapeDtypeStruct((G, TG * chunks, lanes), F32),
        grid_spec=grid_spec,
        compiler_params=pltpu.CompilerParams(
            dimension_semantics=("arbitrary", "arbitrary"), vmem_limit_bytes=MOE_VMEM_LIMIT),
        name="routed_experts",
    )(off, tok_s, w_s, h2.reshape(G, TG * chunks, lanes), exp_gate, exp_up, exp_down)
    return out.reshape(N, chunks, lanes)


def _final_kernel(x1_ref, h2_ref, rt_ref, mod_ref, wg_ref, wu_ref, wd_ref, *rest, final):
    if final:
        fg_ref, o_ref = rest
    else:
        (o_ref,) = rest
    hb = _load_row_tiles(h2_ref.at[0]).astype(BF16)
    hmid = _silu(_dot(hb, wg_ref[...])) * _dot(hb, wu_ref[...])
    shared = _dot(hmid.astype(BF16), wd_ref[...])
    x2 = x1_ref[0] + mod_ref[0, 5:6, :] * (_load_row_tiles(rt_ref.at[0]) + shared)
    if final:
        x2 = _rms(x2, fg_ref[...])
    o_ref[0] = x2


def shared_and_residual(x1, h2, routed, mod, sh_gate, sh_up, sh_down, final_gain=None):
    B, T, D = x1.shape
    F = sh_gate.shape[1]
    tm = min(512, T)
    final = final_gain is not None
    tile = pl.BlockSpec((1, tm, D), lambda b, i: (b, i, 0))
    row_tiles = pl.BlockSpec((1, tm, D // HEAD_LANES, HEAD_LANES), lambda b, i: (b, i, 0, 0))
    in_specs = [tile, row_tiles, row_tiles,
                pl.BlockSpec((1, 6, D), lambda b, i: (b, 0, 0)),
                pl.BlockSpec((D, F), lambda b, i: (0, 0)),
                pl.BlockSpec((D, F), lambda b, i: (0, 0)),
                pl.BlockSpec((F, D), lambda b, i: (0, 0))]
    args = [x1, h2, routed, mod, sh_gate, sh_up, sh_down]
    if final:
        in_specs.append(pl.BlockSpec((1, D), lambda b, i: (0, 0)))
        args.append(final_gain.reshape(1, D))
    return pl.pallas_call(
        functools.partial(_final_kernel, final=final),
        out_shape=jax.ShapeDtypeStruct((B, T, D), F32),
        grid=(B, T // tm),
        in_specs=in_specs,
        out_specs=tile,
        compiler_params=pltpu.CompilerParams(
            dimension_semantics=("arbitrary", "arbitrary"), vmem_limit_bytes=VMEM_LIMIT),
        name="shared_and_residual",
    )(*args)


def moe_block(x1, h2, logits, mod, router_bias, exp_gate, exp_up, exp_down, sh_gate, sh_up, sh_down,
              final_gain=None):
    B, T, D = x1.shape
    top_idx, top_w, counts = route(logits, router_bias, min(MOE_TOKEN_GROUP, B * T))
    chunks = D // HEAD_LANES
    routed = routed_experts(h2.reshape(B * T, chunks, HEAD_LANES), top_idx, top_w, counts,
                            exp_gate, exp_up, exp_down)
    return shared_and_residual(x1, h2, routed.reshape(B, T, chunks, HEAD_LANES), mod,
                               sh_gate.astype(BF16), sh_up.astype(BF16), sh_down.astype(BF16), final_gain)


def kernel(x, c, hgrn_lb_logits, l0_norm1, l0_norm2, l0_w_mod, l0_b_mod, l0_w_in, l0_w_out, l0_hgrn_norm, l0_diff_lq1, l0_diff_lk1, l0_diff_lq2, l0_diff_lk2, l0_diff_subln, l0_router, l0_router_bias, l0_exp_gate, l0_exp_up, l0_exp_down, l0_sh_gate, l0_sh_up, l0_sh_down, l1_norm1, l1_norm2, l1_w_mod, l1_b_mod, l1_w_in, l1_w_out, l1_conv_w, l1_conv_b, l1_ml_i_bias, l1_ml_f_bias, l1_ml_norm, l1_na_rpb, l1_router, l1_router_bias, l1_exp_gate, l1_exp_up, l1_exp_down, l1_sh_gate, l1_sh_up, l1_sh_down, final_norm):
    G = GROUP_WIDTH
    lb_all = jnp.cumsum(jax.nn.softmax(hgrn_lb_logits.astype(F32), axis=0), axis=0)
    layer_idx = 0
    lambda_init = 0.8 - 0.6 * math.exp(-0.3 * layer_idx)
    lam = (jnp.exp(jnp.sum(l0_diff_lq1.astype(F32) * l0_diff_lk1.astype(F32)))
           - jnp.exp(jnp.sum(l0_diff_lq2.astype(F32) * l0_diff_lk2.astype(F32))) + lambda_init)

    mod0 = ada_mod(c, l0_w_mod, l0_b_mod)
    y0 = in_proj(x, mod0, l0_norm1, l0_w_in.astype(BF16))
    a_out = hgrn2(y0, lb_all[0], l0_hgrn_norm)
    b_out = diff_attention(y0, lam, l0_diff_subln, layer_idx)
    x1, h2, logits = out_proj(a_out, b_out, x, mod0, l0_norm2, l0_w_out.astype(BF16), l0_router)
    xa = moe_block(x1, h2, logits, mod0, l0_router_bias, l0_exp_gate, l0_exp_up, l0_exp_down,
                   l0_sh_gate, l0_sh_up, l0_sh_down)

    mod1 = ada_mod(c, l1_w_mod, l1_b_mod)
    n_gate = l1_w_in.shape[1] - 7 * G
    w_main = jnp.concatenate([l1_w_in[:, :4 * G], l1_w_in[:, 4 * G + n_gate:]], axis=1).astype(BF16)
    w_gate = jnp.pad(l1_w_in[:, 4 * G:4 * G + n_gate], ((0, 0), (0, HEAD_LANES - n_gate)))
    y1, gates = in_proj(xa, mod1, l1_norm1, w_main, w_gate)
    c_out = mlstm(y1, gates, l1_conv_w, l1_conv_b, l1_ml_i_bias, l1_ml_f_bias, l1_ml_norm)
    d_out = neighbourhood_attention(y1, l1_na_rpb)
    x1, h2, logits = out_proj(c_out, d_out, xa, mod1, l1_norm2, l1_w_out.astype(BF16), l1_router)
    return moe_block(x1, h2, logits, mod1, l1_router_bias, l1_exp_gate, l1_exp_up, l1_exp_down,
                     l1_sh_gate, l1_sh_up, l1_sh_down, final_gain=final_norm)
```

```python
import functools
import math

import numpy as np
import jax
import jax.numpy as jnp
from jax import lax
from jax.experimental import pallas as pl
from jax.experimental.pallas import tpu as pltpu

F32 = jnp.float32
BF16 = jnp.bfloat16
HIGHEST = lax.Precision.HIGHEST
EPS = 1e-6

GRID_W = 64
GROUP_WIDTH = 512
HEAD_LANES = 128
HG_CHUNK = 64
ML_CHUNK = 128
STEP_CHUNKS = 4
ML_CONV = 5
DA_DIM = 64
ROPE_DIM = 16
ROPE_THETA = 500000.0
NA_ROWS = 8
NA_COLS = 16
NA_DIM = 64
NA_STEP_ROWS = 8
N_EXPERTS = 128
TOP_K = 8
N_GROUPS = 8
TOPK_GROUPS = 4
ROUTED_SCALE = 2.5
MOE_ROWS = 320
MOE_TOKEN_GROUP = 4096
MOE_EXPERTS_PER_STEP = 2
MOE_LOOP_UNROLL = 2
SUBLANES = 8
NEG_BIG = -1e30
VMEM_LIMIT = 48 * 1024 * 1024
MOE_VMEM_LIMIT = 56 * 1024 * 1024


def _dot(a, b, **kw):
    return jnp.dot(a, b, preferred_element_type=F32, **kw)


def _dot_nt(a, b):
    return lax.dot_general(a, b, (((1,), (1,)), ((), ())), preferred_element_type=F32)


def _dot_tn(a, b):
    return lax.dot_general(a, b, (((0,), (0,)), ((), ())), preferred_element_type=F32)


def _sigmoid(x):
    return jax.nn.sigmoid(x)


def _silu(x):
    return x * jax.nn.sigmoid(x)


def _log_sigmoid(x):
    return jnp.minimum(x, 0.0) - jnp.log(1.0 + jnp.exp(-jnp.abs(x)))


def _rms(x, gain):
    return x * lax.rsqrt(jnp.mean(x * x, axis=-1, keepdims=True) + EPS) * gain


def _store_row_tiles(ref, val):
    for c in range(ref.shape[1]):
        ref[:, c, :] = val[:, c * HEAD_LANES:(c + 1) * HEAD_LANES]


def _load_row_tiles(ref):
    return jnp.concatenate([ref[:, c, :] for c in range(ref.shape[1])], axis=1)


def _mod_kernel(c_ref, w_ref, b_ref, o_ref):
    o_ref[...] = _dot(_silu(c_ref[...]), w_ref[...], precision=HIGHEST) + b_ref[...]


def ada_mod(c, w_mod, b_mod):
    B, D = c.shape
    N = w_mod.shape[1]
    tn = 1024
    out = pl.pallas_call(
        _mod_kernel,
        out_shape=jax.ShapeDtypeStruct((B, N), F32),
        grid=(N // tn,),
        in_specs=[pl.BlockSpec((B, D), lambda j: (0, 0)),
                  pl.BlockSpec((D, tn), lambda j: (0, j)),
                  pl.BlockSpec((1, tn), lambda j: (0, j))],
        out_specs=pl.BlockSpec((B, tn), lambda j: (0, j)),
        name="ada_mod",
    )(c, w_mod, b_mod.reshape(1, N))
    return out.reshape(B, 6, D)


def _in_kernel(x_ref, mod_ref, gain_ref, w_ref, *rest, has_gate):
    if has_gate:
        wg_ref, o_ref, og_ref, h_ref = rest
    else:
        o_ref, h_ref = rest

    @pl.when(pl.program_id(2) == 0)
    def _():
        h = _rms(x_ref[0], gain_ref[...]) * (1.0 + mod_ref[0, 1:2, :]) + mod_ref[0, 0:1, :]
        h_ref[...] = h.astype(BF16)
        if has_gate:
            og_ref[0] = _dot(h, wg_ref[...], precision=HIGHEST)

    o_ref[0] = _dot(h_ref[...], w_ref[...]).astype(o_ref.dtype)


def in_proj(x, mod, gain, w_bf16, w_gate=None):
    B, T, D = x.shape
    N = w_bf16.shape[1]
    tm = min(1024, T)
    tn = 512
    has_gate = w_gate is not None
    in_specs = [pl.BlockSpec((1, tm, D), lambda b, i, j: (b, i, 0)),
                pl.BlockSpec((1, 6, D), lambda b, i, j: (b, 0, 0)),
                pl.BlockSpec((1, D), lambda b, i, j: (0, 0)),
                pl.BlockSpec((D, tn), lambda b, i, j: (0, j))]
    out_shape = [jax.ShapeDtypeStruct((B, T, N), BF16)]
    out_specs = [pl.BlockSpec((1, tm, tn), lambda b, i, j: (b, i, j))]
    args = [x, mod, gain.reshape(1, D), w_bf16]
    if has_gate:
        in_specs.append(pl.BlockSpec((D, HEAD_LANES), lambda b, i, j: (0, 0)))
        out_shape.append(jax.ShapeDtypeStruct((B, T, HEAD_LANES), F32))
        out_specs.append(pl.BlockSpec((1, tm, HEAD_LANES), lambda b, i, j: (b, i, 0)))
        args.append(w_gate)
    res = pl.pallas_call(
        functools.partial(_in_kernel, has_gate=has_gate),
        out_shape=out_shape,
        grid=(B, T // tm, N // tn),
        in_specs=in_specs,
        out_specs=out_specs,
        scratch_shapes=[pltpu.VMEM((tm, D), BF16)],
        compiler_params=pltpu.CompilerParams(
            dimension_semantics=("arbitrary", "arbitrary", "arbitrary")),
        name="in_proj",
    )(*args)
    return res if has_gate else res[0]


def _hgrn_consts(C):
    t = np.arange(C)
    tri = (t[None, :] <= t[:, None]).astype(np.float32)
    triT = np.ascontiguousarray(tri.T)
    wf, wb, mf = [tri], [triT], []
    levels = int(round(math.log2(C)))
    for l in range(levels):
        size = C >> l
        blk = t // size
        r = blk * size + size // 2
        wf.append(tri - tri[r - 1])
        wb.append(triT - triT[r])
        upper = (t % size) >= size // 2
        mf.append(((blk[:, None] == blk[None, :]) & upper[:, None] & (~upper)[None, :]).astype(np.float32))
    mf.append(np.eye(C, dtype=np.float32))
    ones = np.ones((8, C), np.float32)
    wf.append(ones)
    wb.append(ones)
    mf = np.stack(mf)
    mb = np.ascontiguousarray(np.transpose(mf, (0, 2, 1)))
    return np.concatenate(wf), np.concatenate(wb), mf, mb


def _split2(x):
    hi = x.astype(BF16)
    lo = (x - hi.astype(F32)).astype(BF16)
    return hi, lo


def _hgrn_kernel(q_ref, i_ref, ff_ref, fb_ref, g_ref, lb_ref, gain_ref, wf_ref, wb_ref, mf_ref, mb_ref,
                 o_ref, of_ref, ob_ref, *, C, T):
    n = T // C
    levels = int(round(math.log2(C)))
    dv = q_ref.shape[-1]

    def prepare(c, f_ref, lbd, w_ref):
        sl = pl.ds(pl.multiple_of(c * C, C), C)
        q = _silu(q_ref[0, sl, :].astype(F32))
        v = i_ref[0, sl, :]
        fg = lbd + (1.0 - lbd) * _sigmoid(f_ref[0, sl, :].astype(F32))
        lf = jnp.log(fg)
        d2 = _dot(w_ref[...], jnp.concatenate(_split2(lf), axis=1))
        dall = d2[:, 0:dv] + d2[:, dv:2 * dv]
        return dict(sl=sl, q=q, k=1.0 - fg, v=v.astype(BF16), dall=dall)

    def scores(p, m_ref):
        q, k, dall = p["q"], p["k"], p["dall"]
        attn = m_ref[levels] * _dot_nt(q.astype(BF16), k.astype(BF16))
        for l in range(levels):
            e = jnp.exp(-jnp.abs(dall[(l + 1) * C:(l + 2) * C]))
            attn = attn + m_ref[l] * _dot_nt((q * e).astype(BF16), (k * e).astype(BF16))
        return attn.astype(BF16)

    st0 = jnp.zeros((dv, dv), F32)

    def step(j, carry):
        states = list(carry)
        work = []
        for u in range(STEP_CHUNKS):
            work.append((0, prepare(STEP_CHUNKS * j + u, ff_ref, lb_ref[0:1, :], wf_ref), mf_ref, of_ref))
            work.append((1, prepare(n - 1 - STEP_CHUNKS * j - u, fb_ref, lb_ref[1:2, :], wb_ref), mb_ref, ob_ref))
        attn = [scores(p, m_ref) for _, p, m_ref, _ in work]
        local = []
        for (_, p, _, _), a in zip(work, attn):
            cum = p["dall"][0:C]
            tot = p["dall"][(levels + 1) * C:(levels + 1) * C + 1]
            kt = p["k"] * jnp.exp(tot - cum)
            local.append((_dot(a, p["v"]), _dot_tn(p["v"], kt.astype(BF16)), cum, tot))
        for (d, p, _, out_ref), (o_in, incr, cum, tot) in zip(work, local):
            st = states[d]
            out_ref[p["sl"], :] = o_in + _dot_nt((p["q"] * jnp.exp(cum)).astype(BF16), st.astype(BF16))
            states[d] = st * jnp.exp(tot) + incr
        return tuple(states)

    lax.fori_loop(0, n // STEP_CHUNKS, step, (st0, st0))

    rt = min(512, T)
    for r0 in range(0, T, rt):
        sl = slice(r0, r0 + rt)
        o_ref[0, sl, :] = _rms(of_ref[sl, :] + ob_ref[sl, :], gain_ref[...]) * _silu(g_ref[0, sl, :].astype(F32))


def hgrn2(y, lb, norm_gain):
    B, T, _ = y.shape
    H = GROUP_WIDTH // HEAD_LANES
    C = HG_CHUNK
    wf, wb, mf, mb = _hgrn_consts(C)
    wf, wb = jnp.asarray(wf, BF16), jnp.asarray(wb, BF16)
    mf, mb = jnp.asarray(mf), jnp.asarray(mb)

    def col(group):
        return pl.BlockSpec((1, T, HEAD_LANES), lambda b, h, group=group: (b, 0, group * H + h))

    def const(a):
        nd = a.ndim
        return pl.BlockSpec(a.shape, lambda b, h, nd=nd: (0,) * nd)

    return pl.pallas_call(
        functools.partial(_hgrn_kernel, C=C, T=T),
        out_shape=jax.ShapeDtypeStruct((B, T, GROUP_WIDTH), F32),
        grid=(B, H),
        in_specs=[col(0), col(1), col(2), col(3), col(4),
                  pl.BlockSpec((2, HEAD_LANES), lambda b, h: (0, h)),
                  pl.BlockSpec((1, HEAD_LANES), lambda b, h: (0, 0)),
                  const(wf), const(wb), const(mf), const(mb)],
        out_specs=pl.BlockSpec((1, T, HEAD_LANES), lambda b, h: (b, 0, h)),
        scratch_shapes=[pltpu.VMEM((T, HEAD_LANES), F32), pltpu.VMEM((T, HEAD_LANES), F32)],
        compiler_params=pltpu.CompilerParams(
            dimension_semantics=("arbitrary", "arbitrary"), vmem_limit_bytes=VMEM_LIMIT),
        name="hgrn2",
    )(y, y, y, y, y, lb, norm_gain.reshape(1, HEAD_LANES), wf, wb, mf, mb)


def _rope_tables(T):
    pos = np.arange(T, dtype=np.float32)
    inv_freq = (ROPE_THETA ** (-np.arange(0, ROPE_DIM, 2, dtype=np.float32) / ROPE_DIM)).astype(np.float32)
    ang = pos[:, None] * inv_freq[None, :]
    cos, sin = np.cos(ang), np.sin(ang)
    half = ROPE_DIM // 2
    c = np.ones((T, HEAD_LANES), np.float32)
    s_prev = np.zeros((T, HEAD_LANES), np.float32)
    s_next = np.zeros((T, HEAD_LANES), np.float32)
    for base in range(0, HEAD_LANES, DA_DIM):
        c[:, base:base + half] = cos
        c[:, base + half:base + ROPE_DIM] = cos
        s_next[:, base:base + half] = -sin
        s_prev[:, base + half:base + ROPE_DIM] = sin
    return jnp.asarray(c), jnp.asarray(s_prev), jnp.asarray(s_next)


def _rope(x, c, s_prev, s_next):
    half = ROPE_DIM // 2
    lanes = x.shape[-1]
    return (x * c + pltpu.roll(x, half, axis=1) * s_prev
            + pltpu.roll(x, lanes - half, axis=1) * s_next)


def _diff_kernel(lam_ref, q_ref, k_ref, v_ref, cq_ref, spq_ref, snq_ref, ck_ref, spk_ref, snk_ref,
                 subln_ref, o_ref, kr_ref, vb_ref, *, T, out_scale):
    rt = min(512, T)

    @pl.when(pl.program_id(2) == 0)
    def _():
        for r0 in range(0, T, rt):
            sl = slice(r0, r0 + rt)
            kr_ref[sl, :] = _rope(k_ref[0, sl, :].astype(F32), ck_ref[sl, :], spk_ref[sl, :], snk_ref[sl, :]).astype(BF16)
            vb_ref[sl, 0:HEAD_LANES] = v_ref[0, sl, :].astype(BF16)
            lane = lax.broadcasted_iota(jnp.int32, (rt, HEAD_LANES), 1)
            vb_ref[sl, HEAD_LANES:2 * HEAD_LANES] = jnp.where(lane == 0, 1.0, 0.0).astype(BF16)

    q = _rope(q_ref[0].astype(F32), cq_ref[...], spq_ref[...], snq_ref[...]) * (DA_DIM ** -0.5 * math.log2(math.e))
    lam = lam_ref[0, 0]
    v = vb_ref[...]
    map_of_lane = lax.broadcasted_iota(jnp.int32, q.shape, 1) // DA_DIM

    scores = [_dot_nt(jnp.where(map_of_lane == m, q, 0.0).astype(BF16), kr_ref[...]) for m in range(2)]
    probs = [jnp.exp2(s - jnp.max(s, axis=-1, keepdims=True)).astype(BF16) for s in scores]
    pv = []
    for e in probs:
        full = _dot(e, v)
        pv.append(full[:, 0:HEAD_LANES] / full[:, HEAD_LANES:HEAD_LANES + 1])
    o = pv[0] - lam * pv[1]
    o_ref[0] = _rms(o, subln_ref[...]) * out_scale


def diff_attention(y, lam, subln, layer_idx):
    B, T, _ = y.shape
    H = GROUP_WIDTH // HEAD_LANES
    tq = min(256, T)
    lambda_init = 0.8 - 0.6 * math.exp(-0.3 * layer_idx)
    c, sp, sn = _rope_tables(T)

    def col(group, rows):
        if rows == T:
            return pl.BlockSpec((1, T, HEAD_LANES), lambda b, h, i, group=group: (b, 0, group * H + h))
        return pl.BlockSpec((1, rows, HEAD_LANES), lambda b, h, i, group=group: (b, i, group * H + h))

    tab_q = pl.BlockSpec((tq, HEAD_LANES), lambda b, h, i: (i, 0))
    tab_k = pl.BlockSpec((T, HEAD_LANES), lambda b, h, i: (0, 0))
    return pl.pallas_call(
        functools.partial(_diff_kernel, T=T, out_scale=1.0 - lambda_init),
        out_shape=jax.ShapeDtypeStruct((B, T, GROUP_WIDTH), F32),
        grid=(B, H, T // tq),
        in_specs=[pl.BlockSpec(memory_space=pltpu.SMEM),
                  col(5, tq), col(6, T), col(7, T),
                  tab_q, tab_q, tab_q, tab_k, tab_k, tab_k,
                  pl.BlockSpec((1, HEAD_LANES), lambda b, h, i: (0, 0))],
        out_specs=pl.BlockSpec((1, tq, HEAD_LANES), lambda b, h, i: (b, i, h)),
        scratch_shapes=[pltpu.VMEM((T, HEAD_LANES), BF16), pltpu.VMEM((T, 2 * HEAD_LANES), BF16)],
        compiler_params=pltpu.CompilerParams(
            dimension_semantics=("arbitrary", "arbitrary", "arbitrary"), vmem_limit_bytes=VMEM_LIMIT),
        name="diff_attention",
    )(lam.reshape(1, 1), y, y, y, c, sp, sn, c, sp, sn, subln.reshape(1, HEAD_LANES))


def _mlstm_kernel(q_ref, k_ref, v_ref, og_ref, gt_ref, cwq_ref, cwk_ref, cbq_ref, cbk_ref, gbias_ref,
                  gain_ref, tri_ref, o_ref, qc_ref, kc_ref, xp_ref, gx_ref, hf_ref, hb_ref, gxt_ref, va_ref,
                  vta_ref, *, C, T, dk):
    n = T // C
    head = pl.program_id(1)
    pad = 8
    half = ML_CONV // 2
    rt = min(512, T)

    xp_ref[0:pad, :] = jnp.zeros((pad, dk), F32)
    xp_ref[pad + T:pad + T + pad, :] = jnp.zeros((pad, dk), F32)
    for src, cw_ref, cb_ref, dst, scale in ((q_ref, cwq_ref, cbq_ref, qc_ref, 1.0),
                                            (k_ref, cwk_ref, cbk_ref, kc_ref, dk ** -0.5)):
        xp_ref[pad:pad + T, :] = src[0].astype(F32)
        for r0 in range(0, T, rt):
            acc = jnp.zeros((rt, dk), F32) + cb_ref[...]
            for j in range(ML_CONV):
                acc = acc + xp_ref[pad + r0 + j - half:pad + r0 + j - half + rt, :] * cw_ref[j:j + 1, :]
            dst[r0:r0 + rt, :] = _silu(acc) * scale

    lane = lax.broadcasted_iota(jnp.int32, (rt, HEAD_LANES), 1)
    is_f = (lane % 8) >= 4
    for r0 in range(0, T, rt):
        g = gt_ref[0, r0:r0 + rt, :] + gbias_ref[...]
        p = jnp.where(is_f, _log_sigmoid(g), g)
        x = jnp.zeros((rt, HEAD_LANES), F32)
        for j, src_lane in enumerate((0, 4, 8, 12)):
            colv = jnp.sum(jnp.where(lane == src_lane + head, p, 0.0), axis=1, keepdims=True)
            x = jnp.where(lane == j, colv, x)
        gx_ref[r0:r0 + rt, :] = x
        gxt_ref[:, r0:r0 + rt] = x.T[0:SUBLANES, :]
        v = v_ref[0, r0:r0 + rt, :].astype(F32)
        va_ref[r0:r0 + rt, 0:dk] = v.astype(BF16)
        va_ref[r0:r0 + rt, dk:2 * dk] = jnp.where(lane == 0, 1.0, 0.0).astype(BF16)
        vta_ref[0:dk, r0:r0 + rt] = v.T.astype(BF16)
        sub = lax.broadcasted_iota(jnp.int32, (dk, rt), 0)
        vta_ref[dk:2 * dk, r0:r0 + rt] = jnp.where(sub == 0, 1.0, 0.0).astype(BF16)

    row = lax.broadcasted_iota(jnp.int32, (C, C), 0)
    colm = lax.broadcasted_iota(jnp.int32, (C, C), 1)

    init = (jnp.zeros((2 * dk, dk), F32), jnp.full((1, 1), NEG_BIG, F32))

    def step(j, carry):
        carries = list(carry)
        work = []
        for u in range(STEP_CHUNKS):
            work.append((0, STEP_CHUNKS * j + u, hf_ref))
            work.append((1, n - 1 - STEP_CHUNKS * j - u, hb_ref))
        chunks = []
        for d, c, out_ref in work:
            sl = pl.ds(pl.multiple_of(c * C, C), C)
            x = gx_ref[sl, :]
            xr = gxt_ref[:, sl]
            chunks.append(dict(d=d, sl=sl, out=out_ref, x=x, xr=xr, q=qc_ref[sl, :], k=kc_ref[sl, :],
                               cumx=_dot(tri_ref[d], x, precision=HIGHEST),
                               cumr=_dot(xr, tri_ref[1 - d], precision=HIGHEST)))
        for p in chunks:
            p["qk"] = _dot_nt(p["q"].astype(BF16), p["k"].astype(BF16))
        for p in chunks:
            d, x, xr, cumx, cumr = p["d"], p["x"], p["xr"], p["cumx"], p["cumr"]
            mask = (colm <= row) if d == 0 else (colm >= row)
            ig_c = x[:, 2 * d:2 * d + 1]
            ig_r = xr[2 * d:2 * d + 1, :]
            cum_c = cumx[:, 2 * d + 1:2 * d + 2]
            cum_r = cumr[2 * d + 1:2 * d + 2, :]
            tot = jnp.sum(x[:, 2 * d + 1:2 * d + 2], axis=0, keepdims=True)
            dmat = jnp.where(mask, cum_c - cum_r + ig_r, -jnp.inf)
            dmax = jnp.max(dmat, axis=1, keepdims=True)
            a = p["qk"] * jnp.exp(dmat - dmax)
            ds = tot - cum_c + ig_c
            dsmax = jnp.max(ds, axis=0, keepdims=True)
            kw = p["k"] * jnp.exp(ds - dsmax)
            p.update(cum_c=cum_c, tot=tot, dmax=dmax, dsmax=dsmax, a=a.astype(BF16), kw=kw.astype(BF16))
        for p in chunks:
            p["num"] = _dot(p["a"], va_ref[p["sl"], :])
            p["upd"] = _dot(vta_ref[:, p["sl"]], p["kw"])
        for p in chunks:
            state, m = carries[p["d"]]
            g = p["cum_c"] + m
            mt = jnp.maximum(g, p["dmax"])
            full = (jnp.exp(g - mt) * _dot_nt(p["q"].astype(BF16), state.astype(BF16))
                    + jnp.exp(p["dmax"] - mt) * p["num"])
            den = full[:, dk:dk + 1]
            p["out"][p["sl"], :] = full[:, 0:dk] / jnp.maximum(jnp.abs(den), jnp.exp(-mt))
            m_new = jnp.maximum(p["tot"] + m, p["dsmax"])
            decay = jnp.exp(p["tot"] + m - m_new)
            scale = jnp.exp(p["dsmax"] - m_new)
            carries[p["d"]] = (decay * state + scale * p["upd"], m_new)
        return tuple(carries)

    lax.fori_loop(0, n // STEP_CHUNKS, step, (init, init))

    for r0 in range(0, T, rt):
        sl = slice(r0, r0 + rt)
        o_ref[0, sl, :] = _rms(hf_ref[sl, :] + hb_ref[sl, :], gain_ref[...]) * _sigmoid(og_ref[0, sl, :].astype(F32))


def mlstm(y, gates, conv_w, conv_b, i_bias, f_bias, norm_gain):
    B, T, _ = y.shape
    H = GROUP_WIDTH // HEAD_LANES
    C = min(ML_CHUNK, T)
    t = np.arange(C)
    tri = np.stack([(t[None, :] <= t[:, None]), (t[None, :] >= t[:, None])]).astype(np.float32)
    gbias = jnp.zeros((1, HEAD_LANES), F32)
    gbias = gbias.at[0, 0:4].set(i_bias[0]).at[0, 4:8].set(f_bias[0])
    gbias = gbias.at[0, 8:12].set(i_bias[1]).at[0, 12:16].set(f_bias[1])

    def col(group):
        return pl.BlockSpec((1, T, HEAD_LANES), lambda b, h, group=group: (b, 0, group * H + h))

    conv_q = pl.BlockSpec((ML_CONV, HEAD_LANES), lambda b, h: (0, h))
    conv_k = pl.BlockSpec((ML_CONV, HEAD_LANES), lambda b, h: (0, H + h))
    bias_q = pl.BlockSpec((1, HEAD_LANES), lambda b, h: (0, h))
    bias_k = pl.BlockSpec((1, HEAD_LANES), lambda b, h: (0, H + h))
    cb = conv_b.reshape(1, -1)
    return pl.pallas_call(
        functools.partial(_mlstm_kernel, C=C, T=T, dk=HEAD_LANES),
        out_shape=jax.ShapeDtypeStruct((B, T, GROUP_WIDTH), F32),
        grid=(B, H),
        in_specs=[col(0), col(1), col(2), col(3),
                  pl.BlockSpec((1, T, HEAD_LANES), lambda b, h: (b, 0, 0)),
                  conv_q, conv_k, bias_q, bias_k,
                  pl.BlockSpec((1, HEAD_LANES), lambda b, h: (0, 0)),
                  pl.BlockSpec((1, HEAD_LANES), lambda b, h: (0, 0)),
                  pl.BlockSpec((2, C, C), lambda b, h: (0, 0, 0))],
        out_specs=pl.BlockSpec((1, T, HEAD_LANES), lambda b, h: (b, 0, h)),
        scratch_shapes=[pltpu.VMEM((T, HEAD_LANES), F32), pltpu.VMEM((T, HEAD_LANES), F32),
                        pltpu.VMEM((T + 16, HEAD_LANES), F32), pltpu.VMEM((T, HEAD_LANES), F32),
                        pltpu.VMEM((T, HEAD_LANES), F32), pltpu.VMEM((T, HEAD_LANES), F32),
                        pltpu.VMEM((SUBLANES, T), F32), pltpu.VMEM((T, 2 * HEAD_LANES), BF16),
                        pltpu.VMEM((2 * HEAD_LANES, T), BF16)],
        compiler_params=pltpu.CompilerParams(
            dimension_semantics=("arbitrary", "arbitrary"), vmem_limit_bytes=VMEM_LIMIT),
        name="mlstm",
    )(y, y, y, y, gates, conv_w, conv_w, cb, cb, gbias, norm_gain.reshape(1, HEAD_LANES), jnp.asarray(tri))


def _na_bias_table(rpb, rows):
    kr = min(NA_ROWS, rows)
    c = np.arange(GRID_W)
    cstart = np.clip(c - NA_COLS // 2, 0, GRID_W - NA_COLS)
    kc = np.arange(GRID_W)
    valid = (kc[None, :] >= cstart[:, None]) & (kc[None, :] < cstart[:, None] + NA_COLS)
    coff = np.clip(kc[None, :] - c[:, None] + NA_COLS - 1, 0, 2 * NA_COLS - 2)
    di = np.arange(kr)
    i = np.arange(kr)
    roff = i[None, :] - di[:, None] + NA_ROWS - 1
    heads = rpb.shape[0]
    cols = jnp.where(jnp.asarray(valid)[None, None], rpb.astype(F32)[:, :, coff], NEG_BIG)
    tab = jnp.take(cols, jnp.asarray(roff.reshape(-1)), axis=1)
    tab = tab.reshape(heads, kr, kr, GRID_W, GRID_W).transpose(0, 1, 3, 2, 4)
    return tab.reshape(heads, kr, GRID_W, kr * GRID_W)


def _na_kernel(q_ref, k_ref, v_ref, bm_ref, o_ref, *, rows, kr):
    W = GRID_W
    heads_per_block = HEAD_LANES // NA_DIM

    head_of_lane = lax.broadcasted_iota(jnp.int32, (W, HEAD_LANES), 1) // NA_DIM

    def body(j, _):
        work = []
        for u in range(NA_STEP_ROWS):
            r = NA_STEP_ROWS * j + u
            rs = jnp.clip(r - kr // 2, 0, rows - kr)
            di = r - rs
            qs = pl.ds(pl.multiple_of(r * W, W), W)
            ks = pl.ds(pl.multiple_of(rs * W, W), kr * W)
            q = q_ref[0, qs, :].astype(F32) * (NA_DIM ** -0.5)
            kw = k_ref[0, ks, :].astype(BF16)
            for hh in range(heads_per_block):
                s = _dot_nt(jnp.where(head_of_lane == hh, q, 0.0).astype(BF16), kw) + bm_ref[hh, di]
                work.append((u, hh, qs, ks, s))
        probs = []
        for u, hh, qs, ks, s in work:
            e = jnp.exp(s - jnp.max(s, axis=-1, keepdims=True))
            probs.append((e.astype(BF16), jnp.sum(e, axis=-1, keepdims=True)))
        outs = {}
        for (u, hh, qs, ks, s), (e, l) in zip(work, probs):
            o = _dot(e, v_ref[0, ks, :].astype(BF16)) / l
            outs[u] = o if hh == 0 else jnp.where(head_of_lane == hh, o, outs[u])
            if hh == heads_per_block - 1:
                o_ref[0, qs, :] = outs[u]
        return 0

    lax.fori_loop(0, rows // NA_STEP_ROWS, body, 0)


def neighbourhood_attention(y, rpb):
    B, T, _ = y.shape
    rows = T // GRID_W
    kr = min(NA_ROWS, rows)
    HB = GROUP_WIDTH // HEAD_LANES
    hpb = HEAD_LANES // NA_DIM
    bm = _na_bias_table(rpb, rows)

    def col(group):
        return pl.BlockSpec((1, T, HEAD_LANES), lambda b, h, group=group: (b, 0, group * HB + h))

    return pl.pallas_call(
        functools.partial(_na_kernel, rows=rows, kr=kr),
        out_shape=jax.ShapeDtypeStruct((B, T, GROUP_WIDTH), F32),
        grid=(B, HB),
        in_specs=[col(4), col(5), col(6),
                  pl.BlockSpec((hpb, kr, GRID_W, kr * GRID_W), lambda b, h: (h, 0, 0, 0))],
        out_specs=pl.BlockSpec((1, T, HEAD_LANES), lambda b, h: (b, 0, h)),
        compiler_params=pltpu.CompilerParams(
            dimension_semantics=("arbitrary", "arbitrary"), vmem_limit_bytes=VMEM_LIMIT),
        name="neighbourhood_attention",
    )(y, y, y, bm)


def _out_kernel(a_ref, b_ref, x_ref, mod_ref, gain_ref, w_ref, r_ref, x1_ref, h2_ref, lg_ref):
    G = a_ref.shape[-1]
    y = _dot(a_ref[0].astype(BF16), w_ref[0:G, :]) + _dot(b_ref[0].astype(BF16), w_ref[G:2 * G, :])
    x1 = x_ref[0] + mod_ref[0, 2:3, :] * y
    x1_ref[0] = x1
    h2 = _rms(x1, gain_ref[...]) * (1.0 + mod_ref[0, 4:5, :]) + mod_ref[0, 3:4, :]
    _store_row_tiles(h2_ref.at[0], h2)
    lg_ref[...] = lax.dot_general(r_ref[...], h2, (((1,), (1,)), ((), ())), precision=HIGHEST,
                                  preferred_element_type=F32)


def out_proj(a_out, b_out, x, mod, gain2, w_out_bf16, router):
    B, T, D = x.shape
    G = a_out.shape[-1]
    E = router.shape[1]
    tm = min(512, T)
    nt = T // tm
    return pl.pallas_call(
        _out_kernel,
        out_shape=[jax.ShapeDtypeStruct((B, T, D), F32),
                   jax.ShapeDtypeStruct((B, T, D // HEAD_LANES, HEAD_LANES), F32),
                   jax.ShapeDtypeStruct((E, B * T), F32)],
        grid=(B, T // tm),
        in_specs=[pl.BlockSpec((1, tm, G), lambda b, i: (b, i, 0)),
                  pl.BlockSpec((1, tm, G), lambda b, i: (b, i, 0)),
                  pl.BlockSpec((1, tm, D), lambda b, i: (b, i, 0)),
                  pl.BlockSpec((1, 6, D), lambda b, i: (b, 0, 0)),
                  pl.BlockSpec((1, D), lambda b, i: (0, 0)),
                  pl.BlockSpec((2 * G, D), lambda b, i: (0, 0)),
                  pl.BlockSpec((E, D), lambda b, i: (0, 0))],
        out_specs=[pl.BlockSpec((1, tm, D), lambda b, i: (b, i, 0)),
                   pl.BlockSpec((1, tm, D // HEAD_LANES, HEAD_LANES), lambda b, i: (b, i, 0, 0)),
                   pl.BlockSpec((E, tm), lambda b, i, nt=nt: (0, b * nt + i))],
        compiler_params=pltpu.CompilerParams(
            dimension_semantics=("arbitrary", "arbitrary"), vmem_limit_bytes=VMEM_LIMIT),
        name="out_proj",
    )(a_out, b_out, x, mod, gain2.reshape(1, D), w_out_bf16, router.T)


def _route_kernel(lg_ref, bias_ref, idx_ref, w_ref, cnt_ref, *, tiles_per_group):
    @pl.when(pl.program_id(0) % tiles_per_group == 0)
    def _():
        cnt_ref[...] = jnp.zeros(cnt_ref.shape, F32)

    scores = _sigmoid(lg_ref[...])
    sel = scores + bias_ref[...]
    E, tm = sel.shape
    per_group = E // N_GROUPS
    neg = -jnp.inf
    eid = lax.broadcasted_iota(jnp.int32, (E, tm), 0).astype(F32)
    eid_g = lax.broadcasted_iota(jnp.int32, (per_group, tm), 0).astype(F32)

    def first_argmax(x, ids, sentinel):
        m = jnp.max(x, axis=0, keepdims=True)
        i = jnp.min(jnp.where(x == m, ids, sentinel), axis=0, keepdims=True)
        return m, i

    parts, gscore = [], []
    for g in range(N_GROUPS):
        x = sel[g * per_group:(g + 1) * per_group]
        parts.append(x)
        m1, i1 = first_argmax(x, eid_g, float(per_group))
        m2 = jnp.max(jnp.where(eid_g == i1, neg, x), axis=0, keepdims=True)
        gscore.append(m1 + m2)
    kept = []
    for g in range(N_GROUPS):
        beaten = jnp.zeros((1, tm), F32)
        for o in range(N_GROUPS):
            if o == g:
                continue
            wins = (gscore[o] >= gscore[g]) if o < g else (gscore[o] > gscore[g])
            beaten = beaten + wins.astype(F32)
        kept.append(jnp.where(beaten < TOPK_GROUPS, parts[g], neg))
    sel = jnp.concatenate(kept, axis=0)

    ids, vals = [], []
    w_sum = jnp.zeros((1, tm), F32)
    chosen = jnp.zeros((E, tm), F32)
    for k in range(TOP_K):
        _, i = first_argmax(sel, eid, float(E))
        hit = eid == i
        val = jnp.sum(jnp.where(hit, scores, 0.0), axis=0, keepdims=True)
        sel = jnp.where(hit, neg, sel)
        chosen = jnp.where(hit, 1.0, chosen)
        ids.append(i)
        vals.append(val)
        w_sum = w_sum + val
    idx_ref[...] = jnp.concatenate(ids, axis=0).astype(jnp.int32)
    w_ref[...] = jnp.concatenate(vals, axis=0) / w_sum * ROUTED_SCALE
    part = chosen[:, 0:HEAD_LANES]
    for l0 in range(HEAD_LANES, tm, HEAD_LANES):
        part = part + chosen[:, l0:l0 + HEAD_LANES]
    cnt_ref[0] = cnt_ref[0] + part


def route(logits_t, router_bias, group_tokens):
    E, N = logits_t.shape
    tm = min(512, N)
    tpg = group_tokens // tm
    idx, w, cnt = pl.pallas_call(
        functools.partial(_route_kernel, tiles_per_group=tpg),
        out_shape=[jax.ShapeDtypeStruct((TOP_K, N), jnp.int32), jax.ShapeDtypeStruct((TOP_K, N), F32),
                   jax.ShapeDtypeStruct((N // group_tokens, E, HEAD_LANES), F32)],
        grid=(N // tm,),
        in_specs=[pl.BlockSpec((E, tm), lambda i: (0, i)), pl.BlockSpec((E, 1), lambda i: (0, 0))],
        out_specs=[pl.BlockSpec((TOP_K, tm), lambda i: (0, i)), pl.BlockSpec((TOP_K, tm), lambda i: (0, i)),
                   pl.BlockSpec((1, E, HEAD_LANES), lambda i, tpg=tpg: (i // tpg, 0, 0))],
        compiler_params=pltpu.CompilerParams(dimension_semantics=("arbitrary",)),
        name="route",
    )(logits_t, router_bias.reshape(E, 1))
    return idx, w, jnp.sum(cnt, axis=-1).astype(jnp.int32)


def _loop_unrolled(n, body):
    u = MOE_LOOP_UNROLL
    lax.fori_loop(0, n // u, functools.partial(body, count=u), 0)
    lax.fori_loop(n // u * u, n, functools.partial(body, count=1), 0)


def _moe_kernel(off_ref, tok_ref, wl_ref, x_ref, wg_ref, wu_ref, wd_ref, acc_ref, xg_ref, yb_ref,
                *, R, E, per_group, experts_per_step):
    g = pl.program_id(0)
    step = pl.program_id(1)

    @pl.when(step == 0)
    def _():
        acc_ref[...] = jnp.zeros(acc_ref.shape, F32)

    @pl.when((g == 0) & (step == 0))
    def _():
        xg_ref[...] = jnp.zeros(xg_ref.shape, F32)

    U = SUBLANES
    chunks = U
    lanes = x_ref.shape[-1]

    def token_tile(ref, t8):
        return ref.at[0, pl.ds(pl.multiple_of(t8, U), U), :]

    def gather(le, s0, nr):
        def body(j, _, count):
            for gg in range(count):
                base = s0 + (j * count + gg) * U
                for i in range(U):
                    xg_ref[le, j * count + gg, pl.ds(i, chunks, stride=U), :] = (
                        token_tile(x_ref, tok_ref[base + i])[...])
            return 0

        whole = jnp.minimum((nr + U - 1) // U, (per_group - s0) // U)
        _loop_unrolled(whole, body)

        def single(r, _):
            xg_ref[le, r // U, pl.ds(r % U, chunks, stride=U), :] = token_tile(x_ref, tok_ref[s0 + r])[...]
            return 0

        lax.fori_loop(whole * U, nr, single, 0)

    def load_block(le):
        return jnp.concatenate([xg_ref[le, :, c * U:(c + 1) * U, :].reshape(R, lanes) for c in range(chunks)],
                               axis=1).astype(BF16)

    def store_block(le, y):
        for c in range(chunks):
            yb_ref[le, :, c * U:(c + 1) * U, :] = y[:, c * lanes:(c + 1) * lanes].reshape(R // U, U, lanes)

    def scatter(le, s0, nr):
        def group(j, _, count):
            base = s0 + j * count * U
            rows = range(count * U)
            tiles = [token_tile(acc_ref, tok_ref[base + i]) for i in rows]
            wts = [wl_ref[base + i] for i in rows]
            new = [tiles[i][...] + wts[i] * yb_ref[le, j * count + i // U, pl.ds(i % U, chunks, stride=U), :]
                   for i in rows]
            for i in rows:
                tiles[i][...] = new[i]
            return 0

        groups = nr // U
        _loop_unrolled(groups, group)

        def single(r, _):
            tile = token_tile(acc_ref, tok_ref[s0 + r])
            row = yb_ref[le, r // U, pl.ds(r % U, chunks, stride=U), :]
            tile[...] = tile[...] + wl_ref[s0 + r] * row
            return 0

        lax.fori_loop(groups * U, nr, single, 0)

    segments = []
    for le in range(experts_per_step):
        segment = g * E + step * experts_per_step + le
        seg = off_ref[segment]
        segments.append((seg - g * per_group, off_ref[segment + 1] - seg))

    for le, (start, cnt) in enumerate(segments):
        gather(le, start, jnp.minimum(R, cnt))
    xs = [load_block(le) for le in range(experts_per_step)]
    gu = [(_dot(xs[le], wg_ref[le].astype(BF16)), _dot(xs[le], wu_ref[le].astype(BF16)))
          for le in range(experts_per_step)]
    hs = [(_silu(gate) * up).astype(BF16) for gate, up in gu]
    for le in range(experts_per_step):
        store_block(le, _dot(hs[le], wd_ref[le].astype(BF16)))
    for le, (start, cnt) in enumerate(segments):
        scatter(le, start, jnp.minimum(R, cnt))

    for le, (start, cnt) in enumerate(segments):
        def more(sb, _, le=le, start=start, cnt=cnt):
            s0 = start + sb * R
            nr = jnp.minimum(R, cnt - sb * R)
            gather(le, s0, nr)
            x = load_block(le)
            hmid = _silu(_dot(x, wg_ref[le].astype(BF16))) * _dot(x, wu_ref[le].astype(BF16))
            store_block(le, _dot(hmid.astype(BF16), wd_ref[le].astype(BF16)))
            scatter(le, s0, nr)
            return 0

        lax.fori_loop(1, (cnt + R - 1) // R, more, 0)


def routed_experts(h2, top_idx, top_w, counts, exp_gate, exp_up, exp_down):
    N, chunks, lanes = h2.shape
    D = chunks * lanes
    E, _, F = exp_gate.shape
    TG = min(MOE_TOKEN_GROUP, N)
    G = N // TG
    per_group = TG * TOP_K
    key = (jnp.arange(N, dtype=jnp.int32)[None, :] // TG) * E + top_idx
    order = jnp.argsort(key.reshape(-1))
    assert chunks == SUBLANES
    tok_s = (((order % N) % TG) * chunks).astype(jnp.int32)
    w_s = top_w.reshape(-1)[order]
    off = jnp.concatenate([jnp.zeros((1,), jnp.int32), jnp.cumsum(counts.reshape(-1)).astype(jnp.int32)])

    EPS = MOE_EXPERTS_PER_STEP
    grid_spec = pltpu.PrefetchScalarGridSpec(
        num_scalar_prefetch=1,
        grid=(G, E // EPS),
        in_specs=[pl.BlockSpec((per_group,), lambda g, e, off: (g,), memory_space=pltpu.SMEM),
                  pl.BlockSpec((per_group,), lambda g, e, off: (g,), memory_space=pltpu.SMEM),
                  pl.BlockSpec((1, TG * chunks, lanes), lambda g, e, off: (g, 0, 0),
                               pipeline_mode=pl.Buffered(1)),
                  pl.BlockSpec((EPS, D, F), lambda g, e, off: (e, 0, 0)),
                  pl.BlockSpec((EPS, D, F), lambda g, e, off: (e, 0, 0)),
                  pl.BlockSpec((EPS, F, D), lambda g, e, off: (e, 0, 0))],
        out_specs=pl.BlockSpec((1, TG * chunks, lanes), lambda g, e, off: (g, 0, 0),
                               pipeline_mode=pl.Buffered(1)),
        scratch_shapes=[pltpu.VMEM((EPS, MOE_ROWS // SUBLANES, chunks * SUBLANES, lanes), F32),
                        pltpu.VMEM((EPS, MOE_ROWS // SUBLANES, chunks * SUBLANES, lanes), F32)],
    )
    out = pl.pallas_call(
        functools.partial(_moe_kernel, R=MOE_ROWS, E=E, per_group=per_group, experts_per_step=EPS),
        out_shape=jax.ShapeDtypeStruct((G, TG * chunks, lanes), F32),
        grid_spec=grid_spec,
        compiler_params=pltpu.CompilerParams(
            dimension_semantics=("arbitrary", "arbitrary"), vmem_limit_bytes=MOE_VMEM_LIMIT),
        name="routed_experts",
    )(off, tok_s, w_s, h2.reshape(G, TG * chunks, lanes), exp_gate, exp_up, exp_down)
    return out.reshape(N, chunks, lanes)


def _final_kernel(x1_ref, h2_ref, rt_ref, mod_ref, wg_ref, wu_ref, wd_ref, *rest, final):
    if final:
        fg_ref, o_ref = rest
    else:
        (o_ref,) = rest
    hb = _load_row_tiles(h2_ref.at[0]).astype(BF16)
    hmid = _silu(_dot(hb, wg_ref[...])) * _dot(hb, wu_ref[...])
    shared = _dot(hmid.astype(BF16), wd_ref[...])
    x2 = x1_ref[0] + mod_ref[0, 5:6, :] * (_load_row_tiles(rt_ref.at[0]) + shared)
    if final:
        x2 = _rms(x2, fg_ref[...])
    o_ref[0] = x2


def shared_and_residual(x1, h2, routed, mod, sh_gate, sh_up, sh_down, final_gain=None):
    B, T, D = x1.shape
    F = sh_gate.shape[1]
    tm = min(512, T)
    final = final_gain is not None
    tile = pl.BlockSpec((1, tm, D), lambda b, i: (b, i, 0))
    row_tiles = pl.BlockSpec((1, tm, D // HEAD_LANES, HEAD_LANES), lambda b, i: (b, i, 0, 0))
    in_specs = [tile, row_tiles, row_tiles,
                pl.BlockSpec((1, 6, D), lambda b, i: (b, 0, 0)),
                pl.BlockSpec((D, F), lambda b, i: (0, 0)),
                pl.BlockSpec((D, F), lambda b, i: (0, 0)),
                pl.BlockSpec((F, D), lambda b, i: (0, 0))]
    args = [x1, h2, routed, mod, sh_gate, sh_up, sh_down]
    if final:
        in_specs.append(pl.BlockSpec((1, D), lambda b, i: (0, 0)))
        args.append(final_gain.reshape(1, D))
    return pl.pallas_call(
        functools.partial(_final_kernel, final=final),
        out_shape=jax.ShapeDtypeStruct((B, T, D), F32),
        grid=(B, T // tm),
        in_specs=in_specs,
        out_specs=tile,
        compiler_params=pltpu.CompilerParams(
            dimension_semantics=("arbitrary", "arbitrary"), vmem_limit_bytes=VMEM_LIMIT),
        name="shared_and_residual",
    )(*args)


def moe_block(x1, h2, logits, mod, router_bias, exp_gate, exp_up, exp_down, sh_gate, sh_up, sh_down,
              final_gain=None):
    B, T, D = x1.shape
    top_idx, top_w, counts = route(logits, router_bias, min(MOE_TOKEN_GROUP, B * T))
    chunks = D // HEAD_LANES
    routed = routed_experts(h2.reshape(B * T, chunks, HEAD_LANES), top_idx, top_w, counts,
                            exp_gate, exp_up, exp_down)
    return shared_and_residual(x1, h2, routed.reshape(B, T, chunks, HEAD_LANES), mod,
                               sh_gate.astype(BF16), sh_up.astype(BF16), sh_down.astype(BF16), final_gain)


def kernel(x, c, hgrn_lb_logits, l0_norm1, l0_norm2, l0_w_mod, l0_b_mod, l0_w_in, l0_w_out, l0_hgrn_norm, l0_diff_lq1, l0_diff_lk1, l0_diff_lq2, l0_diff_lk2, l0_diff_subln, l0_router, l0_router_bias, l0_exp_gate, l0_exp_up, l0_exp_down, l0_sh_gate, l0_sh_up, l0_sh_down, l1_norm1, l1_norm2, l1_w_mod, l1_b_mod, l1_w_in, l1_w_out, l1_conv_w, l1_conv_b, l1_ml_i_bias, l1_ml_f_bias, l1_ml_norm, l1_na_rpb, l1_router, l1_router_bias, l1_exp_gate, l1_exp_up, l1_exp_down, l1_sh_gate, l1_sh_up, l1_sh_down, final_norm):
    G = GROUP_WIDTH
    lb_all = jnp.cumsum(jax.nn.softmax(hgrn_lb_logits.astype(F32), axis=0), axis=0)
    layer_idx = 0
    lambda_init = 0.8 - 0.6 * math.exp(-0.3 * layer_idx)
    lam = (jnp.exp(jnp.sum(l0_diff_lq1.astype(F32) * l0_diff_lk1.astype(F32)))
           - jnp.exp(jnp.sum(l0_diff_lq2.astype(F32) * l0_diff_lk2.astype(F32))) + lambda_init)

    mod0 = ada_mod(c, l0_w_mod, l0_b_mod)
    y0 = in_proj(x, mod0, l0_norm1, l0_w_in.astype(BF16))
    a_out = hgrn2(y0, lb_all[0], l0_hgrn_norm)
    b_out = diff_attention(y0, lam, l0_diff_subln, layer_idx)
    x1, h2, logits = out_proj(a_out, b_out, x, mod0, l0_norm2, l0_w_out.astype(BF16), l0_router)
    xa = moe_block(x1, h2, logits, mod0, l0_router_bias, l0_exp_gate, l0_exp_up, l0_exp_down,
                   l0_sh_gate, l0_sh_up, l0_sh_down)

    mod1 = ada_mod(c, l1_w_mod, l1_b_mod)
    n_gate = l1_w_in.shape[1] - 7 * G
    w_main = jnp.concatenate([l1_w_in[:, :4 * G], l1_w_in[:, 4 * G + n_gate:]], axis=1).astype(BF16)
    w_gate = jnp.pad(l1_w_in[:, 4 * G:4 * G + n_gate], ((0, 0), (0, HEAD_LANES - n_gate)))
    y1, gates = in_proj(xa, mod1, l1_norm1, w_main, w_gate)
    c_out = mlstm(y1, gates, l1_conv_w, l1_conv_b, l1_ml_i_bias, l1_ml_f_bias, l1_ml_norm)
    d_out = neighbourhood_attention(y1, l1_na_rpb)
    x1, h2, logits = out_proj(c_out, d_out, xa, mod1, l1_norm2, l1_w_out.astype(BF16), l1_router)
    return moe_block(x1, h2, logits, mod1, l1_router_bias, l1_exp_gate, l1_exp_up, l1_exp_down,
                     l1_sh_gate, l1_sh_up, l1_sh_down, final_gain=final_norm)
```

```python
import functools
import math

import numpy as np
import jax
import jax.numpy as jnp
from jax import lax
from jax.experimental import pallas as pl
from jax.experimental.pallas import tpu as pltpu

F32 = jnp.float32
BF16 = jnp.bfloat16
HIGHEST = lax.Precision.HIGHEST
EPS = 1e-6

GRID_W = 64
GROUP_WIDTH = 512
HEAD_LANES = 128
HG_CHUNK = 64
ML_CHUNK = 128
STEP_CHUNKS = 4
ML_CONV = 5
DA_DIM = 64
ROPE_DIM = 16
ROPE_THETA = 500000.0
NA_ROWS = 8
NA_COLS = 16
NA_DIM = 64
NA_STEP_ROWS = 8
N_EXPERTS = 128
TOP_K = 8
N_GROUPS = 8
TOPK_GROUPS = 4
ROUTED_SCALE = 2.5
MOE_ROWS = 320
MOE_TOKEN_GROUP = 4096
MOE_EXPERTS_PER_STEP = 2
MOE_LOOP_UNROLL = 2
SUBLANES = 8
NEG_BIG = -1e30
VMEM_LIMIT = 48 * 1024 * 1024
MOE_VMEM_LIMIT = 56 * 1024 * 1024


def _dot(a, b, **kw):
    return jnp.dot(a, b, preferred_element_type=F32, **kw)


def _dot_nt(a, b):
    return lax.dot_general(a, b, (((1,), (1,)), ((), ())), preferred_element_type=F32)


def _dot_tn(a, b):
    return lax.dot_general(a, b, (((0,), (0,)), ((), ())), preferred_element_type=F32)


def _sigmoid(x):
    return jax.nn.sigmoid(x)


def _silu(x):
    return x * jax.nn.sigmoid(x)


def _log_sigmoid(x):
    return jnp.minimum(x, 0.0) - jnp.log(1.0 + jnp.exp(-jnp.abs(x)))


def _rms(x, gain):
    return x * lax.rsqrt(jnp.mean(x * x, axis=-1, keepdims=True) + EPS) * gain


def _store_row_tiles(ref, val):
    for c in range(ref.shape[1]):
        ref[:, c, :] = val[:, c * HEAD_LANES:(c + 1) * HEAD_LANES]


def _load_row_tiles(ref):
    return jnp.concatenate([ref[:, c, :] for c in range(ref.shape[1])], axis=1)


def _mod_kernel(c_ref, w_ref, b_ref, o_ref):
    o_ref[...] = _dot(_silu(c_ref[...]), w_ref[...], precision=HIGHEST) + b_ref[...]


def ada_mod(c, w_mod, b_mod):
    B, D = c.shape
    N = w_mod.shape[1]
    tn = 1024
    out = pl.pallas_call(
        _mod_kernel,
        out_shape=jax.ShapeDtypeStruct((B, N), F32),
        grid=(N // tn,),
        in_specs=[pl.BlockSpec((B, D), lambda j: (0, 0)),
                  pl.BlockSpec((D, tn), lambda j: (0, j)),
                  pl.BlockSpec((1, tn), lambda j: (0, j))],
        out_specs=pl.BlockSpec((B, tn), lambda j: (0, j)),
        name="ada_mod",
    )(c, w_mod, b_mod.reshape(1, N))
    return out.reshape(B, 6, D)


def _in_kernel(x_ref, mod_ref, gain_ref, w_ref, *rest, has_gate):
    if has_gate:
        wg_ref, o_ref, og_ref, h_ref = rest
    else:
        o_ref, h_ref = rest

    @pl.when(pl.program_id(2) == 0)
    def _():
        h = _rms(x_ref[0], gain_ref[...]) * (1.0 + mod_ref[0, 1:2, :]) + mod_ref[0, 0:1, :]
        h_ref[...] = h.astype(BF16)
        if has_gate:
            og_ref[0] = _dot(h, wg_ref[...], precision=HIGHEST)

    o_ref[0] = _dot(h_ref[...], w_ref[...]).astype(o_ref.dtype)


def in_proj(x, mod, gain, w_bf16, w_gate=None):
    B, T, D = x.shape
    N = w_bf16.shape[1]
    tm = min(1024, T)
    tn = 512
    has_gate = w_gate is not None
    in_specs = [pl.BlockSpec((1, tm, D), lambda b, i, j: (b, i, 0)),
                pl.BlockSpec((1, 6, D), lambda b, i, j: (b, 0, 0)),
                pl.BlockSpec((1, D), lambda b, i, j: (0, 0)),
                pl.BlockSpec((D, tn), lambda b, i, j: (0, j))]
    out_shape = [jax.ShapeDtypeStruct((B, T, N), BF16)]
    out_specs = [pl.BlockSpec((1, tm, tn), lambda b, i, j: (b, i, j))]
    args = [x, mod, gain.reshape(1, D), w_bf16]
    if has_gate:
        in_specs.append(pl.BlockSpec((D, HEAD_LANES), lambda b, i, j: (0, 0)))
        out_shape.append(jax.ShapeDtypeStruct((B, T, HEAD_LANES), F32))
        out_specs.append(pl.BlockSpec((1, tm, HEAD_LANES), lambda b, i, j: (b, i, 0)))
        args.append(w_gate)
    res = pl.pallas_call(
        functools.partial(_in_kernel, has_gate=has_gate),
        out_shape=out_shape,
        grid=(B, T // tm, N // tn),
        in_specs=in_specs,
        out_specs=out_specs,
        scratch_shapes=[pltpu.VMEM((tm, D), BF16)],
        compiler_params=pltpu.CompilerParams(
            dimension_semantics=("arbitrary", "arbitrary", "arbitrary")),
        name="in_proj",
    )(*args)
    return res if has_gate else res[0]


def _hgrn_consts(C):
    t = np.arange(C)
    tri = (t[None, :] <= t[:, None]).astype(np.float32)
    triT = np.ascontiguousarray(tri.T)
    wf, wb, mf = [tri], [triT], []
    levels = int(round(math.log2(C)))
    for l in range(levels):
        size = C >> l
        blk = t // size
        r = blk * size + size // 2
        wf.append(tri - tri[r - 1])
        wb.append(triT - triT[r])
        upper = (t % size) >= size // 2
        mf.append(((blk[:, None] == blk[None, :]) & upper[:, None] & (~upper)[None, :]).astype(np.float32))
    mf.append(np.eye(C, dtype=np.float32))
    ones = np.ones((8, C), np.float32)
    wf.append(ones)
    wb.append(ones)
    mf = np.stack(mf)
    mb = np.ascontiguousarray(np.transpose(mf, (0, 2, 1)))
    return np.concatenate(wf), np.concatenate(wb), mf, mb


def _split2(x):
    hi = x.astype(BF16)
    lo = (x - hi.astype(F32)).astype(BF16)
    return hi, lo


def _hgrn_kernel(q_ref, i_ref, ff_ref, fb_ref, g_ref, lb_ref, gain_ref, wf_ref, wb_ref, mf_ref, mb_ref,
                 o_ref, of_ref, ob_ref, *, C, T):
    n = T // C
    levels = int(round(math.log2(C)))
    dv = q_ref.shape[-1]

    def prepare(c, f_ref, lbd, w_ref):
        sl = pl.ds(pl.multiple_of(c * C, C), C)
        q = _silu(q_ref[0, sl, :].astype(F32))
        v = i_ref[0, sl, :]
        fg = lbd + (1.0 - lbd) * _sigmoid(f_ref[0, sl, :].astype(F32))
        lf = jnp.log(fg)
        d2 = _dot(w_ref[...], jnp.concatenate(_split2(lf), axis=1))
        dall = d2[:, 0:dv] + d2[:, dv:2 * dv]
        return dict(sl=sl, q=q, k=1.0 - fg, v=v.astype(BF16), dall=dall)

    def scores(p, m_ref):
        q, k, dall = p["q"], p["k"], p["dall"]
        attn = m_ref[levels] * _dot_nt(q.astype(BF16), k.astype(BF16))
        for l in range(levels):
            e = jnp.exp(-jnp.abs(dall[(l + 1) * C:(l + 2) * C]))
            attn = attn + m_ref[l] * _dot_nt((q * e).astype(BF16), (k * e).astype(BF16))
        return attn.astype(BF16)

    st0 = jnp.zeros((dv, dv), F32)

    def step(j, carry):
        states = list(carry)
        work = []
        for u in range(STEP_CHUNKS):
            work.append((0, prepare(STEP_CHUNKS * j + u, ff_ref, lb_ref[0:1, :], wf_ref), mf_ref, of_ref))
            work.append((1, prepare(n - 1 - STEP_CHUNKS * j - u, fb_ref, lb_ref[1:2, :], wb_ref), mb_ref, ob_ref))
        attn = [scores(p, m_ref) for _, p, m_ref, _ in work]
        local = []
        for (_, p, _, _), a in zip(work, attn):
            cum = p["dall"][0:C]
            tot = p["dall"][(levels + 1) * C:(levels + 1) * C + 1]
            kt = p["k"] * jnp.exp(tot - cum)
            local.append((_dot(a, p["v"]), _dot_tn(p["v"], kt.astype(BF16)), cum, tot))
        for (d, p, _, out_ref), (o_in, incr, cum, tot) in zip(work, local):
            st = states[d]
            out_ref[p["sl"], :] = o_in + _dot_nt((p["q"] * jnp.exp(cum)).astype(BF16), st.astype(BF16))
            states[d] = st * jnp.exp(tot) + incr
        return tuple(states)

    lax.fori_loop(0, n // STEP_CHUNKS, step, (st0, st0))

    rt = min(512, T)
    for r0 in range(0, T, rt):
        sl = slice(r0, r0 + rt)
        o_ref[0, sl, :] = (_rms(of_ref[sl, :] + ob_ref[sl, :], gain_ref[...])
                           * _silu(g_ref[0, sl, :].astype(F32))).astype(o_ref.dtype)


def hgrn2(y, lb, norm_gain):
    B, T, _ = y.shape
    H = GROUP_WIDTH // HEAD_LANES
    C = HG_CHUNK
    wf, wb, mf, mb = _hgrn_consts(C)
    wf, wb = jnp.asarray(wf, BF16), jnp.asarray(wb, BF16)
    mf, mb = jnp.asarray(mf), jnp.asarray(mb)

    def col(group):
        return pl.BlockSpec((1, T, HEAD_LANES), lambda b, h, group=group: (b, 0, group * H + h))

    def const(a):
        nd = a.ndim
        return pl.BlockSpec(a.shape, lambda b, h, nd=nd: (0,) * nd)

    return pl.pallas_call(
        functools.partial(_hgrn_kernel, C=C, T=T),
        out_shape=jax.ShapeDtypeStruct((B, T, GROUP_WIDTH), BF16),
        grid=(B, H),
        in_specs=[col(0), col(1), col(2), col(3), col(4),
                  pl.BlockSpec((2, HEAD_LANES), lambda b, h: (0, h)),
                  pl.BlockSpec((1, HEAD_LANES), lambda b, h: (0, 0)),
                  const(wf), const(wb), const(mf), const(mb)],
        out_specs=pl.BlockSpec((1, T, HEAD_LANES), lambda b, h: (b, 0, h)),
        scratch_shapes=[pltpu.VMEM((T, HEAD_LANES), F32), pltpu.VMEM((T, HEAD_LANES), F32)],
        compiler_params=pltpu.CompilerParams(
            dimension_semantics=("arbitrary", "arbitrary"), vmem_limit_bytes=VMEM_LIMIT),
        name="hgrn2",
    )(y, y, y, y, y, lb, norm_gain.reshape(1, HEAD_LANES), wf, wb, mf, mb)


def _rope_tables(T):
    pos = np.arange(T, dtype=np.float32)
    inv_freq = (ROPE_THETA ** (-np.arange(0, ROPE_DIM, 2, dtype=np.float32) / ROPE_DIM)).astype(np.float32)
    ang = pos[:, None] * inv_freq[None, :]
    cos, sin = np.cos(ang), np.sin(ang)
    half = ROPE_DIM // 2
    c = np.ones((T, HEAD_LANES), np.float32)
    s_prev = np.zeros((T, HEAD_LANES), np.float32)
    s_next = np.zeros((T, HEAD_LANES), np.float32)
    for base in range(0, HEAD_LANES, DA_DIM):
        c[:, base:base + half] = cos
        c[:, base + half:base + ROPE_DIM] = cos
        s_next[:, base:base + half] = -sin
        s_prev[:, base + half:base + ROPE_DIM] = sin
    return jnp.asarray(c), jnp.asarray(s_prev), jnp.asarray(s_next)


def _rope(x, c, s_prev, s_next):
    half = ROPE_DIM // 2
    lanes = x.shape[-1]
    return (x * c + pltpu.roll(x, half, axis=1) * s_prev
            + pltpu.roll(x, lanes - half, axis=1) * s_next)


def _diff_kernel(lam_ref, q_ref, k_ref, v_ref, cq_ref, spq_ref, snq_ref, ck_ref, spk_ref, snk_ref,
                 subln_ref, o_ref, kr_ref, vb_ref, *, T, out_scale):
    rt = min(512, T)

    @pl.when(pl.program_id(2) == 0)
    def _():
        for r0 in range(0, T, rt):
            sl = slice(r0, r0 + rt)
            kr_ref[sl, :] = _rope(k_ref[0, sl, :].astype(F32), ck_ref[sl, :], spk_ref[sl, :], snk_ref[sl, :]).astype(BF16)
            vb_ref[sl, 0:HEAD_LANES] = v_ref[0, sl, :].astype(BF16)
            lane = lax.broadcasted_iota(jnp.int32, (rt, HEAD_LANES), 1)
            vb_ref[sl, HEAD_LANES:2 * HEAD_LANES] = jnp.where(lane == 0, 1.0, 0.0).astype(BF16)

    q = _rope(q_ref[0].astype(F32), cq_ref[...], spq_ref[...], snq_ref[...]) * (DA_DIM ** -0.5 * math.log2(math.e))
    lam = lam_ref[0, 0]
    v = vb_ref[...]
    map_of_lane = lax.broadcasted_iota(jnp.int32, q.shape, 1) // DA_DIM

    scores = [_dot_nt(jnp.where(map_of_lane == m, q, 0.0).astype(BF16), kr_ref[...]) for m in range(2)]
    probs = [jnp.exp2(s - jnp.max(s, axis=-1, keepdims=True)).astype(BF16) for s in scores]
    pv = []
    for e in probs:
        full = _dot(e, v)
        pv.append(full[:, 0:HEAD_LANES] / full[:, HEAD_LANES:HEAD_LANES + 1])
    o = pv[0] - lam * pv[1]
    o_ref[0] = (_rms(o, subln_ref[...]) * out_scale).astype(o_ref.dtype)


def diff_attention(y, lam, subln, layer_idx):
    B, T, _ = y.shape
    H = GROUP_WIDTH // HEAD_LANES
    tq = min(256, T)
    lambda_init = 0.8 - 0.6 * math.exp(-0.3 * layer_idx)
    c, sp, sn = _rope_tables(T)

    def col(group, rows):
        if rows == T:
            return pl.BlockSpec((1, T, HEAD_LANES), lambda b, h, i, group=group: (b, 0, group * H + h))
        return pl.BlockSpec((1, rows, HEAD_LANES), lambda b, h, i, group=group: (b, i, group * H + h))

    tab_q = pl.BlockSpec((tq, HEAD_LANES), lambda b, h, i: (i, 0))
    tab_k = pl.BlockSpec((T, HEAD_LANES), lambda b, h, i: (0, 0))
    return pl.pallas_call(
        functools.partial(_diff_kernel, T=T, out_scale=1.0 - lambda_init),
        out_shape=jax.ShapeDtypeStruct((B, T, GROUP_WIDTH), BF16),
        grid=(B, H, T // tq),
        in_specs=[pl.BlockSpec(memory_space=pltpu.SMEM),
                  col(5, tq), col(6, T), col(7, T),
                  tab_q, tab_q, tab_q, tab_k, tab_k, tab_k,
                  pl.BlockSpec((1, HEAD_LANES), lambda b, h, i: (0, 0))],
        out_specs=pl.BlockSpec((1, tq, HEAD_LANES), lambda b, h, i: (b, i, h)),
        scratch_shapes=[pltpu.VMEM((T, HEAD_LANES), BF16), pltpu.VMEM((T, 2 * HEAD_LANES), BF16)],
        compiler_params=pltpu.CompilerParams(
            dimension_semantics=("arbitrary", "arbitrary", "arbitrary"), vmem_limit_bytes=VMEM_LIMIT),
        name="diff_attention",
    )(lam.reshape(1, 1), y, y, y, c, sp, sn, c, sp, sn, subln.reshape(1, HEAD_LANES))


def _mlstm_kernel(q_ref, k_ref, v_ref, og_ref, gt_ref, cwq_ref, cwk_ref, cbq_ref, cbk_ref, gbias_ref,
                  gain_ref, tri_ref, o_ref, qc_ref, kc_ref, xp_ref, gx_ref, hf_ref, hb_ref, gxt_ref, va_ref,
                  vta_ref, *, C, T, dk):
    n = T // C
    head = pl.program_id(1)
    pad = 8
    half = ML_CONV // 2
    rt = min(512, T)

    xp_ref[0:pad, :] = jnp.zeros((pad, dk), F32)
    xp_ref[pad + T:pad + T + pad, :] = jnp.zeros((pad, dk), F32)
    for src, cw_ref, cb_ref, dst, scale in ((q_ref, cwq_ref, cbq_ref, qc_ref, 1.0),
                                            (k_ref, cwk_ref, cbk_ref, kc_ref, dk ** -0.5)):
        xp_ref[pad:pad + T, :] = src[0].astype(F32)
        for r0 in range(0, T, rt):
            acc = jnp.zeros((rt, dk), F32) + cb_ref[...]
            for j in range(ML_CONV):
                acc = acc + xp_ref[pad + r0 + j - half:pad + r0 + j - half + rt, :] * cw_ref[j:j + 1, :]
            dst[r0:r0 + rt, :] = _silu(acc) * scale

    lane = lax.broadcasted_iota(jnp.int32, (rt, HEAD_LANES), 1)
    is_f = (lane % 8) >= 4
    for r0 in range(0, T, rt):
        g = gt_ref[0, r0:r0 + rt, :] + gbias_ref[...]
        p = jnp.where(is_f, _log_sigmoid(g), g)
        x = jnp.zeros((rt, HEAD_LANES), F32)
        for j, src_lane in enumerate((0, 4, 8, 12)):
            colv = jnp.sum(jnp.where(lane == src_lane + head, p, 0.0), axis=1, keepdims=True)
            x = jnp.where(lane == j, colv, x)
        gx_ref[r0:r0 + rt, :] = x
        gxt_ref[:, r0:r0 + rt] = x.T[0:SUBLANES, :]
        v = v_ref[0, r0:r0 + rt, :].astype(F32)
        va_ref[r0:r0 + rt, 0:dk] = v.astype(BF16)
        va_ref[r0:r0 + rt, dk:2 * dk] = jnp.where(lane == 0, 1.0, 0.0).astype(BF16)
        vta_ref[0:dk, r0:r0 + rt] = v.T.astype(BF16)
        sub = lax.broadcasted_iota(jnp.int32, (dk, rt), 0)
        vta_ref[dk:2 * dk, r0:r0 + rt] = jnp.where(sub == 0, 1.0, 0.0).astype(BF16)

    row = lax.broadcasted_iota(jnp.int32, (C, C), 0)
    colm = lax.broadcasted_iota(jnp.int32, (C, C), 1)

    init = (jnp.zeros((2 * dk, dk), F32), jnp.full((1, 1), NEG_BIG, F32))

    def step(j, carry):
        carries = list(carry)
        work = []
        for u in range(STEP_CHUNKS):
            work.append((0, STEP_CHUNKS * j + u, hf_ref))
            work.append((1, n - 1 - STEP_CHUNKS * j - u, hb_ref))
        chunks = []
        for d, c, out_ref in work:
            sl = pl.ds(pl.multiple_of(c * C, C), C)
            x = gx_ref[sl, :]
            xr = gxt_ref[:, sl]
            chunks.append(dict(d=d, sl=sl, out=out_ref, x=x, xr=xr, q=qc_ref[sl, :], k=kc_ref[sl, :],
                               cumx=_dot(tri_ref[d], x, precision=HIGHEST),
                               cumr=_dot(xr, tri_ref[1 - d], precision=HIGHEST)))
        for p in chunks:
            p["qk"] = _dot_nt(p["q"].astype(BF16), p["k"].astype(BF16))
        for p in chunks:
            d, x, xr, cumx, cumr = p["d"], p["x"], p["xr"], p["cumx"], p["cumr"]
            mask = (colm <= row) if d == 0 else (colm >= row)
            ig_c = x[:, 2 * d:2 * d + 1]
            ig_r = xr[2 * d:2 * d + 1, :]
            cum_c = cumx[:, 2 * d + 1:2 * d + 2]
            cum_r = cumr[2 * d + 1:2 * d + 2, :]
            tot = jnp.sum(x[:, 2 * d + 1:2 * d + 2], axis=0, keepdims=True)
            dmat = jnp.where(mask, cum_c - cum_r + ig_r, -jnp.inf)
            dmax = jnp.max(dmat, axis=1, keepdims=True)
            a = p["qk"] * jnp.exp(dmat - dmax)
            ds = tot - cum_c + ig_c
            dsmax = jnp.max(ds, axis=0, keepdims=True)
            kw = p["k"] * jnp.exp(ds - dsmax)
            p.update(cum_c=cum_c, tot=tot, dmax=dmax, dsmax=dsmax, a=a.astype(BF16), kw=kw.astype(BF16))
        for p in chunks:
            p["num"] = _dot(p["a"], va_ref[p["sl"], :])
            p["upd"] = _dot(vta_ref[:, p["sl"]], p["kw"])
        for p in chunks:
            state, m = carries[p["d"]]
            g = p["cum_c"] + m
            mt = jnp.maximum(g, p["dmax"])
            full = (jnp.exp(g - mt) * _dot_nt(p["q"].astype(BF16), state.astype(BF16))
                    + jnp.exp(p["dmax"] - mt) * p["num"])
            den = full[:, dk:dk + 1]
            p["out"][p["sl"], :] = full[:, 0:dk] / jnp.maximum(jnp.abs(den), jnp.exp(-mt))
            m_new = jnp.maximum(p["tot"] + m, p["dsmax"])
            decay = jnp.exp(p["tot"] + m - m_new)
            scale = jnp.exp(p["dsmax"] - m_new)
            carries[p["d"]] = (decay * state + scale * p["upd"], m_new)
        return tuple(carries)

    lax.fori_loop(0, n // STEP_CHUNKS, step, (init, init))

    for r0 in range(0, T, rt):
        sl = slice(r0, r0 + rt)
        o_ref[0, sl, :] = (_rms(hf_ref[sl, :] + hb_ref[sl, :], gain_ref[...])
                           * _sigmoid(og_ref[0, sl, :].astype(F32))).astype(o_ref.dtype)


def mlstm(y, gates, conv_w, conv_b, i_bias, f_bias, norm_gain):
    B, T, _ = y.shape
    H = GROUP_WIDTH // HEAD_LANES
    C = min(ML_CHUNK, T)
    t = np.arange(C)
    tri = np.stack([(t[None, :] <= t[:, None]), (t[None, :] >= t[:, None])]).astype(np.float32)
    gbias = jnp.zeros((1, HEAD_LANES), F32)
    gbias = gbias.at[0, 0:4].set(i_bias[0]).at[0, 4:8].set(f_bias[0])
    gbias = gbias.at[0, 8:12].set(i_bias[1]).at[0, 12:16].set(f_bias[1])

    def col(group):
        return pl.BlockSpec((1, T, HEAD_LANES), lambda b, h, group=group: (b, 0, group * H + h))

    conv_q = pl.BlockSpec((ML_CONV, HEAD_LANES), lambda b, h: (0, h))
    conv_k = pl.BlockSpec((ML_CONV, HEAD_LANES), lambda b, h: (0, H + h))
    bias_q = pl.BlockSpec((1, HEAD_LANES), lambda b, h: (0, h))
    bias_k = pl.BlockSpec((1, HEAD_LANES), lambda b, h: (0, H + h))
    cb = conv_b.reshape(1, -1)
    return pl.pallas_call(
        functools.partial(_mlstm_kernel, C=C, T=T, dk=HEAD_LANES),
        out_shape=jax.ShapeDtypeStruct((B, T, GROUP_WIDTH), BF16),
        grid=(B, H),
        in_specs=[col(0), col(1), col(2), col(3),
                  pl.BlockSpec((1, T, HEAD_LANES), lambda b, h: (b, 0, 0)),
                  conv_q, conv_k, bias_q, bias_k,
                  pl.BlockSpec((1, HEAD_LANES), lambda b, h: (0, 0)),
                  pl.BlockSpec((1, HEAD_LANES), lambda b, h: (0, 0)),
                  pl.BlockSpec((2, C, C), lambda b, h: (0, 0, 0))],
        out_specs=pl.BlockSpec((1, T, HEAD_LANES), lambda b, h: (b, 0, h)),
        scratch_shapes=[pltpu.VMEM((T, HEAD_LANES), F32), pltpu.VMEM((T, HEAD_LANES), F32),
                        pltpu.VMEM((T + 16, HEAD_LANES), F32), pltpu.VMEM((T, HEAD_LANES), F32),
                        pltpu.VMEM((T, HEAD_LANES), F32), pltpu.VMEM((T, HEAD_LANES), F32),
                        pltpu.VMEM((SUBLANES, T), F32), pltpu.VMEM((T, 2 * HEAD_LANES), BF16),
                        pltpu.VMEM((2 * HEAD_LANES, T), BF16)],
        compiler_params=pltpu.CompilerParams(
            dimension_semantics=("arbitrary", "arbitrary"), vmem_limit_bytes=VMEM_LIMIT),
        name="mlstm",
    )(y, y, y, y, gates, conv_w, conv_w, cb, cb, gbias, norm_gain.reshape(1, HEAD_LANES), jnp.asarray(tri))


def _na_bias_table(rpb, rows):
    kr = min(NA_ROWS, rows)
    c = np.arange(GRID_W)
    cstart = np.clip(c - NA_COLS // 2, 0, GRID_W - NA_COLS)
    kc = np.arange(GRID_W)
    valid = (kc[None, :] >= cstart[:, None]) & (kc[None, :] < cstart[:, None] + NA_COLS)
    coff = np.clip(kc[None, :] - c[:, None] + NA_COLS - 1, 0, 2 * NA_COLS - 2)
    di = np.arange(kr)
    i = np.arange(kr)
    roff = i[None, :] - di[:, None] + NA_ROWS - 1
    heads = rpb.shape[0]
    cols = jnp.where(jnp.asarray(valid)[None, None], rpb.astype(F32)[:, :, coff], NEG_BIG)
    tab = jnp.take(cols, jnp.asarray(roff.reshape(-1)), axis=1)
    tab = tab.reshape(heads, kr, kr, GRID_W, GRID_W).transpose(0, 1, 3, 2, 4)
    return tab.reshape(heads, kr, GRID_W, kr * GRID_W)


def _na_kernel(q_ref, k_ref, v_ref, bm_ref, o_ref, *, rows, kr):
    W = GRID_W
    heads_per_block = HEAD_LANES // NA_DIM

    head_of_lane = lax.broadcasted_iota(jnp.int32, (W, HEAD_LANES), 1) // NA_DIM

    def body(j, _):
        work = []
        for u in range(NA_STEP_ROWS):
            r = NA_STEP_ROWS * j + u
            rs = jnp.clip(r - kr // 2, 0, rows - kr)
            di = r - rs
            qs = pl.ds(pl.multiple_of(r * W, W), W)
            ks = pl.ds(pl.multiple_of(rs * W, W), kr * W)
            q = q_ref[0, qs, :].astype(F32) * (NA_DIM ** -0.5)
            kw = k_ref[0, ks, :].astype(BF16)
            for hh in range(heads_per_block):
                s = _dot_nt(jnp.where(head_of_lane == hh, q, 0.0).astype(BF16), kw) + bm_ref[hh, di]
                work.append((u, hh, qs, ks, s))
        probs = []
        for u, hh, qs, ks, s in work:
            e = jnp.exp(s - jnp.max(s, axis=-1, keepdims=True))
            probs.append((e.astype(BF16), jnp.sum(e, axis=-1, keepdims=True)))
        outs = {}
        for (u, hh, qs, ks, s), (e, l) in zip(work, probs):
            o = _dot(e, v_ref[0, ks, :].astype(BF16)) / l
            outs[u] = o if hh == 0 else jnp.where(head_of_lane == hh, o, outs[u])
            if hh == heads_per_block - 1:
                o_ref[0, qs, :] = outs[u].astype(o_ref.dtype)
        return 0

    lax.fori_loop(0, rows // NA_STEP_ROWS, body, 0)


def neighbourhood_attention(y, rpb):
    B, T, _ = y.shape
    rows = T // GRID_W
    kr = min(NA_ROWS, rows)
    HB = GROUP_WIDTH // HEAD_LANES
    hpb = HEAD_LANES // NA_DIM
    bm = _na_bias_table(rpb, rows)

    def col(group):
        return pl.BlockSpec((1, T, HEAD_LANES), lambda b, h, group=group: (b, 0, group * HB + h))

    return pl.pallas_call(
        functools.partial(_na_kernel, rows=rows, kr=kr),
        out_shape=jax.ShapeDtypeStruct((B, T, GROUP_WIDTH), BF16),
        grid=(B, HB),
        in_specs=[col(4), col(5), col(6),
                  pl.BlockSpec((hpb, kr, GRID_W, kr * GRID_W), lambda b, h: (h, 0, 0, 0))],
        out_specs=pl.BlockSpec((1, T, HEAD_LANES), lambda b, h: (b, 0, h)),
        compiler_params=pltpu.CompilerParams(
            dimension_semantics=("arbitrary", "arbitrary"), vmem_limit_bytes=VMEM_LIMIT),
        name="neighbourhood_attention",
    )(y, y, y, bm)


def _out_kernel(a_ref, b_ref, x_ref, mod_ref, gain_ref, w_ref, r_ref, x1_ref, h2_ref, lg_ref):
    G = a_ref.shape[-1]
    y = _dot(a_ref[0].astype(BF16), w_ref[0:G, :]) + _dot(b_ref[0].astype(BF16), w_ref[G:2 * G, :])
    x1 = x_ref[0] + mod_ref[0, 2:3, :] * y
    x1_ref[0] = x1
    h2 = _rms(x1, gain_ref[...]) * (1.0 + mod_ref[0, 4:5, :]) + mod_ref[0, 3:4, :]
    _store_row_tiles(h2_ref.at[0], h2)
    lg_ref[...] = lax.dot_general(r_ref[...], h2, (((1,), (1,)), ((), ())), precision=HIGHEST,
                                  preferred_element_type=F32)


def out_proj(a_out, b_out, x, mod, gain2, w_out_bf16, router):
    B, T, D = x.shape
    G = a_out.shape[-1]
    E = router.shape[1]
    tm = min(512, T)
    nt = T // tm
    return pl.pallas_call(
        _out_kernel,
        out_shape=[jax.ShapeDtypeStruct((B, T, D), F32),
                   jax.ShapeDtypeStruct((B, T, D // HEAD_LANES, HEAD_LANES), F32),
                   jax.ShapeDtypeStruct((E, B * T), F32)],
        grid=(B, T // tm),
        in_specs=[pl.BlockSpec((1, tm, G), lambda b, i: (b, i, 0)),
                  pl.BlockSpec((1, tm, G), lambda b, i: (b, i, 0)),
                  pl.BlockSpec((1, tm, D), lambda b, i: (b, i, 0)),
                  pl.BlockSpec((1, 6, D), lambda b, i: (b, 0, 0)),
                  pl.BlockSpec((1, D), lambda b, i: (0, 0)),
                  pl.BlockSpec((2 * G, D), lambda b, i: (0, 0)),
                  pl.BlockSpec((E, D), lambda b, i: (0, 0))],
        out_specs=[pl.BlockSpec((1, tm, D), lambda b, i: (b, i, 0)),
                   pl.BlockSpec((1, tm, D // HEAD_LANES, HEAD_LANES), lambda b, i: (b, i, 0, 0)),
                   pl.BlockSpec((E, tm), lambda b, i, nt=nt: (0, b * nt + i))],
        compiler_params=pltpu.CompilerParams(
            dimension_semantics=("arbitrary", "arbitrary"), vmem_limit_bytes=VMEM_LIMIT),
        name="out_proj",
    )(a_out, b_out, x, mod, gain2.reshape(1, D), w_out_bf16, router.T)


def _route_kernel(lg_ref, bias_ref, idx_ref, w_ref, cnt_ref, *, tiles_per_group):
    @pl.when(pl.program_id(0) % tiles_per_group == 0)
    def _():
        cnt_ref[...] = jnp.zeros(cnt_ref.shape, F32)

    scores = _sigmoid(lg_ref[...])
    sel = scores + bias_ref[...]
    E, tm = sel.shape
    per_group = E // N_GROUPS
    neg = -jnp.inf
    eid = lax.broadcasted_iota(jnp.int32, (E, tm), 0).astype(F32)
    eid_g = lax.broadcasted_iota(jnp.int32, (per_group, tm), 0).astype(F32)

    def first_argmax(x, ids, sentinel):
        m = jnp.max(x, axis=0, keepdims=True)
        i = jnp.min(jnp.where(x == m, ids, sentinel), axis=0, keepdims=True)
        return m, i

    parts, gscore = [], []
    for g in range(N_GROUPS):
        x = sel[g * per_group:(g + 1) * per_group]
        parts.append(x)
        m1, i1 = first_argmax(x, eid_g, float(per_group))
        m2 = jnp.max(jnp.where(eid_g == i1, neg, x), axis=0, keepdims=True)
        gscore.append(m1 + m2)
    kept = []
    for g in range(N_GROUPS):
        beaten = jnp.zeros((1, tm), F32)
        for o in range(N_GROUPS):
            if o == g:
                continue
            wins = (gscore[o] >= gscore[g]) if o < g else (gscore[o] > gscore[g])
            beaten = beaten + wins.astype(F32)
        kept.append(jnp.where(beaten < TOPK_GROUPS, parts[g], neg))
    sel = jnp.concatenate(kept, axis=0)

    ids, vals = [], []
    w_sum = jnp.zeros((1, tm), F32)
    chosen = jnp.zeros((E, tm), F32)
    for k in range(TOP_K):
        _, i = first_argmax(sel, eid, float(E))
        hit = eid == i
        val = jnp.sum(jnp.where(hit, scores, 0.0), axis=0, keepdims=True)
        sel = jnp.where(hit, neg, sel)
        chosen = jnp.where(hit, 1.0, chosen)
        ids.append(i)
        vals.append(val)
        w_sum = w_sum + val
    idx_ref[...] = jnp.concatenate(ids, axis=0).astype(jnp.int32)
    w_ref[...] = jnp.concatenate(vals, axis=0) / w_sum * ROUTED_SCALE
    part = chosen[:, 0:HEAD_LANES]
    for l0 in range(HEAD_LANES, tm, HEAD_LANES):
        part = part + chosen[:, l0:l0 + HEAD_LANES]
    cnt_ref[0] = cnt_ref[0] + part


def route(logits_t, router_bias, group_tokens):
    E, N = logits_t.shape
    tm = min(512, N)
    tpg = group_tokens // tm
    idx, w, cnt = pl.pallas_call(
        functools.partial(_route_kernel, tiles_per_group=tpg),
        out_shape=[jax.ShapeDtypeStruct((TOP_K, N), jnp.int32), jax.ShapeDtypeStruct((TOP_K, N), F32),
                   jax.ShapeDtypeStruct((N // group_tokens, E, HEAD_LANES), F32)],
        grid=(N // tm,),
        in_specs=[pl.BlockSpec((E, tm), lambda i: (0, i)), pl.BlockSpec((E, 1), lambda i: (0, 0))],
        out_specs=[pl.BlockSpec((TOP_K, tm), lambda i: (0, i)), pl.BlockSpec((TOP_K, tm), lambda i: (0, i)),
                   pl.BlockSpec((1, E, HEAD_LANES), lambda i, tpg=tpg: (i // tpg, 0, 0))],
        compiler_params=pltpu.CompilerParams(dimension_semantics=("arbitrary",)),
        name="route",
    )(logits_t, router_bias.reshape(E, 1))
    return idx, w, jnp.sum(cnt, axis=-1).astype(jnp.int32)


def _loop_unrolled(n, body):
    u = MOE_LOOP_UNROLL
    lax.fori_loop(0, n // u, functools.partial(body, count=u), 0)
    lax.fori_loop(n // u * u, n, functools.partial(body, count=1), 0)


def _moe_kernel(off_ref, tok_ref, wl_ref, x_ref, wg_ref, wu_ref, wd_ref, acc_ref, xg_ref, yb_ref,
                *, R, E, per_group, experts_per_step):
    g = pl.program_id(0)
    step = pl.program_id(1)

    @pl.when(step == 0)
    def _():
        acc_ref[...] = jnp.zeros(acc_ref.shape, F32)

    @pl.when((g == 0) & (step == 0))
    def _():
        xg_ref[...] = jnp.zeros(xg_ref.shape, F32)

    U = SUBLANES
    chunks = U
    lanes = x_ref.shape[-1]

    def token_tile(ref, t8):
        return ref.at[0, pl.ds(pl.multiple_of(t8, U), U), :]

    def gather(le, s0, nr):
        def body(j, _, count):
            for gg in range(count):
                base = s0 + (j * count + gg) * U
                for i in range(U):
                    xg_ref[le, j * count + gg, pl.ds(i, chunks, stride=U), :] = (
                        token_tile(x_ref, tok_ref[base + i])[...])
            return 0

        whole = jnp.minimum((nr + U - 1) // U, (per_group - s0) // U)
        _loop_unrolled(whole, body)

        def single(r, _):
            xg_ref[le, r // U, pl.ds(r % U, chunks, stride=U), :] = token_tile(x_ref, tok_ref[s0 + r])[...]
            return 0

        lax.fori_loop(whole * U, nr, single, 0)

    def load_block(le):
        return jnp.concatenate([xg_ref[le, :, c * U:(c + 1) * U, :].reshape(R, lanes) for c in range(chunks)],
                               axis=1).astype(BF16)

    def store_block(le, y):
        for c in range(chunks):
            yb_ref[le, :, c * U:(c + 1) * U, :] = y[:, c * lanes:(c + 1) * lanes].reshape(R // U, U, lanes)

    def scatter(le, s0, nr):
        def group(j, _, count):
            base = s0 + j * count * U
            rows = range(count * U)
            tiles = [token_tile(acc_ref, tok_ref[base + i]) for i in rows]
            wts = [wl_ref[base + i] for i in rows]
            new = [tiles[i][...] + wts[i] * yb_ref[le, j * count + i // U, pl.ds(i % U, chunks, stride=U), :]
                   for i in rows]
            for i in rows:
                tiles[i][...] = new[i]
            return 0

        groups = nr // U
        _loop_unrolled(groups, group)

        def single(r, _):
            tile = token_tile(acc_ref, tok_ref[s0 + r])
            row = yb_ref[le, r // U, pl.ds(r % U, chunks, stride=U), :]
            tile[...] = tile[...] + wl_ref[s0 + r] * row
            return 0

        lax.fori_loop(groups * U, nr, single, 0)

    segments = []
    for le in range(experts_per_step):
        segment = g * E + step * experts_per_step + le
        seg = off_ref[segment]
        segments.append((seg - g * per_group, off_ref[segment + 1] - seg))

    for le, (start, cnt) in enumerate(segments):
        gather(le, start, jnp.minimum(R, cnt))
    xs = [load_block(le) for le in range(experts_per_step)]
    gu = [(_dot(xs[le], wg_ref[le].astype(BF16)), _dot(xs[le], wu_ref[le].astype(BF16)))
          for le in range(experts_per_step)]
    hs = [(_silu(gate) * up).astype(BF16) for gate, up in gu]
    for le in range(experts_per_step):
        store_block(le, _dot(hs[le], wd_ref[le].astype(BF16)))
    for le, (start, cnt) in enumerate(segments):
        scatter(le, start, jnp.minimum(R, cnt))

    for le, (start, cnt) in enumerate(segments):
        def more(sb, _, le=le, start=start, cnt=cnt):
            s0 = start + sb * R
            nr = jnp.minimum(R, cnt - sb * R)
            gather(le, s0, nr)
            x = load_block(le)
            hmid = _silu(_dot(x, wg_ref[le].astype(BF16))) * _dot(x, wu_ref[le].astype(BF16))
            store_block(le, _dot(hmid.astype(BF16), wd_ref[le].astype(BF16)))
            scatter(le, s0, nr)
            return 0

        lax.fori_loop(1, (cnt + R - 1) // R, more, 0)


def routed_experts(h2, top_idx, top_w, counts, exp_gate, exp_up, exp_down):
    N, chunks, lanes = h2.shape
    D = chunks * lanes
    E, _, F = exp_gate.shape
    TG = min(MOE_TOKEN_GROUP, N)
    G = N // TG
    per_group = TG * TOP_K
    key = (jnp.arange(N, dtype=jnp.int32)[None, :] // TG) * E + top_idx
    order = jnp.argsort(key.reshape(-1))
    assert chunks == SUBLANES
    tok_s = (((order % N) % TG) * chunks).astype(jnp.int32)
    w_s = top_w.reshape(-1)[order]
    off = jnp.concatenate([jnp.zeros((1,), jnp.int32), jnp.cumsum(counts.reshape(-1)).astype(jnp.int32)])

    EPS = MOE_EXPERTS_PER_STEP
    grid_spec = pltpu.PrefetchScalarGridSpec(
        num_scalar_prefetch=1,
        grid=(G, E // EPS),
        in_specs=[pl.BlockSpec((per_group,), lambda g, e, off: (g,), memory_space=pltpu.SMEM),
                  pl.BlockSpec((per_group,), lambda g, e, off: (g,), memory_space=pltpu.SMEM),
                  pl.BlockSpec((1, TG * chunks, lanes), lambda g, e, off: (g, 0, 0),
                               pipeline_mode=pl.Buffered(1)),
                  pl.BlockSpec((EPS, D, F), lambda g, e, off: (e, 0, 0)),
                  pl.BlockSpec((EPS, D, F), lambda g, e, off: (e, 0, 0)),
                  pl.BlockSpec((EPS, F, D), lambda g, e, off: (e, 0, 0))],
        out_specs=pl.BlockSpec((1, TG * chunks, lanes), lambda g, e, off: (g, 0, 0),
                               pipeline_mode=pl.Buffered(1)),
        scratch_shapes=[pltpu.VMEM((EPS, MOE_ROWS // SUBLANES, chunks * SUBLANES, lanes), F32),
                        pltpu.VMEM((EPS, MOE_ROWS // SUBLANES, chunks * SUBLANES, lanes), F32)],
    )
    out = pl.pallas_call(
        functools.partial(_moe_kernel, R=MOE_ROWS, E=E, per_group=per_group, experts_per_step=EPS),
        out_shape=jax.ShapeDtypeStruct((G, TG * chunks, lanes), F32),
        grid_spec=grid_spec,
        compiler_params=pltpu.CompilerParams(
            dimension_semantics=("arbitrary", "arbitrary"), vmem_limit_bytes=MOE_VMEM_LIMIT),
        name="routed_experts",
    )(off, tok_s, w_s, h2.reshape(G, TG * chunks, lanes), exp_gate, exp_up, exp_down)
    return out.reshape(N, chunks, lanes)


def _final_kernel(x1_ref, h2_ref, rt_ref, mod_ref, wg_ref, wu_ref, wd_ref, *rest, final):
    if final:
        fg_ref, o_ref = rest
    else:
        (o_ref,) = rest
    hb = _load_row_tiles(h2_ref.at[0]).astype(BF16)
    hmid = _silu(_dot(hb, wg_ref[...])) * _dot(hb, wu_ref[...])
    shared = _dot(hmid.astype(BF16), wd_ref[...])
    x2 = x1_ref[0] + mod_ref[0, 5:6, :] * (_load_row_tiles(rt_ref.at[0]) + shared)
    if final:
        x2 = _rms(x2, fg_ref[...])
    o_ref[0] = x2


def shared_and_residual(x1, h2, routed, mod, sh_gate, sh_up, sh_down, final_gain=None):
    B, T, D = x1.shape
    F = sh_gate.shape[1]
    tm = min(512, T)
    final = final_gain is not None
    tile = pl.BlockSpec((1, tm, D), lambda b, i: (b, i, 0))
    row_tiles = pl.BlockSpec((1, tm, D // HEAD_LANES, HEAD_LANES), lambda b, i: (b, i, 0, 0))
    in_specs = [tile, row_tiles, row_tiles,
                pl.BlockSpec((1, 6, D), lambda b, i: (b, 0, 0)),
                pl.BlockSpec((D, F), lambda b, i: (0, 0)),
                pl.BlockSpec((D, F), lambda b, i: (0, 0)),
                pl.BlockSpec((F, D), lambda b, i: (0, 0))]
    args = [x1, h2, routed, mod, sh_gate, sh_up, sh_down]
    if final:
        in_specs.append(pl.BlockSpec((1, D), lambda b, i: (0, 0)))
        args.append(final_gain.reshape(1, D))
    return pl.pallas_call(
        functools.partial(_final_kernel, final=final),
        out_shape=jax.ShapeDtypeStruct((B, T, D), F32),
        grid=(B, T // tm),
        in_specs=in_specs,
        out_specs=tile,
        compiler_params=pltpu.CompilerParams(
            dimension_semantics=("arbitrary", "arbitrary"), vmem_limit_bytes=VMEM_LIMIT),
        name="shared_and_residual",
    )(*args)


def moe_block(x1, h2, logits, mod, router_bias, exp_gate, exp_up, exp_down, sh_gate, sh_up, sh_down,
              final_gain=None):
    B, T, D = x1.shape
    top_idx, top_w, counts = route(logits, router_bias, min(MOE_TOKEN_GROUP, B * T))
    chunks = D // HEAD_LANES
    routed = routed_experts(h2.reshape(B * T, chunks, HEAD_LANES), top_idx, top_w, counts,
                            exp_gate, exp_up, exp_down)
    return shared_and_residual(x1, h2, routed.reshape(B, T, chunks, HEAD_LANES), mod,
                               sh_gate.astype(BF16), sh_up.astype(BF16), sh_down.astype(BF16), final_gain)


def kernel(x, c, hgrn_lb_logits, l0_norm1, l0_norm2, l0_w_mod, l0_b_mod, l0_w_in, l0_w_out, l0_hgrn_norm, l0_diff_lq1, l0_diff_lk1, l0_diff_lq2, l0_diff_lk2, l0_diff_subln, l0_router, l0_router_bias, l0_exp_gate, l0_exp_up, l0_exp_down, l0_sh_gate, l0_sh_up, l0_sh_down, l1_norm1, l1_norm2, l1_w_mod, l1_b_mod, l1_w_in, l1_w_out, l1_conv_w, l1_conv_b, l1_ml_i_bias, l1_ml_f_bias, l1_ml_norm, l1_na_rpb, l1_router, l1_router_bias, l1_exp_gate, l1_exp_up, l1_exp_down, l1_sh_gate, l1_sh_up, l1_sh_down, final_norm):
    G = GROUP_WIDTH
    lb_all = jnp.cumsum(jax.nn.softmax(hgrn_lb_logits.astype(F32), axis=0), axis=0)
    layer_idx = 0
    lambda_init = 0.8 - 0.6 * math.exp(-0.3 * layer_idx)
    lam = (jnp.exp(jnp.sum(l0_diff_lq1.astype(F32) * l0_diff_lk1.astype(F32)))
           - jnp.exp(jnp.sum(l0_diff_lq2.astype(F32) * l0_diff_lk2.astype(F32))) + lambda_init)

    mod0 = ada_mod(c, l0_w_mod, l0_b_mod)
    y0 = in_proj(x, mod0, l0_norm1, l0_w_in.astype(BF16))
    a_out = hgrn2(y0, lb_all[0], l0_hgrn_norm)
    b_out = diff_attention(y0, lam, l0_diff_subln, layer_idx)
    x1, h2, logits = out_proj(a_out, b_out, x, mod0, l0_norm2, l0_w_out.astype(BF16), l0_router)
    xa = moe_block(x1, h2, logits, mod0, l0_router_bias, l0_exp_gate, l0_exp_up, l0_exp_down,
                   l0_sh_gate, l0_sh_up, l0_sh_down)

    mod1 = ada_mod(c, l1_w_mod, l1_b_mod)
    n_gate = l1_w_in.shape[1] - 7 * G
    w_main = jnp.concatenate([l1_w_in[:, :4 * G], l1_w_in[:, 4 * G + n_gate:]], axis=1).astype(BF16)
    w_gate = jnp.pad(l1_w_in[:, 4 * G:4 * G + n_gate], ((0, 0), (0, HEAD_LANES - n_gate)))
    y1, gates = in_proj(xa, mod1, l1_norm1, w_main, w_gate)
    c_out = mlstm(y1, gates, l1_conv_w, l1_conv_b, l1_ml_i_bias, l1_ml_f_bias, l1_ml_norm)
    d_out = neighbourhood_attention(y1, l1_na_rpb)
    x1, h2, logits = out_proj(c_out, d_out, xa, mod1, l1_norm2, l1_w_out.astype(BF16), l1_router)
    return moe_block(x1, h2, logits, mod1, l1_router_bias, l1_exp_gate, l1_exp_up, l1_exp_down,
                     l1_sh_gate, l1_sh_up, l1_sh_down, final_gain=final_norm)
```
